```python
import math
import jax, jax.numpy as jnp
from jax import lax
import numpy as np

D_MODEL = 1024
BATCH = 2
SEQ = 8192
DEPTH = 2

GRID_W = 64
CTX_LEN = 256
HEAD_DIM = 64
ROPE_THETA = 10000.0
EPS = 1e-6
Q_BLOCK = 128

A_HEADS = 4
A_VDIM = 2 * HEAD_DIM
A_WIDTH = A_HEADS * A_VDIM
B_HEADS = 8
B_KV = 2
B_REP = B_HEADS // B_KV
B_WIDTH = B_HEADS * HEAD_DIM
C_WIDTH = 512
C_KW = 31
P_HEADS = 8
P_NKEYS = 128
P_EXPERTS = P_NKEYS * P_NKEYS
P_TOPK = 16
P_DKEY = 128
P_BLOCK = 128

COL_KA = A_HEADS * 2 * HEAD_DIM
COL_VA = A_WIDTH
COL_KB = B_KV * HEAD_DIM
COL_VB = B_KV * HEAD_DIM
COL_QA = A_HEADS * 2 * HEAD_DIM
COL_QB = B_HEADS * HEAD_DIM
COL_GLU = 2 * C_WIDTH
COL_GATE = 3 * D_MODEL
KV_COLS = COL_KA + COL_VA + COL_KB + COL_VB
Q_COLS = COL_QA + COL_QB
IN_COLS = KV_COLS + Q_COLS + COL_GLU + COL_GATE
KV_SPLITS = [COL_KA, COL_KA + COL_VA, COL_KA + COL_VA + COL_KB]
MAIN_SPLITS = [KV_COLS, KV_COLS + Q_COLS, KV_COLS + Q_COLS + COL_GLU]

kernel_name = "hybrid_diffattn_gqa_conformer_peer_dit"


def rms_norm(x, w=None):
    xf = x.astype(jnp.float32)
    y = xf * lax.rsqrt(jnp.mean(xf * xf, axis=-1, keepdims=True) + EPS)
    if w is not None:
        y = y * w.astype(jnp.float32)
    return y.astype(x.dtype)


def modulate(x, shift, scale):
    return rms_norm(x) * (1 + scale) + shift


def axial_rope(n):
    t = jnp.arange(n, dtype=jnp.int32)
    row = (t // GRID_W).astype(jnp.float32)
    col = (t % GRID_W).astype(jnp.float32)
    axis_dim = HEAD_DIM // 2
    inv = ROPE_THETA ** (-jnp.arange(0, axis_dim, 2, dtype=jnp.float32) / axis_dim)
    ar = row[:, None] * inv[None, :]
    ac = col[:, None] * inv[None, :]
    ang = jnp.concatenate([ar, ar, ac, ac], axis=-1)
    return jnp.cos(ang), jnp.sin(ang)


def apply_rope(x, cos, sin):
    x1, x2, x3, x4 = jnp.split(x, 4, axis=-1)
    rot = jnp.concatenate([-x2, x1, -x4, x3], axis=-1)
    return (x.astype(jnp.float32) * cos + rot.astype(jnp.float32) * sin).astype(x.dtype)


def kv_heads(p_kv, k_norm_w):
    b, n = p_kv.shape[:2]
    k_a, v_a, k_b, v_b = jnp.split(p_kv, KV_SPLITS, axis=-1)
    k_a = k_a.reshape(b, n, A_HEADS, 2, HEAD_DIM).transpose(0, 2, 3, 1, 4)
    v_a = v_a.reshape(b, n, A_HEADS, A_VDIM).transpose(0, 2, 1, 3)
    k_b = rms_norm(k_b.reshape(b, n, B_KV, HEAD_DIM), k_norm_w).transpose(0, 2, 1, 3)
    v_b = v_b.reshape(b, n, B_KV, HEAD_DIM).transpose(0, 2, 1, 3)
    return k_a, v_a, k_b, v_b


def q_heads(p_q, q_norm_w):
    b, n = p_q.shape[:2]
    q_a, q_b = jnp.split(p_q, [COL_QA], axis=-1)
    q_a = q_a.reshape(b, n, A_HEADS, 2, HEAD_DIM).transpose(0, 2, 3, 1, 4)
    q_b = rms_norm(q_b.reshape(b, n, B_KV, B_REP, HEAD_DIM), q_norm_w).transpose(0, 2, 3, 1, 4)
    return q_a, q_b


def diff_core(q, k, v, lam):
    s = jnp.einsum('bhmqd,bhmkd->bhmqk', q, k).astype(jnp.float32) * (HEAD_DIM ** -0.5)
    p = jax.nn.softmax(s, axis=-1)
    a = p[:, :, 0] - lam * p[:, :, 1]
    return jnp.einsum('bhqk,bhkv->bhqv', a.astype(v.dtype), v)


def gqa_core(q, k, v):
    s = jnp.einsum('bgrqd,bgkd->bgrqk', q, k).astype(jnp.float32) * (HEAD_DIM ** -0.5)
    p = jax.nn.softmax(s, axis=-1)
    return jnp.einsum('bgrqk,bgkv->bgrqv', p.astype(v.dtype), v)


def sweep_blocks(core, q, *rest):
    s, d = q.shape[-2], q.shape[-1]
    nb = s // Q_BLOCK
    qb = jnp.moveaxis(q.reshape(q.shape[:-2] + (nb, Q_BLOCK, d)), -3, 0)
    ob = lax.map(lambda qq: core(qq, *rest), qb)
    ob = jnp.moveaxis(ob, 0, -3)
    return ob.reshape(ob.shape[:-3] + (s, ob.shape[-1]))


def conformer_conv(glu_in, conv_w, conv_b, ln_w, ln_b):
    a, g = jnp.split(glu_in, 2, axis=-1)
    u = a * jax.nn.sigmoid(g)
    y = lax.conv_general_dilated(
        u, conv_w.astype(u.dtype)[:, None, :], window_strides=(1,),
        padding=[(C_KW // 2, C_KW // 2)], dimension_numbers=('NWC', 'WIO', 'NWC'),
        feature_group_count=C_WIDTH) + conv_b
    yf = y.astype(jnp.float32)
    mu = jnp.mean(yf, axis=-1, keepdims=True)
    var = jnp.mean(jnp.square(yf - mu), axis=-1, keepdims=True)
    yn = (yf - mu) * lax.rsqrt(var + EPS) * ln_w.astype(jnp.float32) + ln_b.astype(jnp.float32)
    return jax.nn.silu(yn).astype(u.dtype)


def merge_branches(o_a, o_b, glu_in, gate_logits, lam_init, subln_w, conv_w, conv_b, ln_w, ln_b,
                   w_a, w_b, w_c, w_o):
    b, _, n, _ = o_a.shape
    o_a = rms_norm(o_a.transpose(0, 2, 1, 3), subln_w) * (1.0 - lam_init)
    o_a = o_a.reshape(b, n, A_WIDTH)
    o_b = o_b.transpose(0, 3, 1, 2, 4).reshape(b, n, B_WIDTH)
    o_c = conformer_conv(glu_in, conv_w, conv_b, ln_w, ln_b)
    g_a, g_b, g_c = jnp.split(jax.nn.sigmoid(gate_logits), 3, axis=-1)
    merged = g_a * (o_a @ w_a) + g_b * (o_b @ w_b) + g_c * (o_c @ w_c)
    return merged @ w_o


def peer(h, w_q, sub_keys, u_tab, v_tab):
    shp = h.shape
    t = h.reshape(-1, D_MODEL)
    n_tok = t.shape[0]
    q = (t @ w_q).reshape(n_tok, P_HEADS, 2, P_DKEY // 2)
    s = jnp.einsum('thpd,hpkd->thpk', q, sub_keys).astype(jnp.float32)
    sv, si = lax.top_k(s, P_TOPK)
    cand = (sv[:, :, 0, :, None] + sv[:, :, 1, None, :]).reshape(n_tok, P_HEADS, P_TOPK * P_TOPK)
    cand_idx = (si[:, :, 0, :, None] * P_NKEYS + si[:, :, 1, None, :]).reshape(n_tok, P_HEADS, P_TOPK * P_TOPK)
    fv, fi = lax.top_k(cand, P_TOPK)
    eidx = jnp.take_along_axis(cand_idx, fi, axis=-1)
    g = jax.nn.softmax(fv, axis=-1).astype(t.dtype)
    nb = n_tok // P_BLOCK

    def block(args):
        tb, eb, gb = args
        act = jax.nn.gelu(jnp.einsum('pd,pkd->pk', tb, u_tab[eb]), approximate=False)
        return jnp.einsum('pk,pkd->pd', act * gb, v_tab[eb])

    out = lax.map(block, (t.reshape(nb, P_BLOCK, D_MODEL),
                          eidx.reshape(nb, P_BLOCK, P_HEADS * P_TOPK),
                          g.reshape(nb, P_BLOCK, P_HEADS * P_TOPK)))
    return out.reshape(shp)


def setup_inputs(seed: int = 0) -> dict:
    key = jax.random.key(seed)
    ks = jax.random.split(key, 32)
    f32 = jnp.float32
    nrm = lambda k, shape, s: jax.random.normal(k, shape, f32) * s
    L = DEPTH
    return {
        "x": nrm(ks[0], (BATCH, SEQ, D_MODEL), 1.0),
        "c": nrm(ks[1], (BATCH, D_MODEL), 1.0),
        "ctx": nrm(ks[2], (BATCH, CTX_LEN, D_MODEL), 1.0),
        "c_ctx": nrm(ks[3], (D_MODEL,), 1.0),
        "w_ada": nrm(ks[4], (L, D_MODEL, 6 * D_MODEL), D_MODEL ** -0.5),
        "b_ada": nrm(ks[5], (L, 6 * D_MODEL), 0.01),
        "w_in": nrm(ks[6], (L, D_MODEL, IN_COLS), D_MODEL ** -0.5),
        "lam_q1": nrm(ks[7], (L, HEAD_DIM), 0.1),
        "lam_k1": nrm(ks[8], (L, HEAD_DIM), 0.1),
        "lam_q2": nrm(ks[9], (L, HEAD_DIM), 0.1),
        "lam_k2": nrm(ks[10], (L, HEAD_DIM), 0.1),
        "subln_w": 1.0 + nrm(ks[11], (L, A_VDIM), 0.01),
        "q_norm_w": 1.0 + nrm(ks[12], (L, HEAD_DIM), 0.01),
        "k_norm_w": 1.0 + nrm(ks[13], (L, HEAD_DIM), 0.01),
        "conv_w": nrm(ks[14], (L, C_KW, C_WIDTH), C_KW ** -0.5),
        "conv_b": nrm(ks[15], (L, C_WIDTH), 0.01),
        "conv_ln_w": 1.0 + nrm(ks[16], (L, C_WIDTH), 0.01),
        "conv_ln_b": nrm(ks[17], (L, C_WIDTH), 0.01),
        "w_branch_a": nrm(ks[18], (L, A_WIDTH, D_MODEL), A_WIDTH ** -0.5),
        "w_branch_b": nrm(ks[19], (L, B_WIDTH, D_MODEL), B_WIDTH ** -0.5),
        "w_branch_c": nrm(ks[20], (L, C_WIDTH, D_MODEL), C_WIDTH ** -0.5),
        "w_out": nrm(ks[21], (L, D_MODEL, D_MODEL), D_MODEL ** -0.5),
        "peer_wq": nrm(ks[22], (L, D_MODEL, P_HEADS * P_DKEY), D_MODEL ** -0.5),
        "peer_keys": nrm(ks[23], (L, P_HEADS, 2, P_NKEYS, P_DKEY // 2), (P_DKEY // 2) ** -0.5),
        "peer_u": nrm(ks[24], (L, P_EXPERTS, D_MODEL), D_MODEL ** -0.5),
        "peer_v": nrm(ks[25], (L, P_EXPERTS, D_MODEL), P_HEADS ** -0.5),
        "final_norm_w": 1.0 + nrm(ks[26], (D_MODEL,), 0.01),
    }


def reference(x, c, ctx, c_ctx, w_ada, b_ada, w_in, lam_q1, lam_k1, lam_q2, lam_k2, subln_w,
              q_norm_w, k_norm_w, conv_w, conv_b, conv_ln_w, conv_ln_b, w_branch_a, w_branch_b,
              w_branch_c, w_out, peer_wq, peer_keys, peer_u, peer_v, final_norm_w):
    n_lat = x.shape[1]
    cos, sin = axial_rope(n_lat)
    sc = jax.nn.silu(c)
    scc = jax.nn.silu(c_ctx)
    y = ctx
    for l in range(DEPTH):
        last = l == DEPTH - 1
        lam_init = 0.8 - 0.6 * math.exp(-0.3 * l)
        lam = (jnp.exp(jnp.sum(lam_q1[l] * lam_k1[l]).astype(jnp.float32))
               - jnp.exp(jnp.sum(lam_q2[l] * lam_k2[l]).astype(jnp.float32)) + lam_init)
        mod = (sc @ w_ada[l] + b_ada[l])[:, None, :]
        sh1, sc1, g1, sh2, sc2, g2 = jnp.split(mod, 6, axis=-1)
        modc = scc @ w_ada[l] + b_ada[l]
        csh1, csc1, cg1, csh2, csc2, cg2 = jnp.split(modc, 6, axis=-1)
        branch_params = (subln_w[l], conv_w[l], conv_b[l], conv_ln_w[l], conv_ln_b[l],
                         w_branch_a[l], w_branch_b[l], w_branch_c[l], w_out[l])

        hc = modulate(y, csh1, csc1)
        if last:
            pc_kv = hc @ w_in[l][:, :KV_COLS]
        else:
            pc = hc @ w_in[l]
            pc_kv, pc_q, pc_glu, pc_gate = jnp.split(pc, MAIN_SPLITS, axis=-1)
        ka_c, va_c, kb_c, vb_c = kv_heads(pc_kv, k_norm_w[l])

        hx = modulate(x, sh1, sc1)
        px = hx @ w_in[l]
        px_kv, px_q, px_glu, px_gate = jnp.split(px, MAIN_SPLITS, axis=-1)
        ka_x, va_x, kb_x, vb_x = kv_heads(px_kv, k_norm_w[l])
        qa_x, qb_x = q_heads(px_q, q_norm_w[l])
        ka_all = jnp.concatenate([apply_rope(ka_x, cos, sin), ka_c], axis=-2)
        va_all = jnp.concatenate([va_x, va_c], axis=-2)
        kb_all = jnp.concatenate([apply_rope(kb_x, cos, sin), kb_c], axis=-2)
        vb_all = jnp.concatenate([vb_x, vb_c], axis=-2)
        o_a = sweep_blocks(diff_core, apply_rope(qa_x, cos, sin), ka_all, va_all, lam)
        o_b = sweep_blocks(gqa_core, apply_rope(qb_x, cos, sin), kb_all, vb_all)
        x = x + g1 * merge_branches(o_a, o_b, px_glu, px_gate, lam_init, *branch_params)
        x = x + g2 * peer(modulate(x, sh2, sc2), peer_wq[l], peer_keys[l], peer_u[l], peer_v[l])

        if not last:
            qa_c, qb_c = q_heads(pc_q, q_norm_w[l])
            oa_c = diff_core(qa_c, ka_c, va_c, lam)
            ob_c = gqa_core(qb_c, kb_c, vb_c)
            y = y + cg1 * merge_branches(oa_c, ob_c, pc_glu, pc_gate, lam_init, *branch_params)
            y = y + cg2 * peer(modulate(y, csh2, csc2), peer_wq[l], peer_keys[l], peer_u[l], peer_v[l])
    return rms_norm(x, final_norm_w)
```

```python
import functools
import math

import jax
import jax.numpy as jnp
from jax import lax
from jax.experimental import pallas as pl
from jax.experimental.pallas import tpu as pltpu

F32 = jnp.float32
BF16 = jnp.bfloat16

D_MODEL = 1024
DEPTH = 2
GRID_W = 64
HEAD_DIM = 64
ROPE_THETA = 10000.0
EPS = 1e-6
A_HEADS = 4
B_KV = 2
C_WIDTH = 512
C_KW = 31
P_HEADS = 8
P_NKEYS = 128
P_TOPK = 16

COL_KA, COL_VA, COL_KB, COL_VB, COL_QA, COL_QB, COL_GLU = 512, 512, 128, 128, 512, 512, 1024
OFF_KA = 0
OFF_VA = OFF_KA + COL_KA
OFF_KB = OFF_VA + COL_VA
OFF_QA = OFF_KB + COL_KB + COL_VB
OFF_QB = OFF_QA + COL_QA
OFF_GLU = OFF_QB + COL_QB
OFF_GATE = OFF_GLU + COL_GLU

V7X_LANES = 128
V7X_VMEM_LIMIT = 56 * 1024 * 1024
HALO = 16

TOK_BLOCK = 256
ATT_TQ = 256
ATT_TK = 512
PEER_BLOCK = 512
PEER_ECHUNK = 1024
MOD_ROWS = 8
ROW_GROUP = 16

Q_SCALE = (HEAD_DIM ** -0.5) * math.log2(math.e)
NEG_BIG = -1e30
NT_DIMS = (((1,), (1,)), ((), ()))


def _cparams(sem):
    return pltpu.CompilerParams(dimension_semantics=sem, vmem_limit_bytes=V7X_VMEM_LIMIT)


def _modulated(x, mod, shift_row, scale_row):
    ms = jnp.mean(x * x, axis=-1, keepdims=True)
    xn = x * lax.rsqrt(ms + EPS)
    return xn * (1.0 + mod[scale_row:scale_row + 1]) + mod[shift_row:shift_row + 1]


def _ada_kernel(c_ref, w_ref, b_ref, o_ref):
    c = c_ref[...]
    sc = c * jax.nn.sigmoid(c)
    o_ref[0] = jnp.dot(sc, w_ref[0], preferred_element_type=F32,
                       precision=lax.Precision.HIGHEST) + b_ref[0]


def _ada_rows(cvec, w_ada, b_ada):
    depth, d, n = w_ada.shape
    tn = 1536
    return pl.pallas_call(
        _ada_kernel,
        grid=(depth, n // tn),
        in_specs=[pl.BlockSpec((MOD_ROWS, d), lambda l, j: (0, 0)),
                  pl.BlockSpec((1, d, tn), lambda l, j: (l, 0, j)),
                  pl.BlockSpec((1, 1, tn), lambda l, j: (l, 0, j))],
        out_specs=pl.BlockSpec((1, MOD_ROWS, tn), lambda l, j: (l, 0, j)),
        out_shape=jax.ShapeDtypeStruct((depth, MOD_ROWS, n), F32),
        compiler_params=_cparams(("arbitrary", "arbitrary")),
    )(cvec, w_ada, b_ada.reshape(depth, 1, n))


def _inproj_kernel(x_ref, mod_ref, w_ref, cos_ref, sa_ref, sb_ref, qnw_ref, knw_ref,
                   qa_ref, ka_ref, va_ref, qb_ref, kbd_ref, vbd_ref, u_ref):
    bt = x_ref.shape[0]
    h = _modulated(x_ref[...], mod_ref[0], 0, 1).astype(BF16)
    cos, sa, sb = cos_ref[...], sa_ref[...], sb_ref[...]
    lo = lax.broadcasted_iota(jnp.int32, (bt, V7X_LANES), 1) < HEAD_DIM

    def proj(c0, width):
        return jnp.dot(h, w_ref[:, c0:c0 + width], preferred_element_type=F32)

    def rope(x):
        return x * cos + pltpu.roll(x, V7X_LANES - 16, 1) * sa + pltpu.roll(x, 16, 1) * sb

    def headnorm(x, w):
        x2 = x * x
        s_lo = jnp.sum(jnp.where(lo, x2, 0.0), axis=-1, keepdims=True)
        s_hi = jnp.sum(jnp.where(lo, 0.0, x2), axis=-1, keepdims=True)
        r = jnp.where(lo, lax.rsqrt(s_lo / HEAD_DIM + EPS), lax.rsqrt(s_hi / HEAD_DIM + EPS))
        return x * r * w

    def tile(p, c):
        return p[:, c * V7X_LANES:(c + 1) * V7X_LANES]

    p = proj(OFF_KA, COL_KA)
    for c in range(COL_KA // V7X_LANES):
        ka_ref[:, c * V7X_LANES:(c + 1) * V7X_LANES] = rope(tile(p, c)).astype(BF16)

    va_ref[...] = proj(OFF_VA, COL_VA).astype(BF16)

    p = proj(OFF_KB, COL_KB + COL_VB)
    kb = rope(headnorm(tile(p, 0), knw_ref[...]))
    kb_sw = pltpu.roll(kb, HEAD_DIM, 1)
    kbd_ref[0] = jnp.where(lo, kb, kb_sw).astype(BF16)
    kbd_ref[1] = jnp.where(lo, kb_sw, kb).astype(BF16)
    vb = tile(p, 1)
    vb_sw = pltpu.roll(vb, HEAD_DIM, 1)
    vbd_ref[0] = jnp.where(lo, vb, vb_sw).astype(BF16)
    vbd_ref[1] = jnp.where(lo, vb_sw, vb).astype(BF16)

    p = proj(OFF_QA, COL_QA)
    for c in range(COL_QA // V7X_LANES):
        qa_ref[:, c * V7X_LANES:(c + 1) * V7X_LANES] = (rope(tile(p, c)) * Q_SCALE).astype(BF16)

    p = proj(OFF_QB, COL_QB)
    qnw = qnw_ref[...]
    for c in range(COL_QB // V7X_LANES):
        qb_ref[:, c * V7X_LANES:(c + 1) * V7X_LANES] = (
            rope(headnorm(tile(p, c), qnw)) * Q_SCALE).astype(BF16)

    p = proj(OFF_GLU, COL_GLU)
    u_ref[...] = p[:, :C_WIDTH] * jax.nn.sigmoid(p[:, C_WIDTH:])


def _inproj(xy, mod_l, w_main, tabs, qnw, knw, dims):
    t_rows, nps, nbatch = dims["T"], dims["S"] // TOK_BLOCK, dims["B"]
    bt = TOK_BLOCK
    row = lambda r: (r, 0)
    full = lambda r: (0, 0)
    out_shapes = (
        jax.ShapeDtypeStruct((t_rows, COL_QA), BF16),
        jax.ShapeDtypeStruct((t_rows, COL_KA), BF16),
        jax.ShapeDtypeStruct((t_rows, COL_VA), BF16),
        jax.ShapeDtypeStruct((t_rows, COL_QB), BF16),
        jax.ShapeDtypeStruct((B_KV, t_rows, V7X_LANES), BF16),
        jax.ShapeDtypeStruct((B_KV, t_rows, V7X_LANES), BF16),
        jax.ShapeDtypeStruct((t_rows, C_WIDTH), F32),
    )
    grp = pl.BlockSpec((B_KV, bt, V7X_LANES), lambda r: (0, r, 0))
    return pl.pallas_call(
        _inproj_kernel,
        grid=(t_rows // bt,),
        in_specs=[pl.BlockSpec((bt, D_MODEL), row),
                  pl.BlockSpec((1, 6, D_MODEL), lambda r: (jnp.minimum(r // nps, nbatch), 0, 0)),
                  pl.BlockSpec((D_MODEL, OFF_GATE), full),
                  pl.BlockSpec((bt, V7X_LANES), row),
                  pl.BlockSpec((bt, V7X_LANES), row),
                  pl.BlockSpec((bt, V7X_LANES), row),
                  pl.BlockSpec((1, V7X_LANES), full),
                  pl.BlockSpec((1, V7X_LANES), full)],
        out_specs=(pl.BlockSpec((bt, COL_QA), row), pl.BlockSpec((bt, COL_KA), row),
                   pl.BlockSpec((bt, COL_VA), row), pl.BlockSpec((bt, COL_QB), row),
                   grp, grp, pl.BlockSpec((bt, C_WIDTH), row)),
        out_shape=out_shapes,
        compiler_params=_cparams(("arbitrary",)),
    )(xy, mod_l, w_main, tabs[0], tabs[1], tabs[2], qnw, knw)


def _softmax_step(s, v, m, l, acc):
    mn = jnp.maximum(m, jnp.max(s, axis=-1, keepdims=True))
    p = jnp.exp2(s - mn)
    alpha = jnp.exp2(m - mn)
    l = alpha * l + jnp.sum(p, axis=-1, keepdims=True)
    acc = alpha * acc + jnp.dot(p.astype(BF16), v, preferred_element_type=F32)
    return mn, l, acc


def _attn_a_kernel(q_ref, kl_ref, vl_ref, kc_ref, vc_ref, lamv_ref, sw_ref, o_ref, *,
                   nq_lat, n_lat_chunks, tk, lam_init):
    tq = q_ref.shape[0]
    q = q_ref[...]
    lo = lax.broadcasted_iota(jnp.int32, (tq, V7X_LANES), 1) < HEAD_DIM
    zero = jnp.zeros_like(q)
    q1 = jnp.where(lo, q, zero)
    q2 = jnp.where(lo, zero, q)

    def step(k, v, carry):
        m1, l1, a1, m2, l2, a2 = carry
        s1 = lax.dot_general(q1, k, NT_DIMS, preferred_element_type=F32)
        m1, l1, a1 = _softmax_step(s1, v, m1, l1, a1)
        s2 = lax.dot_general(q2, k, NT_DIMS, preferred_element_type=F32)
        m2, l2, a2 = _softmax_step(s2, v, m2, l2, a2)
        return m1, l1, a1, m2, l2, a2

    def lat_body(j, carry):
        off = pl.multiple_of(j * tk, tk)
        return step(kl_ref[pl.ds(off, tk), :], vl_ref[pl.ds(off, tk), :], carry)

    col = jnp.full((tq, 1), NEG_BIG, F32)
    zc = jnp.zeros((tq, 1), F32)
    za = jnp.zeros((tq, V7X_LANES), F32)
    n_lat = jnp.where(pl.program_id(2) < nq_lat, n_lat_chunks, 0)
    carry = lax.fori_loop(0, n_lat, lat_body, (col, zc, za, col, zc, za))
    m1, l1, a1, m2, l2, a2 = step(kc_ref[...], vc_ref[...], carry)

    lv = lamv_ref[...]
    lam = (jnp.exp(jnp.sum(lv[0:1] * lv[1:2], axis=-1, keepdims=True))
           - jnp.exp(jnp.sum(lv[2:3] * lv[3:4], axis=-1, keepdims=True)) + lam_init)
    o = a1 / l1 - lam * (a2 / l2)
    ms = jnp.mean(o * o, axis=-1, keepdims=True)
    o_ref[...] = (o * lax.rsqrt(ms + EPS) * sw_ref[...] * (1.0 - lam_init)).astype(BF16)


def _attn_b_kernel(q_ref, kl_ref, vl_ref, kc_ref, vc_ref, o_ref, *, nq_lat, n_lat_chunks, tk):
    tq = q_ref.shape[0]
    lo = lax.broadcasted_iota(jnp.int32, (tq, V7X_LANES), 1) < HEAD_DIM
    parts = []
    for c in range(2):
        qc = q_ref[:, c * V7X_LANES:(c + 1) * V7X_LANES]
        zero = jnp.zeros_like(qc)
        parts += [jnp.where(lo, qc, zero), jnp.where(lo, zero, qc)]
    q4 = jnp.concatenate(parts, axis=0)

    def step(k, v, carry):
        m, l, a = carry
        s = lax.dot_general(q4, k, NT_DIMS, preferred_element_type=F32)
        return _softmax_step(s, v, m, l, a)

    def lat_body(j, carry):
        off = pl.multiple_of(j * tk, tk)
        return step(kl_ref[0, pl.ds(off, tk), :], vl_ref[0, pl.ds(off, tk), :], carry)

    carry = (jnp.full((4 * tq, 1), NEG_BIG, F32), jnp.zeros((4 * tq, 1), F32),
             jnp.zeros((4 * tq, V7X_LANES), F32))
    n_lat = jnp.where(pl.program_id(2) < nq_lat, n_lat_chunks, 0)
    carry = lax.fori_loop(0, n_lat, lat_body, carry)
    m, l, a = step(kc_ref[0], vc_ref[0], carry)
    o = a / l
    for c in range(2):
        o_ref[:, c * V7X_LANES:(c + 1) * V7X_LANES] = jnp.where(
            lo, o[(2 * c) * tq:(2 * c + 1) * tq], o[(2 * c + 1) * tq:(2 * c + 2) * tq]).astype(BF16)


def _q_block_index(b, qi, nq_lat, nq_ctx, nbatch):
    return jnp.where(qi < nq_lat, b * nq_lat + qi, nbatch * nq_lat + b * nq_ctx + (qi - nq_lat))


def _attention(qa, ka, va, qb, kbd, vbd, lamv, subln_w, lam_init, with_ctx_queries, dims):
    s_len, ctx_len, nbatch, t_rows = dims["S"], dims["CTX"], dims["B"], dims["T"]
    tq, tk = ATT_TQ, min(ATT_TK, s_len)
    nq_lat, nq_ctx = s_len // tq, ctx_len // tq
    nq = nq_lat + (nq_ctx if with_ctx_queries else 0)
    ctx_blk0 = (nbatch * s_len) // ctx_len
    qidx = functools.partial(_q_block_index, nq_lat=nq_lat, nq_ctx=nq_ctx, nbatch=nbatch)
    sem = ("arbitrary", "arbitrary", "arbitrary")

    oa = pl.pallas_call(
        functools.partial(_attn_a_kernel, nq_lat=nq_lat, n_lat_chunks=s_len // tk, tk=tk,
                          lam_init=lam_init),
        grid=(nbatch, A_HEADS, nq),
        in_specs=[pl.BlockSpec((tq, V7X_LANES), lambda b, h, qi: (qidx(b, qi), h)),
                  pl.BlockSpec((s_len, V7X_LANES), lambda b, h, qi: (b, h)),
                  pl.BlockSpec((s_len, V7X_LANES), lambda b, h, qi: (b, h)),
                  pl.BlockSpec((ctx_len, V7X_LANES), lambda b, h, qi: (ctx_blk0 + b, h)),
                  pl.BlockSpec((ctx_len, V7X_LANES), lambda b, h, qi: (ctx_blk0 + b, h)),
                  pl.BlockSpec((4, HEAD_DIM), lambda b, h, qi: (0, 0)),
                  pl.BlockSpec((1, V7X_LANES), lambda b, h, qi: (0, 0))],
        out_specs=pl.BlockSpec((tq, V7X_LANES), lambda b, h, qi: (qidx(b, qi), h)),
        out_shape=jax.ShapeDtypeStruct((t_rows, A_HEADS * V7X_LANES), BF16),
        compiler_params=_cparams(sem),
    )(qa, ka, va, ka, va, lamv, subln_w)

    gw = 2 * V7X_LANES
    ob = pl.pallas_call(
        functools.partial(_attn_b_kernel, nq_lat=nq_lat, n_lat_chunks=s_len // tk, tk=tk),
        grid=(nbatch, B_KV, nq),
        in_specs=[pl.BlockSpec((tq, gw), lambda b, g, qi: (qidx(b, qi), g)),
                  pl.BlockSpec((1, s_len, V7X_LANES), lambda b, g, qi: (g, b, 0)),
                  pl.BlockSpec((1, s_len, V7X_LANES), lambda b, g, qi: (g, b, 0)),
                  pl.BlockSpec((1, ctx_len, V7X_LANES), lambda b, g, qi: (g, ctx_blk0 + b, 0)),
                  pl.BlockSpec((1, ctx_len, V7X_LANES), lambda b, g, qi: (g, ctx_blk0 + b, 0))],
        out_specs=pl.BlockSpec((tq, gw), lambda b, g, qi: (qidx(b, qi), g)),
        out_shape=jax.ShapeDtypeStruct((t_rows, B_KV * gw), BF16),
        compiler_params=_cparams(sem),
    )(qb, kbd, vbd, kbd, vbd)
    return oa, ob


def _merge_kernel(x_ref, mod_ref, oa_ref, ob_ref, u_ref, up_ref, un_ref, cw_ref, cb_ref,
                  lnw_ref, lnb_ref, wg_ref, wa_ref, wb_ref, wc_ref, wo_ref, o_ref, uext_ref, *,
                  nps, ncps, n_lat_blocks):
    bt = x_ref.shape[0]
    r = pl.program_id(0)
    is_lat = r < n_lat_blocks
    pos = jnp.where(is_lat, r % nps, (r - n_lat_blocks) % ncps)
    last_pos = jnp.where(is_lat, nps - 1, ncps - 1)
    keep_prev = (pos != 0).astype(F32)
    keep_next = (pos != last_pos).astype(F32)

    x = x_ref[...]
    mod = mod_ref[0]
    h = _modulated(x, mod, 0, 1).astype(BF16)

    uext_ref[0:HALO, :] = up_ref[...] * keep_prev
    uext_ref[HALO:HALO + bt, :] = u_ref[...]
    uext_ref[HALO + bt:2 * HALO + bt, :] = un_ref[...] * keep_next
    cw = cw_ref[...]
    y = jnp.zeros((bt, C_WIDTH), F32) + cb_ref[...]
    base = HALO - C_KW // 2
    for k in range(C_KW):
        y = y + cw[k:k + 1, :] * uext_ref[base + k:base + k + bt, :]
    mu = jnp.mean(y, axis=-1, keepdims=True)
    yc = y - mu
    var = jnp.mean(yc * yc, axis=-1, keepdims=True)
    yn = yc * lax.rsqrt(var + EPS) * lnw_ref[...] + lnb_ref[...]
    oc = (yn * jax.nn.sigmoid(yn)).astype(BF16)

    def gate(i):
        logits = jnp.dot(h, wg_ref[:, i * D_MODEL:(i + 1) * D_MODEL], preferred_element_type=F32)
        return jax.nn.sigmoid(logits)

    merged = gate(0) * jnp.dot(oa_ref[...], wa_ref[...], preferred_element_type=F32)
    merged = merged + gate(1) * jnp.dot(ob_ref[...], wb_ref[...], preferred_element_type=F32)
    merged = merged + gate(2) * jnp.dot(oc, wc_ref[...], preferred_element_type=F32)
    out = jnp.dot(merged.astype(BF16), wo_ref[...], preferred_element_type=F32)
    o_ref[...] = x + mod[2:3] * out


def _merge(xy, mod_l, oa, ob, u, cw, cb, lnw, lnb, wg, wa, wb, wc, wo, n_blocks, dims):
    bt = TOK_BLOCK
    t_rows, nps, nbatch = dims["T"], dims["S"] // bt, dims["B"]
    hpb = bt // HALO
    n_halo = t_rows // HALO
    row = lambda r: (r, 0)
    full = lambda r: (0, 0)
    return pl.pallas_call(
        functools.partial(_merge_kernel, nps=nps, ncps=dims["CTX"] // bt,
                          n_lat_blocks=nbatch * nps),
        grid=(n_blocks,),
        in_specs=[pl.BlockSpec((bt, D_MODEL), row),
                  pl.BlockSpec((1, 6, D_MODEL), lambda r: (jnp.minimum(r // nps, nbatch), 0, 0)),
                  pl.BlockSpec((bt, 4 * V7X_LANES), row),
                  pl.BlockSpec((bt, 4 * V7X_LANES), row),
                  pl.BlockSpec((bt, C_WIDTH), row),
                  pl.BlockSpec((HALO, C_WIDTH), lambda r: (jnp.maximum(r * hpb - 1, 0), 0)),
                  pl.BlockSpec((HALO, C_WIDTH), lambda r: (jnp.minimum((r + 1) * hpb, n_halo - 1), 0)),
                  pl.BlockSpec((C_KW + 1, C_WIDTH), full),
                  pl.BlockSpec((1, C_WIDTH), full),
                  pl.BlockSpec((1, C_WIDTH), full),
                  pl.BlockSpec((1, C_WIDTH), full),
                  pl.BlockSpec((D_MODEL, 3 * D_MODEL), full),
                  pl.BlockSpec((C_WIDTH, D_MODEL), full),
                  pl.BlockSpec((C_WIDTH, D_MODEL), full),
                  pl.BlockSpec((C_WIDTH, D_MODEL), full),
                  pl.BlockSpec((D_MODEL, D_MODEL), full)],
        out_specs=pl.BlockSpec((bt, D_MODEL), row),
        out_shape=jax.ShapeDtypeStruct((n_blocks * bt, D_MODEL), F32),
        scratch_shapes=[pltpu.VMEM((bt + 2 * HALO, C_WIDTH), F32)],
        compiler_params=_cparams(("arbitrary",)),
    )(xy, mod_l, oa, ob, u, u, u, cw, cb, lnw, lnb, wg, wa, wb, wc, wo)


def _topk_desc(s, k):
    row = lax.broadcasted_iota(jnp.int32, (k, s.shape[1]), 0)

    def body(i, carry):
        work, vals = carry
        m = jnp.max(work, axis=0, keepdims=True)
        vals = jnp.where(row == i, m, vals)
        work = jnp.where(work == m, -jnp.inf, work)
        return work, vals

    _, vals = lax.fori_loop(0, k, body, (s, jnp.zeros((k, s.shape[1]), F32)))
    return vals


def _peer_kernel(x_ref, mod_ref, wq_ref, k1_ref, k2_ref, u_ref, vt_ref, fnw_ref, o_ref,
                 t_scr, s1_scr, e1_scr, s2_scr, e2_scr, tau_scr, a_scr, m_scr, acc_scr, *,
                 n_chunks, final_norm):
    bt = x_ref.shape[0]
    c = pl.program_id(1)
    ipc = u_ref.shape[0] // P_NKEYS
    n_lt = bt // V7X_LANES

    @pl.when(c == 0)
    def _select():
        hb = _modulated(x_ref[...], mod_ref[0], 3, 4).astype(BF16)
        t_scr[...] = hb
        q = jnp.dot(hb, wq_ref[...], preferred_element_type=F32).astype(BF16)
        s1p = lax.dot_general(k1_ref[...], q, NT_DIMS, preferred_element_type=F32)
        for lt in range(n_lt):
            s1_scr[lt] = s1p[:, lt * V7X_LANES:(lt + 1) * V7X_LANES]
        head_row = lax.broadcasted_iota(jnp.int32, (P_HEADS, bt), 0)
        thr1 = jnp.zeros((P_HEADS, bt), F32)
        max1 = jnp.zeros((P_HEADS, bt), F32)
        for hh in range(P_HEADS):
            qh = q[:, hh * V7X_LANES:(hh + 1) * V7X_LANES]
            s1 = jnp.concatenate([s1_scr[lt, pl.ds(hh, P_NKEYS, stride=P_HEADS), :]
                                  for lt in range(n_lt)], axis=1)
            s2 = lax.dot_general(k2_ref[hh], qh, NT_DIMS, preferred_element_type=F32)
            sv1 = _topk_desc(s1, P_TOPK)
            sv2 = _topk_desc(s2, P_TOPK)
            cand = jnp.concatenate([sv1[k:k + 1] + sv2 for k in range(P_TOPK)], axis=0)
            top = _topk_desc(cand, P_TOPK)
            z = jnp.sum(jnp.exp(top - top[0:1]), axis=0, keepdims=True)
            s2_scr[hh] = jnp.where(s2 >= sv2[P_TOPK - 1:P_TOPK], s2, NEG_BIG)
            e2_scr[hh] = jnp.exp(s2 - sv2[0:1]) / z
            tau_scr[hh] = jnp.broadcast_to(top[P_TOPK - 1:P_TOPK], (8, bt))
            thr1 = jnp.where(head_row == hh, sv1[P_TOPK - 1:P_TOPK], thr1)
            max1 = jnp.where(head_row == hh, sv1[0:1], max1)
        for lt in range(n_lt):
            ls = slice(lt * V7X_LANES, (lt + 1) * V7X_LANES)
            s1t = s1_scr[lt].reshape(P_NKEYS, P_HEADS, V7X_LANES)
            s1_scr[lt] = jnp.where(s1t >= thr1[None, :, ls], s1t, NEG_BIG).reshape(
                P_NKEYS * P_HEADS, V7X_LANES)
            e1_scr[lt] = jnp.exp(s1t - max1[None, :, ls]).reshape(P_NKEYS * P_HEADS, V7X_LANES)
        acc_scr[...] = jnp.zeros_like(acc_scr)

    a_scr[...] = lax.dot_general(u_ref[...], t_scr[...], NT_DIMS, preferred_element_type=F32)

    def i_body(ii, _):
        i = c * ipc + ii
        roff = pl.multiple_of(ii * P_NKEYS, P_NKEYS)
        hrow = pl.ds(pl.multiple_of(i * P_HEADS, P_HEADS), P_HEADS)
        for lt in range(n_lt):
            ls = slice(lt * V7X_LANES, (lt + 1) * V7X_LANES)
            w = [jnp.zeros((ROW_GROUP, V7X_LANES), F32) for _ in range(P_NKEYS // ROW_GROUP)]
            s1t = s1_scr[lt, hrow, :]
            e1t = e1_scr[lt, hrow, :]
            for hh in range(P_HEADS):
                s1b = s1t[hh:hh + 1]
                e1b = e1t[hh:hh + 1]
                tau = tau_scr[hh, 0:1, ls]
                for jg in range(P_NKEYS // ROW_GROUP):
                    js = slice(jg * ROW_GROUP, (jg + 1) * ROW_GROUP)
                    cval = s1b + s2_scr[hh, js, ls]
                    w[jg] = w[jg] + jnp.where(cval >= tau, e1b * e2_scr[hh, js, ls], 0.0)
            for jg in range(P_NKEYS // ROW_GROUP):
                rows = pl.ds(pl.multiple_of(roff + jg * ROW_GROUP, ROW_GROUP), ROW_GROUP)
                a = a_scr[rows, ls]
                g = 0.5 * a * (1.0 + lax.erf(a * (2.0 ** -0.5)))
                m_scr[rows, ls] = (g * w[jg]).astype(BF16)
        return 0

    lax.fori_loop(0, ipc, i_body, 0)
    acc_scr[...] += jnp.dot(vt_ref[...], m_scr[...], preferred_element_type=F32)

    @pl.when(c == n_chunks - 1)
    def _finish():
        x = x_ref[...]
        y = x + mod_ref[0][5:6] * acc_scr[...].T
        if final_norm:
            ms = jnp.mean(y * y, axis=-1, keepdims=True)
            y = y * lax.rsqrt(ms + EPS) * fnw_ref[...]
        o_ref[...] = y


def _peer(x1, mod_l, wq, k1p, k2p, u_tab, vt_tab, fnw, n_blocks, final_norm, dims):
    bt, ec = PEER_BLOCK, PEER_ECHUNK
    nps, nbatch = dims["S"] // bt, dims["B"]
    n_exp = u_tab.shape[0]
    n_chunks = n_exp // ec
    return pl.pallas_call(
        functools.partial(_peer_kernel, n_chunks=n_chunks, final_norm=final_norm),
        grid=(n_blocks, n_chunks),
        in_specs=[pl.BlockSpec((bt, D_MODEL), lambda r, c: (r, 0)),
                  pl.BlockSpec((1, 6, D_MODEL), lambda r, c: (jnp.minimum(r // nps, nbatch), 0, 0)),
                  pl.BlockSpec((D_MODEL, P_HEADS * V7X_LANES), lambda r, c: (0, 0)),
                  pl.BlockSpec((P_NKEYS * P_HEADS, P_HEADS * V7X_LANES), lambda r, c: (0, 0)),
                  pl.BlockSpec((P_HEADS, P_NKEYS, V7X_LANES), lambda r, c: (0, 0, 0)),
                  pl.BlockSpec((ec, D_MODEL), lambda r, c: (c, 0)),
                  pl.BlockSpec((D_MODEL, ec), lambda r, c: (0, c)),
                  pl.BlockSpec((1, D_MODEL), lambda r, c: (0, 0))],
        out_specs=pl.BlockSpec((bt, D_MODEL), lambda r, c: (r, 0)),
        out_shape=jax.ShapeDtypeStruct((n_blocks * bt, D_MODEL), F32),
        scratch_shapes=[pltpu.VMEM((bt, D_MODEL), BF16),
                        pltpu.VMEM((bt // V7X_LANES, P_NKEYS * P_HEADS, V7X_LANES), F32),
                        pltpu.VMEM((bt // V7X_LANES, P_NKEYS * P_HEADS, V7X_LANES), F32),
                        pltpu.VMEM((P_HEADS, P_NKEYS, bt), F32),
                        pltpu.VMEM((P_HEADS, P_NKEYS, bt), F32),
                        pltpu.VMEM((P_HEADS, 8, bt), F32),
                        pltpu.VMEM((ec, bt), F32),
                        pltpu.VMEM((ec, bt), BF16),
                        pltpu.VMEM((D_MODEL, bt), F32)],
        compiler_params=_cparams(("arbitrary", "arbitrary")),
    )(x1, mod_l, wq, k1p, k2p, u_tab, vt_tab, fnw)


def _rope_tables(s_len, nbatch, n_ctx_rows):
    t = jnp.arange(s_len, dtype=jnp.int32)
    row = (t // GRID_W).astype(F32)
    col = (t % GRID_W).astype(F32)
    axis_dim = HEAD_DIM // 2
    inv = ROPE_THETA ** (-jnp.arange(0, axis_dim, 2, dtype=F32) / axis_dim)
    ar = row[:, None] * inv[None, :]
    ac = col[:, None] * inv[None, :]
    ang = jnp.concatenate([ar, ar, ac, ac], axis=-1)
    cos, sin = jnp.cos(ang), jnp.sin(ang)
    quarter = jnp.arange(HEAD_DIM) // (HEAD_DIM // 4)
    first = (quarter % 2 == 0)[None, :]
    sin_a = jnp.where(first, -sin, 0.0)
    sin_b = jnp.where(first, 0.0, sin)

    def expand(tab, ctx_fill):
        lat = jnp.tile(jnp.concatenate([tab, tab], axis=-1), (nbatch, 1))
        return jnp.concatenate([lat, jnp.full((n_ctx_rows, V7X_LANES), ctx_fill, F32)], axis=0)

    return expand(cos, 1.0), expand(sin_a, 0.0), expand(sin_b, 0.0)


def _padded_keys(keys_l):
    z = jnp.zeros_like(keys_l[:, 0])
    k0 = jnp.concatenate([keys_l[:, 0], z], axis=-1).transpose(1, 0, 2)
    eye = jnp.eye(P_HEADS, dtype=keys_l.dtype)
    k1p = (k0[:, :, None, :] * eye[None, :, :, None]).reshape(
        P_NKEYS * P_HEADS, P_HEADS * V7X_LANES)
    k2p = jnp.concatenate([z, keys_l[:, 1]], axis=-1)
    return k1p.astype(BF16), k2p.astype(BF16)


def kernel(x, c, ctx, c_ctx, w_ada, b_ada, w_in, lam_q1, lam_k1, lam_q2, lam_k2, subln_w, q_norm_w, k_norm_w, conv_w, conv_b, conv_ln_w, conv_ln_b, w_branch_a, w_branch_b, w_branch_c, w_out, peer_wq, peer_keys, peer_u, peer_v, final_norm_w):
    nbatch, s_len, d = x.shape
    ctx_len = ctx.shape[1]
    depth = w_ada.shape[0]
    n_lat, n_ctx = nbatch * s_len, nbatch * ctx_len
    dims = {"B": nbatch, "S": s_len, "CTX": ctx_len, "T": n_lat + n_ctx}
    assert d == D_MODEL and nbatch + 1 <= MOD_ROWS
    assert s_len % PEER_BLOCK == 0 and ctx_len % ATT_TQ == 0 and ctx_len % TOK_BLOCK == 0
    assert n_ctx % PEER_BLOCK == 0 and n_lat % ctx_len == 0

    cvec = jnp.zeros((MOD_ROWS, d), F32).at[:nbatch].set(c).at[nbatch].set(c_ctx)
    mod = _ada_rows(cvec, w_ada, b_ada).reshape(depth, MOD_ROWS, 6, d)
    tabs = _rope_tables(s_len, nbatch, n_ctx)
    tile2 = lambda v: jnp.concatenate([v, v], axis=-1)[None, :]

    xy = jnp.concatenate([x.reshape(n_lat, d), ctx.reshape(n_ctx, d)], axis=0)
    for l in range(depth):
        last = l == depth - 1
        lam_init = 0.8 - 0.6 * math.exp(-0.3 * l)
        w_bf = w_in[l].astype(BF16)
        lamv = jnp.stack([lam_q1[l], lam_k1[l], lam_q2[l], lam_k2[l]], axis=0)

        qa, ka, va, qb, kbd, vbd, u = _inproj(
            xy, mod[l], w_bf[:, :OFF_GATE], tabs, tile2(q_norm_w[l]), tile2(k_norm_w[l]), dims)
        oa, ob = _attention(qa, ka, va, qb, kbd, vbd, lamv, subln_w[l][None, :], lam_init,
                            not last, dims)
        n_rows = n_lat if last else n_lat + n_ctx
        cw = jnp.concatenate([conv_w[l], jnp.zeros((1, C_WIDTH), F32)], axis=0)
        x1 = _merge(xy, mod[l], oa, ob, u, cw, conv_b[l][None, :], conv_ln_w[l][None, :],
                    conv_ln_b[l][None, :], w_bf[:, OFF_GATE:], w_branch_a[l].astype(BF16),
                    w_branch_b[l].astype(BF16), w_branch_c[l].astype(BF16),
                    w_out[l].astype(BF16), n_rows // TOK_BLOCK, dims)
        k1p, k2p = _padded_keys(peer_keys[l])
        xy = _peer(x1, mod[l], peer_wq[l].astype(BF16), k1p, k2p,
                   peer_u[l].astype(BF16), peer_v[l].T.astype(BF16), final_norm_w[None, :],
                   n_rows // PEER_BLOCK, last, dims)
    return xy.reshape(nbatch, s_len, d)
```

```python
import functools
import math

import jax
import jax.numpy as jnp
from jax import lax
from jax.experimental import pallas as pl
from jax.experimental.pallas import tpu as pltpu

F32 = jnp.float32
BF16 = jnp.bfloat16

D_MODEL = 1024
DEPTH = 2
GRID_W = 64
HEAD_DIM = 64
ROPE_THETA = 10000.0
EPS = 1e-6
A_HEADS = 4
B_KV = 2
C_WIDTH = 512
C_KW = 31
P_HEADS = 8
P_NKEYS = 128
P_TOPK = 16

COL_KA, COL_VA, COL_KB, COL_VB, COL_QA, COL_QB, COL_GLU = 512, 512, 128, 128, 512, 512, 1024
OFF_KA = 0
OFF_VA = OFF_KA + COL_KA
OFF_KB = OFF_VA + COL_VA
OFF_QA = OFF_KB + COL_KB + COL_VB
OFF_QB = OFF_QA + COL_QA
OFF_GLU = OFF_QB + COL_QB
OFF_GATE = OFF_GLU + COL_GLU

V7X_LANES = 128
V7X_VMEM_LIMIT = 56 * 1024 * 1024
HALO = 16

TOK_BLOCK = 256
ATT_TQ = 256
ATT_TK = 512
ATT_UNROLL = 8
PEER_BLOCK = 512
PEER_ECHUNK = 1024
MOD_ROWS = 8
ROW_GROUP = 16

Q_SCALE = (HEAD_DIM ** -0.5) * math.log2(math.e)
NEG_BIG = -1e30
NT_DIMS = (((1,), (1,)), ((), ()))


def _cparams(sem):
    return pltpu.CompilerParams(dimension_semantics=sem, vmem_limit_bytes=V7X_VMEM_LIMIT)


def _modulated(x, mod, shift_row, scale_row):
    ms = jnp.mean(x * x, axis=-1, keepdims=True)
    xn = x * lax.rsqrt(ms + EPS)
    return xn * (1.0 + mod[scale_row:scale_row + 1]) + mod[shift_row:shift_row + 1]


def _ada_kernel(c_ref, w_ref, b_ref, o_ref):
    c = c_ref[...]
    sc = c * jax.nn.sigmoid(c)
    o_ref[0] = jnp.dot(sc, w_ref[0], preferred_element_type=F32,
                       precision=lax.Precision.HIGHEST) + b_ref[0]


def _ada_rows(cvec, w_ada, b_ada):
    depth, d, n = w_ada.shape
    tn = 1536
    return pl.pallas_call(
        _ada_kernel,
        grid=(depth, n // tn),
        in_specs=[pl.BlockSpec((MOD_ROWS, d), lambda l, j: (0, 0)),
                  pl.BlockSpec((1, d, tn), lambda l, j: (l, 0, j)),
                  pl.BlockSpec((1, 1, tn), lambda l, j: (l, 0, j))],
        out_specs=pl.BlockSpec((1, MOD_ROWS, tn), lambda l, j: (l, 0, j)),
        out_shape=jax.ShapeDtypeStruct((depth, MOD_ROWS, n), F32),
        compiler_params=_cparams(("arbitrary", "arbitrary")),
        name="ada_rows",
    )(cvec, w_ada, b_ada.reshape(depth, 1, n))


def _inproj_kernel(x_ref, mod_ref, w_ref, cos_ref, sa_ref, sb_ref, qnw_ref, knw_ref,
                   qa_ref, ka_ref, va_ref, qb_ref, kbd_ref, vbd_ref, u_ref):
    bt = x_ref.shape[0]
    h = _modulated(x_ref[...], mod_ref[0], 0, 1).astype(BF16)
    cos, sa, sb = cos_ref[...], sa_ref[...], sb_ref[...]
    lo = lax.broadcasted_iota(jnp.int32, (bt, V7X_LANES), 1) < HEAD_DIM

    def proj(c0, width):
        return jnp.dot(h, w_ref[:, c0:c0 + width], preferred_element_type=F32)

    def rope(x):
        return x * cos + pltpu.roll(x, V7X_LANES - 16, 1) * sa + pltpu.roll(x, 16, 1) * sb

    def headnorm(x, w):
        x2 = x * x
        s_lo = jnp.sum(jnp.where(lo, x2, 0.0), axis=-1, keepdims=True)
        s_hi = jnp.sum(jnp.where(lo, 0.0, x2), axis=-1, keepdims=True)
        r = jnp.where(lo, lax.rsqrt(s_lo / HEAD_DIM + EPS), lax.rsqrt(s_hi / HEAD_DIM + EPS))
        return x * r * w

    def tile(p, c):
        return p[:, c * V7X_LANES:(c + 1) * V7X_LANES]

    p = proj(OFF_KA, COL_KA)
    for c in range(COL_KA // V7X_LANES):
        ka_ref[:, c * V7X_LANES:(c + 1) * V7X_LANES] = rope(tile(p, c)).astype(BF16)

    va_ref[...] = proj(OFF_VA, COL_VA).astype(BF16)

    p = proj(OFF_KB, COL_KB + COL_VB)
    kb = rope(headnorm(tile(p, 0), knw_ref[...]))
    kb_sw = pltpu.roll(kb, HEAD_DIM, 1)
    kbd_ref[0] = jnp.where(lo, kb, kb_sw).astype(BF16)
    kbd_ref[1] = jnp.where(lo, kb_sw, kb).astype(BF16)
    vb = tile(p, 1)
    vb_sw = pltpu.roll(vb, HEAD_DIM, 1)
    vbd_ref[0] = jnp.where(lo, vb, vb_sw).astype(BF16)
    vbd_ref[1] = jnp.where(lo, vb_sw, vb).astype(BF16)

    p = proj(OFF_QA, COL_QA)
    for c in range(COL_QA // V7X_LANES):
        qa_ref[:, c * V7X_LANES:(c + 1) * V7X_LANES] = (rope(tile(p, c)) * Q_SCALE).astype(BF16)

    p = proj(OFF_QB, COL_QB)
    qnw = qnw_ref[...]
    for c in range(COL_QB // V7X_LANES):
        qb_ref[:, c * V7X_LANES:(c + 1) * V7X_LANES] = (
            rope(headnorm(tile(p, c), qnw)) * Q_SCALE).astype(BF16)

    p = proj(OFF_GLU, COL_GLU)
    u_ref[...] = p[:, :C_WIDTH] * jax.nn.sigmoid(p[:, C_WIDTH:])


def _inproj(xy, mod_l, w_main, tabs, qnw, knw, dims):
    t_rows, nps, nbatch = dims["T"], dims["S"] // TOK_BLOCK, dims["B"]
    bt = TOK_BLOCK
    row = lambda r: (r, 0)
    full = lambda r: (0, 0)
    out_shapes = (
        jax.ShapeDtypeStruct((t_rows, COL_QA), BF16),
        jax.ShapeDtypeStruct((t_rows, COL_KA), BF16),
        jax.ShapeDtypeStruct((t_rows, COL_VA), BF16),
        jax.ShapeDtypeStruct((t_rows, COL_QB), BF16),
        jax.ShapeDtypeStruct((B_KV, t_rows, V7X_LANES), BF16),
        jax.ShapeDtypeStruct((B_KV, t_rows, V7X_LANES), BF16),
        jax.ShapeDtypeStruct((t_rows, C_WIDTH), F32),
    )
    grp = pl.BlockSpec((B_KV, bt, V7X_LANES), lambda r: (0, r, 0))
    return pl.pallas_call(
        _inproj_kernel,
        grid=(t_rows // bt,),
        in_specs=[pl.BlockSpec((bt, D_MODEL), row),
                  pl.BlockSpec((1, 6, D_MODEL), lambda r: (jnp.minimum(r // nps, nbatch), 0, 0)),
                  pl.BlockSpec((D_MODEL, OFF_GATE), full),
                  pl.BlockSpec((bt, V7X_LANES), row),
                  pl.BlockSpec((bt, V7X_LANES), row),
                  pl.BlockSpec((bt, V7X_LANES), row),
                  pl.BlockSpec((1, V7X_LANES), full),
                  pl.BlockSpec((1, V7X_LANES), full)],
        out_specs=(pl.BlockSpec((bt, COL_QA), row), pl.BlockSpec((bt, COL_KA), row),
                   pl.BlockSpec((bt, COL_VA), row), pl.BlockSpec((bt, COL_QB), row),
                   grp, grp, pl.BlockSpec((bt, C_WIDTH), row)),
        out_shape=out_shapes,
        compiler_params=_cparams(("arbitrary",)),
        name="inproj",
    )(xy, mod_l, w_main, tabs[0], tabs[1], tabs[2], qnw, knw)


def _kv_unroll(n_chunks):
    return ATT_UNROLL if n_chunks % ATT_UNROLL == 0 else 1


def _softmax_step(s, v, m, l, acc):
    mn = jnp.maximum(m, jnp.max(s, axis=-1, keepdims=True))
    p = jnp.exp2(s - mn)
    alpha = jnp.exp2(m - mn)
    l = alpha * l + jnp.sum(p, axis=-1, keepdims=True)
    acc = alpha * acc + jnp.dot(p.astype(BF16), v, preferred_element_type=F32)
    return mn, l, acc


def _flash_rows(q, load_lat, kc, vc, *, nq_lat, n_lat_chunks, tk):
    rows = q.shape[0]

    def step(k, v, carry):
        m, l, a = carry
        s = lax.dot_general(q, k, NT_DIMS, preferred_element_type=F32)
        return _softmax_step(s, v, m, l, a)

    unroll = _kv_unroll(n_lat_chunks)

    def lat_body(j, carry):
        for r in range(unroll):
            off = pl.multiple_of((j * unroll + r) * tk, tk)
            carry = step(*load_lat(off), carry)
        return carry

    carry = (jnp.full((rows, 1), NEG_BIG, F32), jnp.zeros((rows, 1), F32),
             jnp.zeros((rows, V7X_LANES), F32))
    n_lat = jnp.where(pl.program_id(2) < nq_lat, n_lat_chunks // unroll, 0)
    carry = lax.fori_loop(0, n_lat, lat_body, carry)
    _, l, a = step(kc, vc, carry)
    return a / l


def _attn_a_kernel(q_ref, kl_ref, vl_ref, kc_ref, vc_ref, lamv_ref, sw_ref, o_ref, *,
                   nq_lat, n_lat_chunks, tk, lam_init):
    tq = q_ref.shape[0]
    q = q_ref[...]
    lo = lax.broadcasted_iota(jnp.int32, (tq, V7X_LANES), 1) < HEAD_DIM
    zero = jnp.zeros_like(q)
    q12 = jnp.concatenate([jnp.where(lo, q, zero), jnp.where(lo, zero, q)], axis=0)
    o12 = _flash_rows(q12, lambda off: (kl_ref[pl.ds(off, tk), :], vl_ref[pl.ds(off, tk), :]),
                      kc_ref[...], vc_ref[...], nq_lat=nq_lat, n_lat_chunks=n_lat_chunks, tk=tk)

    lv = lamv_ref[...]
    lam = (jnp.exp(jnp.sum(lv[0:1] * lv[1:2], axis=-1, keepdims=True))
           - jnp.exp(jnp.sum(lv[2:3] * lv[3:4], axis=-1, keepdims=True)) + lam_init)
    o = o12[:tq] - lam * o12[tq:]
    ms = jnp.mean(o * o, axis=-1, keepdims=True)
    o_ref[...] = (o * lax.rsqrt(ms + EPS) * sw_ref[...] * (1.0 - lam_init)).astype(BF16)


def _attn_b_kernel(q_ref, kl_ref, vl_ref, kc_ref, vc_ref, o_ref, *, nq_lat, n_lat_chunks, tk):
    tq = q_ref.shape[0]
    lo = lax.broadcasted_iota(jnp.int32, (tq, V7X_LANES), 1) < HEAD_DIM
    parts = []
    for c in range(2):
        qc = q_ref[:, c * V7X_LANES:(c + 1) * V7X_LANES]
        zero = jnp.zeros_like(qc)
        parts += [jnp.where(lo, qc, zero), jnp.where(lo, zero, qc)]
    q4 = jnp.concatenate(parts, axis=0)
    o = _flash_rows(q4, lambda off: (kl_ref[0, pl.ds(off, tk), :], vl_ref[0, pl.ds(off, tk), :]),
                    kc_ref[0], vc_ref[0], nq_lat=nq_lat, n_lat_chunks=n_lat_chunks, tk=tk)
    for c in range(2):
        o_ref[:, c * V7X_LANES:(c + 1) * V7X_LANES] = jnp.where(
            lo, o[(2 * c) * tq:(2 * c + 1) * tq], o[(2 * c + 1) * tq:(2 * c + 2) * tq]).astype(BF16)


def _q_block_index(b, qi, nq_lat, nq_ctx, nbatch):
    return jnp.where(qi < nq_lat, b * nq_lat + qi, nbatch * nq_lat + b * nq_ctx + (qi - nq_lat))


def _attention(qa, ka, va, qb, kbd, vbd, lamv, subln_w, lam_init, with_ctx_queries, dims):
    s_len, ctx_len, nbatch, t_rows = dims["S"], dims["CTX"], dims["B"], dims["T"]
    tq, tk = ATT_TQ, min(ATT_TK, s_len)
    nq_lat, nq_ctx = s_len // tq, ctx_len // tq
    nq = nq_lat + (nq_ctx if with_ctx_queries else 0)
    ctx_blk0 = (nbatch * s_len) // ctx_len
    qidx = functools.partial(_q_block_index, nq_lat=nq_lat, nq_ctx=nq_ctx, nbatch=nbatch)
    sem = ("arbitrary", "arbitrary", "arbitrary")

    oa = pl.pallas_call(
        functools.partial(_attn_a_kernel, nq_lat=nq_lat, n_lat_chunks=s_len // tk, tk=tk,
                          lam_init=lam_init),
        grid=(nbatch, A_HEADS, nq),
        in_specs=[pl.BlockSpec((tq, V7X_LANES), lambda b, h, qi: (qidx(b, qi), h)),
                  pl.BlockSpec((s_len, V7X_LANES), lambda b, h, qi: (b, h)),
                  pl.BlockSpec((s_len, V7X_LANES), lambda b, h, qi: (b, h)),
                  pl.BlockSpec((ctx_len, V7X_LANES), lambda b, h, qi: (ctx_blk0 + b, h)),
                  pl.BlockSpec((ctx_len, V7X_LANES), lambda b, h, qi: (ctx_blk0 + b, h)),
                  pl.BlockSpec((4, HEAD_DIM), lambda b, h, qi: (0, 0)),
                  pl.BlockSpec((1, V7X_LANES), lambda b, h, qi: (0, 0))],
        out_specs=pl.BlockSpec((tq, V7X_LANES), lambda b, h, qi: (qidx(b, qi), h)),
        out_shape=jax.ShapeDtypeStruct((t_rows, A_HEADS * V7X_LANES), BF16),
        compiler_params=_cparams(sem),
        name="attn_a",
    )(qa, ka, va, ka, va, lamv, subln_w)

    gw = 2 * V7X_LANES
    ob = pl.pallas_call(
        functools.partial(_attn_b_kernel, nq_lat=nq_lat, n_lat_chunks=s_len // tk, tk=tk),
        grid=(nbatch, B_KV, nq),
        in_specs=[pl.BlockSpec((tq, gw), lambda b, g, qi: (qidx(b, qi), g)),
                  pl.BlockSpec((1, s_len, V7X_LANES), lambda b, g, qi: (g, b, 0)),
                  pl.BlockSpec((1, s_len, V7X_LANES), lambda b, g, qi: (g, b, 0)),
                  pl.BlockSpec((1, ctx_len, V7X_LANES), lambda b, g, qi: (g, ctx_blk0 + b, 0)),
                  pl.BlockSpec((1, ctx_len, V7X_LANES), lambda b, g, qi: (g, ctx_blk0 + b, 0))],
        out_specs=pl.BlockSpec((tq, gw), lambda b, g, qi: (qidx(b, qi), g)),
        out_shape=jax.ShapeDtypeStruct((t_rows, B_KV * gw), BF16),
        compiler_params=_cparams(sem),
        name="attn_b",
    )(qb, kbd, vbd, kbd, vbd)
    return oa, ob


def _merge_kernel(x_ref, mod_ref, oa_ref, ob_ref, u_ref, up_ref, un_ref, cw_ref, cb_ref,
                  lnw_ref, lnb_ref, wg_ref, wa_ref, wb_ref, wc_ref, wo_ref, o_ref, uext_ref, *,
                  nps, ncps, n_lat_blocks):
    bt = x_ref.shape[0]
    r = pl.program_id(0)
    is_lat = r < n_lat_blocks
    pos = jnp.where(is_lat, r % nps, (r - n_lat_blocks) % ncps)
    last_pos = jnp.where(is_lat, nps - 1, ncps - 1)
    keep_prev = (pos != 0).astype(F32)
    keep_next = (pos != last_pos).astype(F32)

    x = x_ref[...]
    mod = mod_ref[0]
    h = _modulated(x, mod, 0, 1).astype(BF16)

    uext_ref[0:HALO, :] = up_ref[...] * keep_prev
    uext_ref[HALO:HALO + bt, :] = u_ref[...]
    uext_ref[HALO + bt:2 * HALO + bt, :] = un_ref[...] * keep_next
    cw = cw_ref[...]
    y = jnp.zeros((bt, C_WIDTH), F32) + cb_ref[...]
    base = HALO - C_KW // 2
    for k in range(C_KW):
        y = y + cw[k:k + 1, :] * uext_ref[base + k:base + k + bt, :]
    mu = jnp.mean(y, axis=-1, keepdims=True)
    yc = y - mu
    var = jnp.mean(yc * yc, axis=-1, keepdims=True)
    yn = yc * lax.rsqrt(var + EPS) * lnw_ref[...] + lnb_ref[...]
    oc = (yn * jax.nn.sigmoid(yn)).astype(BF16)

    def gate(i):
        logits = jnp.dot(h, wg_ref[:, i * D_MODEL:(i + 1) * D_MODEL], preferred_element_type=F32)
        return jax.nn.sigmoid(logits)

    merged = gate(0) * jnp.dot(oa_ref[...], wa_ref[...], preferred_element_type=F32)
    merged = merged + gate(1) * jnp.dot(ob_ref[...], wb_ref[...], preferred_element_type=F32)
    merged = merged + gate(2) * jnp.dot(oc, wc_ref[...], preferred_element_type=F32)
    out = jnp.dot(merged.astype(BF16), wo_ref[...], preferred_element_type=F32)
    o_ref[...] = x + mod[2:3] * out


def _merge(xy, mod_l, oa, ob, u, cw, cb, lnw, lnb, wg, wa, wb, wc, wo, n_blocks, dims):
    bt = TOK_BLOCK
    t_rows, nps, nbatch = dims["T"], dims["S"] // bt, dims["B"]
    hpb = bt // HALO
    n_halo = t_rows // HALO
    row = lambda r: (r, 0)
    full = lambda r: (0, 0)
    return pl.pallas_call(
        functools.partial(_merge_kernel, nps=nps, ncps=dims["CTX"] // bt,
                          n_lat_blocks=nbatch * nps),
        grid=(n_blocks,),
        in_specs=[pl.BlockSpec((bt, D_MODEL), row),
                  pl.BlockSpec((1, 6, D_MODEL), lambda r: (jnp.minimum(r // nps, nbatch), 0, 0)),
                  pl.BlockSpec((bt, 4 * V7X_LANES), row),
                  pl.BlockSpec((bt, 4 * V7X_LANES), row),
                  pl.BlockSpec((bt, C_WIDTH), row),
                  pl.BlockSpec((HALO, C_WIDTH), lambda r: (jnp.maximum(r * hpb - 1, 0), 0)),
                  pl.BlockSpec((HALO, C_WIDTH), lambda r: (jnp.minimum((r + 1) * hpb, n_halo - 1), 0)),
                  pl.BlockSpec((C_KW + 1, C_WIDTH), full),
                  pl.BlockSpec((1, C_WIDTH), full),
                  pl.BlockSpec((1, C_WIDTH), full),
                  pl.BlockSpec((1, C_WIDTH), full),
                  pl.BlockSpec((D_MODEL, 3 * D_MODEL), full),
                  pl.BlockSpec((C_WIDTH, D_MODEL), full),
                  pl.BlockSpec((C_WIDTH, D_MODEL), full),
                  pl.BlockSpec((C_WIDTH, D_MODEL), full),
                  pl.BlockSpec((D_MODEL, D_MODEL), full)],
        out_specs=pl.BlockSpec((bt, D_MODEL), row),
        out_shape=jax.ShapeDtypeStruct((n_blocks * bt, D_MODEL), F32),
        scratch_shapes=[pltpu.VMEM((bt + 2 * HALO, C_WIDTH), F32)],
        compiler_params=_cparams(("arbitrary",)),
        name="merge",
    )(xy, mod_l, oa, ob, u, u, u, cw, cb, lnw, lnb, wg, wa, wb, wc, wo)


def _topk_desc(s, k):
    row = lax.broadcasted_iota(jnp.int32, (k, s.shape[1]), 0)

    def body(i, carry):
        work, vals = carry
        m = jnp.max(work, axis=0, keepdims=True)
        vals = jnp.where(row == i, m, vals)
        work = jnp.where(work == m, -jnp.inf, work)
        return work, vals

    _, vals = lax.fori_loop(0, k, body, (s, jnp.zeros((k, s.shape[1]), F32)))
    return vals


def _peer_kernel(x_ref, mod_ref, wq_ref, k1_ref, k2_ref, u_ref, vt_ref, fnw_ref, o_ref,
                 t_scr, s1_scr, e1_scr, s2_scr, e2_scr, tau_scr, a_scr, m_scr, acc_scr, *,
                 n_chunks, final_norm):
    bt = x_ref.shape[0]
    c = pl.program_id(1)
    ipc = u_ref.shape[0] // P_NKEYS
    n_lt = bt // V7X_LANES

    @pl.when(c == 0)
    def _select():
        hb = _modulated(x_ref[...], mod_ref[0], 3, 4).astype(BF16)
        t_scr[...] = hb
        q = jnp.dot(hb, wq_ref[...], preferred_element_type=F32).astype(BF16)
        s1p = lax.dot_general(k1_ref[...], q, NT_DIMS, preferred_element_type=F32)
        for lt in range(n_lt):
            s1_scr[lt] = s1p[:, lt * V7X_LANES:(lt + 1) * V7X_LANES]
        head_row = lax.broadcasted_iota(jnp.int32, (P_HEADS, bt), 0)
        thr1 = jnp.zeros((P_HEADS, bt), F32)
        max1 = jnp.zeros((P_HEADS, bt), F32)
        for hh in range(P_HEADS):
            qh = q[:, hh * V7X_LANES:(hh + 1) * V7X_LANES]
            s1 = jnp.concatenate([s1_scr[lt, pl.ds(hh, P_NKEYS, stride=P_HEADS), :]
                                  for lt in range(n_lt)], axis=1)
            s2 = lax.dot_general(k2_ref[hh], qh, NT_DIMS, preferred_element_type=F32)
            sv1 = _topk_desc(s1, P_TOPK)
            sv2 = _topk_desc(s2, P_TOPK)
            cand = jnp.concatenate([sv1[k:k + 1] + sv2 for k in range(P_TOPK)], axis=0)
            top = _topk_desc(cand, P_TOPK)
            z = jnp.sum(jnp.exp(top - top[0:1]), axis=0, keepdims=True)
            s2_scr[hh] = jnp.where(s2 >= sv2[P_TOPK - 1:P_TOPK], s2, NEG_BIG)
            e2_scr[hh] = jnp.exp(s2 - sv2[0:1]) / z
            tau_scr[hh] = jnp.broadcast_to(top[P_TOPK - 1:P_TOPK], (8, bt))
            thr1 = jnp.where(head_row == hh, sv1[P_TOPK - 1:P_TOPK], thr1)
            max1 = jnp.where(head_row == hh, sv1[0:1], max1)
        for lt in range(n_lt):
            ls = slice(lt * V7X_LANES, (lt + 1) * V7X_LANES)
            s1t = s1_scr[lt].reshape(P_NKEYS, P_HEADS, V7X_LANES)
            s1_scr[lt] = jnp.where(s1t >= thr1[None, :, ls], s1t, NEG_BIG).reshape(
                P_NKEYS * P_HEADS, V7X_LANES)
            e1_scr[lt] = jnp.exp(s1t - max1[None, :, ls]).reshape(P_NKEYS * P_HEADS, V7X_LANES)
        acc_scr[...] = jnp.zeros_like(acc_scr)

    a_scr[...] = lax.dot_general(u_ref[...], t_scr[...], NT_DIMS, preferred_element_type=F32)

    def i_body(ii, _):
        i = c * ipc + ii
        roff = pl.multiple_of(ii * P_NKEYS, P_NKEYS)
        hrow = pl.ds(pl.multiple_of(i * P_HEADS, P_HEADS), P_HEADS)
        for lt in range(n_lt):
            ls = slice(lt * V7X_LANES, (lt + 1) * V7X_LANES)
            w = [jnp.zeros((ROW_GROUP, V7X_LANES), F32) for _ in range(P_NKEYS // ROW_GROUP)]
            s1t = s1_scr[lt, hrow, :]
            e1t = e1_scr[lt, hrow, :]
            for hh in range(P_HEADS):
                s1b = s1t[hh:hh + 1]
                e1b = e1t[hh:hh + 1]
                tau = tau_scr[hh, 0:1, ls]
                for jg in range(P_NKEYS // ROW_GROUP):
                    js = slice(jg * ROW_GROUP, (jg + 1) * ROW_GROUP)
                    cval = s1b + s2_scr[hh, js, ls]
                    w[jg] = w[jg] + jnp.where(cval >= tau, e1b * e2_scr[hh, js, ls], 0.0)
            for jg in range(P_NKEYS // ROW_GROUP):
                rows = pl.ds(pl.multiple_of(roff + jg * ROW_GROUP, ROW_GROUP), ROW_GROUP)
                a = a_scr[rows, ls]
                g = 0.5 * a * (1.0 + lax.erf(a * (2.0 ** -0.5)))
                m_scr[rows, ls] = (g * w[jg]).astype(BF16)
        return 0

    lax.fori_loop(0, ipc, i_body, 0)
    acc_scr[...] += jnp.dot(vt_ref[...], m_scr[...], preferred_element_type=F32)

    @pl.when(c == n_chunks - 1)
    def _finish():
        x = x_ref[...]
        y = x + mod_ref[0][5:6] * acc_scr[...].T
        if final_norm:
            ms = jnp.mean(y * y, axis=-1, keepdims=True)
            y = y * lax.rsqrt(ms + EPS) * fnw_ref[...]
        o_ref[...] = y


def _peer(x1, mod_l, wq, k1p, k2p, u_tab, vt_tab, fnw, n_blocks, final_norm, dims):
    bt, ec = PEER_BLOCK, PEER_ECHUNK
    nps, nbatch = dims["S"] // bt, dims["B"]
    n_exp = u_tab.shape[0]
    n_chunks = n_exp // ec
    return pl.pallas_call(
        functools.partial(_peer_kernel, n_chunks=n_chunks, final_norm=final_norm),
        grid=(n_blocks, n_chunks),
        in_specs=[pl.BlockSpec((bt, D_MODEL), lambda r, c: (r, 0)),
                  pl.BlockSpec((1, 6, D_MODEL), lambda r, c: (jnp.minimum(r // nps, nbatch), 0, 0)),
                  pl.BlockSpec((D_MODEL, P_HEADS * V7X_LANES), lambda r, c: (0, 0)),
                  pl.BlockSpec((P_NKEYS * P_HEADS, P_HEADS * V7X_LANES), lambda r, c: (0, 0)),
                  pl.BlockSpec((P_HEADS, P_NKEYS, V7X_LANES), lambda r, c: (0, 0, 0)),
                  pl.BlockSpec((ec, D_MODEL), lambda r, c: (c, 0)),
                  pl.BlockSpec((D_MODEL, ec), lambda r, c: (0, c)),
                  pl.BlockSpec((1, D_MODEL), lambda r, c: (0, 0))],
        out_specs=pl.BlockSpec((bt, D_MODEL), lambda r, c: (r, 0)),
        out_shape=jax.ShapeDtypeStruct((n_blocks * bt, D_MODEL), F32),
        scratch_shapes=[pltpu.VMEM((bt, D_MODEL), BF16),
                        pltpu.VMEM((bt // V7X_LANES, P_NKEYS * P_HEADS, V7X_LANES), F32),
                        pltpu.VMEM((bt // V7X_LANES, P_NKEYS * P_HEADS, V7X_LANES), F32),
                        pltpu.VMEM((P_HEADS, P_NKEYS, bt), F32),
                        pltpu.VMEM((P_HEADS, P_NKEYS, bt), F32),
                        pltpu.VMEM((P_HEADS, 8, bt), F32),
                        pltpu.VMEM((ec, bt), F32),
                        pltpu.VMEM((ec, bt), BF16),
                        pltpu.VMEM((D_MODEL, bt), F32)],
        compiler_params=_cparams(("arbitrary", "arbitrary")),
        name="peer",
    )(x1, mod_l, wq, k1p, k2p, u_tab, vt_tab, fnw)


def _rope_tables(s_len, nbatch, n_ctx_rows):
    t = jnp.arange(s_len, dtype=jnp.int32)
    row = (t // GRID_W).astype(F32)
    col = (t % GRID_W).astype(F32)
    axis_dim = HEAD_DIM // 2
    inv = ROPE_THETA ** (-jnp.arange(0, axis_dim, 2, dtype=F32) / axis_dim)
    ar = row[:, None] * inv[None, :]
    ac = col[:, None] * inv[None, :]
    ang = jnp.concatenate([ar, ar, ac, ac], axis=-1)
    cos, sin = jnp.cos(ang), jnp.sin(ang)
    quarter = jnp.arange(HEAD_DIM) // (HEAD_DIM // 4)
    first = (quarter % 2 == 0)[None, :]
    sin_a = jnp.where(first, -sin, 0.0)
    sin_b = jnp.where(first, 0.0, sin)

    def expand(tab, ctx_fill):
        lat = jnp.tile(jnp.concatenate([tab, tab], axis=-1), (nbatch, 1))
        return jnp.concatenate([lat, jnp.full((n_ctx_rows, V7X_LANES), ctx_fill, F32)], axis=0)

    return expand(cos, 1.0), expand(sin_a, 0.0), expand(sin_b, 0.0)


def _padded_keys(keys_l):
    z = jnp.zeros_like(keys_l[:, 0])
    k0 = jnp.concatenate([keys_l[:, 0], z], axis=-1).transpose(1, 0, 2)
    eye = jnp.eye(P_HEADS, dtype=keys_l.dtype)
    k1p = (k0[:, :, None, :] * eye[None, :, :, None]).reshape(
        P_NKEYS * P_HEADS, P_HEADS * V7X_LANES)
    k2p = jnp.concatenate([z, keys_l[:, 1]], axis=-1)
    return k1p.astype(BF16), k2p.astype(BF16)


def kernel(x, c, ctx, c_ctx, w_ada, b_ada, w_in, lam_q1, lam_k1, lam_q2, lam_k2, subln_w, q_norm_w, k_norm_w, conv_w, conv_b, conv_ln_w, conv_ln_b, w_branch_a, w_branch_b, w_branch_c, w_out, peer_wq, peer_keys, peer_u, peer_v, final_norm_w):
    nbatch, s_len, d = x.shape
    ctx_len = ctx.shape[1]
    depth = w_ada.shape[0]
    n_lat, n_ctx = nbatch * s_len, nbatch * ctx_len
    dims = {"B": nbatch, "S": s_len, "CTX": ctx_len, "T": n_lat + n_ctx}
    assert d == D_MODEL and nbatch + 1 <= MOD_ROWS
    assert s_len % PEER_BLOCK == 0 and ctx_len % ATT_TQ == 0 and ctx_len % TOK_BLOCK == 0
    assert n_ctx % PEER_BLOCK == 0 and n_lat % ctx_len == 0

    cvec = jnp.zeros((MOD_ROWS, d), F32).at[:nbatch].set(c).at[nbatch].set(c_ctx)
    mod = _ada_rows(cvec, w_ada, b_ada).reshape(depth, MOD_ROWS, 6, d)
    tabs = _rope_tables(s_len, nbatch, n_ctx)
    tile2 = lambda v: jnp.concatenate([v, v], axis=-1)[None, :]

    xy = jnp.concatenate([x.reshape(n_lat, d), ctx.reshape(n_ctx, d)], axis=0)
    for l in range(depth):
        last = l == depth - 1
        lam_init = 0.8 - 0.6 * math.exp(-0.3 * l)
        w_bf = w_in[l].astype(BF16)
        lamv = jnp.stack([lam_q1[l], lam_k1[l], lam_q2[l], lam_k2[l]], axis=0)

        qa, ka, va, qb, kbd, vbd, u = _inproj(
            xy, mod[l], w_bf[:, :OFF_GATE], tabs, tile2(q_norm_w[l]), tile2(k_norm_w[l]), dims)
        oa, ob = _attention(qa, ka, va, qb, kbd, vbd, lamv, subln_w[l][None, :], lam_init,
                            not last, dims)
        n_rows = n_lat if last else n_lat + n_ctx
        cw = jnp.concatenate([conv_w[l], jnp.zeros((1, C_WIDTH), F32)], axis=0)
        x1 = _merge(xy, mod[l], oa, ob, u, cw, conv_b[l][None, :], conv_ln_w[l][None, :],
                    conv_ln_b[l][None, :], w_bf[:, OFF_GATE:], w_branch_a[l].astype(BF16),
                    w_branch_b[l].astype(BF16), w_branch_c[l].astype(BF16),
                    w_out[l].astype(BF16), n_rows // TOK_BLOCK, dims)
        k1p, k2p = _padded_keys(peer_keys[l])
        xy = _peer(x1, mod[l], peer_wq[l].astype(BF16), k1p, k2p,
                   peer_u[l].astype(BF16), peer_v[l].T.astype(BF16), final_norm_w[None, :],
                   n_rows // PEER_BLOCK, last, dims)
    return xy.reshape(nbatch, s_len, d)
```

```python
import functools
import math

import jax
import jax.numpy as jnp
from jax import lax
from jax.experimental import pallas as pl
from jax.experimental.pallas import tpu as pltpu

F32 = jnp.float32
BF16 = jnp.bfloat16

D_MODEL = 1024
DEPTH = 2
GRID_W = 64
HEAD_DIM = 64
ROPE_THETA = 10000.0
EPS = 1e-6
A_HEADS = 4
B_KV = 2
C_WIDTH = 512
C_KW = 31
P_HEADS = 8
P_NKEYS = 128
P_TOPK = 16

COL_KA, COL_VA, COL_KB, COL_VB, COL_QA, COL_QB, COL_GLU = 512, 512, 128, 128, 512, 512, 1024
OFF_KA = 0
OFF_VA = OFF_KA + COL_KA
OFF_KB = OFF_VA + COL_VA
OFF_QA = OFF_KB + COL_KB + COL_VB
OFF_QB = OFF_QA + COL_QA
OFF_GLU = OFF_QB + COL_QB
OFF_GATE = OFF_GLU + COL_GLU

V7X_LANES = 128
V7X_VMEM_LIMIT = 56 * 1024 * 1024
HALO = 16

TOK_BLOCK = 256
ATT_TQ = 256
ATT_TK = 512
ATT_UNROLL = 8
PEER_BLOCK = 512
PEER_ECHUNK = 1024
MOD_ROWS = 8
ROW_GROUP = 16

Q_SCALE = (HEAD_DIM ** -0.5) * math.log2(math.e)
NEG_BIG = -1e30
NT_DIMS = (((1,), (1,)), ((), ()))


def _cparams(sem):
    return pltpu.CompilerParams(dimension_semantics=sem, vmem_limit_bytes=V7X_VMEM_LIMIT)


def _modulated(x, mod, shift_row, scale_row):
    ms = jnp.mean(x * x, axis=-1, keepdims=True)
    xn = x * lax.rsqrt(ms + EPS)
    return xn * (1.0 + mod[scale_row:scale_row + 1]) + mod[shift_row:shift_row + 1]


def _ada_kernel(c_ref, w_ref, b_ref, o_ref):
    c = c_ref[...]
    sc = c * jax.nn.sigmoid(c)
    o_ref[0] = jnp.dot(sc, w_ref[0], preferred_element_type=F32,
                       precision=lax.Precision.HIGHEST) + b_ref[0]


def _ada_rows(cvec, w_ada, b_ada):
    depth, d, n = w_ada.shape
    tn = 1536
    return pl.pallas_call(
        _ada_kernel,
        grid=(depth, n // tn),
        in_specs=[pl.BlockSpec((MOD_ROWS, d), lambda l, j: (0, 0)),
                  pl.BlockSpec((1, d, tn), lambda l, j: (l, 0, j)),
                  pl.BlockSpec((1, 1, tn), lambda l, j: (l, 0, j))],
        out_specs=pl.BlockSpec((1, MOD_ROWS, tn), lambda l, j: (l, 0, j)),
        out_shape=jax.ShapeDtypeStruct((depth, MOD_ROWS, n), F32),
        compiler_params=_cparams(("arbitrary", "arbitrary")),
        name="ada_rows",
    )(cvec, w_ada, b_ada.reshape(depth, 1, n))


def _inproj_kernel(x_ref, mod_ref, w_ref, cos_ref, sa_ref, sb_ref, qnw_ref, knw_ref,
                   qa_ref, ka_ref, va_ref, qb_ref, kbd_ref, vbd_ref, u_ref):
    bt = x_ref.shape[0]
    h = _modulated(x_ref[...], mod_ref[0], 0, 1).astype(BF16)
    cos, sa, sb = cos_ref[...], sa_ref[...], sb_ref[...]
    lo = lax.broadcasted_iota(jnp.int32, (bt, V7X_LANES), 1) < HEAD_DIM

    def proj(c0, width):
        return jnp.dot(h, w_ref[:, c0:c0 + width], preferred_element_type=F32)

    def rope(x):
        return x * cos + pltpu.roll(x, V7X_LANES - 16, 1) * sa + pltpu.roll(x, 16, 1) * sb

    def headnorm(x, w):
        x2 = x * x
        s_lo = jnp.sum(jnp.where(lo, x2, 0.0), axis=-1, keepdims=True)
        s_hi = jnp.sum(jnp.where(lo, 0.0, x2), axis=-1, keepdims=True)
        r = jnp.where(lo, lax.rsqrt(s_lo / HEAD_DIM + EPS), lax.rsqrt(s_hi / HEAD_DIM + EPS))
        return x * r * w

    def tile(p, c):
        return p[:, c * V7X_LANES:(c + 1) * V7X_LANES]

    p = proj(OFF_KA, COL_KA)
    for c in range(COL_KA // V7X_LANES):
        ka_ref[:, c * V7X_LANES:(c + 1) * V7X_LANES] = rope(tile(p, c)).astype(BF16)

    va_ref[...] = proj(OFF_VA, COL_VA).astype(BF16)

    p = proj(OFF_KB, COL_KB + COL_VB)
    kb = rope(headnorm(tile(p, 0), knw_ref[...]))
    kb_sw = pltpu.roll(kb, HEAD_DIM, 1)
    kbd_ref[0] = jnp.where(lo, kb, kb_sw).astype(BF16)
    kbd_ref[1] = jnp.where(lo, kb_sw, kb).astype(BF16)
    vb = tile(p, 1)
    vb_sw = pltpu.roll(vb, HEAD_DIM, 1)
    vbd_ref[0] = jnp.where(lo, vb, vb_sw).astype(BF16)
    vbd_ref[1] = jnp.where(lo, vb_sw, vb).astype(BF16)

    p = proj(OFF_QA, COL_QA)
    for c in range(COL_QA // V7X_LANES):
        qa_ref[:, c * V7X_LANES:(c + 1) * V7X_LANES] = (rope(tile(p, c)) * Q_SCALE).astype(BF16)

    p = proj(OFF_QB, COL_QB)
    qnw = qnw_ref[...]
    for c in range(COL_QB // V7X_LANES):
        qb_ref[:, c * V7X_LANES:(c + 1) * V7X_LANES] = (
            rope(headnorm(tile(p, c), qnw)) * Q_SCALE).astype(BF16)

    p = proj(OFF_GLU, COL_GLU)
    u_ref[...] = p[:, :C_WIDTH] * jax.nn.sigmoid(p[:, C_WIDTH:])


def _inproj(xy, mod_l, w_main, tabs, qnw, knw, dims):
    t_rows, nps, nbatch = dims["T"], dims["S"] // TOK_BLOCK, dims["B"]
    bt = TOK_BLOCK
    row = lambda r: (r, 0)
    full = lambda r: (0, 0)
    out_shapes = (
        jax.ShapeDtypeStruct((t_rows, COL_QA), BF16),
        jax.ShapeDtypeStruct((t_rows, COL_KA), BF16),
        jax.ShapeDtypeStruct((t_rows, COL_VA), BF16),
        jax.ShapeDtypeStruct((t_rows, COL_QB), BF16),
        jax.ShapeDtypeStruct((B_KV, t_rows, V7X_LANES), BF16),
        jax.ShapeDtypeStruct((B_KV, t_rows, V7X_LANES), BF16),
        jax.ShapeDtypeStruct((t_rows, C_WIDTH), F32),
    )
    grp = pl.BlockSpec((B_KV, bt, V7X_LANES), lambda r: (0, r, 0))
    return pl.pallas_call(
        _inproj_kernel,
        grid=(t_rows // bt,),
        in_specs=[pl.BlockSpec((bt, D_MODEL), row),
                  pl.BlockSpec((1, 6, D_MODEL), lambda r: (jnp.minimum(r // nps, nbatch), 0, 0)),
                  pl.BlockSpec((D_MODEL, OFF_GATE), full),
                  pl.BlockSpec((bt, V7X_LANES), row),
                  pl.BlockSpec((bt, V7X_LANES), row),
                  pl.BlockSpec((bt, V7X_LANES), row),
                  pl.BlockSpec((1, V7X_LANES), full),
                  pl.BlockSpec((1, V7X_LANES), full)],
        out_specs=(pl.BlockSpec((bt, COL_QA), row), pl.BlockSpec((bt, COL_KA), row),
                   pl.BlockSpec((bt, COL_VA), row), pl.BlockSpec((bt, COL_QB), row),
                   grp, grp, pl.BlockSpec((bt, C_WIDTH), row)),
        out_shape=out_shapes,
        compiler_params=_cparams(("arbitrary",)),
        name="inproj",
    )(xy, mod_l, w_main, tabs[0], tabs[1], tabs[2], qnw, knw)


def _kv_unroll(n_chunks):
    return ATT_UNROLL if n_chunks % ATT_UNROLL == 0 else 1


def _softmax_step(s, v, m, l, acc):
    mn = jnp.maximum(m, jnp.max(s, axis=-1, keepdims=True))
    p = jnp.exp2(s - mn)
    alpha = jnp.exp2(m - mn)
    l = alpha * l + jnp.sum(p, axis=-1, keepdims=True)
    acc = alpha * acc + jnp.dot(p.astype(BF16), v, preferred_element_type=F32)
    return mn, l, acc


def _flash_rows(q, load_lat, kc, vc, *, nq_lat, n_lat_chunks, tk):
    rows = q.shape[0]

    def step(k, v, carry):
        m, l, a = carry
        s = lax.dot_general(q, k, NT_DIMS, preferred_element_type=F32)
        return _softmax_step(s, v, m, l, a)

    unroll = _kv_unroll(n_lat_chunks)

    def lat_body(j, carry):
        for r in range(unroll):
            off = pl.multiple_of((j * unroll + r) * tk, tk)
            carry = step(*load_lat(off), carry)
        return carry

    carry = (jnp.full((rows, 1), NEG_BIG, F32), jnp.zeros((rows, 1), F32),
             jnp.zeros((rows, V7X_LANES), F32))
    n_lat = jnp.where(pl.program_id(2) < nq_lat, n_lat_chunks // unroll, 0)
    carry = lax.fori_loop(0, n_lat, lat_body, carry)
    _, l, a = step(kc, vc, carry)
    return a / l


def _attn_a_kernel(q_ref, kl_ref, vl_ref, kc_ref, vc_ref, lamv_ref, sw_ref, o_ref, *,
                   nq_lat, n_lat_chunks, tk, lam_init):
    tq = q_ref.shape[0]
    q = q_ref[...]
    lo = lax.broadcasted_iota(jnp.int32, (tq, V7X_LANES), 1) < HEAD_DIM
    zero = jnp.zeros_like(q)
    q12 = jnp.concatenate([jnp.where(lo, q, zero), jnp.where(lo, zero, q)], axis=0)
    o12 = _flash_rows(q12, lambda off: (kl_ref[pl.ds(off, tk), :], vl_ref[pl.ds(off, tk), :]),
                      kc_ref[...], vc_ref[...], nq_lat=nq_lat, n_lat_chunks=n_lat_chunks, tk=tk)

    lv = lamv_ref[...]
    lam = (jnp.exp(jnp.sum(lv[0:1] * lv[1:2], axis=-1, keepdims=True))
           - jnp.exp(jnp.sum(lv[2:3] * lv[3:4], axis=-1, keepdims=True)) + lam_init)
    o = o12[:tq] - lam * o12[tq:]
    ms = jnp.mean(o * o, axis=-1, keepdims=True)
    o_ref[...] = (o * lax.rsqrt(ms + EPS) * sw_ref[...] * (1.0 - lam_init)).astype(BF16)


def _attn_b_kernel(q_ref, kl_ref, vl_ref, kc_ref, vc_ref, o_ref, *, nq_lat, n_lat_chunks, tk):
    tq = q_ref.shape[0]
    lo = lax.broadcasted_iota(jnp.int32, (tq, V7X_LANES), 1) < HEAD_DIM
    parts = []
    for c in range(2):
        qc = q_ref[:, c * V7X_LANES:(c + 1) * V7X_LANES]
        zero = jnp.zeros_like(qc)
        parts += [jnp.where(lo, qc, zero), jnp.where(lo, zero, qc)]
    q4 = jnp.concatenate(parts, axis=0)
    o = _flash_rows(q4, lambda off: (kl_ref[0, pl.ds(off, tk), :], vl_ref[0, pl.ds(off, tk), :]),
                    kc_ref[0], vc_ref[0], nq_lat=nq_lat, n_lat_chunks=n_lat_chunks, tk=tk)
    for c in range(2):
        o_ref[:, c * V7X_LANES:(c + 1) * V7X_LANES] = jnp.where(
            lo, o[(2 * c) * tq:(2 * c + 1) * tq], o[(2 * c + 1) * tq:(2 * c + 2) * tq]).astype(BF16)


def _q_block_index(b, qi, nq_lat, nq_ctx, nbatch):
    return jnp.where(qi < nq_lat, b * nq_lat + qi, nbatch * nq_lat + b * nq_ctx + (qi - nq_lat))


def _attention(qa, ka, va, qb, kbd, vbd, lamv, subln_w, lam_init, with_ctx_queries, dims):
    s_len, ctx_len, nbatch, t_rows = dims["S"], dims["CTX"], dims["B"], dims["T"]
    tq, tk = ATT_TQ, min(ATT_TK, s_len)
    nq_lat, nq_ctx = s_len // tq, ctx_len // tq
    nq = nq_lat + (nq_ctx if with_ctx_queries else 0)
    ctx_blk0 = (nbatch * s_len) // ctx_len
    qidx = functools.partial(_q_block_index, nq_lat=nq_lat, nq_ctx=nq_ctx, nbatch=nbatch)
    sem = ("arbitrary", "arbitrary", "arbitrary")

    oa = pl.pallas_call(
        functools.partial(_attn_a_kernel, nq_lat=nq_lat, n_lat_chunks=s_len // tk, tk=tk,
                          lam_init=lam_init),
        grid=(nbatch, A_HEADS, nq),
        in_specs=[pl.BlockSpec((tq, V7X_LANES), lambda b, h, qi: (qidx(b, qi), h)),
                  pl.BlockSpec((s_len, V7X_LANES), lambda b, h, qi: (b, h)),
                  pl.BlockSpec((s_len, V7X_LANES), lambda b, h, qi: (b, h)),
                  pl.BlockSpec((ctx_len, V7X_LANES), lambda b, h, qi: (ctx_blk0 + b, h)),
                  pl.BlockSpec((ctx_len, V7X_LANES), lambda b, h, qi: (ctx_blk0 + b, h)),
                  pl.BlockSpec((4, HEAD_DIM), lambda b, h, qi: (0, 0)),
                  pl.BlockSpec((1, V7X_LANES), lambda b, h, qi: (0, 0))],
        out_specs=pl.BlockSpec((tq, V7X_LANES), lambda b, h, qi: (qidx(b, qi), h)),
        out_shape=jax.ShapeDtypeStruct((t_rows, A_HEADS * V7X_LANES), BF16),
        compiler_params=_cparams(sem),
        name="attn_a",
    )(qa, ka, va, ka, va, lamv, subln_w)

    gw = 2 * V7X_LANES
    ob = pl.pallas_call(
        functools.partial(_attn_b_kernel, nq_lat=nq_lat, n_lat_chunks=s_len // tk, tk=tk),
        grid=(nbatch, B_KV, nq),
        in_specs=[pl.BlockSpec((tq, gw), lambda b, g, qi: (qidx(b, qi), g)),
                  pl.BlockSpec((1, s_len, V7X_LANES), lambda b, g, qi: (g, b, 0)),
                  pl.BlockSpec((1, s_len, V7X_LANES), lambda b, g, qi: (g, b, 0)),
                  pl.BlockSpec((1, ctx_len, V7X_LANES), lambda b, g, qi: (g, ctx_blk0 + b, 0)),
                  pl.BlockSpec((1, ctx_len, V7X_LANES), lambda b, g, qi: (g, ctx_blk0 + b, 0))],
        out_specs=pl.BlockSpec((tq, gw), lambda b, g, qi: (qidx(b, qi), g)),
        out_shape=jax.ShapeDtypeStruct((t_rows, B_KV * gw), BF16),
        compiler_params=_cparams(sem),
        name="attn_b",
    )(qb, kbd, vbd, kbd, vbd)
    return oa, ob


def _merge_kernel(x_ref, mod_ref, oa_ref, ob_ref, u_ref, up_ref, un_ref, cw_ref, cb_ref,
                  lnw_ref, lnb_ref, wg_ref, wa_ref, wb_ref, wc_ref, wo_ref, o_ref, uext_ref, *,
                  nps, ncps, n_lat_blocks):
    bt = x_ref.shape[0]
    r = pl.program_id(0)
    is_lat = r < n_lat_blocks
    pos = jnp.where(is_lat, r % nps, (r - n_lat_blocks) % ncps)
    last_pos = jnp.where(is_lat, nps - 1, ncps - 1)
    keep_prev = (pos != 0).astype(F32)
    keep_next = (pos != last_pos).astype(F32)

    x = x_ref[...]
    mod = mod_ref[0]
    h = _modulated(x, mod, 0, 1).astype(BF16)

    uext_ref[0:HALO, :] = up_ref[...] * keep_prev
    uext_ref[HALO:HALO + bt, :] = u_ref[...]
    uext_ref[HALO + bt:2 * HALO + bt, :] = un_ref[...] * keep_next
    cw = cw_ref[...]
    y = jnp.zeros((bt, C_WIDTH), F32) + cb_ref[...]
    base = HALO - C_KW // 2
    for k in range(C_KW):
        y = y + cw[k:k + 1, :] * uext_ref[base + k:base + k + bt, :]
    mu = jnp.mean(y, axis=-1, keepdims=True)
    yc = y - mu
    var = jnp.mean(yc * yc, axis=-1, keepdims=True)
    yn = yc * lax.rsqrt(var + EPS) * lnw_ref[...] + lnb_ref[...]
    oc = (yn * jax.nn.sigmoid(yn)).astype(BF16)

    def gate(i):
        logits = jnp.dot(h, wg_ref[:, i * D_MODEL:(i + 1) * D_MODEL], preferred_element_type=F32)
        return jax.nn.sigmoid(logits)

    merged = gate(0) * jnp.dot(oa_ref[...], wa_ref[...], preferred_element_type=F32)
    merged = merged + gate(1) * jnp.dot(ob_ref[...], wb_ref[...], preferred_element_type=F32)
    merged = merged + gate(2) * jnp.dot(oc, wc_ref[...], preferred_element_type=F32)
    out = jnp.dot(merged.astype(BF16), wo_ref[...], preferred_element_type=F32)
    o_ref[...] = x + mod[2:3] * out


def _merge(xy, mod_l, oa, ob, u, cw, cb, lnw, lnb, wg, wa, wb, wc, wo, n_blocks, dims):
    bt = TOK_BLOCK
    t_rows, nps, nbatch = dims["T"], dims["S"] // bt, dims["B"]
    hpb = bt // HALO
    n_halo = t_rows // HALO
    row = lambda r: (r, 0)
    full = lambda r: (0, 0)
    return pl.pallas_call(
        functools.partial(_merge_kernel, nps=nps, ncps=dims["CTX"] // bt,
                          n_lat_blocks=nbatch * nps),
        grid=(n_blocks,),
        in_specs=[pl.BlockSpec((bt, D_MODEL), row),
                  pl.BlockSpec((1, 6, D_MODEL), lambda r: (jnp.minimum(r // nps, nbatch), 0, 0)),
                  pl.BlockSpec((bt, 4 * V7X_LANES), row),
                  pl.BlockSpec((bt, 4 * V7X_LANES), row),
                  pl.BlockSpec((bt, C_WIDTH), row),
                  pl.BlockSpec((HALO, C_WIDTH), lambda r: (jnp.maximum(r * hpb - 1, 0), 0)),
                  pl.BlockSpec((HALO, C_WIDTH), lambda r: (jnp.minimum((r + 1) * hpb, n_halo - 1), 0)),
                  pl.BlockSpec((C_KW + 1, C_WIDTH), full),
                  pl.BlockSpec((1, C_WIDTH), full),
                  pl.BlockSpec((1, C_WIDTH), full),
                  pl.BlockSpec((1, C_WIDTH), full),
                  pl.BlockSpec((D_MODEL, 3 * D_MODEL), full),
                  pl.BlockSpec((C_WIDTH, D_MODEL), full),
                  pl.BlockSpec((C_WIDTH, D_MODEL), full),
                  pl.BlockSpec((C_WIDTH, D_MODEL), full),
                  pl.BlockSpec((D_MODEL, D_MODEL), full)],
        out_specs=pl.BlockSpec((bt, D_MODEL), row),
        out_shape=jax.ShapeDtypeStruct((n_blocks * bt, D_MODEL), F32),
        scratch_shapes=[pltpu.VMEM((bt + 2 * HALO, C_WIDTH), F32)],
        compiler_params=_cparams(("arbitrary",)),
        name="merge",
    )(xy, mod_l, oa, ob, u, u, u, cw, cb, lnw, lnb, wg, wa, wb, wc, wo)


def _topk_desc(s, k):
    row = lax.broadcasted_iota(jnp.int32, (k, s.shape[1]), 0)

    def body(i, carry):
        work, vals = carry
        m = jnp.max(work, axis=0, keepdims=True)
        vals = jnp.where(row == i, m, vals)
        work = jnp.where(work == m, -jnp.inf, work)
        return work, vals

    _, vals = lax.fori_loop(0, k, body, (s, jnp.zeros((k, s.shape[1]), F32)))
    return vals


def _candidate_sums(sv1, sv2):
    half = P_TOPK // 2
    lead, rest = sv1[:half], sv1[half:]
    row = lax.broadcasted_iota(jnp.int32, lead.shape, 0)
    tiles = [lead + sv2[0:1], rest + sv2[0:1], lead + sv2[1:2]]
    for k2 in range(2, half):
        tiles.append(jnp.where(row < P_TOPK // (k2 + 1), lead + sv2[k2:k2 + 1], -jnp.inf))
    tiles.append(sv1[0:1] + sv2[half:])
    return jnp.concatenate(tiles, axis=0)


def _peer_kernel(x_ref, mod_ref, wq_ref, k1_ref, k2_ref, u_ref, vt_ref, fnw_ref, o_ref,
                 t_scr, need_scr, e1_scr, r2_scr, e2_scr, a_scr, m_scr, acc_scr, *,
                 n_chunks, final_norm):
    bt = x_ref.shape[0]
    c = pl.program_id(1)
    ipc = u_ref.shape[0] // P_NKEYS
    n_lt = bt // V7X_LANES

    @pl.when(c == 0)
    def _select():
        hb = _modulated(x_ref[...], mod_ref[0], 3, 4).astype(BF16)
        t_scr[...] = hb
        q = jnp.dot(hb, wq_ref[...], preferred_element_type=F32).astype(BF16)
        s1p = lax.dot_general(k1_ref[...], q, NT_DIMS, preferred_element_type=F32)
        for lt in range(n_lt):
            need_scr[lt] = s1p[:, lt * V7X_LANES:(lt + 1) * V7X_LANES]
        head_row = lax.broadcasted_iota(jnp.int32, (P_HEADS, bt), 0)
        zero_t = jnp.zeros((P_HEADS, bt), F32)
        thr1, max1, tau_t = zero_t, zero_t, zero_t
        sv2_t = [zero_t] * P_TOPK
        for hh in range(P_HEADS):
            mine = head_row == hh
            qh = q[:, hh * V7X_LANES:(hh + 1) * V7X_LANES]
            s1 = jnp.concatenate([need_scr[lt, pl.ds(hh, P_NKEYS, stride=P_HEADS), :]
                                  for lt in range(n_lt)], axis=1)
            s2 = lax.dot_general(k2_ref[hh], qh, NT_DIMS, preferred_element_type=F32)
            sv1 = _topk_desc(s1, P_TOPK)
            sv2 = _topk_desc(s2, P_TOPK)
            top = _topk_desc(_candidate_sums(sv1, sv2), P_TOPK)
            z = jnp.sum(jnp.exp(top - top[0:1]), axis=0, keepdims=True)
            rank2 = jnp.zeros_like(s2)
            for k in range(P_TOPK):
                rank2 = rank2 + jnp.where(sv2[k:k + 1] > s2, 1.0, 0.0)
                sv2_t[k] = jnp.where(mine, sv2[k:k + 1], sv2_t[k])
            r2_scr[hh] = rank2.astype(BF16)
            e2_scr[hh] = (jnp.exp(s2 - sv2[0:1]) / z).astype(BF16)
            tau_t = jnp.where(mine, top[P_TOPK - 1:P_TOPK], tau_t)
            thr1 = jnp.where(mine, sv1[P_TOPK - 1:P_TOPK], thr1)
            max1 = jnp.where(mine, sv1[0:1], max1)
        for lt in range(n_lt):
            ls = slice(lt * V7X_LANES, (lt + 1) * V7X_LANES)
            s1t = need_scr[lt].reshape(P_NKEYS, P_HEADS, V7X_LANES)
            count = jnp.zeros_like(s1t)
            for k in range(P_TOPK):
                count = count + jnp.where(s1t + sv2_t[k][None, :, ls] >= tau_t[None, :, ls], 1.0, 0.0)
            need_scr[lt] = jnp.where(s1t >= thr1[None, :, ls], count, 0.0).reshape(
                P_NKEYS * P_HEADS, V7X_LANES)
            e1_scr[lt] = jnp.exp(s1t - max1[None, :, ls]).reshape(P_NKEYS * P_HEADS, V7X_LANES)
        acc_scr[...] = jnp.zeros_like(acc_scr)
        a_scr[...] = jnp.zeros_like(a_scr)

    cprev = jnp.maximum(c - 1, 0)
    for ii in range(ipc):
        roff = ii * P_NKEYS
        hrow = pl.ds(pl.multiple_of((cprev * ipc + ii) * P_HEADS, P_HEADS), P_HEADS)
        for lt in range(n_lt):
            ls = slice(lt * V7X_LANES, (lt + 1) * V7X_LANES)
            w = [jnp.zeros((ROW_GROUP, V7X_LANES), BF16) for _ in range(P_NKEYS // ROW_GROUP)]
            need_t = need_scr[lt, hrow, :]
            e1_t = e1_scr[lt, hrow, :]
            zero = jnp.zeros((ROW_GROUP, V7X_LANES), BF16)
            for hh in range(P_HEADS):
                tile = (ROW_GROUP, V7X_LANES)
                need_b = jnp.broadcast_to(need_t[hh:hh + 1], tile).astype(BF16)
                e1_b = jnp.broadcast_to(e1_t[hh:hh + 1], tile).astype(BF16)
                for jg in range(P_NKEYS // ROW_GROUP):
                    js = slice(jg * ROW_GROUP, (jg + 1) * ROW_GROUP)
                    hit = r2_scr[hh, js, ls] < need_b
                    w[jg] = w[jg] + jnp.where(hit, e1_b * e2_scr[hh, js, ls], zero)
            for jg in range(P_NKEYS // ROW_GROUP):
                rows = slice(roff + jg * ROW_GROUP, roff + (jg + 1) * ROW_GROUP)
                a = a_scr[rows, ls]
                g = 0.5 * a * (1.0 + lax.erf(a * (2.0 ** -0.5)))
                m_scr[rows, ls] = g.astype(BF16) * w[jg]

    acc_scr[...] += jnp.dot(vt_ref[...], m_scr[...], preferred_element_type=F32)
    a_scr[...] = lax.dot_general(u_ref[...], t_scr[...], NT_DIMS, preferred_element_type=F32)

    @pl.when(c == n_chunks)
    def _finish():
        x = x_ref[...]
        y = x + mod_ref[0][5:6] * acc_scr[...].T
        if final_norm:
            ms = jnp.mean(y * y, axis=-1, keepdims=True)
            y = y * lax.rsqrt(ms + EPS) * fnw_ref[...]
        o_ref[...] = y


def _peer(x1, mod_l, wq, k1p, k2p, u_tab, vt_tab, fnw, n_blocks, final_norm, dims):
    bt, ec = PEER_BLOCK, PEER_ECHUNK
    nps, nbatch = dims["S"] // bt, dims["B"]
    n_exp = u_tab.shape[0]
    n_chunks = n_exp // ec
    return pl.pallas_call(
        functools.partial(_peer_kernel, n_chunks=n_chunks, final_norm=final_norm),
        grid=(n_blocks, n_chunks + 1),
        in_specs=[pl.BlockSpec((bt, D_MODEL), lambda r, c: (r, 0)),
                  pl.BlockSpec((1, 6, D_MODEL), lambda r, c: (jnp.minimum(r // nps, nbatch), 0, 0)),
                  pl.BlockSpec((D_MODEL, P_HEADS * V7X_LANES), lambda r, c: (0, 0)),
                  pl.BlockSpec((P_NKEYS * P_HEADS, P_HEADS * V7X_LANES), lambda r, c: (0, 0)),
                  pl.BlockSpec((P_HEADS, P_NKEYS, V7X_LANES), lambda r, c: (0, 0, 0)),
                  pl.BlockSpec((ec, D_MODEL), lambda r, c: (jnp.minimum(c, n_chunks - 1), 0)),
                  pl.BlockSpec((D_MODEL, ec), lambda r, c: (0, jnp.maximum(c - 1, 0))),
                  pl.BlockSpec((1, D_MODEL), lambda r, c: (0, 0))],
        out_specs=pl.BlockSpec((bt, D_MODEL), lambda r, c: (r, 0)),
        out_shape=jax.ShapeDtypeStruct((n_blocks * bt, D_MODEL), F32),
        scratch_shapes=[pltpu.VMEM((bt, D_MODEL), BF16),
                        pltpu.VMEM((bt // V7X_LANES, P_NKEYS * P_HEADS, V7X_LANES), F32),
                        pltpu.VMEM((bt // V7X_LANES, P_NKEYS * P_HEADS, V7X_LANES), F32),
                        pltpu.VMEM((P_HEADS, P_NKEYS, bt), BF16),
                        pltpu.VMEM((P_HEADS, P_NKEYS, bt), BF16),
                        pltpu.VMEM((ec, bt), F32),
                        pltpu.VMEM((ec, bt), BF16),
                        pltpu.VMEM((D_MODEL, bt), F32)],
        compiler_params=_cparams(("arbitrary", "arbitrary")),
        name="peer",
    )(x1, mod_l, wq, k1p, k2p, u_tab, vt_tab, fnw)


def _rope_tables(s_len, nbatch, n_ctx_rows):
    t = jnp.arange(s_len, dtype=jnp.int32)
    row = (t // GRID_W).astype(F32)
    col = (t % GRID_W).astype(F32)
    axis_dim = HEAD_DIM // 2
    inv = ROPE_THETA ** (-jnp.arange(0, axis_dim, 2, dtype=F32) / axis_dim)
    ar = row[:, None] * inv[None, :]
    ac = col[:, None] * inv[None, :]
    ang = jnp.concatenate([ar, ar, ac, ac], axis=-1)
    cos, sin = jnp.cos(ang), jnp.sin(ang)
    quarter = jnp.arange(HEAD_DIM) // (HEAD_DIM // 4)
    first = (quarter % 2 == 0)[None, :]
    sin_a = jnp.where(first, -sin, 0.0)
    sin_b = jnp.where(first, 0.0, sin)

    def expand(tab, ctx_fill):
        lat = jnp.tile(jnp.concatenate([tab, tab], axis=-1), (nbatch, 1))
        return jnp.concatenate([lat, jnp.full((n_ctx_rows, V7X_LANES), ctx_fill, F32)], axis=0)

    return expand(cos, 1.0), expand(sin_a, 0.0), expand(sin_b, 0.0)


def _padded_keys(keys_l):
    z = jnp.zeros_like(keys_l[:, 0])
    k0 = jnp.concatenate([keys_l[:, 0], z], axis=-1).transpose(1, 0, 2)
    eye = jnp.eye(P_HEADS, dtype=keys_l.dtype)
    k1p = (k0[:, :, None, :] * eye[None, :, :, None]).reshape(
        P_NKEYS * P_HEADS, P_HEADS * V7X_LANES)
    k2p = jnp.concatenate([z, keys_l[:, 1]], axis=-1)
    return k1p.astype(BF16), k2p.astype(BF16)


def kernel(x, c, ctx, c_ctx, w_ada, b_ada, w_in, lam_q1, lam_k1, lam_q2, lam_k2, subln_w, q_norm_w, k_norm_w, conv_w, conv_b, conv_ln_w, conv_ln_b, w_branch_a, w_branch_b, w_branch_c, w_out, peer_wq, peer_keys, peer_u, peer_v, final_norm_w):
    nbatch, s_len, d = x.shape
    ctx_len = ctx.shape[1]
    depth = w_ada.shape[0]
    n_lat, n_ctx = nbatch * s_len, nbatch * ctx_len
    dims = {"B": nbatch, "S": s_len, "CTX": ctx_len, "T": n_lat + n_ctx}
    assert d == D_MODEL and nbatch + 1 <= MOD_ROWS
    assert s_len % PEER_BLOCK == 0 and ctx_len % ATT_TQ == 0 and ctx_len % TOK_BLOCK == 0
    assert n_ctx % PEER_BLOCK == 0 and n_lat % ctx_len == 0

    cvec = jnp.zeros((MOD_ROWS, d), F32).at[:nbatch].set(c).at[nbatch].set(c_ctx)
    mod = _ada_rows(cvec, w_ada, b_ada).reshape(depth, MOD_ROWS, 6, d)
    tabs = _rope_tables(s_len, nbatch, n_ctx)
    tile2 = lambda v: jnp.concatenate([v, v], axis=-1)[None, :]

    xy = jnp.concatenate([x.reshape(n_lat, d), ctx.reshape(n_ctx, d)], axis=0)
    for l in range(depth):
        last = l == depth - 1
        lam_init = 0.8 - 0.6 * math.exp(-0.3 * l)
        w_bf = w_in[l].astype(BF16)
        lamv = jnp.stack([lam_q1[l], lam_k1[l], lam_q2[l], lam_k2[l]], axis=0)

        qa, ka, va, qb, kbd, vbd, u = _inproj(
            xy, mod[l], w_bf[:, :OFF_GATE], tabs, tile2(q_norm_w[l]), tile2(k_norm_w[l]), dims)
        oa, ob = _attention(qa, ka, va, qb, kbd, vbd, lamv, subln_w[l][None, :], lam_init,
                            not last, dims)
        n_rows = n_lat if last else n_lat + n_ctx
        cw = jnp.concatenate([conv_w[l], jnp.zeros((1, C_WIDTH), F32)], axis=0)
        x1 = _merge(xy, mod[l], oa, ob, u, cw, conv_b[l][None, :], conv_ln_w[l][None, :],
                    conv_ln_b[l][None, :], w_bf[:, OFF_GATE:], w_branch_a[l].astype(BF16),
                    w_branch_b[l].astype(BF16), w_branch_c[l].astype(BF16),
                    w_out[l].astype(BF16), n_rows // TOK_BLOCK, dims)
        k1p, k2p = _padded_keys(peer_keys[l])
        xy = _peer(x1, mod[l], peer_wq[l].astype(BF16), k1p, k2p,
                   peer_u[l].astype(BF16), peer_v[l].T.astype(BF16), final_norm_w[None, :],
                   n_rows // PEER_BLOCK, last, dims)
    return xy.reshape(nbatch, s_len, d)
```

```python
import functools
import math

import jax
import jax.numpy as jnp
from jax import lax
from jax.experimental import pallas as pl
from jax.experimental.pallas import tpu as pltpu

F32 = jnp.float32
BF16 = jnp.bfloat16

D_MODEL = 1024
DEPTH = 2
GRID_W = 64
HEAD_DIM = 64
ROPE_THETA = 10000.0
EPS = 1e-6
A_HEADS = 4
B_KV = 2
C_WIDTH = 512
C_KW = 31
P_HEADS = 8
P_NKEYS = 128
P_TOPK = 16

COL_KA, COL_VA, COL_KB, COL_VB, COL_QA, COL_QB, COL_GLU = 512, 512, 128, 128, 512, 512, 1024
OFF_KA = 0
OFF_VA = OFF_KA + COL_KA
OFF_KB = OFF_VA + COL_VA
OFF_QA = OFF_KB + COL_KB + COL_VB
OFF_QB = OFF_QA + COL_QA
OFF_GLU = OFF_QB + COL_QB
OFF_GATE = OFF_GLU + COL_GLU

V7X_LANES = 128
V7X_VMEM_LIMIT = 56 * 1024 * 1024
HALO = 16

TOK_BLOCK = 256
ATT_TQ = 256
ATT_TK = 512
ATT_UNROLL = 8
PEER_BLOCK = 512
PEER_ECHUNK = 1024
MOD_ROWS = 8
ROW_GROUP = 16

Q_SCALE = (HEAD_DIM ** -0.5) * math.log2(math.e)
NEG_BIG = -1e30
NT_DIMS = (((1,), (1,)), ((), ()))


def _cparams(sem):
    return pltpu.CompilerParams(dimension_semantics=sem, vmem_limit_bytes=V7X_VMEM_LIMIT)


def _modulated(x, mod, shift_row, scale_row):
    ms = jnp.mean(x * x, axis=-1, keepdims=True)
    xn = x * lax.rsqrt(ms + EPS)
    return xn * (1.0 + mod[scale_row:scale_row + 1]) + mod[shift_row:shift_row + 1]


def _ada_kernel(c_ref, w_ref, b_ref, o_ref):
    c = c_ref[...]
    sc = c * jax.nn.sigmoid(c)
    o_ref[0] = jnp.dot(sc, w_ref[0], preferred_element_type=F32,
                       precision=lax.Precision.HIGHEST) + b_ref[0]


def _ada_rows(cvec, w_ada, b_ada):
    depth, d, n = w_ada.shape
    tn = 1536
    return pl.pallas_call(
        _ada_kernel,
        grid=(depth, n // tn),
        in_specs=[pl.BlockSpec((MOD_ROWS, d), lambda l, j: (0, 0)),
                  pl.BlockSpec((1, d, tn), lambda l, j: (l, 0, j)),
                  pl.BlockSpec((1, 1, tn), lambda l, j: (l, 0, j))],
        out_specs=pl.BlockSpec((1, MOD_ROWS, tn), lambda l, j: (l, 0, j)),
        out_shape=jax.ShapeDtypeStruct((depth, MOD_ROWS, n), F32),
        compiler_params=_cparams(("arbitrary", "arbitrary")),
        name="ada_rows",
    )(cvec, w_ada, b_ada.reshape(depth, 1, n))


def _inproj_kernel(x_ref, mod_ref, w_ref, cos_ref, sa_ref, sb_ref, qnw_ref, knw_ref,
                   qa_ref, ka_ref, va_ref, qb_ref, kbd_ref, vbd_ref, u_ref):
    bt = x_ref.shape[0]
    h = _modulated(x_ref[...], mod_ref[0], 0, 1).astype(BF16)
    cos, sa, sb = cos_ref[...], sa_ref[...], sb_ref[...]
    lo = lax.broadcasted_iota(jnp.int32, (bt, V7X_LANES), 1) < HEAD_DIM

    def proj(c0, width):
        return jnp.dot(h, w_ref[:, c0:c0 + width], preferred_element_type=F32)

    def rope(x):
        return x * cos + pltpu.roll(x, V7X_LANES - 16, 1) * sa + pltpu.roll(x, 16, 1) * sb

    def headnorm(x, w):
        x2 = x * x
        s_lo = jnp.sum(jnp.where(lo, x2, 0.0), axis=-1, keepdims=True)
        s_hi = jnp.sum(jnp.where(lo, 0.0, x2), axis=-1, keepdims=True)
        r = jnp.where(lo, lax.rsqrt(s_lo / HEAD_DIM + EPS), lax.rsqrt(s_hi / HEAD_DIM + EPS))
        return x * r * w

    def tile(p, c):
        return p[:, c * V7X_LANES:(c + 1) * V7X_LANES]

    p = proj(OFF_KA, COL_KA)
    for c in range(A_HEADS):
        ka_ref[c] = rope(tile(p, c)).astype(BF16)

    p = proj(OFF_VA, COL_VA)
    for c in range(A_HEADS):
        va_ref[c] = tile(p, c).astype(BF16)

    p = proj(OFF_KB, COL_KB + COL_VB)
    kb = rope(headnorm(tile(p, 0), knw_ref[...]))
    kb_sw = pltpu.roll(kb, HEAD_DIM, 1)
    kbd_ref[0] = jnp.where(lo, kb, kb_sw).astype(BF16)
    kbd_ref[1] = jnp.where(lo, kb_sw, kb).astype(BF16)
    vb = tile(p, 1)
    vb_sw = pltpu.roll(vb, HEAD_DIM, 1)
    vbd_ref[0] = jnp.where(lo, vb, vb_sw).astype(BF16)
    vbd_ref[1] = jnp.where(lo, vb_sw, vb).astype(BF16)

    p = proj(OFF_QA, COL_QA)
    for c in range(COL_QA // V7X_LANES):
        qa_ref[:, c * V7X_LANES:(c + 1) * V7X_LANES] = (rope(tile(p, c)) * Q_SCALE).astype(BF16)

    p = proj(OFF_QB, COL_QB)
    qnw = qnw_ref[...]
    for c in range(COL_QB // V7X_LANES):
        qb_ref[:, c * V7X_LANES:(c + 1) * V7X_LANES] = (
            rope(headnorm(tile(p, c), qnw)) * Q_SCALE).astype(BF16)

    p = proj(OFF_GLU, COL_GLU)
    u_ref[...] = p[:, :C_WIDTH] * jax.nn.sigmoid(p[:, C_WIDTH:])


def _inproj(xy, mod_l, w_main, tabs, qnw, knw, dims):
    t_rows, nps, nbatch = dims["T"], dims["S"] // TOK_BLOCK, dims["B"]
    bt = TOK_BLOCK
    row = lambda r: (r, 0)
    full = lambda r: (0, 0)
    out_shapes = (
        jax.ShapeDtypeStruct((t_rows, COL_QA), BF16),
        jax.ShapeDtypeStruct((A_HEADS, t_rows, V7X_LANES), BF16),
        jax.ShapeDtypeStruct((A_HEADS, t_rows, V7X_LANES), BF16),
        jax.ShapeDtypeStruct((t_rows, COL_QB), BF16),
        jax.ShapeDtypeStruct((B_KV, t_rows, V7X_LANES), BF16),
        jax.ShapeDtypeStruct((B_KV, t_rows, V7X_LANES), BF16),
        jax.ShapeDtypeStruct((t_rows, C_WIDTH), F32),
    )
    grp = pl.BlockSpec((B_KV, bt, V7X_LANES), lambda r: (0, r, 0))
    heads = pl.BlockSpec((A_HEADS, bt, V7X_LANES), lambda r: (0, r, 0))
    return pl.pallas_call(
        _inproj_kernel,
        grid=(t_rows // bt,),
        in_specs=[pl.BlockSpec((bt, D_MODEL), row),
                  pl.BlockSpec((1, 6, D_MODEL), lambda r: (jnp.minimum(r // nps, nbatch), 0, 0)),
                  pl.BlockSpec((D_MODEL, OFF_GATE), full),
                  pl.BlockSpec((bt, V7X_LANES), row),
                  pl.BlockSpec((bt, V7X_LANES), row),
                  pl.BlockSpec((bt, V7X_LANES), row),
                  pl.BlockSpec((1, V7X_LANES), full),
                  pl.BlockSpec((1, V7X_LANES), full)],
        out_specs=(pl.BlockSpec((bt, COL_QA), row), heads, heads, pl.BlockSpec((bt, COL_QB), row),
                   grp, grp, pl.BlockSpec((bt, C_WIDTH), row)),
        out_shape=out_shapes,
        compiler_params=_cparams(("arbitrary",)),
        name="inproj",
    )(xy, mod_l, w_main, tabs[0], tabs[1], tabs[2], qnw, knw)


def _kv_unroll(n_chunks):
    return ATT_UNROLL if n_chunks % ATT_UNROLL == 0 else 1


def _softmax_step(s, v, m, l, acc):
    mn = jnp.maximum(m, jnp.max(s, axis=-1, keepdims=True))
    p = jnp.exp2(s - mn)
    alpha = jnp.exp2(m - mn)
    l = alpha * l + jnp.sum(p, axis=-1, keepdims=True)
    acc = alpha * acc + jnp.dot(p.astype(BF16), v, preferred_element_type=F32)
    return mn, l, acc


def _flash_rows(q, load_lat, kc, vc, *, nq_lat, n_lat_chunks, tk):
    rows = q.shape[0]

    def step(k, v, carry):
        m, l, a = carry
        s = lax.dot_general(q, k, NT_DIMS, preferred_element_type=F32)
        return _softmax_step(s, v, m, l, a)

    unroll = _kv_unroll(n_lat_chunks)

    def lat_body(j, carry):
        for r in range(unroll):
            off = pl.multiple_of((j * unroll + r) * tk, tk)
            carry = step(*load_lat(off), carry)
        return carry

    carry = (jnp.full((rows, 1), NEG_BIG, F32), jnp.zeros((rows, 1), F32),
             jnp.zeros((rows, V7X_LANES), F32))
    n_lat = jnp.where(pl.program_id(2) < nq_lat, n_lat_chunks // unroll, 0)
    carry = lax.fori_loop(0, n_lat, lat_body, carry)
    _, l, a = step(kc, vc, carry)
    return a / l


def _attn_a_kernel(q_ref, kl_ref, vl_ref, kc_ref, vc_ref, lamv_ref, sw_ref, o_ref, *,
                   nq_lat, n_lat_chunks, tk, lam_init):
    tq = q_ref.shape[0]
    q = q_ref[...]
    lo = lax.broadcasted_iota(jnp.int32, (tq, V7X_LANES), 1) < HEAD_DIM
    zero = jnp.zeros_like(q)
    q12 = jnp.concatenate([jnp.where(lo, q, zero), jnp.where(lo, zero, q)], axis=0)
    o12 = _flash_rows(q12, lambda off: (kl_ref[0, pl.ds(off, tk), :], vl_ref[0, pl.ds(off, tk), :]),
                      kc_ref[0], vc_ref[0], nq_lat=nq_lat, n_lat_chunks=n_lat_chunks, tk=tk)

    lv = lamv_ref[...]
    lam = (jnp.exp(jnp.sum(lv[0:1] * lv[1:2], axis=-1, keepdims=True))
           - jnp.exp(jnp.sum(lv[2:3] * lv[3:4], axis=-1, keepdims=True)) + lam_init)
    o = o12[:tq] - lam * o12[tq:]
    ms = jnp.mean(o * o, axis=-1, keepdims=True)
    o_ref[...] = (o * lax.rsqrt(ms + EPS) * sw_ref[...] * (1.0 - lam_init)).astype(BF16)


def _attn_b_kernel(q_ref, kl_ref, vl_ref, kc_ref, vc_ref, o_ref, *, nq_lat, n_lat_chunks, tk):
    tq = q_ref.shape[0]
    lo = lax.broadcasted_iota(jnp.int32, (tq, V7X_LANES), 1) < HEAD_DIM
    parts = []
    for c in range(2):
        qc = q_ref[:, c * V7X_LANES:(c + 1) * V7X_LANES]
        zero = jnp.zeros_like(qc)
        parts += [jnp.where(lo, qc, zero), jnp.where(lo, zero, qc)]
    q4 = jnp.concatenate(parts, axis=0)
    o = _flash_rows(q4, lambda off: (kl_ref[0, pl.ds(off, tk), :], vl_ref[0, pl.ds(off, tk), :]),
                    kc_ref[0], vc_ref[0], nq_lat=nq_lat, n_lat_chunks=n_lat_chunks, tk=tk)
    for c in range(2):
        o_ref[:, c * V7X_LANES:(c + 1) * V7X_LANES] = jnp.where(
            lo, o[(2 * c) * tq:(2 * c + 1) * tq], o[(2 * c + 1) * tq:(2 * c + 2) * tq]).astype(BF16)


def _q_block_index(b, qi, nq_lat, nq_ctx, nbatch):
    return jnp.where(qi < nq_lat, b * nq_lat + qi, nbatch * nq_lat + b * nq_ctx + (qi - nq_lat))


def _attention(qa, ka, va, qb, kbd, vbd, lamv, subln_w, lam_init, with_ctx_queries, dims):
    s_len, ctx_len, nbatch, t_rows = dims["S"], dims["CTX"], dims["B"], dims["T"]
    tq, tk = ATT_TQ, min(ATT_TK, s_len)
    nq_lat, nq_ctx = s_len // tq, ctx_len // tq
    nq = nq_lat + (nq_ctx if with_ctx_queries else 0)
    ctx_blk0 = (nbatch * s_len) // ctx_len
    qidx = functools.partial(_q_block_index, nq_lat=nq_lat, nq_ctx=nq_ctx, nbatch=nbatch)
    sem = ("arbitrary", "arbitrary", "arbitrary")

    oa = pl.pallas_call(
        functools.partial(_attn_a_kernel, nq_lat=nq_lat, n_lat_chunks=s_len // tk, tk=tk,
                          lam_init=lam_init),
        grid=(nbatch, A_HEADS, nq),
        in_specs=[pl.BlockSpec((tq, V7X_LANES), lambda b, h, qi: (qidx(b, qi), h)),
                  pl.BlockSpec((1, s_len, V7X_LANES), lambda b, h, qi: (h, b, 0)),
                  pl.BlockSpec((1, s_len, V7X_LANES), lambda b, h, qi: (h, b, 0)),
                  pl.BlockSpec((1, ctx_len, V7X_LANES), lambda b, h, qi: (h, ctx_blk0 + b, 0)),
                  pl.BlockSpec((1, ctx_len, V7X_LANES), lambda b, h, qi: (h, ctx_blk0 + b, 0)),
                  pl.BlockSpec((4, HEAD_DIM), lambda b, h, qi: (0, 0)),
                  pl.BlockSpec((1, V7X_LANES), lambda b, h, qi: (0, 0))],
        out_specs=pl.BlockSpec((tq, V7X_LANES), lambda b, h, qi: (qidx(b, qi), h)),
        out_shape=jax.ShapeDtypeStruct((t_rows, A_HEADS * V7X_LANES), BF16),
        compiler_params=_cparams(sem),
        name="attn_a",
    )(qa, ka, va, ka, va, lamv, subln_w)

    gw = 2 * V7X_LANES
    ob = pl.pallas_call(
        functools.partial(_attn_b_kernel, nq_lat=nq_lat, n_lat_chunks=s_len // tk, tk=tk),
        grid=(nbatch, B_KV, nq),
        in_specs=[pl.BlockSpec((tq, gw), lambda b, g, qi: (qidx(b, qi), g)),
                  pl.BlockSpec((1, s_len, V7X_LANES), lambda b, g, qi: (g, b, 0)),
                  pl.BlockSpec((1, s_len, V7X_LANES), lambda b, g, qi: (g, b, 0)),
                  pl.BlockSpec((1, ctx_len, V7X_LANES), lambda b, g, qi: (g, ctx_blk0 + b, 0)),
                  pl.BlockSpec((1, ctx_len, V7X_LANES), lambda b, g, qi: (g, ctx_blk0 + b, 0))],
        out_specs=pl.BlockSpec((tq, gw), lambda b, g, qi: (qidx(b, qi), g)),
        out_shape=jax.ShapeDtypeStruct((t_rows, B_KV * gw), BF16),
        compiler_params=_cparams(sem),
        name="attn_b",
    )(qb, kbd, vbd, kbd, vbd)
    return oa, ob


def _merge_kernel(x_ref, mod_ref, oa_ref, ob_ref, u_ref, up_ref, un_ref, cw_ref, cb_ref,
                  lnw_ref, lnb_ref, wg_ref, wa_ref, wb_ref, wc_ref, wo_ref, o_ref, uext_ref, *,
                  nps, ncps, n_lat_blocks):
    bt = x_ref.shape[0]
    r = pl.program_id(0)
    is_lat = r < n_lat_blocks
    pos = jnp.where(is_lat, r % nps, (r - n_lat_blocks) % ncps)
    last_pos = jnp.where(is_lat, nps - 1, ncps - 1)
    keep_prev = (pos != 0).astype(F32)
    keep_next = (pos != last_pos).astype(F32)

    x = x_ref[...]
    mod = mod_ref[0]
    h = _modulated(x, mod, 0, 1).astype(BF16)

    uext_ref[0:HALO, :] = up_ref[...] * keep_prev
    uext_ref[HALO:HALO + bt, :] = u_ref[...]
    uext_ref[HALO + bt:2 * HALO + bt, :] = un_ref[...] * keep_next
    cw = cw_ref[...]
    y = jnp.zeros((bt, C_WIDTH), F32) + cb_ref[...]
    base = HALO - C_KW // 2
    for k in range(C_KW):
        y = y + cw[k:k + 1, :] * uext_ref[base + k:base + k + bt, :]
    mu = jnp.mean(y, axis=-1, keepdims=True)
    yc = y - mu
    var = jnp.mean(yc * yc, axis=-1, keepdims=True)
    yn = yc * lax.rsqrt(var + EPS) * lnw_ref[...] + lnb_ref[...]
    oc = (yn * jax.nn.sigmoid(yn)).astype(BF16)

    def gate(i):
        logits = jnp.dot(h, wg_ref[:, i * D_MODEL:(i + 1) * D_MODEL], preferred_element_type=F32)
        return jax.nn.sigmoid(logits)

    merged = gate(0) * jnp.dot(oa_ref[...], wa_ref[...], preferred_element_type=F32)
    merged = merged + gate(1) * jnp.dot(ob_ref[...], wb_ref[...], preferred_element_type=F32)
    merged = merged + gate(2) * jnp.dot(oc, wc_ref[...], preferred_element_type=F32)
    out = jnp.dot(merged.astype(BF16), wo_ref[...], preferred_element_type=F32)
    o_ref[...] = x + mod[2:3] * out


def _merge(xy, mod_l, oa, ob, u, cw, cb, lnw, lnb, wg, wa, wb, wc, wo, n_blocks, dims):
    bt = TOK_BLOCK
    t_rows, nps, nbatch = dims["T"], dims["S"] // bt, dims["B"]
    hpb = bt // HALO
    n_halo = t_rows // HALO
    row = lambda r: (r, 0)
    full = lambda r: (0, 0)
    return pl.pallas_call(
        functools.partial(_merge_kernel, nps=nps, ncps=dims["CTX"] // bt,
                          n_lat_blocks=nbatch * nps),
        grid=(n_blocks,),
        in_specs=[pl.BlockSpec((bt, D_MODEL), row),
                  pl.BlockSpec((1, 6, D_MODEL), lambda r: (jnp.minimum(r // nps, nbatch), 0, 0)),
                  pl.BlockSpec((bt, 4 * V7X_LANES), row),
                  pl.BlockSpec((bt, 4 * V7X_LANES), row),
                  pl.BlockSpec((bt, C_WIDTH), row),
                  pl.BlockSpec((HALO, C_WIDTH), lambda r: (jnp.maximum(r * hpb - 1, 0), 0)),
                  pl.BlockSpec((HALO, C_WIDTH), lambda r: (jnp.minimum((r + 1) * hpb, n_halo - 1), 0)),
                  pl.BlockSpec((C_KW + 1, C_WIDTH), full),
                  pl.BlockSpec((1, C_WIDTH), full),
                  pl.BlockSpec((1, C_WIDTH), full),
                  pl.BlockSpec((1, C_WIDTH), full),
                  pl.BlockSpec((D_MODEL, 3 * D_MODEL), full),
                  pl.BlockSpec((C_WIDTH, D_MODEL), full),
                  pl.BlockSpec((C_WIDTH, D_MODEL), full),
                  pl.BlockSpec((C_WIDTH, D_MODEL), full),
                  pl.BlockSpec((D_MODEL, D_MODEL), full)],
        out_specs=pl.BlockSpec((bt, D_MODEL), row),
        out_shape=jax.ShapeDtypeStruct((n_blocks * bt, D_MODEL), F32),
        scratch_shapes=[pltpu.VMEM((bt + 2 * HALO, C_WIDTH), F32)],
        compiler_params=_cparams(("arbitrary",)),
        name="merge",
    )(xy, mod_l, oa, ob, u, u, u, cw, cb, lnw, lnb, wg, wa, wb, wc, wo)


def _topk_desc(s, k):
    row = lax.broadcasted_iota(jnp.int32, (k, s.shape[1]), 0)

    def body(i, carry):
        work, vals = carry
        m = jnp.max(work, axis=0, keepdims=True)
        vals = jnp.where(row == i, m, vals)
        work = jnp.where(work == m, -jnp.inf, work)
        return work, vals

    _, vals = lax.fori_loop(0, k, body, (s, jnp.zeros((k, s.shape[1]), F32)))
    return vals


def _candidate_sums(sv1, sv2):
    half = P_TOPK // 2
    lead, rest = sv1[:half], sv1[half:]
    row = lax.broadcasted_iota(jnp.int32, lead.shape, 0)
    tiles = [lead + sv2[0:1], rest + sv2[0:1], lead + sv2[1:2]]
    for k2 in range(2, half):
        tiles.append(jnp.where(row < P_TOPK // (k2 + 1), lead + sv2[k2:k2 + 1], -jnp.inf))
    tiles.append(sv1[0:1] + sv2[half:])
    return jnp.concatenate(tiles, axis=0)


def _peer_kernel(x_ref, mod_ref, wq_ref, k1_ref, k2_ref, u_ref, vt_ref, fnw_ref, o_ref,
                 t_scr, need_scr, e1_scr, r2_scr, e2_scr, a_scr, m_scr, acc_scr, *,
                 n_chunks, final_norm):
    bt = x_ref.shape[0]
    c = pl.program_id(1)
    ipc = u_ref.shape[0] // P_NKEYS
    n_lt = bt // V7X_LANES

    @pl.when(c == 0)
    def _select():
        hb = _modulated(x_ref[...], mod_ref[0], 3, 4).astype(BF16)
        t_scr[...] = hb
        q = jnp.dot(hb, wq_ref[...], preferred_element_type=F32).astype(BF16)
        s1p = lax.dot_general(k1_ref[...], q, NT_DIMS, preferred_element_type=F32)
        for lt in range(n_lt):
            need_scr[lt] = s1p[:, lt * V7X_LANES:(lt + 1) * V7X_LANES]
        head_row = lax.broadcasted_iota(jnp.int32, (P_HEADS, bt), 0)
        zero_t = jnp.zeros((P_HEADS, bt), F32)
        thr1, max1, tau_t = zero_t, zero_t, zero_t
        sv2_t = [zero_t] * P_TOPK
        for hh in range(P_HEADS):
            mine = head_row == hh
            qh = q[:, hh * V7X_LANES:(hh + 1) * V7X_LANES]
            s1 = jnp.concatenate([need_scr[lt, pl.ds(hh, P_NKEYS, stride=P_HEADS), :]
                                  for lt in range(n_lt)], axis=1)
            s2 = lax.dot_general(k2_ref[hh], qh, NT_DIMS, preferred_element_type=F32)
            sv1 = _topk_desc(s1, P_TOPK)
            sv2 = _topk_desc(s2, P_TOPK)
            top = _topk_desc(_candidate_sums(sv1, sv2), P_TOPK)
            z = jnp.sum(jnp.exp(top - top[0:1]), axis=0, keepdims=True)
            rank2 = jnp.zeros_like(s2)
            for k in range(P_TOPK):
                rank2 = rank2 + jnp.where(sv2[k:k + 1] > s2, 1.0, 0.0)
                sv2_t[k] = jnp.where(mine, sv2[k:k + 1], sv2_t[k])
            r2_scr[hh] = rank2.astype(BF16)
            e2_scr[hh] = (jnp.exp(s2 - sv2[0:1]) / z).astype(BF16)
            tau_t = jnp.where(mine, top[P_TOPK - 1:P_TOPK], tau_t)
            thr1 = jnp.where(mine, sv1[P_TOPK - 1:P_TOPK], thr1)
            max1 = jnp.where(mine, sv1[0:1], max1)
        for lt in range(n_lt):
            ls = slice(lt * V7X_LANES, (lt + 1) * V7X_LANES)
            s1t = need_scr[lt].reshape(P_NKEYS, P_HEADS, V7X_LANES)
            count = jnp.zeros_like(s1t)
            for k in range(P_TOPK):
                count = count + jnp.where(s1t + sv2_t[k][None, :, ls] >= tau_t[None, :, ls], 1.0, 0.0)
            need_scr[lt] = jnp.where(s1t >= thr1[None, :, ls], count, 0.0).reshape(
                P_NKEYS * P_HEADS, V7X_LANES)
            e1_scr[lt] = jnp.exp(s1t - max1[None, :, ls]).reshape(P_NKEYS * P_HEADS, V7X_LANES)
        acc_scr[...] = jnp.zeros_like(acc_scr)
        a_scr[...] = jnp.zeros_like(a_scr)

    cprev = jnp.maximum(c - 1, 0)
    for ii in range(ipc):
        roff = ii * P_NKEYS
        hrow = pl.ds(pl.multiple_of((cprev * ipc + ii) * P_HEADS, P_HEADS), P_HEADS)
        for lt in range(n_lt):
            ls = slice(lt * V7X_LANES, (lt + 1) * V7X_LANES)
            w = [jnp.zeros((ROW_GROUP, V7X_LANES), BF16) for _ in range(P_NKEYS // ROW_GROUP)]
            need_t = need_scr[lt, hrow, :]
            e1_t = e1_scr[lt, hrow, :]
            zero = jnp.zeros((ROW_GROUP, V7X_LANES), BF16)
            for hh in range(P_HEADS):
                tile = (ROW_GROUP, V7X_LANES)
                need_b = jnp.broadcast_to(need_t[hh:hh + 1], tile).astype(BF16)
                e1_b = jnp.broadcast_to(e1_t[hh:hh + 1], tile).astype(BF16)
                for jg in range(P_NKEYS // ROW_GROUP):
                    js = slice(jg * ROW_GROUP, (jg + 1) * ROW_GROUP)
                    hit = r2_scr[hh, js, ls] < need_b
                    w[jg] = w[jg] + jnp.where(hit, e1_b * e2_scr[hh, js, ls], zero)
            for jg in range(P_NKEYS // ROW_GROUP):
                rows = slice(roff + jg * ROW_GROUP, roff + (jg + 1) * ROW_GROUP)
                a = a_scr[rows, ls]
                g = 0.5 * a * (1.0 + lax.erf(a * (2.0 ** -0.5)))
                m_scr[rows, ls] = g.astype(BF16) * w[jg]

    acc_scr[...] += jnp.dot(vt_ref[0], m_scr[...], preferred_element_type=F32)
    a_scr[...] = lax.dot_general(u_ref[...], t_scr[...], NT_DIMS, preferred_element_type=F32)

    @pl.when(c == n_chunks)
    def _finish():
        x = x_ref[...]
        y = x + mod_ref[0][5:6] * acc_scr[...].T
        if final_norm:
            ms = jnp.mean(y * y, axis=-1, keepdims=True)
            y = y * lax.rsqrt(ms + EPS) * fnw_ref[...]
        o_ref[...] = y


def _peer(x1, mod_l, wq, k1p, k2p, u_tab, vt_tab, fnw, n_blocks, final_norm, dims):
    bt, ec = PEER_BLOCK, PEER_ECHUNK
    nps, nbatch = dims["S"] // bt, dims["B"]
    n_chunks = vt_tab.shape[0]
    return pl.pallas_call(
        functools.partial(_peer_kernel, n_chunks=n_chunks, final_norm=final_norm),
        grid=(n_blocks, n_chunks + 1),
        in_specs=[pl.BlockSpec((bt, D_MODEL), lambda r, c: (r, 0)),
                  pl.BlockSpec((1, 6, D_MODEL), lambda r, c: (jnp.minimum(r // nps, nbatch), 0, 0)),
                  pl.BlockSpec((D_MODEL, P_HEADS * V7X_LANES), lambda r, c: (0, 0)),
                  pl.BlockSpec((P_NKEYS * P_HEADS, P_HEADS * V7X_LANES), lambda r, c: (0, 0)),
                  pl.BlockSpec((P_HEADS, P_NKEYS, V7X_LANES), lambda r, c: (0, 0, 0)),
                  pl.BlockSpec((ec, D_MODEL), lambda r, c: (jnp.minimum(c, n_chunks - 1), 0)),
                  pl.BlockSpec((1, D_MODEL, ec), lambda r, c: (jnp.maximum(c - 1, 0), 0, 0)),
                  pl.BlockSpec((1, D_MODEL), lambda r, c: (0, 0))],
        out_specs=pl.BlockSpec((bt, D_MODEL), lambda r, c: (r, 0)),
        out_shape=jax.ShapeDtypeStruct((n_blocks * bt, D_MODEL), F32),
        scratch_shapes=[pltpu.VMEM((bt, D_MODEL), BF16),
                        pltpu.VMEM((bt // V7X_LANES, P_NKEYS * P_HEADS, V7X_LANES), F32),
                        pltpu.VMEM((bt // V7X_LANES, P_NKEYS * P_HEADS, V7X_LANES), F32),
                        pltpu.VMEM((P_HEADS, P_NKEYS, bt), BF16),
                        pltpu.VMEM((P_HEADS, P_NKEYS, bt), BF16),
                        pltpu.VMEM((ec, bt), F32),
                        pltpu.VMEM((ec, bt), BF16),
                        pltpu.VMEM((D_MODEL, bt), F32)],
        compiler_params=_cparams(("arbitrary", "arbitrary")),
        name="peer",
    )(x1, mod_l, wq, k1p, k2p, u_tab, vt_tab, fnw)


def _rope_tables(s_len, nbatch, n_ctx_rows):
    t = jnp.arange(s_len, dtype=jnp.int32)
    row = (t // GRID_W).astype(F32)
    col = (t % GRID_W).astype(F32)
    axis_dim = HEAD_DIM // 2
    inv = ROPE_THETA ** (-jnp.arange(0, axis_dim, 2, dtype=F32) / axis_dim)
    ar = row[:, None] * inv[None, :]
    ac = col[:, None] * inv[None, :]
    ang = jnp.concatenate([ar, ar, ac, ac], axis=-1)
    cos, sin = jnp.cos(ang), jnp.sin(ang)
    quarter = jnp.arange(HEAD_DIM) // (HEAD_DIM // 4)
    first = (quarter % 2 == 0)[None, :]
    sin_a = jnp.where(first, -sin, 0.0)
    sin_b = jnp.where(first, 0.0, sin)

    def expand(tab, ctx_fill):
        lat = jnp.tile(jnp.concatenate([tab, tab], axis=-1), (nbatch, 1))
        return jnp.concatenate([lat, jnp.full((n_ctx_rows, V7X_LANES), ctx_fill, F32)], axis=0)

    return expand(cos, 1.0), expand(sin_a, 0.0), expand(sin_b, 0.0)


def _padded_keys(keys_l):
    z = jnp.zeros_like(keys_l[:, 0])
    k0 = jnp.concatenate([keys_l[:, 0], z], axis=-1).transpose(1, 0, 2)
    eye = jnp.eye(P_HEADS, dtype=keys_l.dtype)
    k1p = (k0[:, :, None, :] * eye[None, :, :, None]).reshape(
        P_NKEYS * P_HEADS, P_HEADS * V7X_LANES)
    k2p = jnp.concatenate([z, keys_l[:, 1]], axis=-1)
    return k1p.astype(BF16), k2p.astype(BF16)


def _chunked_transpose(v_tab):
    n_exp, d = v_tab.shape
    return v_tab.astype(BF16).reshape(n_exp // PEER_ECHUNK, PEER_ECHUNK, d).transpose(0, 2, 1)


def kernel(x, c, ctx, c_ctx, w_ada, b_ada, w_in, lam_q1, lam_k1, lam_q2, lam_k2, subln_w, q_norm_w, k_norm_w, conv_w, conv_b, conv_ln_w, conv_ln_b, w_branch_a, w_branch_b, w_branch_c, w_out, peer_wq, peer_keys, peer_u, peer_v, final_norm_w):
    nbatch, s_len, d = x.shape
    ctx_len = ctx.shape[1]
    depth = w_ada.shape[0]
    n_lat, n_ctx = nbatch * s_len, nbatch * ctx_len
    dims = {"B": nbatch, "S": s_len, "CTX": ctx_len, "T": n_lat + n_ctx}
    assert d == D_MODEL and nbatch + 1 <= MOD_ROWS
    assert s_len % PEER_BLOCK == 0 and ctx_len % ATT_TQ == 0 and ctx_len % TOK_BLOCK == 0
    assert n_ctx % PEER_BLOCK == 0 and n_lat % ctx_len == 0

    cvec = jnp.zeros((MOD_ROWS, d), F32).at[:nbatch].set(c).at[nbatch].set(c_ctx)
    mod = _ada_rows(cvec, w_ada, b_ada).reshape(depth, MOD_ROWS, 6, d)
    tabs = _rope_tables(s_len, nbatch, n_ctx)
    tile2 = lambda v: jnp.concatenate([v, v], axis=-1)[None, :]

    xy = jnp.concatenate([x.reshape(n_lat, d), ctx.reshape(n_ctx, d)], axis=0)
    for l in range(depth):
        last = l == depth - 1
        lam_init = 0.8 - 0.6 * math.exp(-0.3 * l)
        w_bf = w_in[l].astype(BF16)
        lamv = jnp.stack([lam_q1[l], lam_k1[l], lam_q2[l], lam_k2[l]], axis=0)

        qa, ka, va, qb, kbd, vbd, u = _inproj(
            xy, mod[l], w_bf[:, :OFF_GATE], tabs, tile2(q_norm_w[l]), tile2(k_norm_w[l]), dims)
        oa, ob = _attention(qa, ka, va, qb, kbd, vbd, lamv, subln_w[l][None, :], lam_init,
                            not last, dims)
        n_rows = n_lat if last else n_lat + n_ctx
        cw = jnp.concatenate([conv_w[l], jnp.zeros((1, C_WIDTH), F32)], axis=0)
        x1 = _merge(xy, mod[l], oa, ob, u, cw, conv_b[l][None, :], conv_ln_w[l][None, :],
                    conv_ln_b[l][None, :], w_bf[:, OFF_GATE:], w_branch_a[l].astype(BF16),
                    w_branch_b[l].astype(BF16), w_branch_c[l].astype(BF16),
                    w_out[l].astype(BF16), n_rows // TOK_BLOCK, dims)
        k1p, k2p = _padded_keys(peer_keys[l])
        xy = _peer(x1, mod[l], peer_wq[l].astype(BF16), k1p, k2p,
                   peer_u[l].astype(BF16), _chunked_transpose(peer_v[l]), final_norm_w[None, :],
                   n_rows // PEER_BLOCK, last, dims)
    return xy.reshape(nbatch, s_len, d)
```

```python
import functools
import math

import jax
import jax.numpy as jnp
from jax import lax
from jax.experimental import pallas as pl
from jax.experimental.pallas import tpu as pltpu

F32 = jnp.float32
BF16 = jnp.bfloat16

D_MODEL = 1024
DEPTH = 2
GRID_W = 64
HEAD_DIM = 64
ROPE_THETA = 10000.0
EPS = 1e-6
A_HEADS = 4
B_KV = 2
C_WIDTH = 512
C_KW = 31
P_HEADS = 8
P_NKEYS = 128
P_TOPK = 16

COL_KA, COL_VA, COL_KB, COL_VB, COL_QA, COL_QB, COL_GLU = 512, 512, 128, 128, 512, 512, 1024
OFF_KA = 0
OFF_VA = OFF_KA + COL_KA
OFF_KB = OFF_VA + COL_VA
OFF_QA = OFF_KB + COL_KB + COL_VB
OFF_QB = OFF_QA + COL_QA
OFF_GLU = OFF_QB + COL_QB
OFF_GATE = OFF_GLU + COL_GLU

V7X_LANES = 128
V7X_VMEM_LIMIT = 56 * 1024 * 1024
HALO = 16

TOK_BLOCK = 256
ATT_TQ = 256
ATT_TK = 512
ATT_UNROLL = 8
PEER_BLOCK = 512
PEER_ECHUNK = 1024
MOD_ROWS = 8
ROW_GROUP = 16

Q_SCALE = (HEAD_DIM ** -0.5) * math.log2(math.e)
NEG_BIG = -1e30
NT_DIMS = (((1,), (1,)), ((), ()))


def _cparams(sem):
    return pltpu.CompilerParams(dimension_semantics=sem, vmem_limit_bytes=V7X_VMEM_LIMIT)


def _modulated(x, mod, shift_row, scale_row):
    ms = jnp.mean(x * x, axis=-1, keepdims=True)
    xn = x * lax.rsqrt(ms + EPS)
    return xn * (1.0 + mod[scale_row:scale_row + 1]) + mod[shift_row:shift_row + 1]


def _ada_kernel(c_ref, w_ref, b_ref, o_ref):
    c = c_ref[...]
    sc = c * jax.nn.sigmoid(c)
    o_ref[0] = jnp.dot(sc, w_ref[0], preferred_element_type=F32,
                       precision=lax.Precision.HIGHEST) + b_ref[0]


def _ada_rows(cvec, w_ada, b_ada):
    depth, d, n = w_ada.shape
    tn = 1536
    return pl.pallas_call(
        _ada_kernel,
        grid=(depth, n // tn),
        in_specs=[pl.BlockSpec((MOD_ROWS, d), lambda l, j: (0, 0)),
                  pl.BlockSpec((1, d, tn), lambda l, j: (l, 0, j)),
                  pl.BlockSpec((1, 1, tn), lambda l, j: (l, 0, j))],
        out_specs=pl.BlockSpec((1, MOD_ROWS, tn), lambda l, j: (l, 0, j)),
        out_shape=jax.ShapeDtypeStruct((depth, MOD_ROWS, n), F32),
        compiler_params=_cparams(("arbitrary", "arbitrary")),
        name="ada_rows",
    )(cvec, w_ada, b_ada.reshape(depth, 1, n))


def _inproj_kernel(x_ref, mod_ref, w_ref, cos_ref, sa_ref, sb_ref, qnw_ref, knw_ref,
                   qa_ref, ka_ref, va_ref, qb_ref, kbd_ref, vbd_ref, u_ref):
    bt = x_ref.shape[0]
    h = _modulated(x_ref[...], mod_ref[0], 0, 1).astype(BF16)
    cos, sa, sb = cos_ref[...], sa_ref[...], sb_ref[...]
    lo = lax.broadcasted_iota(jnp.int32, (bt, V7X_LANES), 1) < HEAD_DIM

    def proj(c0, width):
        return jnp.dot(h, w_ref[:, c0:c0 + width], preferred_element_type=F32)

    def rope(x):
        return x * cos + pltpu.roll(x, V7X_LANES - 16, 1) * sa + pltpu.roll(x, 16, 1) * sb

    def headnorm(x, w):
        x2 = x * x
        s_lo = jnp.sum(jnp.where(lo, x2, 0.0), axis=-1, keepdims=True)
        s_hi = jnp.sum(jnp.where(lo, 0.0, x2), axis=-1, keepdims=True)
        r = jnp.where(lo, lax.rsqrt(s_lo / HEAD_DIM + EPS), lax.rsqrt(s_hi / HEAD_DIM + EPS))
        return x * r * w

    def tile(p, c):
        return p[:, c * V7X_LANES:(c + 1) * V7X_LANES]

    p = proj(OFF_KA, COL_KA)
    for c in range(A_HEADS):
        ka_ref[c] = rope(tile(p, c)).astype(BF16)

    p = proj(OFF_VA, COL_VA)
    for c in range(A_HEADS):
        va_ref[c] = tile(p, c).astype(BF16)

    p = proj(OFF_KB, COL_KB + COL_VB)
    kb = rope(headnorm(tile(p, 0), knw_ref[...]))
    kb_sw = pltpu.roll(kb, HEAD_DIM, 1)
    kbd_ref[0] = jnp.where(lo, kb, kb_sw).astype(BF16)
    kbd_ref[1] = jnp.where(lo, kb_sw, kb).astype(BF16)
    vb = tile(p, 1)
    vb_sw = pltpu.roll(vb, HEAD_DIM, 1)
    vbd_ref[0] = jnp.where(lo, vb, vb_sw).astype(BF16)
    vbd_ref[1] = jnp.where(lo, vb_sw, vb).astype(BF16)

    p = proj(OFF_QA, COL_QA)
    for c in range(COL_QA // V7X_LANES):
        qa_ref[:, c * V7X_LANES:(c + 1) * V7X_LANES] = (rope(tile(p, c)) * Q_SCALE).astype(BF16)

    p = proj(OFF_QB, COL_QB)
    qnw = qnw_ref[...]
    for c in range(COL_QB // V7X_LANES):
        qb_ref[:, c * V7X_LANES:(c + 1) * V7X_LANES] = (
            rope(headnorm(tile(p, c), qnw)) * Q_SCALE).astype(BF16)

    p = proj(OFF_GLU, COL_GLU)
    u_ref[...] = p[:, :C_WIDTH] * jax.nn.sigmoid(p[:, C_WIDTH:])


def _inproj(xy, mod_l, w_main, tabs, qnw, knw, dims):
    t_rows, nps, nbatch = dims["T"], dims["S"] // TOK_BLOCK, dims["B"]
    bt = TOK_BLOCK
    row = lambda r: (r, 0)
    full = lambda r: (0, 0)
    out_shapes = (
        jax.ShapeDtypeStruct((t_rows, COL_QA), BF16),
        jax.ShapeDtypeStruct((A_HEADS, t_rows, V7X_LANES), BF16),
        jax.ShapeDtypeStruct((A_HEADS, t_rows, V7X_LANES), BF16),
        jax.ShapeDtypeStruct((t_rows, COL_QB), BF16),
        jax.ShapeDtypeStruct((B_KV, t_rows, V7X_LANES), BF16),
        jax.ShapeDtypeStruct((B_KV, t_rows, V7X_LANES), BF16),
        jax.ShapeDtypeStruct((t_rows, C_WIDTH), F32),
    )
    grp = pl.BlockSpec((B_KV, bt, V7X_LANES), lambda r: (0, r, 0))
    heads = pl.BlockSpec((A_HEADS, bt, V7X_LANES), lambda r: (0, r, 0))
    return pl.pallas_call(
        _inproj_kernel,
        grid=(t_rows // bt,),
        in_specs=[pl.BlockSpec((bt, D_MODEL), row),
                  pl.BlockSpec((1, 6, D_MODEL), lambda r: (jnp.minimum(r // nps, nbatch), 0, 0)),
                  pl.BlockSpec((D_MODEL, OFF_GATE), full),
                  pl.BlockSpec((bt, V7X_LANES), row),
                  pl.BlockSpec((bt, V7X_LANES), row),
                  pl.BlockSpec((bt, V7X_LANES), row),
                  pl.BlockSpec((1, V7X_LANES), full),
                  pl.BlockSpec((1, V7X_LANES), full)],
        out_specs=(pl.BlockSpec((bt, COL_QA), row), heads, heads, pl.BlockSpec((bt, COL_QB), row),
                   grp, grp, pl.BlockSpec((bt, C_WIDTH), row)),
        out_shape=out_shapes,
        compiler_params=_cparams(("arbitrary",)),
        name="inproj",
    )(xy, mod_l, w_main, tabs[0], tabs[1], tabs[2], qnw, knw)


def _kv_unroll(n_chunks):
    return ATT_UNROLL if n_chunks % ATT_UNROLL == 0 else 1


def _softmax_step(s, v, m, l, acc):
    mn = jnp.maximum(m, jnp.max(s, axis=-1, keepdims=True))
    p = jnp.exp2(s - mn)
    alpha = jnp.exp2(m - mn)
    l = alpha * l + jnp.sum(p, axis=-1, keepdims=True)
    acc = alpha * acc + jnp.dot(p.astype(BF16), v, preferred_element_type=F32)
    return mn, l, acc


def _flash_rows(q, load_lat, kc, vc, *, nq_lat, n_lat_chunks, tk):
    rows = q.shape[0]

    def step(k, v, carry):
        m, l, a = carry
        s = lax.dot_general(q, k, NT_DIMS, preferred_element_type=F32)
        return _softmax_step(s, v, m, l, a)

    unroll = _kv_unroll(n_lat_chunks)

    def lat_body(j, carry):
        for r in range(unroll):
            off = pl.multiple_of((j * unroll + r) * tk, tk)
            carry = step(*load_lat(off), carry)
        return carry

    carry = (jnp.full((rows, 1), NEG_BIG, F32), jnp.zeros((rows, 1), F32),
             jnp.zeros((rows, V7X_LANES), F32))
    n_lat = jnp.where(pl.program_id(2) < nq_lat, n_lat_chunks // unroll, 0)
    carry = lax.fori_loop(0, n_lat, lat_body, carry)
    _, l, a = step(kc, vc, carry)
    return a / l


def _attn_a_kernel(q_ref, kl_ref, vl_ref, kc_ref, vc_ref, lamv_ref, sw_ref, o_ref, *,
                   nq_lat, n_lat_chunks, tk, lam_init):
    tq = q_ref.shape[0]
    q = q_ref[...]
    lo = lax.broadcasted_iota(jnp.int32, (tq, V7X_LANES), 1) < HEAD_DIM
    zero = jnp.zeros_like(q)
    q12 = jnp.concatenate([jnp.where(lo, q, zero), jnp.where(lo, zero, q)], axis=0)
    o12 = _flash_rows(q12, lambda off: (kl_ref[0, pl.ds(off, tk), :], vl_ref[0, pl.ds(off, tk), :]),
                      kc_ref[0], vc_ref[0], nq_lat=nq_lat, n_lat_chunks=n_lat_chunks, tk=tk)

    lv = lamv_ref[...]
    lam = (jnp.exp(jnp.sum(lv[0:1] * lv[1:2], axis=-1, keepdims=True))
           - jnp.exp(jnp.sum(lv[2:3] * lv[3:4], axis=-1, keepdims=True)) + lam_init)
    o = o12[:tq] - lam * o12[tq:]
    ms = jnp.mean(o * o, axis=-1, keepdims=True)
    o_ref[...] = (o * lax.rsqrt(ms + EPS) * sw_ref[...] * (1.0 - lam_init)).astype(BF16)


def _attn_b_kernel(q_ref, kl_ref, vl_ref, kc_ref, vc_ref, o_ref, *, nq_lat, n_lat_chunks, tk):
    tq = q_ref.shape[0]
    lo = lax.broadcasted_iota(jnp.int32, (tq, V7X_LANES), 1) < HEAD_DIM
    parts = []
    for c in range(2):
        qc = q_ref[:, c * V7X_LANES:(c + 1) * V7X_LANES]
        zero = jnp.zeros_like(qc)
        parts += [jnp.where(lo, qc, zero), jnp.where(lo, zero, qc)]
    q4 = jnp.concatenate(parts, axis=0)
    o = _flash_rows(q4, lambda off: (kl_ref[0, pl.ds(off, tk), :], vl_ref[0, pl.ds(off, tk), :]),
                    kc_ref[0], vc_ref[0], nq_lat=nq_lat, n_lat_chunks=n_lat_chunks, tk=tk)
    for c in range(2):
        o_ref[:, c * V7X_LANES:(c + 1) * V7X_LANES] = jnp.where(
            lo, o[(2 * c) * tq:(2 * c + 1) * tq], o[(2 * c + 1) * tq:(2 * c + 2) * tq]).astype(BF16)


def _q_block_index(b, qi, nq_lat, nq_ctx, nbatch):
    return jnp.where(qi < nq_lat, b * nq_lat + qi, nbatch * nq_lat + b * nq_ctx + (qi - nq_lat))


def _attention(qa, ka, va, qb, kbd, vbd, lamv, subln_w, lam_init, with_ctx_queries, dims):
    s_len, ctx_len, nbatch, t_rows = dims["S"], dims["CTX"], dims["B"], dims["T"]
    tq, tk = ATT_TQ, min(ATT_TK, s_len)
    nq_lat, nq_ctx = s_len // tq, ctx_len // tq
    nq = nq_lat + (nq_ctx if with_ctx_queries else 0)
    ctx_blk0 = (nbatch * s_len) // ctx_len
    qidx = functools.partial(_q_block_index, nq_lat=nq_lat, nq_ctx=nq_ctx, nbatch=nbatch)
    sem = ("arbitrary", "arbitrary", "arbitrary")

    oa = pl.pallas_call(
        functools.partial(_attn_a_kernel, nq_lat=nq_lat, n_lat_chunks=s_len // tk, tk=tk,
                          lam_init=lam_init),
        grid=(nbatch, A_HEADS, nq),
        in_specs=[pl.BlockSpec((tq, V7X_LANES), lambda b, h, qi: (qidx(b, qi), h)),
                  pl.BlockSpec((1, s_len, V7X_LANES), lambda b, h, qi: (h, b, 0)),
                  pl.BlockSpec((1, s_len, V7X_LANES), lambda b, h, qi: (h, b, 0)),
                  pl.BlockSpec((1, ctx_len, V7X_LANES), lambda b, h, qi: (h, ctx_blk0 + b, 0)),
                  pl.BlockSpec((1, ctx_len, V7X_LANES), lambda b, h, qi: (h, ctx_blk0 + b, 0)),
                  pl.BlockSpec((4, HEAD_DIM), lambda b, h, qi: (0, 0)),
                  pl.BlockSpec((1, V7X_LANES), lambda b, h, qi: (0, 0))],
        out_specs=pl.BlockSpec((tq, V7X_LANES), lambda b, h, qi: (qidx(b, qi), h)),
        out_shape=jax.ShapeDtypeStruct((t_rows, A_HEADS * V7X_LANES), BF16),
        compiler_params=_cparams(sem),
        name="attn_a",
    )(qa, ka, va, ka, va, lamv, subln_w)

    gw = 2 * V7X_LANES
    ob = pl.pallas_call(
        functools.partial(_attn_b_kernel, nq_lat=nq_lat, n_lat_chunks=s_len // tk, tk=tk),
        grid=(nbatch, B_KV, nq),
        in_specs=[pl.BlockSpec((tq, gw), lambda b, g, qi: (qidx(b, qi), g)),
                  pl.BlockSpec((1, s_len, V7X_LANES), lambda b, g, qi: (g, b, 0)),
                  pl.BlockSpec((1, s_len, V7X_LANES), lambda b, g, qi: (g, b, 0)),
                  pl.BlockSpec((1, ctx_len, V7X_LANES), lambda b, g, qi: (g, ctx_blk0 + b, 0)),
                  pl.BlockSpec((1, ctx_len, V7X_LANES), lambda b, g, qi: (g, ctx_blk0 + b, 0))],
        out_specs=pl.BlockSpec((tq, gw), lambda b, g, qi: (qidx(b, qi), g)),
        out_shape=jax.ShapeDtypeStruct((t_rows, B_KV * gw), BF16),
        compiler_params=_cparams(sem),
        name="attn_b",
    )(qb, kbd, vbd, kbd, vbd)
    return oa, ob


def _merge_kernel(x_ref, mod_ref, oa_ref, ob_ref, u_ref, up_ref, un_ref, cw_ref, cb_ref,
                  lnw_ref, lnb_ref, wg_ref, wa_ref, wb_ref, wc_ref, wo_ref, o_ref, uext_ref, *,
                  nps, ncps, n_lat_blocks):
    bt = x_ref.shape[0]
    r = pl.program_id(0)
    is_lat = r < n_lat_blocks
    pos = jnp.where(is_lat, r % nps, (r - n_lat_blocks) % ncps)
    last_pos = jnp.where(is_lat, nps - 1, ncps - 1)
    keep_prev = (pos != 0).astype(F32)
    keep_next = (pos != last_pos).astype(F32)

    x = x_ref[...]
    mod = mod_ref[0]
    h = _modulated(x, mod, 0, 1).astype(BF16)

    uext_ref[0:HALO, :] = up_ref[...] * keep_prev
    uext_ref[HALO:HALO + bt, :] = u_ref[...]
    uext_ref[HALO + bt:2 * HALO + bt, :] = un_ref[...] * keep_next
    cw = cw_ref[...]
    y = jnp.zeros((bt, C_WIDTH), F32) + cb_ref[...]
    base = HALO - C_KW // 2
    for k in range(C_KW):
        y = y + cw[k:k + 1, :] * uext_ref[base + k:base + k + bt, :]
    mu = jnp.mean(y, axis=-1, keepdims=True)
    yc = y - mu
    var = jnp.mean(yc * yc, axis=-1, keepdims=True)
    yn = yc * lax.rsqrt(var + EPS) * lnw_ref[...] + lnb_ref[...]
    oc = (yn * jax.nn.sigmoid(yn)).astype(BF16)

    def gate(i):
        logits = jnp.dot(h, wg_ref[:, i * D_MODEL:(i + 1) * D_MODEL], preferred_element_type=F32)
        return jax.nn.sigmoid(logits)

    merged = gate(0) * jnp.dot(oa_ref[...], wa_ref[...], preferred_element_type=F32)
    merged = merged + gate(1) * jnp.dot(ob_ref[...], wb_ref[...], preferred_element_type=F32)
    merged = merged + gate(2) * jnp.dot(oc, wc_ref[...], preferred_element_type=F32)
    out = jnp.dot(merged.astype(BF16), wo_ref[...], preferred_element_type=F32)
    o_ref[...] = x + mod[2:3] * out


def _merge(xy, mod_l, oa, ob, u, cw, cb, lnw, lnb, wg, wa, wb, wc, wo, n_blocks, dims):
    bt = TOK_BLOCK
    t_rows, nps, nbatch = dims["T"], dims["S"] // bt, dims["B"]
    hpb = bt // HALO
    n_halo = t_rows // HALO
    row = lambda r: (r, 0)
    full = lambda r: (0, 0)
    return pl.pallas_call(
        functools.partial(_merge_kernel, nps=nps, ncps=dims["CTX"] // bt,
                          n_lat_blocks=nbatch * nps),
        grid=(n_blocks,),
        in_specs=[pl.BlockSpec((bt, D_MODEL), row),
                  pl.BlockSpec((1, 6, D_MODEL), lambda r: (jnp.minimum(r // nps, nbatch), 0, 0)),
                  pl.BlockSpec((bt, 4 * V7X_LANES), row),
                  pl.BlockSpec((bt, 4 * V7X_LANES), row),
                  pl.BlockSpec((bt, C_WIDTH), row),
                  pl.BlockSpec((HALO, C_WIDTH), lambda r: (jnp.maximum(r * hpb - 1, 0), 0)),
                  pl.BlockSpec((HALO, C_WIDTH), lambda r: (jnp.minimum((r + 1) * hpb, n_halo - 1), 0)),
                  pl.BlockSpec((C_KW + 1, C_WIDTH), full),
                  pl.BlockSpec((1, C_WIDTH), full),
                  pl.BlockSpec((1, C_WIDTH), full),
                  pl.BlockSpec((1, C_WIDTH), full),
                  pl.BlockSpec((D_MODEL, 3 * D_MODEL), full),
                  pl.BlockSpec((C_WIDTH, D_MODEL), full),
                  pl.BlockSpec((C_WIDTH, D_MODEL), full),
                  pl.BlockSpec((C_WIDTH, D_MODEL), full),
                  pl.BlockSpec((D_MODEL, D_MODEL), full)],
        out_specs=pl.BlockSpec((bt, D_MODEL), row),
        out_shape=jax.ShapeDtypeStruct((n_blocks * bt, D_MODEL), F32),
        scratch_shapes=[pltpu.VMEM((bt + 2 * HALO, C_WIDTH), F32)],
        compiler_params=_cparams(("arbitrary",)),
        name="merge",
    )(xy, mod_l, oa, ob, u, u, u, cw, cb, lnw, lnb, wg, wa, wb, wc, wo)


def _topk_desc(s, k):
    row = lax.broadcasted_iota(jnp.int32, (k, s.shape[1]), 0)

    def body(i, carry):
        work, vals = carry
        m = jnp.max(work, axis=0, keepdims=True)
        vals = jnp.where(row == i, m, vals)
        work = jnp.where(work == m, -jnp.inf, work)
        return work, vals

    _, vals = lax.fori_loop(0, k, body, (s, jnp.zeros((k, s.shape[1]), F32)))
    return vals


def _candidate_sums(sv1, sv2):
    half = P_TOPK // 2
    lead, rest = sv1[:half], sv1[half:]
    row = lax.broadcasted_iota(jnp.int32, lead.shape, 0)
    tiles = [lead + sv2[0:1], rest + sv2[0:1], lead + sv2[1:2]]
    for k2 in range(2, half):
        tiles.append(jnp.where(row < P_TOPK // (k2 + 1), lead + sv2[k2:k2 + 1], -jnp.inf))
    tiles.append(sv1[0:1] + sv2[half:])
    return jnp.concatenate(tiles, axis=0)


def _peer_select_kernel(x_ref, mod_ref, wq_ref, k1_ref, k2_ref,
                        t_ref, need_ref, e1_ref, r2_ref, e2_ref):
    bt = x_ref.shape[0]
    n_lt = bt // V7X_LANES
    hb = _modulated(x_ref[...], mod_ref[0], 3, 4).astype(BF16)
    t_ref[...] = hb
    q = jnp.dot(hb, wq_ref[...], preferred_element_type=F32).astype(BF16)
    s1p = lax.dot_general(k1_ref[...], q, NT_DIMS, preferred_element_type=F32)
    for lt in range(n_lt):
        need_ref[0, lt] = s1p[:, lt * V7X_LANES:(lt + 1) * V7X_LANES]
    head_row = lax.broadcasted_iota(jnp.int32, (P_HEADS, bt), 0)
    zero_t = jnp.zeros((P_HEADS, bt), F32)
    thr1, max1, tau_t = zero_t, zero_t, zero_t
    sv2_t = [zero_t] * P_TOPK
    for hh in range(P_HEADS):
        mine = head_row == hh
        qh = q[:, hh * V7X_LANES:(hh + 1) * V7X_LANES]
        s1 = jnp.concatenate([need_ref[0, lt, pl.ds(hh, P_NKEYS, stride=P_HEADS), :]
                              for lt in range(n_lt)], axis=1)
        s2 = lax.dot_general(k2_ref[hh], qh, NT_DIMS, preferred_element_type=F32)
        sv1 = _topk_desc(s1, P_TOPK)
        sv2 = _topk_desc(s2, P_TOPK)
        top = _topk_desc(_candidate_sums(sv1, sv2), P_TOPK)
        z = jnp.sum(jnp.exp(top - top[0:1]), axis=0, keepdims=True)
        rank2 = jnp.zeros_like(s2)
        for k in range(P_TOPK):
            rank2 = rank2 + jnp.where(sv2[k:k + 1] > s2, 1.0, 0.0)
            sv2_t[k] = jnp.where(mine, sv2[k:k + 1], sv2_t[k])
        r2_ref[0, hh] = rank2.astype(BF16)
        e2_ref[0, hh] = (jnp.exp(s2 - sv2[0:1]) / z).astype(BF16)
        tau_t = jnp.where(mine, top[P_TOPK - 1:P_TOPK], tau_t)
        thr1 = jnp.where(mine, sv1[P_TOPK - 1:P_TOPK], thr1)
        max1 = jnp.where(mine, sv1[0:1], max1)
    for lt in range(n_lt):
        ls = slice(lt * V7X_LANES, (lt + 1) * V7X_LANES)
        s1t = need_ref[0, lt].reshape(P_NKEYS, P_HEADS, V7X_LANES)
        count = jnp.zeros_like(s1t)
        for k in range(P_TOPK):
            count = count + jnp.where(s1t + sv2_t[k][None, :, ls] >= tau_t[None, :, ls], 1.0, 0.0)
        need_ref[0, lt] = jnp.where(s1t >= thr1[None, :, ls], count, 0.0).reshape(
            P_NKEYS * P_HEADS, V7X_LANES)
        e1_ref[0, lt] = jnp.exp(s1t - max1[None, :, ls]).reshape(P_NKEYS * P_HEADS, V7X_LANES)


def _peer_dense_kernel(x_ref, mod_ref, t_ref, need_ref, e1_ref, r2_ref, e2_ref, u_ref, vt_ref,
                       fnw_ref, o_ref, a_scr, m_scr, acc_scr, *, n_chunks, final_norm):
    bt = x_ref.shape[0]
    c = pl.program_id(1)
    ipc = u_ref.shape[0] // P_NKEYS
    n_lt = bt // V7X_LANES

    @pl.when(c == 0)
    def _init():
        acc_scr[...] = jnp.zeros_like(acc_scr)
        a_scr[...] = jnp.zeros_like(a_scr)

    cprev = jnp.maximum(c - 1, 0)
    for ii in range(ipc):
        roff = ii * P_NKEYS
        hrow = pl.ds(pl.multiple_of((cprev * ipc + ii) * P_HEADS, P_HEADS), P_HEADS)
        for lt in range(n_lt):
            ls = slice(lt * V7X_LANES, (lt + 1) * V7X_LANES)
            w = [jnp.zeros((ROW_GROUP, V7X_LANES), BF16) for _ in range(P_NKEYS // ROW_GROUP)]
            need_t = need_ref[0, lt, hrow, :]
            e1_t = e1_ref[0, lt, hrow, :]
            zero = jnp.zeros((ROW_GROUP, V7X_LANES), BF16)
            for hh in range(P_HEADS):
                tile = (ROW_GROUP, V7X_LANES)
                need_b = jnp.broadcast_to(need_t[hh:hh + 1], tile).astype(BF16)
                e1_b = jnp.broadcast_to(e1_t[hh:hh + 1], tile).astype(BF16)
                for jg in range(P_NKEYS // ROW_GROUP):
                    js = slice(jg * ROW_GROUP, (jg + 1) * ROW_GROUP)
                    hit = r2_ref[0, hh, js, ls] < need_b
                    w[jg] = w[jg] + jnp.where(hit, e1_b * e2_ref[0, hh, js, ls], zero)
            for jg in range(P_NKEYS // ROW_GROUP):
                rows = slice(roff + jg * ROW_GROUP, roff + (jg + 1) * ROW_GROUP)
                a = a_scr[rows, ls]
                g = 0.5 * a * (1.0 + lax.erf(a * (2.0 ** -0.5)))
                m_scr[rows, ls] = g.astype(BF16) * w[jg]

    acc_scr[...] += jnp.dot(vt_ref[0], m_scr[...], preferred_element_type=F32)
    a_scr[...] = lax.dot_general(u_ref[...], t_ref[...], NT_DIMS, preferred_element_type=F32)

    @pl.when(c == n_chunks)
    def _finish():
        x = x_ref[...]
        y = x + mod_ref[0][5:6] * acc_scr[...].T
        if final_norm:
            ms = jnp.mean(y * y, axis=-1, keepdims=True)
            y = y * lax.rsqrt(ms + EPS) * fnw_ref[...]
        o_ref[...] = y


def _peer(x1, mod_l, wq, k1p, k2p, u_tab, vt_tab, fnw, n_blocks, final_norm, dims):
    bt, ec = PEER_BLOCK, PEER_ECHUNK
    nps, nbatch = dims["S"] // bt, dims["B"]
    n_chunks = vt_tab.shape[0]
    n_lt = bt // V7X_LANES
    rows_ih = P_NKEYS * P_HEADS
    mod_spec = pl.BlockSpec((1, 6, D_MODEL), lambda r, *_: (jnp.minimum(r // nps, nbatch), 0, 0))
    ih_shape = jax.ShapeDtypeStruct((n_blocks, n_lt, rows_ih, V7X_LANES), F32)
    hj_shape = jax.ShapeDtypeStruct((n_blocks, P_HEADS, P_NKEYS, bt), BF16)
    ih_spec = pl.BlockSpec((1, n_lt, rows_ih, V7X_LANES), lambda r, *_: (r, 0, 0, 0))
    hj_spec = pl.BlockSpec((1, P_HEADS, P_NKEYS, bt), lambda r, *_: (r, 0, 0, 0))
    tok_spec = pl.BlockSpec((bt, D_MODEL), lambda r, *_: (r, 0))

    t_mod, need, e1, rank2, e2 = pl.pallas_call(
        _peer_select_kernel,
        grid=(n_blocks,),
        in_specs=[tok_spec, mod_spec,
                  pl.BlockSpec((D_MODEL, P_HEADS * V7X_LANES), lambda r: (0, 0)),
                  pl.BlockSpec((rows_ih, P_HEADS * V7X_LANES), lambda r: (0, 0)),
                  pl.BlockSpec((P_HEADS, P_NKEYS, V7X_LANES), lambda r: (0, 0, 0))],
        out_specs=(tok_spec, ih_spec, ih_spec, hj_spec, hj_spec),
        out_shape=(jax.ShapeDtypeStruct((n_blocks * bt, D_MODEL), BF16),
                   ih_shape,
                   ih_shape,
                   hj_shape,
                   hj_shape),
        compiler_params=_cparams(("arbitrary",)),
        name="peer_select",
    )(x1, mod_l, wq, k1p, k2p)

    return pl.pallas_call(
        functools.partial(_peer_dense_kernel, n_chunks=n_chunks, final_norm=final_norm),
        grid=(n_blocks, n_chunks + 1),
        in_specs=[tok_spec, mod_spec, tok_spec, ih_spec, ih_spec, hj_spec, hj_spec,
                  pl.BlockSpec((ec, D_MODEL), lambda r, c: (jnp.minimum(c, n_chunks - 1), 0)),
                  pl.BlockSpec((1, D_MODEL, ec), lambda r, c: (jnp.maximum(c - 1, 0), 0, 0)),
                  pl.BlockSpec((1, D_MODEL), lambda r, c: (0, 0))],
        out_specs=tok_spec,
        out_shape=jax.ShapeDtypeStruct((n_blocks * bt, D_MODEL), F32),
        scratch_shapes=[pltpu.VMEM((ec, bt), F32),
                        pltpu.VMEM((ec, bt), BF16),
                        pltpu.VMEM((D_MODEL, bt), F32)],
        compiler_params=_cparams(("arbitrary", "arbitrary")),
        name="peer_dense",
    )(x1, mod_l, t_mod, need, e1, rank2, e2, u_tab, vt_tab, fnw)


def _rope_tables(s_len, nbatch, n_ctx_rows):
    t = jnp.arange(s_len, dtype=jnp.int32)
    row = (t // GRID_W).astype(F32)
    col = (t % GRID_W).astype(F32)
    axis_dim = HEAD_DIM // 2
    inv = ROPE_THETA ** (-jnp.arange(0, axis_dim, 2, dtype=F32) / axis_dim)
    ar = row[:, None] * inv[None, :]
    ac = col[:, None] * inv[None, :]
    ang = jnp.concatenate([ar, ar, ac, ac], axis=-1)
    cos, sin = jnp.cos(ang), jnp.sin(ang)
    quarter = jnp.arange(HEAD_DIM) // (HEAD_DIM // 4)
    first = (quarter % 2 == 0)[None, :]
    sin_a = jnp.where(first, -sin, 0.0)
    sin_b = jnp.where(first, 0.0, sin)

    def expand(tab, ctx_fill):
        lat = jnp.tile(jnp.concatenate([tab, tab], axis=-1), (nbatch, 1))
        return jnp.concatenate([lat, jnp.full((n_ctx_rows, V7X_LANES), ctx_fill, F32)], axis=0)

    return expand(cos, 1.0), expand(sin_a, 0.0), expand(sin_b, 0.0)


def _padded_keys(keys_l):
    z = jnp.zeros_like(keys_l[:, 0])
    k0 = jnp.concatenate([keys_l[:, 0], z], axis=-1).transpose(1, 0, 2)
    eye = jnp.eye(P_HEADS, dtype=keys_l.dtype)
    k1p = (k0[:, :, None, :] * eye[None, :, :, None]).reshape(
        P_NKEYS * P_HEADS, P_HEADS * V7X_LANES)
    k2p = jnp.concatenate([z, keys_l[:, 1]], axis=-1)
    return k1p.astype(BF16), k2p.astype(BF16)


def _chunked_transpose(v_tab):
    n_exp, d = v_tab.shape
    return v_tab.astype(BF16).reshape(n_exp // PEER_ECHUNK, PEER_ECHUNK, d).transpose(0, 2, 1)


def kernel(x, c, ctx, c_ctx, w_ada, b_ada, w_in, lam_q1, lam_k1, lam_q2, lam_k2, subln_w, q_norm_w, k_norm_w, conv_w, conv_b, conv_ln_w, conv_ln_b, w_branch_a, w_branch_b, w_branch_c, w_out, peer_wq, peer_keys, peer_u, peer_v, final_norm_w):
    nbatch, s_len, d = x.shape
    ctx_len = ctx.shape[1]
    depth = w_ada.shape[0]
    n_lat, n_ctx = nbatch * s_len, nbatch * ctx_len
    dims = {"B": nbatch, "S": s_len, "CTX": ctx_len, "T": n_lat + n_ctx}
    assert d == D_MODEL and nbatch + 1 <= MOD_ROWS
    assert s_len % PEER_BLOCK == 0 and ctx_len % ATT_TQ == 0 and ctx_len % TOK_BLOCK == 0
    assert n_ctx % PEER_BLOCK == 0 and n_lat % ctx_len == 0

    cvec = jnp.zeros((MOD_ROWS, d), F32).at[:nbatch].set(c).at[nbatch].set(c_ctx)
    mod = _ada_rows(cvec, w_ada, b_ada).reshape(depth, MOD_ROWS, 6, d)
    tabs = _rope_tables(s_len, nbatch, n_ctx)
    tile2 = lambda v: jnp.concatenate([v, v], axis=-1)[None, :]

    xy = jnp.concatenate([x.reshape(n_lat, d), ctx.reshape(n_ctx, d)], axis=0)
    for l in range(depth):
        last = l == depth - 1
        lam_init = 0.8 - 0.6 * math.exp(-0.3 * l)
        w_bf = w_in[l].astype(BF16)
        lamv = jnp.stack([lam_q1[l], lam_k1[l], lam_q2[l], lam_k2[l]], axis=0)

        qa, ka, va, qb, kbd, vbd, u = _inproj(
            xy, mod[l], w_bf[:, :OFF_GATE], tabs, tile2(q_norm_w[l]), tile2(k_norm_w[l]), dims)
        oa, ob = _attention(qa, ka, va, qb, kbd, vbd, lamv, subln_w[l][None, :], lam_init,
                            not last, dims)
        n_rows = n_lat if last else n_lat + n_ctx
        cw = jnp.concatenate([conv_w[l], jnp.zeros((1, C_WIDTH), F32)], axis=0)
        x1 = _merge(xy, mod[l], oa, ob, u, cw, conv_b[l][None, :], conv_ln_w[l][None, :],
                    conv_ln_b[l][None, :], w_bf[:, OFF_GATE:], w_branch_a[l].astype(BF16),
                    w_branch_b[l].astype(BF16), w_branch_c[l].astype(BF16),
                    w_out[l].astype(BF16), n_rows // TOK_BLOCK, dims)
        k1p, k2p = _padded_keys(peer_keys[l])
        xy = _peer(x1, mod[l], peer_wq[l].astype(BF16), k1p, k2p,
                   peer_u[l].astype(BF16), _chunked_transpose(peer_v[l]), final_norm_w[None, :],
                   n_rows // PEER_BLOCK, last, dims)
    return xy.reshape(nbatch, s_len, d)
```

```python
import functools
import math

import jax
import jax.numpy as jnp
from jax import lax
from jax.experimental import pallas as pl
from jax.experimental.pallas import tpu as pltpu

F32 = jnp.float32
BF16 = jnp.bfloat16

D_MODEL = 1024
DEPTH = 2
GRID_W = 64
HEAD_DIM = 64
ROPE_THETA = 10000.0
EPS = 1e-6
A_HEADS = 4
B_KV = 2
C_WIDTH = 512
C_KW = 31
P_HEADS = 8
P_NKEYS = 128
P_TOPK = 16

COL_KA, COL_VA, COL_KB, COL_VB, COL_QA, COL_QB, COL_GLU = 512, 512, 128, 128, 512, 512, 1024
OFF_KA = 0
OFF_VA = OFF_KA + COL_KA
OFF_KB = OFF_VA + COL_VA
OFF_QA = OFF_KB + COL_KB + COL_VB
OFF_QB = OFF_QA + COL_QA
OFF_GLU = OFF_QB + COL_QB
OFF_GATE = OFF_GLU + COL_GLU

V7X_LANES = 128
V7X_VMEM_LIMIT = 56 * 1024 * 1024
HALO = 16

TOK_BLOCK = 256
ATT_TQ = 256
ATT_TK = 512
ATT_UNROLL = 8
PEER_BLOCK = 512
PEER_ECHUNK = 1024
MOD_ROWS = 8
ROW_GROUP = 16

Q_SCALE = (HEAD_DIM ** -0.5) * math.log2(math.e)
NEG_BIG = -1e30
NT_DIMS = (((1,), (1,)), ((), ()))


def _cparams(sem):
    return pltpu.CompilerParams(dimension_semantics=sem, vmem_limit_bytes=V7X_VMEM_LIMIT)


def _modulated(x, mod, shift_row, scale_row):
    ms = jnp.mean(x * x, axis=-1, keepdims=True)
    xn = x * lax.rsqrt(ms + EPS)
    return xn * (1.0 + mod[scale_row:scale_row + 1]) + mod[shift_row:shift_row + 1]


def _ada_kernel(c_ref, w_ref, b_ref, o_ref):
    c = c_ref[...]
    sc = c * jax.nn.sigmoid(c)
    o_ref[0] = jnp.dot(sc, w_ref[0], preferred_element_type=F32,
                       precision=lax.Precision.HIGHEST) + b_ref[0]


def _ada_rows(cvec, w_ada, b_ada):
    depth, d, n = w_ada.shape
    tn = 1536
    return pl.pallas_call(
        _ada_kernel,
        grid=(depth, n // tn),
        in_specs=[pl.BlockSpec((MOD_ROWS, d), lambda l, j: (0, 0)),
                  pl.BlockSpec((1, d, tn), lambda l, j: (l, 0, j)),
                  pl.BlockSpec((1, 1, tn), lambda l, j: (l, 0, j))],
        out_specs=pl.BlockSpec((1, MOD_ROWS, tn), lambda l, j: (l, 0, j)),
        out_shape=jax.ShapeDtypeStruct((depth, MOD_ROWS, n), F32),
        compiler_params=_cparams(("arbitrary", "arbitrary")),
        name="ada_rows",
    )(cvec, w_ada, b_ada.reshape(depth, 1, n))


def _inproj_kernel(x_ref, mod_ref, w_ref, cos_ref, sa_ref, sb_ref, qnw_ref, knw_ref,
                   qa_ref, ka_ref, va_ref, qb_ref, kbd_ref, vbd_ref, u_ref):
    bt = x_ref.shape[0]
    h = _modulated(x_ref[...], mod_ref[0], 0, 1).astype(BF16)
    cos, sa, sb = cos_ref[...], sa_ref[...], sb_ref[...]
    lo = lax.broadcasted_iota(jnp.int32, (bt, V7X_LANES), 1) < HEAD_DIM

    def proj(c0, width):
        return jnp.dot(h, w_ref[:, c0:c0 + width], preferred_element_type=F32)

    def rope(x):
        return x * cos + pltpu.roll(x, V7X_LANES - 16, 1) * sa + pltpu.roll(x, 16, 1) * sb

    def headnorm(x, w):
        x2 = x * x
        s_lo = jnp.sum(jnp.where(lo, x2, 0.0), axis=-1, keepdims=True)
        s_hi = jnp.sum(jnp.where(lo, 0.0, x2), axis=-1, keepdims=True)
        r = jnp.where(lo, lax.rsqrt(s_lo / HEAD_DIM + EPS), lax.rsqrt(s_hi / HEAD_DIM + EPS))
        return x * r * w

    def tile(p, c):
        return p[:, c * V7X_LANES:(c + 1) * V7X_LANES]

    p = proj(OFF_KA, COL_KA)
    for c in range(A_HEADS):
        ka_ref[c] = rope(tile(p, c)).astype(BF16)

    p = proj(OFF_VA, COL_VA)
    for c in range(A_HEADS):
        va_ref[c] = tile(p, c).astype(BF16)

    p = proj(OFF_KB, COL_KB + COL_VB)
    kb = rope(headnorm(tile(p, 0), knw_ref[...]))
    kb_sw = pltpu.roll(kb, HEAD_DIM, 1)
    kbd_ref[0] = jnp.where(lo, kb, kb_sw).astype(BF16)
    kbd_ref[1] = jnp.where(lo, kb_sw, kb).astype(BF16)
    vb = tile(p, 1)
    vb_sw = pltpu.roll(vb, HEAD_DIM, 1)
    vbd_ref[0] = jnp.where(lo, vb, vb_sw).astype(BF16)
    vbd_ref[1] = jnp.where(lo, vb_sw, vb).astype(BF16)

    p = proj(OFF_QA, COL_QA)
    for c in range(COL_QA // V7X_LANES):
        qa_ref[:, c * V7X_LANES:(c + 1) * V7X_LANES] = (rope(tile(p, c)) * Q_SCALE).astype(BF16)

    p = proj(OFF_QB, COL_QB)
    qnw = qnw_ref[...]
    for c in range(COL_QB // V7X_LANES):
        qb_ref[:, c * V7X_LANES:(c + 1) * V7X_LANES] = (
            rope(headnorm(tile(p, c), qnw)) * Q_SCALE).astype(BF16)

    p = proj(OFF_GLU, COL_GLU)
    u_ref[...] = p[:, :C_WIDTH] * jax.nn.sigmoid(p[:, C_WIDTH:])


def _inproj(xy, mod_l, w_main, tabs, qnw, knw, dims):
    t_rows, nps, nbatch = dims["T"], dims["S"] // TOK_BLOCK, dims["B"]
    bt = TOK_BLOCK
    row = lambda r: (r, 0)
    full = lambda r: (0, 0)
    out_shapes = (
        jax.ShapeDtypeStruct((t_rows, COL_QA), BF16),
        jax.ShapeDtypeStruct((A_HEADS, t_rows, V7X_LANES), BF16),
        jax.ShapeDtypeStruct((A_HEADS, t_rows, V7X_LANES), BF16),
        jax.ShapeDtypeStruct((t_rows, COL_QB), BF16),
        jax.ShapeDtypeStruct((B_KV, t_rows, V7X_LANES), BF16),
        jax.ShapeDtypeStruct((B_KV, t_rows, V7X_LANES), BF16),
        jax.ShapeDtypeStruct((t_rows, C_WIDTH), F32),
    )
    grp = pl.BlockSpec((B_KV, bt, V7X_LANES), lambda r: (0, r, 0))
    heads = pl.BlockSpec((A_HEADS, bt, V7X_LANES), lambda r: (0, r, 0))
    return pl.pallas_call(
        _inproj_kernel,
        grid=(t_rows // bt,),
        in_specs=[pl.BlockSpec((bt, D_MODEL), row),
                  pl.BlockSpec((1, 6, D_MODEL), lambda r: (jnp.minimum(r // nps, nbatch), 0, 0)),
                  pl.BlockSpec((D_MODEL, OFF_GATE), full),
                  pl.BlockSpec((bt, V7X_LANES), row),
                  pl.BlockSpec((bt, V7X_LANES), row),
                  pl.BlockSpec((bt, V7X_LANES), row),
                  pl.BlockSpec((1, V7X_LANES), full),
                  pl.BlockSpec((1, V7X_LANES), full)],
        out_specs=(pl.BlockSpec((bt, COL_QA), row), heads, heads, pl.BlockSpec((bt, COL_QB), row),
                   grp, grp, pl.BlockSpec((bt, C_WIDTH), row)),
        out_shape=out_shapes,
        compiler_params=_cparams(("arbitrary",)),
        name="inproj",
    )(xy, mod_l, w_main, tabs[0], tabs[1], tabs[2], qnw, knw)


def _kv_unroll(n_chunks):
    return ATT_UNROLL if n_chunks % ATT_UNROLL == 0 else 1


def _softmax_step(s, v, m, l, acc):
    mn = jnp.maximum(m, jnp.max(s, axis=-1, keepdims=True))
    p = jnp.exp2(s - mn)
    alpha = jnp.exp2(m - mn)
    l = alpha * l + jnp.sum(p, axis=-1, keepdims=True)
    acc = alpha * acc + jnp.dot(p.astype(BF16), v, preferred_element_type=F32)
    return mn, l, acc


def _flash_rows(q, load_lat, kc, vc, *, nq_lat, n_lat_chunks, tk):
    rows = q.shape[0]

    def step(k, v, carry):
        m, l, a = carry
        s = lax.dot_general(q, k, NT_DIMS, preferred_element_type=F32)
        return _softmax_step(s, v, m, l, a)

    unroll = _kv_unroll(n_lat_chunks)

    def lat_body(j, carry):
        for r in range(unroll):
            off = pl.multiple_of((j * unroll + r) * tk, tk)
            carry = step(*load_lat(off), carry)
        return carry

    carry = (jnp.full((rows, 1), NEG_BIG, F32), jnp.zeros((rows, 1), F32),
             jnp.zeros((rows, V7X_LANES), F32))
    n_lat = jnp.where(pl.program_id(2) < nq_lat, n_lat_chunks // unroll, 0)
    carry = lax.fori_loop(0, n_lat, lat_body, carry)
    _, l, a = step(kc, vc, carry)
    return a / l


def _attn_a_kernel(q_ref, kl_ref, vl_ref, kc_ref, vc_ref, lamv_ref, sw_ref, o_ref, *,
                   nq_lat, n_lat_chunks, tk, lam_init):
    tq = q_ref.shape[0]
    q = q_ref[...]
    lo = lax.broadcasted_iota(jnp.int32, (tq, V7X_LANES), 1) < HEAD_DIM
    zero = jnp.zeros_like(q)
    q12 = jnp.concatenate([jnp.where(lo, q, zero), jnp.where(lo, zero, q)], axis=0)
    o12 = _flash_rows(q12, lambda off: (kl_ref[0, pl.ds(off, tk), :], vl_ref[0, pl.ds(off, tk), :]),
                      kc_ref[0], vc_ref[0], nq_lat=nq_lat, n_lat_chunks=n_lat_chunks, tk=tk)

    lv = lamv_ref[...]
    lam = (jnp.exp(jnp.sum(lv[0:1] * lv[1:2], axis=-1, keepdims=True))
           - jnp.exp(jnp.sum(lv[2:3] * lv[3:4], axis=-1, keepdims=True)) + lam_init)
    o = o12[:tq] - lam * o12[tq:]
    ms = jnp.mean(o * o, axis=-1, keepdims=True)
    o_ref[...] = (o * lax.rsqrt(ms + EPS) * sw_ref[...] * (1.0 - lam_init)).astype(BF16)


def _attn_b_kernel(q_ref, kl_ref, vl_ref, kc_ref, vc_ref, o_ref, *, nq_lat, n_lat_chunks, tk):
    tq = q_ref.shape[0]
    lo = lax.broadcasted_iota(jnp.int32, (tq, V7X_LANES), 1) < HEAD_DIM
    parts = []
    for c in range(2):
        qc = q_ref[:, c * V7X_LANES:(c + 1) * V7X_LANES]
        zero = jnp.zeros_like(qc)
        parts += [jnp.where(lo, qc, zero), jnp.where(lo, zero, qc)]
    q4 = jnp.concatenate(parts, axis=0)
    o = _flash_rows(q4, lambda off: (kl_ref[0, pl.ds(off, tk), :], vl_ref[0, pl.ds(off, tk), :]),
                    kc_ref[0], vc_ref[0], nq_lat=nq_lat, n_lat_chunks=n_lat_chunks, tk=tk)
    for c in range(2):
        o_ref[:, c * V7X_LANES:(c + 1) * V7X_LANES] = jnp.where(
            lo, o[(2 * c) * tq:(2 * c + 1) * tq], o[(2 * c + 1) * tq:(2 * c + 2) * tq]).astype(BF16)


def _q_block_index(b, qi, nq_lat, nq_ctx, nbatch):
    return jnp.where(qi < nq_lat, b * nq_lat + qi, nbatch * nq_lat + b * nq_ctx + (qi - nq_lat))


def _attention(qa, ka, va, qb, kbd, vbd, lamv, subln_w, lam_init, with_ctx_queries, dims):
    s_len, ctx_len, nbatch, t_rows = dims["S"], dims["CTX"], dims["B"], dims["T"]
    tq, tk = ATT_TQ, min(ATT_TK, s_len)
    nq_lat, nq_ctx = s_len // tq, ctx_len // tq
    nq = nq_lat + (nq_ctx if with_ctx_queries else 0)
    ctx_blk0 = (nbatch * s_len) // ctx_len
    qidx = functools.partial(_q_block_index, nq_lat=nq_lat, nq_ctx=nq_ctx, nbatch=nbatch)
    sem = ("arbitrary", "arbitrary", "arbitrary")

    oa = pl.pallas_call(
        functools.partial(_attn_a_kernel, nq_lat=nq_lat, n_lat_chunks=s_len // tk, tk=tk,
                          lam_init=lam_init),
        grid=(nbatch, A_HEADS, nq),
        in_specs=[pl.BlockSpec((tq, V7X_LANES), lambda b, h, qi: (qidx(b, qi), h)),
                  pl.BlockSpec((1, s_len, V7X_LANES), lambda b, h, qi: (h, b, 0)),
                  pl.BlockSpec((1, s_len, V7X_LANES), lambda b, h, qi: (h, b, 0)),
                  pl.BlockSpec((1, ctx_len, V7X_LANES), lambda b, h, qi: (h, ctx_blk0 + b, 0)),
                  pl.BlockSpec((1, ctx_len, V7X_LANES), lambda b, h, qi: (h, ctx_blk0 + b, 0)),
                  pl.BlockSpec((4, HEAD_DIM), lambda b, h, qi: (0, 0)),
                  pl.BlockSpec((1, V7X_LANES), lambda b, h, qi: (0, 0))],
        out_specs=pl.BlockSpec((tq, V7X_LANES), lambda b, h, qi: (qidx(b, qi), h)),
        out_shape=jax.ShapeDtypeStruct((t_rows, A_HEADS * V7X_LANES), BF16),
        compiler_params=_cparams(sem),
        name="attn_a",
    )(qa, ka, va, ka, va, lamv, subln_w)

    gw = 2 * V7X_LANES
    ob = pl.pallas_call(
        functools.partial(_attn_b_kernel, nq_lat=nq_lat, n_lat_chunks=s_len // tk, tk=tk),
        grid=(nbatch, B_KV, nq),
        in_specs=[pl.BlockSpec((tq, gw), lambda b, g, qi: (qidx(b, qi), g)),
                  pl.BlockSpec((1, s_len, V7X_LANES), lambda b, g, qi: (g, b, 0)),
                  pl.BlockSpec((1, s_len, V7X_LANES), lambda b, g, qi: (g, b, 0)),
                  pl.BlockSpec((1, ctx_len, V7X_LANES), lambda b, g, qi: (g, ctx_blk0 + b, 0)),
                  pl.BlockSpec((1, ctx_len, V7X_LANES), lambda b, g, qi: (g, ctx_blk0 + b, 0))],
        out_specs=pl.BlockSpec((tq, gw), lambda b, g, qi: (qidx(b, qi), g)),
        out_shape=jax.ShapeDtypeStruct((t_rows, B_KV * gw), BF16),
        compiler_params=_cparams(sem),
        name="attn_b",
    )(qb, kbd, vbd, kbd, vbd)
    return oa, ob


def _merge_kernel(x_ref, mod_ref, oa_ref, ob_ref, u_ref, up_ref, un_ref, cw_ref, cb_ref,
                  lnw_ref, lnb_ref, wg_ref, wa_ref, wb_ref, wc_ref, wo_ref, o_ref, uext_ref, *,
                  nps, ncps, n_lat_blocks):
    bt = x_ref.shape[0]
    r = pl.program_id(0)
    is_lat = r < n_lat_blocks
    pos = jnp.where(is_lat, r % nps, (r - n_lat_blocks) % ncps)
    last_pos = jnp.where(is_lat, nps - 1, ncps - 1)
    keep_prev = (pos != 0).astype(F32)
    keep_next = (pos != last_pos).astype(F32)

    x = x_ref[...]
    mod = mod_ref[0]
    h = _modulated(x, mod, 0, 1).astype(BF16)

    uext_ref[0:HALO, :] = up_ref[...] * keep_prev
    uext_ref[HALO:HALO + bt, :] = u_ref[...]
    uext_ref[HALO + bt:2 * HALO + bt, :] = un_ref[...] * keep_next
    cw = cw_ref[...]
    y = jnp.zeros((bt, C_WIDTH), F32) + cb_ref[...]
    base = HALO - C_KW // 2
    for k in range(C_KW):
        y = y + cw[k:k + 1, :] * uext_ref[base + k:base + k + bt, :]
    mu = jnp.mean(y, axis=-1, keepdims=True)
    yc = y - mu
    var = jnp.mean(yc * yc, axis=-1, keepdims=True)
    yn = yc * lax.rsqrt(var + EPS) * lnw_ref[...] + lnb_ref[...]
    oc = (yn * jax.nn.sigmoid(yn)).astype(BF16)

    def gate(i):
        logits = jnp.dot(h, wg_ref[:, i * D_MODEL:(i + 1) * D_MODEL], preferred_element_type=F32)
        return jax.nn.sigmoid(logits)

    merged = gate(0) * jnp.dot(oa_ref[...], wa_ref[...], preferred_element_type=F32)
    merged = merged + gate(1) * jnp.dot(ob_ref[...], wb_ref[...], preferred_element_type=F32)
    merged = merged + gate(2) * jnp.dot(oc, wc_ref[...], preferred_element_type=F32)
    out = jnp.dot(merged.astype(BF16), wo_ref[...], preferred_element_type=F32)
    o_ref[...] = x + mod[2:3] * out


def _merge(xy, mod_l, oa, ob, u, cw, cb, lnw, lnb, wg, wa, wb, wc, wo, n_blocks, dims):
    bt = TOK_BLOCK
    t_rows, nps, nbatch = dims["T"], dims["S"] // bt, dims["B"]
    hpb = bt // HALO
    n_halo = t_rows // HALO
    row = lambda r: (r, 0)
    full = lambda r: (0, 0)
    return pl.pallas_call(
        functools.partial(_merge_kernel, nps=nps, ncps=dims["CTX"] // bt,
                          n_lat_blocks=nbatch * nps),
        grid=(n_blocks,),
        in_specs=[pl.BlockSpec((bt, D_MODEL), row),
                  pl.BlockSpec((1, 6, D_MODEL), lambda r: (jnp.minimum(r // nps, nbatch), 0, 0)),
                  pl.BlockSpec((bt, 4 * V7X_LANES), row),
                  pl.BlockSpec((bt, 4 * V7X_LANES), row),
                  pl.BlockSpec((bt, C_WIDTH), row),
                  pl.BlockSpec((HALO, C_WIDTH), lambda r: (jnp.maximum(r * hpb - 1, 0), 0)),
                  pl.BlockSpec((HALO, C_WIDTH), lambda r: (jnp.minimum((r + 1) * hpb, n_halo - 1), 0)),
                  pl.BlockSpec((C_KW + 1, C_WIDTH), full),
                  pl.BlockSpec((1, C_WIDTH), full),
                  pl.BlockSpec((1, C_WIDTH), full),
                  pl.BlockSpec((1, C_WIDTH), full),
                  pl.BlockSpec((D_MODEL, 3 * D_MODEL), full),
                  pl.BlockSpec((C_WIDTH, D_MODEL), full),
                  pl.BlockSpec((C_WIDTH, D_MODEL), full),
                  pl.BlockSpec((C_WIDTH, D_MODEL), full),
                  pl.BlockSpec((D_MODEL, D_MODEL), full)],
        out_specs=pl.BlockSpec((bt, D_MODEL), row),
        out_shape=jax.ShapeDtypeStruct((n_blocks * bt, D_MODEL), F32),
        scratch_shapes=[pltpu.VMEM((bt + 2 * HALO, C_WIDTH), F32)],
        compiler_params=_cparams(("arbitrary",)),
        name="merge",
    )(xy, mod_l, oa, ob, u, u, u, cw, cb, lnw, lnb, wg, wa, wb, wc, wo)


def _topk_desc(s, k):
    row = lax.broadcasted_iota(jnp.int32, (k, s.shape[1]), 0)

    def body(i, carry):
        work, vals = carry
        m = jnp.max(work, axis=0, keepdims=True)
        vals = jnp.where(row == i, m, vals)
        work = jnp.where(work == m, -jnp.inf, work)
        return work, vals

    _, vals = lax.fori_loop(0, k, body, (s, jnp.zeros((k, s.shape[1]), F32)))
    return vals


def _candidate_sums(sv1, sv2):
    half = P_TOPK // 2
    lead, rest = sv1[:half], sv1[half:]
    row = lax.broadcasted_iota(jnp.int32, lead.shape, 0)
    tiles = [lead + sv2[0:1], rest + sv2[0:1], lead + sv2[1:2]]
    for k2 in range(2, half):
        tiles.append(jnp.where(row < P_TOPK // (k2 + 1), lead + sv2[k2:k2 + 1], -jnp.inf))
    tiles.append(sv1[0:1] + sv2[half:])
    return jnp.concatenate(tiles, axis=0)


def _peer_select_kernel(x_ref, mod_ref, wq_ref, k1_ref, k2_ref,
                        t_ref, need_ref, e1_ref, r2_ref, e2_ref):
    bt = x_ref.shape[0]
    n_lt = bt // V7X_LANES
    hb = _modulated(x_ref[...], mod_ref[0], 3, 4).astype(BF16)
    t_ref[...] = hb
    q = jnp.dot(hb, wq_ref[...], preferred_element_type=F32).astype(BF16)
    s1p = lax.dot_general(k1_ref[...], q, NT_DIMS, preferred_element_type=F32)
    for lt in range(n_lt):
        need_ref[0, lt] = s1p[:, lt * V7X_LANES:(lt + 1) * V7X_LANES]
    head_row = lax.broadcasted_iota(jnp.int32, (P_HEADS, bt), 0)
    zero_t = jnp.zeros((P_HEADS, bt), F32)
    thr1, max1, tau_t = zero_t, zero_t, zero_t
    sv2_t = [zero_t] * P_TOPK
    for hh in range(P_HEADS):
        mine = head_row == hh
        qh = q[:, hh * V7X_LANES:(hh + 1) * V7X_LANES]
        s1 = jnp.concatenate([need_ref[0, lt, pl.ds(hh, P_NKEYS, stride=P_HEADS), :]
                              for lt in range(n_lt)], axis=1)
        s2 = lax.dot_general(k2_ref[hh], qh, NT_DIMS, preferred_element_type=F32)
        sv1 = _topk_desc(s1, P_TOPK)
        sv2 = _topk_desc(s2, P_TOPK)
        top = _topk_desc(_candidate_sums(sv1, sv2), P_TOPK)
        z = jnp.sum(jnp.exp(top - top[0:1]), axis=0, keepdims=True)
        rank2 = jnp.zeros_like(s2)
        for k in range(P_TOPK):
            rank2 = rank2 + jnp.where(sv2[k:k + 1] > s2, 1.0, 0.0)
            sv2_t[k] = jnp.where(mine, sv2[k:k + 1], sv2_t[k])
        r2_ref[0, hh] = rank2.astype(BF16)
        e2_ref[0, hh] = (jnp.exp(s2 - sv2[0:1]) / z).astype(BF16)
        tau_t = jnp.where(mine, top[P_TOPK - 1:P_TOPK], tau_t)
        thr1 = jnp.where(mine, sv1[P_TOPK - 1:P_TOPK], thr1)
        max1 = jnp.where(mine, sv1[0:1], max1)
    for lt in range(n_lt):
        ls = slice(lt * V7X_LANES, (lt + 1) * V7X_LANES)
        s1t = need_ref[0, lt].reshape(P_NKEYS, P_HEADS, V7X_LANES)
        count = jnp.zeros_like(s1t)
        for k in range(P_TOPK):
            count = count + jnp.where(s1t + sv2_t[k][None, :, ls] >= tau_t[None, :, ls], 1.0, 0.0)
        need_ref[0, lt] = jnp.where(s1t >= thr1[None, :, ls], count, 0.0).reshape(
            P_NKEYS * P_HEADS, V7X_LANES)
        e1_ref[0, lt] = jnp.exp(s1t - max1[None, :, ls]).reshape(P_NKEYS * P_HEADS, V7X_LANES)


def _peer_dense_kernel(x_ref, mod_ref, t_ref, need_ref, e1_ref, r2_ref, e2_ref, u_ref, vt_ref,
                       fnw_ref, o_ref, a_scr, m_scr, acc_scr, *, n_chunks, final_norm):
    bt = x_ref.shape[0]
    c = pl.program_id(1)
    ipc = u_ref.shape[0] // P_NKEYS
    n_lt = bt // V7X_LANES

    @pl.when(c == 0)
    def _init():
        acc_scr[...] = jnp.zeros_like(acc_scr)
        a_scr[...] = jnp.zeros_like(a_scr)

    cprev = jnp.maximum(c - 1, 0)
    wdt = F32 if final_norm else BF16
    loop_ii = not final_norm

    def one_key(ii):
        roff = ii * P_NKEYS
        if loop_ii:
            roff = pl.multiple_of(roff, P_NKEYS)
        hrow = pl.ds(pl.multiple_of((cprev * ipc + ii) * P_HEADS, P_HEADS), P_HEADS)
        for lt in range(n_lt):
            ls = slice(lt * V7X_LANES, (lt + 1) * V7X_LANES)
            w = [jnp.zeros((ROW_GROUP, V7X_LANES), wdt) for _ in range(P_NKEYS // ROW_GROUP)]
            need_t = need_ref[0, lt, hrow, :]
            e1_t = e1_ref[0, lt, hrow, :]
            zero = jnp.zeros((ROW_GROUP, V7X_LANES), wdt)
            for hh in range(P_HEADS):
                tile = (ROW_GROUP, V7X_LANES)
                need_b = jnp.broadcast_to(need_t[hh:hh + 1], tile).astype(wdt)
                e1_b = jnp.broadcast_to(e1_t[hh:hh + 1], tile).astype(wdt)
                for jg in range(P_NKEYS // ROW_GROUP):
                    js = slice(jg * ROW_GROUP, (jg + 1) * ROW_GROUP)
                    hit = r2_ref[0, hh, js, ls].astype(wdt) < need_b
                    w[jg] = w[jg] + jnp.where(hit, e1_b * e2_ref[0, hh, js, ls].astype(wdt), zero)
            for jg in range(P_NKEYS // ROW_GROUP):
                rows = pl.ds(roff + jg * ROW_GROUP, ROW_GROUP)
                a = a_scr[rows, ls]
                g = 0.5 * a * (1.0 + lax.erf(a * (2.0 ** -0.5)))
                m_scr[rows, ls] = (g.astype(wdt) * w[jg]).astype(BF16)

    if loop_ii:
        lax.fori_loop(0, ipc, lambda ii, _: (one_key(ii), 0)[1], 0)
    else:
        for ii in range(ipc):
            one_key(ii)

    acc_scr[...] += jnp.dot(vt_ref[0], m_scr[...], preferred_element_type=F32)
    a_scr[...] = lax.dot_general(u_ref[...], t_ref[...], NT_DIMS, preferred_element_type=F32)

    @pl.when(c == n_chunks)
    def _finish():
        x = x_ref[...]
        y = x + mod_ref[0][5:6] * acc_scr[...].T
        if final_norm:
            ms = jnp.mean(y * y, axis=-1, keepdims=True)
            y = y * lax.rsqrt(ms + EPS) * fnw_ref[...]
        o_ref[...] = y


def _peer(x1, mod_l, wq, k1p, k2p, u_tab, vt_tab, fnw, n_blocks, final_norm, dims):
    bt, ec = PEER_BLOCK, PEER_ECHUNK
    nps, nbatch = dims["S"] // bt, dims["B"]
    n_chunks = vt_tab.shape[0]
    n_lt = bt // V7X_LANES
    rows_ih = P_NKEYS * P_HEADS
    mod_spec = pl.BlockSpec((1, 6, D_MODEL), lambda r, *_: (jnp.minimum(r // nps, nbatch), 0, 0))
    ih_shape = jax.ShapeDtypeStruct((n_blocks, n_lt, rows_ih, V7X_LANES), F32)
    hj_shape = jax.ShapeDtypeStruct((n_blocks, P_HEADS, P_NKEYS, bt), BF16)
    ih_spec = pl.BlockSpec((1, n_lt, rows_ih, V7X_LANES), lambda r, *_: (r, 0, 0, 0))
    hj_spec = pl.BlockSpec((1, P_HEADS, P_NKEYS, bt), lambda r, *_: (r, 0, 0, 0))
    tok_spec = pl.BlockSpec((bt, D_MODEL), lambda r, *_: (r, 0))

    t_mod, need, e1, rank2, e2 = pl.pallas_call(
        _peer_select_kernel,
        grid=(n_blocks,),
        in_specs=[tok_spec, mod_spec,
                  pl.BlockSpec((D_MODEL, P_HEADS * V7X_LANES), lambda r: (0, 0)),
                  pl.BlockSpec((rows_ih, P_HEADS * V7X_LANES), lambda r: (0, 0)),
                  pl.BlockSpec((P_HEADS, P_NKEYS, V7X_LANES), lambda r: (0, 0, 0))],
        out_specs=(tok_spec, ih_spec, ih_spec, hj_spec, hj_spec),
        out_shape=(jax.ShapeDtypeStruct((n_blocks * bt, D_MODEL), BF16),
                   ih_shape,
                   ih_shape,
                   hj_shape,
                   hj_shape),
        compiler_params=_cparams(("arbitrary",)),
        name="peer_select",
    )(x1, mod_l, wq, k1p, k2p)

    return pl.pallas_call(
        functools.partial(_peer_dense_kernel, n_chunks=n_chunks, final_norm=final_norm),
        grid=(n_blocks, n_chunks + 1),
        in_specs=[tok_spec, mod_spec, tok_spec, ih_spec, ih_spec, hj_spec, hj_spec,
                  pl.BlockSpec((ec, D_MODEL), lambda r, c: (jnp.minimum(c, n_chunks - 1), 0)),
                  pl.BlockSpec((1, D_MODEL, ec), lambda r, c: (jnp.maximum(c - 1, 0), 0, 0)),
                  pl.BlockSpec((1, D_MODEL), lambda r, c: (0, 0))],
        out_specs=tok_spec,
        out_shape=jax.ShapeDtypeStruct((n_blocks * bt, D_MODEL), F32),
        scratch_shapes=[pltpu.VMEM((ec, bt), F32),
                        pltpu.VMEM((ec, bt), BF16),
                        pltpu.VMEM((D_MODEL, bt), F32)],
        compiler_params=_cparams(("arbitrary", "arbitrary")),
        name="peer_dense",
    )(x1, mod_l, t_mod, need, e1, rank2, e2, u_tab, vt_tab, fnw)


def _rope_tables(s_len, nbatch, n_ctx_rows):
    t = jnp.arange(s_len, dtype=jnp.int32)
    row = (t // GRID_W).astype(F32)
    col = (t % GRID_W).astype(F32)
    axis_dim = HEAD_DIM // 2
    inv = ROPE_THETA ** (-jnp.arange(0, axis_dim, 2, dtype=F32) / axis_dim)
    ar = row[:, None] * inv[None, :]
    ac = col[:, None] * inv[None, :]
    ang = jnp.concatenate([ar, ar, ac, ac], axis=-1)
    cos, sin = jnp.cos(ang), jnp.sin(ang)
    quarter = jnp.arange(HEAD_DIM) // (HEAD_DIM // 4)
    first = (quarter % 2 == 0)[None, :]
    sin_a = jnp.where(first, -sin, 0.0)
    sin_b = jnp.where(first, 0.0, sin)

    def expand(tab, ctx_fill):
        lat = jnp.tile(jnp.concatenate([tab, tab], axis=-1), (nbatch, 1))
        return jnp.concatenate([lat, jnp.full((n_ctx_rows, V7X_LANES), ctx_fill, F32)], axis=0)

    return expand(cos, 1.0), expand(sin_a, 0.0), expand(sin_b, 0.0)


def _padded_keys(keys_l):
    z = jnp.zeros_like(keys_l[:, 0])
    k0 = jnp.concatenate([keys_l[:, 0], z], axis=-1).transpose(1, 0, 2)
    eye = jnp.eye(P_HEADS, dtype=keys_l.dtype)
    k1p = (k0[:, :, None, :] * eye[None, :, :, None]).reshape(
        P_NKEYS * P_HEADS, P_HEADS * V7X_LANES)
    k2p = jnp.concatenate([z, keys_l[:, 1]], axis=-1)
    return k1p.astype(BF16), k2p.astype(BF16)


def _chunked_transpose(v_tab):
    n_exp, d = v_tab.shape
    return v_tab.astype(BF16).reshape(n_exp // PEER_ECHUNK, PEER_ECHUNK, d).transpose(0, 2, 1)


def kernel(x, c, ctx, c_ctx, w_ada, b_ada, w_in, lam_q1, lam_k1, lam_q2, lam_k2, subln_w, q_norm_w, k_norm_w, conv_w, conv_b, conv_ln_w, conv_ln_b, w_branch_a, w_branch_b, w_branch_c, w_out, peer_wq, peer_keys, peer_u, peer_v, final_norm_w):
    nbatch, s_len, d = x.shape
    ctx_len = ctx.shape[1]
    depth = w_ada.shape[0]
    n_lat, n_ctx = nbatch * s_len, nbatch * ctx_len
    dims = {"B": nbatch, "S": s_len, "CTX": ctx_len, "T": n_lat + n_ctx}
    assert d == D_MODEL and nbatch + 1 <= MOD_ROWS
    assert s_len % PEER_BLOCK == 0 and ctx_len % ATT_TQ == 0 and ctx_len % TOK_BLOCK == 0
    assert n_ctx % PEER_BLOCK == 0 and n_lat % ctx_len == 0

    cvec = jnp.zeros((MOD_ROWS, d), F32).at[:nbatch].set(c).at[nbatch].set(c_ctx)
    mod = _ada_rows(cvec, w_ada, b_ada).reshape(depth, MOD_ROWS, 6, d)
    tabs = _rope_tables(s_len, nbatch, n_ctx)
    tile2 = lambda v: jnp.concatenate([v, v], axis=-1)[None, :]

    xy = jnp.concatenate([x.reshape(n_lat, d), ctx.reshape(n_ctx, d)], axis=0)
    for l in range(depth):
        last = l == depth - 1
        lam_init = 0.8 - 0.6 * math.exp(-0.3 * l)
        w_bf = w_in[l].astype(BF16)
        lamv = jnp.stack([lam_q1[l], lam_k1[l], lam_q2[l], lam_k2[l]], axis=0)

        qa, ka, va, qb, kbd, vbd, u = _inproj(
            xy, mod[l], w_bf[:, :OFF_GATE], tabs, tile2(q_norm_w[l]), tile2(k_norm_w[l]), dims)
        oa, ob = _attention(qa, ka, va, qb, kbd, vbd, lamv, subln_w[l][None, :], lam_init,
                            not last, dims)
        n_rows = n_lat if last else n_lat + n_ctx
        cw = jnp.concatenate([conv_w[l], jnp.zeros((1, C_WIDTH), F32)], axis=0)
        x1 = _merge(xy, mod[l], oa, ob, u, cw, conv_b[l][None, :], conv_ln_w[l][None, :],
                    conv_ln_b[l][None, :], w_bf[:, OFF_GATE:], w_branch_a[l].astype(BF16),
                    w_branch_b[l].astype(BF16), w_branch_c[l].astype(BF16),
                    w_out[l].astype(BF16), n_rows // TOK_BLOCK, dims)
        k1p, k2p = _padded_keys(peer_keys[l])
        xy = _peer(x1, mod[l], peer_wq[l].astype(BF16), k1p, k2p,
                   peer_u[l].astype(BF16), _chunked_transpose(peer_v[l]), final_norm_w[None, :],
                   n_rows // PEER_BLOCK, last, dims)
    return xy.reshape(nbatch, s_len, d)
```

```python
import functools
import math

import jax
import jax.numpy as jnp
from jax import lax
from jax.experimental import pallas as pl
from jax.experimental.pallas import tpu as pltpu

F32 = jnp.float32
BF16 = jnp.bfloat16

D_MODEL = 1024
DEPTH = 2
GRID_W = 64
HEAD_DIM = 64
ROPE_THETA = 10000.0
EPS = 1e-6
A_HEADS = 4
B_KV = 2
C_WIDTH = 512
C_KW = 31
P_HEADS = 8
P_NKEYS = 128
P_TOPK = 16

COL_KA, COL_VA, COL_KB, COL_VB, COL_QA, COL_QB, COL_GLU = 512, 512, 128, 128, 512, 512, 1024
OFF_KA = 0
OFF_VA = OFF_KA + COL_KA
OFF_KB = OFF_VA + COL_VA
OFF_QA = OFF_KB + COL_KB + COL_VB
OFF_QB = OFF_QA + COL_QA
OFF_GLU = OFF_QB + COL_QB
OFF_GATE = OFF_GLU + COL_GLU

V7X_LANES = 128
V7X_VMEM_LIMIT = 56 * 1024 * 1024
HALO = 16

TOK_BLOCK = 256
ATT_TQ = 256
ATT_TK = 512
ATT_UNROLL = 8
PEER_BLOCK = 512
PEER_ECHUNK = 1024
MOD_ROWS = 8
ROW_GROUP = 16

Q_SCALE = (HEAD_DIM ** -0.5) * math.log2(math.e)
NEG_BIG = -1e30
NT_DIMS = (((1,), (1,)), ((), ()))


def _cparams(sem):
    return pltpu.CompilerParams(dimension_semantics=sem, vmem_limit_bytes=V7X_VMEM_LIMIT)


def _modulated(x, mod, shift_row, scale_row):
    ms = jnp.mean(x * x, axis=-1, keepdims=True)
    xn = x * lax.rsqrt(ms + EPS)
    return xn * (1.0 + mod[scale_row:scale_row + 1]) + mod[shift_row:shift_row + 1]


def _ada_kernel(c_ref, w_ref, b_ref, o_ref):
    c = c_ref[...]
    sc = c * jax.nn.sigmoid(c)
    o_ref[0] = jnp.dot(sc, w_ref[0], preferred_element_type=F32,
                       precision=lax.Precision.HIGHEST) + b_ref[0]


def _ada_rows(cvec, w_ada, b_ada):
    depth, d, n = w_ada.shape
    tn = 1536
    return pl.pallas_call(
        _ada_kernel,
        grid=(depth, n // tn),
        in_specs=[pl.BlockSpec((MOD_ROWS, d), lambda l, j: (0, 0)),
                  pl.BlockSpec((1, d, tn), lambda l, j: (l, 0, j)),
                  pl.BlockSpec((1, 1, tn), lambda l, j: (l, 0, j))],
        out_specs=pl.BlockSpec((1, MOD_ROWS, tn), lambda l, j: (l, 0, j)),
        out_shape=jax.ShapeDtypeStruct((depth, MOD_ROWS, n), F32),
        compiler_params=_cparams(("arbitrary", "arbitrary")),
        name="ada_rows",
    )(cvec, w_ada, b_ada.reshape(depth, 1, n))


def _inproj_kernel(x_ref, mod_ref, w_ref, cos_ref, sa_ref, sb_ref, qnw_ref, knw_ref,
                   qa_ref, ka_ref, va_ref, qb_ref, kbd_ref, vbd_ref, u_ref):
    bt = x_ref.shape[0]
    h = _modulated(x_ref[...], mod_ref[0], 0, 1).astype(BF16)
    cos, sa, sb = cos_ref[...], sa_ref[...], sb_ref[...]
    lo = lax.broadcasted_iota(jnp.int32, (bt, V7X_LANES), 1) < HEAD_DIM

    def proj(c0, width):
        return jnp.dot(h, w_ref[:, c0:c0 + width], preferred_element_type=F32)

    def rope(x):
        return x * cos + pltpu.roll(x, V7X_LANES - 16, 1) * sa + pltpu.roll(x, 16, 1) * sb

    def headnorm(x, w):
        x2 = x * x
        s_lo = jnp.sum(jnp.where(lo, x2, 0.0), axis=-1, keepdims=True)
        s_hi = jnp.sum(jnp.where(lo, 0.0, x2), axis=-1, keepdims=True)
        r = jnp.where(lo, lax.rsqrt(s_lo / HEAD_DIM + EPS), lax.rsqrt(s_hi / HEAD_DIM + EPS))
        return x * r * w

    def tile(p, c):
        return p[:, c * V7X_LANES:(c + 1) * V7X_LANES]

    p = proj(OFF_KA, COL_KA)
    for c in range(A_HEADS):
        ka_ref[c] = rope(tile(p, c)).astype(BF16)

    p = proj(OFF_VA, COL_VA)
    for c in range(A_HEADS):
        va_ref[c] = tile(p, c).astype(BF16)

    p = proj(OFF_KB, COL_KB + COL_VB)
    kb = rope(headnorm(tile(p, 0), knw_ref[...]))
    kb_sw = pltpu.roll(kb, HEAD_DIM, 1)
    kbd_ref[0] = jnp.where(lo, kb, kb_sw).astype(BF16)
    kbd_ref[1] = jnp.where(lo, kb_sw, kb).astype(BF16)
    vb = tile(p, 1)
    vb_sw = pltpu.roll(vb, HEAD_DIM, 1)
    vbd_ref[0] = jnp.where(lo, vb, vb_sw).astype(BF16)
    vbd_ref[1] = jnp.where(lo, vb_sw, vb).astype(BF16)

    p = proj(OFF_QA, COL_QA)
    for c in range(COL_QA // V7X_LANES):
        qa_ref[:, c * V7X_LANES:(c + 1) * V7X_LANES] = (rope(tile(p, c)) * Q_SCALE).astype(BF16)

    p = proj(OFF_QB, COL_QB)
    qnw = qnw_ref[...]
    for c in range(COL_QB // V7X_LANES):
        qb_ref[:, c * V7X_LANES:(c + 1) * V7X_LANES] = (
            rope(headnorm(tile(p, c), qnw)) * Q_SCALE).astype(BF16)

    p = proj(OFF_GLU, COL_GLU)
    u_ref[...] = p[:, :C_WIDTH] * jax.nn.sigmoid(p[:, C_WIDTH:])


def _inproj(xy, mod_l, w_main, tabs, qnw, knw, dims):
    t_rows, nps, nbatch = dims["T"], dims["S"] // TOK_BLOCK, dims["B"]
    bt = TOK_BLOCK
    row = lambda r: (r, 0)
    full = lambda r: (0, 0)
    out_shapes = (
        jax.ShapeDtypeStruct((t_rows, COL_QA), BF16),
        jax.ShapeDtypeStruct((A_HEADS, t_rows, V7X_LANES), BF16),
        jax.ShapeDtypeStruct((A_HEADS, t_rows, V7X_LANES), BF16),
        jax.ShapeDtypeStruct((t_rows, COL_QB), BF16),
        jax.ShapeDtypeStruct((B_KV, t_rows, V7X_LANES), BF16),
        jax.ShapeDtypeStruct((B_KV, t_rows, V7X_LANES), BF16),
        jax.ShapeDtypeStruct((t_rows, C_WIDTH), F32),
    )
    grp = pl.BlockSpec((B_KV, bt, V7X_LANES), lambda r: (0, r, 0))
    heads = pl.BlockSpec((A_HEADS, bt, V7X_LANES), lambda r: (0, r, 0))
    return pl.pallas_call(
        _inproj_kernel,
        grid=(t_rows // bt,),
        in_specs=[pl.BlockSpec((bt, D_MODEL), row),
                  pl.BlockSpec((1, 6, D_MODEL), lambda r: (jnp.minimum(r // nps, nbatch), 0, 0)),
                  pl.BlockSpec((D_MODEL, OFF_GATE), full),
                  pl.BlockSpec((bt, V7X_LANES), row),
                  pl.BlockSpec((bt, V7X_LANES), row),
                  pl.BlockSpec((bt, V7X_LANES), row),
                  pl.BlockSpec((1, V7X_LANES), full),
                  pl.BlockSpec((1, V7X_LANES), full)],
        out_specs=(pl.BlockSpec((bt, COL_QA), row), heads, heads, pl.BlockSpec((bt, COL_QB), row),
                   grp, grp, pl.BlockSpec((bt, C_WIDTH), row)),
        out_shape=out_shapes,
        compiler_params=_cparams(("arbitrary",)),
        name="inproj",
    )(xy, mod_l, w_main, tabs[0], tabs[1], tabs[2], qnw, knw)


def _kv_unroll(n_chunks):
    return ATT_UNROLL if n_chunks % ATT_UNROLL == 0 else 1


def _softmax_step(s, v, m, l, acc):
    mn = jnp.maximum(m, jnp.max(s, axis=-1, keepdims=True))
    p = jnp.exp2(s - mn)
    alpha = jnp.exp2(m - mn)
    l = alpha * l + jnp.sum(p, axis=-1, keepdims=True)
    acc = alpha * acc + jnp.dot(p.astype(BF16), v, preferred_element_type=F32)
    return mn, l, acc


def _flash_rows(q, load_lat, kc, vc, *, nq_lat, n_lat_chunks, tk):
    rows = q.shape[0]

    def step(k, v, carry):
        m, l, a = carry
        s = lax.dot_general(q, k, NT_DIMS, preferred_element_type=F32)
        return _softmax_step(s, v, m, l, a)

    unroll = _kv_unroll(n_lat_chunks)

    def lat_body(j, carry):
        for r in range(unroll):
            off = pl.multiple_of((j * unroll + r) * tk, tk)
            carry = step(*load_lat(off), carry)
        return carry

    carry = (jnp.full((rows, 1), NEG_BIG, F32), jnp.zeros((rows, 1), F32),
             jnp.zeros((rows, V7X_LANES), F32))
    n_lat = jnp.where(pl.program_id(2) < nq_lat, n_lat_chunks // unroll, 0)
    carry = lax.fori_loop(0, n_lat, lat_body, carry)
    _, l, a = step(kc, vc, carry)
    return a / l


def _attn_a_kernel(q_ref, kl_ref, vl_ref, kc_ref, vc_ref, lamv_ref, sw_ref, o_ref, *,
                   nq_lat, n_lat_chunks, tk, lam_init):
    tq = q_ref.shape[0]
    q = q_ref[...]
    lo = lax.broadcasted_iota(jnp.int32, (tq, V7X_LANES), 1) < HEAD_DIM
    zero = jnp.zeros_like(q)
    q12 = jnp.concatenate([jnp.where(lo, q, zero), jnp.where(lo, zero, q)], axis=0)
    o12 = _flash_rows(q12, lambda off: (kl_ref[0, pl.ds(off, tk), :], vl_ref[0, pl.ds(off, tk), :]),
                      kc_ref[0], vc_ref[0], nq_lat=nq_lat, n_lat_chunks=n_lat_chunks, tk=tk)

    lv = lamv_ref[...]
    lam = (jnp.exp(jnp.sum(lv[0:1] * lv[1:2], axis=-1, keepdims=True))
           - jnp.exp(jnp.sum(lv[2:3] * lv[3:4], axis=-1, keepdims=True)) + lam_init)
    o = o12[:tq] - lam * o12[tq:]
    ms = jnp.mean(o * o, axis=-1, keepdims=True)
    o_ref[...] = (o * lax.rsqrt(ms + EPS) * sw_ref[...] * (1.0 - lam_init)).astype(BF16)


def _attn_b_kernel(q_ref, kl_ref, vl_ref, kc_ref, vc_ref, o_ref, *, nq_lat, n_lat_chunks, tk):
    tq = q_ref.shape[0]
    lo = lax.broadcasted_iota(jnp.int32, (tq, V7X_LANES), 1) < HEAD_DIM
    parts = []
    for c in range(2):
        qc = q_ref[:, c * V7X_LANES:(c + 1) * V7X_LANES]
        zero = jnp.zeros_like(qc)
        parts += [jnp.where(lo, qc, zero), jnp.where(lo, zero, qc)]
    q4 = jnp.concatenate(parts, axis=0)
    o = _flash_rows(q4, lambda off: (kl_ref[0, pl.ds(off, tk), :], vl_ref[0, pl.ds(off, tk), :]),
                    kc_ref[0], vc_ref[0], nq_lat=nq_lat, n_lat_chunks=n_lat_chunks, tk=tk)
    for c in range(2):
        o_ref[:, c * V7X_LANES:(c + 1) * V7X_LANES] = jnp.where(
            lo, o[(2 * c) * tq:(2 * c + 1) * tq], o[(2 * c + 1) * tq:(2 * c + 2) * tq]).astype(BF16)


def _q_block_index(b, qi, nq_lat, nq_ctx, nbatch):
    return jnp.where(qi < nq_lat, b * nq_lat + qi, nbatch * nq_lat + b * nq_ctx + (qi - nq_lat))


def _attention(qa, ka, va, qb, kbd, vbd, lamv, subln_w, lam_init, with_ctx_queries, dims):
    s_len, ctx_len, nbatch, t_rows = dims["S"], dims["CTX"], dims["B"], dims["T"]
    tq, tk = ATT_TQ, min(ATT_TK, s_len)
    nq_lat, nq_ctx = s_len // tq, ctx_len // tq
    nq = nq_lat + (nq_ctx if with_ctx_queries else 0)
    ctx_blk0 = (nbatch * s_len) // ctx_len
    qidx = functools.partial(_q_block_index, nq_lat=nq_lat, nq_ctx=nq_ctx, nbatch=nbatch)
    sem = ("arbitrary", "arbitrary", "arbitrary")

    oa = pl.pallas_call(
        functools.partial(_attn_a_kernel, nq_lat=nq_lat, n_lat_chunks=s_len // tk, tk=tk,
                          lam_init=lam_init),
        grid=(nbatch, A_HEADS, nq),
        in_specs=[pl.BlockSpec((tq, V7X_LANES), lambda b, h, qi: (qidx(b, qi), h)),
                  pl.BlockSpec((1, s_len, V7X_LANES), lambda b, h, qi: (h, b, 0)),
                  pl.BlockSpec((1, s_len, V7X_LANES), lambda b, h, qi: (h, b, 0)),
                  pl.BlockSpec((1, ctx_len, V7X_LANES), lambda b, h, qi: (h, ctx_blk0 + b, 0)),
                  pl.BlockSpec((1, ctx_len, V7X_LANES), lambda b, h, qi: (h, ctx_blk0 + b, 0)),
                  pl.BlockSpec((4, HEAD_DIM), lambda b, h, qi: (0, 0)),
                  pl.BlockSpec((1, V7X_LANES), lambda b, h, qi: (0, 0))],
        out_specs=pl.BlockSpec((tq, V7X_LANES), lambda b, h, qi: (qidx(b, qi), h)),
        out_shape=jax.ShapeDtypeStruct((t_rows, A_HEADS * V7X_LANES), BF16),
        compiler_params=_cparams(sem),
        name="attn_a",
    )(qa, ka, va, ka, va, lamv, subln_w)

    gw = 2 * V7X_LANES
    ob = pl.pallas_call(
        functools.partial(_attn_b_kernel, nq_lat=nq_lat, n_lat_chunks=s_len // tk, tk=tk),
        grid=(nbatch, B_KV, nq),
        in_specs=[pl.BlockSpec((tq, gw), lambda b, g, qi: (qidx(b, qi), g)),
                  pl.BlockSpec((1, s_len, V7X_LANES), lambda b, g, qi: (g, b, 0)),
                  pl.BlockSpec((1, s_len, V7X_LANES), lambda b, g, qi: (g, b, 0)),
                  pl.BlockSpec((1, ctx_len, V7X_LANES), lambda b, g, qi: (g, ctx_blk0 + b, 0)),
                  pl.BlockSpec((1, ctx_len, V7X_LANES), lambda b, g, qi: (g, ctx_blk0 + b, 0))],
        out_specs=pl.BlockSpec((tq, gw), lambda b, g, qi: (qidx(b, qi), g)),
        out_shape=jax.ShapeDtypeStruct((t_rows, B_KV * gw), BF16),
        compiler_params=_cparams(sem),
        name="attn_b",
    )(qb, kbd, vbd, kbd, vbd)
    return oa, ob


def _merge_kernel(x_ref, mod_ref, oa_ref, ob_ref, u_ref, up_ref, un_ref, cw_ref, cb_ref,
                  lnw_ref, lnb_ref, wg_ref, wa_ref, wb_ref, wc_ref, wo_ref, o_ref, uext_ref, *,
                  nps, ncps, n_lat_blocks):
    bt = x_ref.shape[0]
    r = pl.program_id(0)
    is_lat = r < n_lat_blocks
    pos = jnp.where(is_lat, r % nps, (r - n_lat_blocks) % ncps)
    last_pos = jnp.where(is_lat, nps - 1, ncps - 1)
    keep_prev = (pos != 0).astype(F32)
    keep_next = (pos != last_pos).astype(F32)

    x = x_ref[...]
    mod = mod_ref[0]
    h = _modulated(x, mod, 0, 1).astype(BF16)

    uext_ref[0:HALO, :] = up_ref[...] * keep_prev
    uext_ref[HALO:HALO + bt, :] = u_ref[...]
    uext_ref[HALO + bt:2 * HALO + bt, :] = un_ref[...] * keep_next
    cw = cw_ref[...]
    y = jnp.zeros((bt, C_WIDTH), F32) + cb_ref[...]
    base = HALO - C_KW // 2
    for k in range(C_KW):
        y = y + cw[k:k + 1, :] * uext_ref[base + k:base + k + bt, :]
    mu = jnp.mean(y, axis=-1, keepdims=True)
    yc = y - mu
    var = jnp.mean(yc * yc, axis=-1, keepdims=True)
    yn = yc * lax.rsqrt(var + EPS) * lnw_ref[...] + lnb_ref[...]
    oc = (yn * jax.nn.sigmoid(yn)).astype(BF16)

    def gate(i):
        logits = jnp.dot(h, wg_ref[:, i * D_MODEL:(i + 1) * D_MODEL], preferred_element_type=F32)
        return jax.nn.sigmoid(logits)

    merged = gate(0) * jnp.dot(oa_ref[...], wa_ref[...], preferred_element_type=F32)
    merged = merged + gate(1) * jnp.dot(ob_ref[...], wb_ref[...], preferred_element_type=F32)
    merged = merged + gate(2) * jnp.dot(oc, wc_ref[...], preferred_element_type=F32)
    out = jnp.dot(merged.astype(BF16), wo_ref[...], preferred_element_type=F32)
    o_ref[...] = x + mod[2:3] * out


def _merge(xy, mod_l, oa, ob, u, cw, cb, lnw, lnb, wg, wa, wb, wc, wo, n_blocks, dims):
    bt = TOK_BLOCK
    t_rows, nps, nbatch = dims["T"], dims["S"] // bt, dims["B"]
    hpb = bt // HALO
    n_halo = t_rows // HALO
    row = lambda r: (r, 0)
    full = lambda r: (0, 0)
    return pl.pallas_call(
        functools.partial(_merge_kernel, nps=nps, ncps=dims["CTX"] // bt,
                          n_lat_blocks=nbatch * nps),
        grid=(n_blocks,),
        in_specs=[pl.BlockSpec((bt, D_MODEL), row),
                  pl.BlockSpec((1, 6, D_MODEL), lambda r: (jnp.minimum(r // nps, nbatch), 0, 0)),
                  pl.BlockSpec((bt, 4 * V7X_LANES), row),
                  pl.BlockSpec((bt, 4 * V7X_LANES), row),
                  pl.BlockSpec((bt, C_WIDTH), row),
                  pl.BlockSpec((HALO, C_WIDTH), lambda r: (jnp.maximum(r * hpb - 1, 0), 0)),
                  pl.BlockSpec((HALO, C_WIDTH), lambda r: (jnp.minimum((r + 1) * hpb, n_halo - 1), 0)),
                  pl.BlockSpec((C_KW + 1, C_WIDTH), full),
                  pl.BlockSpec((1, C_WIDTH), full),
                  pl.BlockSpec((1, C_WIDTH), full),
                  pl.BlockSpec((1, C_WIDTH), full),
                  pl.BlockSpec((D_MODEL, 3 * D_MODEL), full),
                  pl.BlockSpec((C_WIDTH, D_MODEL), full),
                  pl.BlockSpec((C_WIDTH, D_MODEL), full),
                  pl.BlockSpec((C_WIDTH, D_MODEL), full),
                  pl.BlockSpec((D_MODEL, D_MODEL), full)],
        out_specs=pl.BlockSpec((bt, D_MODEL), row),
        out_shape=jax.ShapeDtypeStruct((n_blocks * bt, D_MODEL), F32),
        scratch_shapes=[pltpu.VMEM((bt + 2 * HALO, C_WIDTH), F32)],
        compiler_params=_cparams(("arbitrary",)),
        name="merge",
    )(xy, mod_l, oa, ob, u, u, u, cw, cb, lnw, lnb, wg, wa, wb, wc, wo)


def _topk_desc(s, k):
    row = lax.broadcasted_iota(jnp.int32, (k, s.shape[1]), 0)

    def body(i, carry):
        work, vals = carry
        m = jnp.max(work, axis=0, keepdims=True)
        vals = jnp.where(row == i, m, vals)
        work = jnp.where(work == m, -jnp.inf, work)
        return work, vals

    _, vals = lax.fori_loop(0, k, body, (s, jnp.zeros((k, s.shape[1]), F32)))
    return vals


def _candidate_sums(sv1, sv2):
    half = P_TOPK // 2
    lead, rest = sv1[:half], sv1[half:]
    row = lax.broadcasted_iota(jnp.int32, lead.shape, 0)
    tiles = [lead + sv2[0:1], rest + sv2[0:1], lead + sv2[1:2]]
    for k2 in range(2, half):
        tiles.append(jnp.where(row < P_TOPK // (k2 + 1), lead + sv2[k2:k2 + 1], -jnp.inf))
    tiles.append(sv1[0:1] + sv2[half:])
    return jnp.concatenate(tiles, axis=0)


def _peer_select_kernel(x_ref, mod_ref, wq_ref, k1_ref, k2_ref,
                        t_ref, need_ref, e1_ref, r2_ref, e2_ref):
    bt = x_ref.shape[0]
    n_lt = bt // V7X_LANES
    hb = _modulated(x_ref[...], mod_ref[0], 3, 4).astype(BF16)
    t_ref[...] = hb
    q = jnp.dot(hb, wq_ref[...], preferred_element_type=F32).astype(BF16)
    s1p = lax.dot_general(k1_ref[...], q, NT_DIMS, preferred_element_type=F32)
    for lt in range(n_lt):
        need_ref[0, lt] = s1p[:, lt * V7X_LANES:(lt + 1) * V7X_LANES]
    head_row = lax.broadcasted_iota(jnp.int32, (P_HEADS, bt), 0)
    zero_t = jnp.zeros((P_HEADS, bt), F32)
    thr1, max1, tau_t = zero_t, zero_t, zero_t
    sv2_t = [zero_t] * P_TOPK
    for hh in range(P_HEADS):
        mine = head_row == hh
        qh = q[:, hh * V7X_LANES:(hh + 1) * V7X_LANES]
        s1 = jnp.concatenate([need_ref[0, lt, pl.ds(hh, P_NKEYS, stride=P_HEADS), :]
                              for lt in range(n_lt)], axis=1)
        s2 = lax.dot_general(k2_ref[hh], qh, NT_DIMS, preferred_element_type=F32)
        sv1 = _topk_desc(s1, P_TOPK)
        sv2 = _topk_desc(s2, P_TOPK)
        top = _topk_desc(_candidate_sums(sv1, sv2), P_TOPK)
        z = jnp.sum(jnp.exp(top - top[0:1]), axis=0, keepdims=True)
        rank2 = jnp.zeros_like(s2)
        for k in range(P_TOPK):
            rank2 = rank2 + jnp.where(sv2[k:k + 1] > s2, 1.0, 0.0)
            sv2_t[k] = jnp.where(mine, sv2[k:k + 1], sv2_t[k])
        r2_ref[0, hh] = rank2.astype(BF16)
        e2_ref[0, hh] = (jnp.exp(s2 - sv2[0:1]) / z).astype(BF16)
        tau_t = jnp.where(mine, top[P_TOPK - 1:P_TOPK], tau_t)
        thr1 = jnp.where(mine, sv1[P_TOPK - 1:P_TOPK], thr1)
        max1 = jnp.where(mine, sv1[0:1], max1)
    for lt in range(n_lt):
        ls = slice(lt * V7X_LANES, (lt + 1) * V7X_LANES)
        s1t = need_ref[0, lt].reshape(P_NKEYS, P_HEADS, V7X_LANES)
        count = jnp.zeros_like(s1t)
        for k in range(P_TOPK):
            count = count + jnp.where(s1t + sv2_t[k][None, :, ls] >= tau_t[None, :, ls], 1.0, 0.0)
        need_ref[0, lt] = jnp.where(s1t >= thr1[None, :, ls], count, 0.0).reshape(
            P_NKEYS * P_HEADS, V7X_LANES)
        e1_ref[0, lt] = jnp.exp(s1t - max1[None, :, ls]).reshape(P_NKEYS * P_HEADS, V7X_LANES)


def _peer_dense_kernel(x_ref, mod_ref, t_ref, need_ref, e1_ref, r2_ref, e2_ref, u_ref, vt_ref,
                       fnw_ref, o_ref, a_scr, m_scr, acc_scr, *, n_chunks, final_norm):
    bt = x_ref.shape[0]
    c = pl.program_id(1)
    ipc = u_ref.shape[0] // P_NKEYS
    n_lt = bt // V7X_LANES

    @pl.when(c == 0)
    def _init():
        acc_scr[...] = jnp.zeros_like(acc_scr)
        a_scr[...] = jnp.zeros_like(a_scr)

    cprev = jnp.maximum(c - 1, 0)
    wdt = BF16
    loop_ii = True

    def one_key(ii):
        roff = ii * P_NKEYS
        if loop_ii:
            roff = pl.multiple_of(roff, P_NKEYS)
        hrow = pl.ds(pl.multiple_of((cprev * ipc + ii) * P_HEADS, P_HEADS), P_HEADS)
        for lt in range(n_lt):
            ls = slice(lt * V7X_LANES, (lt + 1) * V7X_LANES)
            w = [jnp.zeros((ROW_GROUP, V7X_LANES), wdt) for _ in range(P_NKEYS // ROW_GROUP)]
            need_t = need_ref[0, lt, hrow, :]
            e1_t = e1_ref[0, lt, hrow, :]
            zero = jnp.zeros((ROW_GROUP, V7X_LANES), wdt)
            for hh in range(P_HEADS):
                tile = (ROW_GROUP, V7X_LANES)
                need_b = jnp.broadcast_to(need_t[hh:hh + 1], tile).astype(wdt)
                e1_b = jnp.broadcast_to(e1_t[hh:hh + 1], tile).astype(wdt)
                for jg in range(P_NKEYS // ROW_GROUP):
                    js = slice(jg * ROW_GROUP, (jg + 1) * ROW_GROUP)
                    hit = r2_ref[0, hh, js, ls].astype(wdt) < need_b
                    w[jg] = w[jg] + jnp.where(hit, e1_b * e2_ref[0, hh, js, ls].astype(wdt), zero)
            for jg in range(P_NKEYS // ROW_GROUP):
                rows = pl.ds(roff + jg * ROW_GROUP, ROW_GROUP)
                a = a_scr[rows, ls]
                g = 0.5 * a * (1.0 + lax.erf(a * (2.0 ** -0.5)))
                m_scr[rows, ls] = (g.astype(wdt) * w[jg]).astype(BF16)

    if loop_ii:
        lax.fori_loop(0, ipc, lambda ii, _: (one_key(ii), 0)[1], 0)
    else:
        for ii in range(ipc):
            one_key(ii)

    acc_scr[...] += jnp.dot(vt_ref[0], m_scr[...], preferred_element_type=F32)
    a_scr[...] = lax.dot_general(u_ref[...], t_ref[...], NT_DIMS, preferred_element_type=F32)

    @pl.when(c == n_chunks)
    def _finish():
        x = x_ref[...]
        y = x + mod_ref[0][5:6] * acc_scr[...].T
        if final_norm:
            ms = jnp.mean(y * y, axis=-1, keepdims=True)
            y = y * lax.rsqrt(ms + EPS) * fnw_ref[...]
        o_ref[...] = y


def _peer(x1, mod_l, wq, k1p, k2p, u_tab, vt_tab, fnw, n_blocks, final_norm, dims):
    bt, ec = PEER_BLOCK, vt_tab.shape[2]
    nps, nbatch = dims["S"] // bt, dims["B"]
    n_chunks = vt_tab.shape[0]
    n_lt = bt // V7X_LANES
    rows_ih = P_NKEYS * P_HEADS
    mod_spec = pl.BlockSpec((1, 6, D_MODEL), lambda r, *_: (jnp.minimum(r // nps, nbatch), 0, 0))
    ih_shape = jax.ShapeDtypeStruct((n_blocks, n_lt, rows_ih, V7X_LANES), F32)
    hj_shape = jax.ShapeDtypeStruct((n_blocks, P_HEADS, P_NKEYS, bt), BF16)
    ih_spec = pl.BlockSpec((1, n_lt, rows_ih, V7X_LANES), lambda r, *_: (r, 0, 0, 0))
    hj_spec = pl.BlockSpec((1, P_HEADS, P_NKEYS, bt), lambda r, *_: (r, 0, 0, 0))
    tok_spec = pl.BlockSpec((bt, D_MODEL), lambda r, *_: (r, 0))

    t_mod, need, e1, rank2, e2 = pl.pallas_call(
        _peer_select_kernel,
        grid=(n_blocks,),
        in_specs=[tok_spec, mod_spec,
                  pl.BlockSpec((D_MODEL, P_HEADS * V7X_LANES), lambda r: (0, 0)),
                  pl.BlockSpec((rows_ih, P_HEADS * V7X_LANES), lambda r: (0, 0)),
                  pl.BlockSpec((P_HEADS, P_NKEYS, V7X_LANES), lambda r: (0, 0, 0))],
        out_specs=(tok_spec, ih_spec, ih_spec, hj_spec, hj_spec),
        out_shape=(jax.ShapeDtypeStruct((n_blocks * bt, D_MODEL), BF16),
                   ih_shape,
                   ih_shape,
                   hj_shape,
                   hj_shape),
        compiler_params=_cparams(("arbitrary",)),
        name="peer_select",
    )(x1, mod_l, wq, k1p, k2p)

    return pl.pallas_call(
        functools.partial(_peer_dense_kernel, n_chunks=n_chunks, final_norm=final_norm),
        grid=(n_blocks, n_chunks + 1),
        in_specs=[tok_spec, mod_spec, tok_spec, ih_spec, ih_spec, hj_spec, hj_spec,
                  pl.BlockSpec((ec, D_MODEL), lambda r, c: (jnp.minimum(c, n_chunks - 1), 0)),
                  pl.BlockSpec((1, D_MODEL, ec), lambda r, c: (jnp.maximum(c - 1, 0), 0, 0)),
                  pl.BlockSpec((1, D_MODEL), lambda r, c: (0, 0))],
        out_specs=tok_spec,
        out_shape=jax.ShapeDtypeStruct((n_blocks * bt, D_MODEL), F32),
        scratch_shapes=[pltpu.VMEM((ec, bt), F32),
                        pltpu.VMEM((ec, bt), BF16),
                        pltpu.VMEM((D_MODEL, bt), F32)],
        compiler_params=_cparams(("arbitrary", "arbitrary")),
        name="peer_dense",
    )(x1, mod_l, t_mod, need, e1, rank2, e2, u_tab, vt_tab, fnw)


def _rope_tables(s_len, nbatch, n_ctx_rows):
    t = jnp.arange(s_len, dtype=jnp.int32)
    row = (t // GRID_W).astype(F32)
    col = (t % GRID_W).astype(F32)
    axis_dim = HEAD_DIM // 2
    inv = ROPE_THETA ** (-jnp.arange(0, axis_dim, 2, dtype=F32) / axis_dim)
    ar = row[:, None] * inv[None, :]
    ac = col[:, None] * inv[None, :]
    ang = jnp.concatenate([ar, ar, ac, ac], axis=-1)
    cos, sin = jnp.cos(ang), jnp.sin(ang)
    quarter = jnp.arange(HEAD_DIM) // (HEAD_DIM // 4)
    first = (quarter % 2 == 0)[None, :]
    sin_a = jnp.where(first, -sin, 0.0)
    sin_b = jnp.where(first, 0.0, sin)

    def expand(tab, ctx_fill):
        lat = jnp.tile(jnp.concatenate([tab, tab], axis=-1), (nbatch, 1))
        return jnp.concatenate([lat, jnp.full((n_ctx_rows, V7X_LANES), ctx_fill, F32)], axis=0)

    return expand(cos, 1.0), expand(sin_a, 0.0), expand(sin_b, 0.0)


def _padded_keys(keys_l):
    z = jnp.zeros_like(keys_l[:, 0])
    k0 = jnp.concatenate([keys_l[:, 0], z], axis=-1).transpose(1, 0, 2)
    eye = jnp.eye(P_HEADS, dtype=keys_l.dtype)
    k1p = (k0[:, :, None, :] * eye[None, :, :, None]).reshape(
        P_NKEYS * P_HEADS, P_HEADS * V7X_LANES)
    k2p = jnp.concatenate([z, keys_l[:, 1]], axis=-1)
    return k1p.astype(BF16), k2p.astype(BF16)


def _chunked_transpose(v_tab, ec):
    n_exp, d = v_tab.shape
    return v_tab.astype(BF16).reshape(n_exp // ec, ec, d).transpose(0, 2, 1)


def kernel(x, c, ctx, c_ctx, w_ada, b_ada, w_in, lam_q1, lam_k1, lam_q2, lam_k2, subln_w, q_norm_w, k_norm_w, conv_w, conv_b, conv_ln_w, conv_ln_b, w_branch_a, w_branch_b, w_branch_c, w_out, peer_wq, peer_keys, peer_u, peer_v, final_norm_w):
    nbatch, s_len, d = x.shape
    ctx_len = ctx.shape[1]
    depth = w_ada.shape[0]
    n_lat, n_ctx = nbatch * s_len, nbatch * ctx_len
    dims = {"B": nbatch, "S": s_len, "CTX": ctx_len, "T": n_lat + n_ctx}
    assert d == D_MODEL and nbatch + 1 <= MOD_ROWS
    assert s_len % PEER_BLOCK == 0 and ctx_len % ATT_TQ == 0 and ctx_len % TOK_BLOCK == 0
    assert n_ctx % PEER_BLOCK == 0 and n_lat % ctx_len == 0

    cvec = jnp.zeros((MOD_ROWS, d), F32).at[:nbatch].set(c).at[nbatch].set(c_ctx)
    mod = _ada_rows(cvec, w_ada, b_ada).reshape(depth, MOD_ROWS, 6, d)
    tabs = _rope_tables(s_len, nbatch, n_ctx)
    tile2 = lambda v: jnp.concatenate([v, v], axis=-1)[None, :]

    xy = jnp.concatenate([x.reshape(n_lat, d), ctx.reshape(n_ctx, d)], axis=0)
    for l in range(depth):
        last = l == depth - 1
        lam_init = 0.8 - 0.6 * math.exp(-0.3 * l)
        w_bf = w_in[l].astype(BF16)
        lamv = jnp.stack([lam_q1[l], lam_k1[l], lam_q2[l], lam_k2[l]], axis=0)

        qa, ka, va, qb, kbd, vbd, u = _inproj(
            xy, mod[l], w_bf[:, :OFF_GATE], tabs, tile2(q_norm_w[l]), tile2(k_norm_w[l]), dims)
        oa, ob = _attention(qa, ka, va, qb, kbd, vbd, lamv, subln_w[l][None, :], lam_init,
                            not last, dims)
        n_rows = n_lat if last else n_lat + n_ctx
        cw = jnp.concatenate([conv_w[l], jnp.zeros((1, C_WIDTH), F32)], axis=0)
        x1 = _merge(xy, mod[l], oa, ob, u, cw, conv_b[l][None, :], conv_ln_w[l][None, :],
                    conv_ln_b[l][None, :], w_bf[:, OFF_GATE:], w_branch_a[l].astype(BF16),
                    w_branch_b[l].astype(BF16), w_branch_c[l].astype(BF16),
                    w_out[l].astype(BF16), n_rows // TOK_BLOCK, dims)
        k1p, k2p = _padded_keys(peer_keys[l])
        xy = _peer(x1, mod[l], peer_wq[l].astype(BF16), k1p, k2p,
                   peer_u[l].astype(BF16), _chunked_transpose(peer_v[l], 512 if last else 2048),
                   final_norm_w[None, :],
                   n_rows // PEER_BLOCK, last, dims)
    return xy.reshape(nbatch, s_len, d)
```

```python
import functools
import math

import jax
import jax.numpy as jnp
from jax import lax
from jax.experimental import pallas as pl
from jax.experimental.pallas import tpu as pltpu

F32 = jnp.float32
BF16 = jnp.bfloat16

D_MODEL = 1024
DEPTH = 2
GRID_W = 64
HEAD_DIM = 64
ROPE_THETA = 10000.0
EPS = 1e-6
A_HEADS = 4
B_KV = 2
C_WIDTH = 512
C_KW = 31
P_HEADS = 8
P_NKEYS = 128
P_TOPK = 16

COL_KA, COL_VA, COL_KB, COL_VB, COL_QA, COL_QB, COL_GLU = 512, 512, 128, 128, 512, 512, 1024
OFF_KA = 0
OFF_VA = OFF_KA + COL_KA
OFF_KB = OFF_VA + COL_VA
OFF_QA = OFF_KB + COL_KB + COL_VB
OFF_QB = OFF_QA + COL_QA
OFF_GLU = OFF_QB + COL_QB
OFF_GATE = OFF_GLU + COL_GLU

V7X_LANES = 128
V7X_VMEM_LIMIT = 56 * 1024 * 1024
HALO = 16

TOK_BLOCK = 256
ATT_TQ = 256
ATT_TK = 512
ATT_UNROLL = 8
PEER_BLOCK = 512
PEER_ECHUNK = 1024
MOD_ROWS = 8
ROW_GROUP = 16

Q_SCALE = (HEAD_DIM ** -0.5) * math.log2(math.e)
NEG_BIG = -1e30
NT_DIMS = (((1,), (1,)), ((), ()))


def _cparams(sem):
    return pltpu.CompilerParams(dimension_semantics=sem, vmem_limit_bytes=V7X_VMEM_LIMIT)


def _modulated(x, mod, shift_row, scale_row):
    ms = jnp.mean(x * x, axis=-1, keepdims=True)
    xn = x * lax.rsqrt(ms + EPS)
    return xn * (1.0 + mod[scale_row:scale_row + 1]) + mod[shift_row:shift_row + 1]


def _ada_kernel(c_ref, w_ref, b_ref, o_ref):
    c = c_ref[...]
    sc = c * jax.nn.sigmoid(c)
    o_ref[0] = jnp.dot(sc, w_ref[0], preferred_element_type=F32,
                       precision=lax.Precision.HIGHEST) + b_ref[0]


def _ada_rows(cvec, w_ada, b_ada):
    depth, d, n = w_ada.shape
    tn = 1536
    return pl.pallas_call(
        _ada_kernel,
        grid=(depth, n // tn),
        in_specs=[pl.BlockSpec((MOD_ROWS, d), lambda l, j: (0, 0)),
                  pl.BlockSpec((1, d, tn), lambda l, j: (l, 0, j)),
                  pl.BlockSpec((1, 1, tn), lambda l, j: (l, 0, j))],
        out_specs=pl.BlockSpec((1, MOD_ROWS, tn), lambda l, j: (l, 0, j)),
        out_shape=jax.ShapeDtypeStruct((depth, MOD_ROWS, n), F32),
        compiler_params=_cparams(("arbitrary", "arbitrary")),
        name="ada_rows",
    )(cvec, w_ada, b_ada.reshape(depth, 1, n))


def _inproj_kernel(x_ref, mod_ref, w_ref, cos_ref, sa_ref, sb_ref, qnw_ref, knw_ref,
                   qa_ref, ka_ref, va_ref, qb_ref, kbd_ref, vbd_ref, u_ref):
    bt = x_ref.shape[0]
    h = _modulated(x_ref[...], mod_ref[0], 0, 1).astype(BF16)
    cos, sa, sb = cos_ref[...], sa_ref[...], sb_ref[...]
    lo = lax.broadcasted_iota(jnp.int32, (bt, V7X_LANES), 1) < HEAD_DIM

    def proj(c0, width):
        return jnp.dot(h, w_ref[:, c0:c0 + width], preferred_element_type=F32)

    def rope(x):
        return x * cos + pltpu.roll(x, V7X_LANES - 16, 1) * sa + pltpu.roll(x, 16, 1) * sb

    def headnorm(x, w):
        x2 = x * x
        s_lo = jnp.sum(jnp.where(lo, x2, 0.0), axis=-1, keepdims=True)
        s_hi = jnp.sum(jnp.where(lo, 0.0, x2), axis=-1, keepdims=True)
        r = jnp.where(lo, lax.rsqrt(s_lo / HEAD_DIM + EPS), lax.rsqrt(s_hi / HEAD_DIM + EPS))
        return x * r * w

    def tile(p, c):
        return p[:, c * V7X_LANES:(c + 1) * V7X_LANES]

    p = proj(OFF_KA, COL_KA)
    for c in range(A_HEADS):
        ka_ref[c] = rope(tile(p, c)).astype(BF16)

    p = proj(OFF_VA, COL_VA)
    for c in range(A_HEADS):
        va_ref[c] = tile(p, c).astype(BF16)

    p = proj(OFF_KB, COL_KB + COL_VB)
    kb = rope(headnorm(tile(p, 0), knw_ref[...]))
    kb_sw = pltpu.roll(kb, HEAD_DIM, 1)
    kbd_ref[0] = jnp.where(lo, kb, kb_sw).astype(BF16)
    kbd_ref[1] = jnp.where(lo, kb_sw, kb).astype(BF16)
    vb = tile(p, 1)
    vb_sw = pltpu.roll(vb, HEAD_DIM, 1)
    vbd_ref[0] = jnp.where(lo, vb, vb_sw).astype(BF16)
    vbd_ref[1] = jnp.where(lo, vb_sw, vb).astype(BF16)

    p = proj(OFF_QA, COL_QA)
    for c in range(COL_QA // V7X_LANES):
        qa_ref[:, c * V7X_LANES:(c + 1) * V7X_LANES] = (rope(tile(p, c)) * Q_SCALE).astype(BF16)

    p = proj(OFF_QB, COL_QB)
    qnw = qnw_ref[...]
    for c in range(COL_QB // V7X_LANES):
        qb_ref[:, c * V7X_LANES:(c + 1) * V7X_LANES] = (
            rope(headnorm(tile(p, c), qnw)) * Q_SCALE).astype(BF16)

    p = proj(OFF_GLU, COL_GLU)
    u_ref[...] = p[:, :C_WIDTH] * jax.nn.sigmoid(p[:, C_WIDTH:])


def _inproj(xy, mod_l, w_main, tabs, qnw, knw, dims):
    t_rows, nps, nbatch = dims["T"], dims["S"] // TOK_BLOCK, dims["B"]
    bt = TOK_BLOCK
    row = lambda r: (r, 0)
    full = lambda r: (0, 0)
    out_shapes = (
        jax.ShapeDtypeStruct((t_rows, COL_QA), BF16),
        jax.ShapeDtypeStruct((A_HEADS, t_rows, V7X_LANES), BF16),
        jax.ShapeDtypeStruct((A_HEADS, t_rows, V7X_LANES), BF16),
        jax.ShapeDtypeStruct((t_rows, COL_QB), BF16),
        jax.ShapeDtypeStruct((B_KV, t_rows, V7X_LANES), BF16),
        jax.ShapeDtypeStruct((B_KV, t_rows, V7X_LANES), BF16),
        jax.ShapeDtypeStruct((t_rows, C_WIDTH), F32),
    )
    grp = pl.BlockSpec((B_KV, bt, V7X_LANES), lambda r: (0, r, 0))
    heads = pl.BlockSpec((A_HEADS, bt, V7X_LANES), lambda r: (0, r, 0))
    return pl.pallas_call(
        _inproj_kernel,
        grid=(t_rows // bt,),
        in_specs=[pl.BlockSpec((bt, D_MODEL), row),
                  pl.BlockSpec((1, 6, D_MODEL), lambda r: (jnp.minimum(r // nps, nbatch), 0, 0)),
                  pl.BlockSpec((D_MODEL, OFF_GATE), full),
                  pl.BlockSpec((bt, V7X_LANES), row),
                  pl.BlockSpec((bt, V7X_LANES), row),
                  pl.BlockSpec((bt, V7X_LANES), row),
                  pl.BlockSpec((1, V7X_LANES), full),
                  pl.BlockSpec((1, V7X_LANES), full)],
        out_specs=(pl.BlockSpec((bt, COL_QA), row), heads, heads, pl.BlockSpec((bt, COL_QB), row),
                   grp, grp, pl.BlockSpec((bt, C_WIDTH), row)),
        out_shape=out_shapes,
        compiler_params=_cparams(("arbitrary",)),
        name="inproj",
    )(xy, mod_l, w_main, tabs[0], tabs[1], tabs[2], qnw, knw)


def _kv_unroll(n_chunks):
    return ATT_UNROLL if n_chunks % ATT_UNROLL == 0 else 1


def _softmax_step(s, v, m, l, acc):
    mn = jnp.maximum(m, jnp.max(s, axis=-1, keepdims=True))
    p = jnp.exp2(s - mn)
    alpha = jnp.exp2(m - mn)
    l = alpha * l + jnp.sum(p, axis=-1, keepdims=True)
    acc = alpha * acc + jnp.dot(p.astype(BF16), v, preferred_element_type=F32)
    return mn, l, acc


def _flash_rows(q, load_lat, kc, vc, *, nq_lat, n_lat_chunks, tk):
    rows = q.shape[0]

    def step(k, v, carry):
        m, l, a = carry
        s = lax.dot_general(q, k, NT_DIMS, preferred_element_type=F32)
        return _softmax_step(s, v, m, l, a)

    unroll = _kv_unroll(n_lat_chunks)

    def lat_body(j, carry):
        for r in range(unroll):
            off = pl.multiple_of((j * unroll + r) * tk, tk)
            carry = step(*load_lat(off), carry)
        return carry

    carry = (jnp.full((rows, 1), NEG_BIG, F32), jnp.zeros((rows, 1), F32),
             jnp.zeros((rows, V7X_LANES), F32))
    n_lat = jnp.where(pl.program_id(2) < nq_lat, n_lat_chunks // unroll, 0)
    carry = lax.fori_loop(0, n_lat, lat_body, carry)
    _, l, a = step(kc, vc, carry)
    return a / l


def _attn_a_kernel(q_ref, kl_ref, vl_ref, kc_ref, vc_ref, lamv_ref, sw_ref, o_ref, *,
                   nq_lat, n_lat_chunks, tk, lam_init):
    tq = q_ref.shape[0]
    q = q_ref[...]
    lo = lax.broadcasted_iota(jnp.int32, (tq, V7X_LANES), 1) < HEAD_DIM
    zero = jnp.zeros_like(q)
    q12 = jnp.concatenate([jnp.where(lo, q, zero), jnp.where(lo, zero, q)], axis=0)
    o12 = _flash_rows(q12, lambda off: (kl_ref[0, pl.ds(off, tk), :], vl_ref[0, pl.ds(off, tk), :]),
                      kc_ref[0], vc_ref[0], nq_lat=nq_lat, n_lat_chunks=n_lat_chunks, tk=tk)

    lv = lamv_ref[...]
    lam = (jnp.exp(jnp.sum(lv[0:1] * lv[1:2], axis=-1, keepdims=True))
           - jnp.exp(jnp.sum(lv[2:3] * lv[3:4], axis=-1, keepdims=True)) + lam_init)
    o = o12[:tq] - lam * o12[tq:]
    ms = jnp.mean(o * o, axis=-1, keepdims=True)
    o_ref[...] = (o * lax.rsqrt(ms + EPS) * sw_ref[...] * (1.0 - lam_init)).astype(BF16)


def _attn_b_kernel(q_ref, kl_ref, vl_ref, kc_ref, vc_ref, o_ref, *, nq_lat, n_lat_chunks, tk):
    tq = q_ref.shape[0]
    lo = lax.broadcasted_iota(jnp.int32, (tq, V7X_LANES), 1) < HEAD_DIM
    parts = []
    for c in range(2):
        qc = q_ref[:, c * V7X_LANES:(c + 1) * V7X_LANES]
        zero = jnp.zeros_like(qc)
        parts += [jnp.where(lo, qc, zero), jnp.where(lo, zero, qc)]
    q4 = jnp.concatenate(parts, axis=0)
    o = _flash_rows(q4, lambda off: (kl_ref[0, pl.ds(off, tk), :], vl_ref[0, pl.ds(off, tk), :]),
                    kc_ref[0], vc_ref[0], nq_lat=nq_lat, n_lat_chunks=n_lat_chunks, tk=tk)
    for c in range(2):
        o_ref[:, c * V7X_LANES:(c + 1) * V7X_LANES] = jnp.where(
            lo, o[(2 * c) * tq:(2 * c + 1) * tq], o[(2 * c + 1) * tq:(2 * c + 2) * tq]).astype(BF16)


def _q_block_index(b, qi, nq_lat, nq_ctx, nbatch):
    return jnp.where(qi < nq_lat, b * nq_lat + qi, nbatch * nq_lat + b * nq_ctx + (qi - nq_lat))


def _attention(qa, ka, va, qb, kbd, vbd, lamv, subln_w, lam_init, with_ctx_queries, dims):
    s_len, ctx_len, nbatch, t_rows = dims["S"], dims["CTX"], dims["B"], dims["T"]
    tq, tk = ATT_TQ, min(ATT_TK, s_len)
    nq_lat, nq_ctx = s_len // tq, ctx_len // tq
    nq = nq_lat + (nq_ctx if with_ctx_queries else 0)
    ctx_blk0 = (nbatch * s_len) // ctx_len
    qidx = functools.partial(_q_block_index, nq_lat=nq_lat, nq_ctx=nq_ctx, nbatch=nbatch)
    sem = ("arbitrary", "arbitrary", "arbitrary")

    oa = pl.pallas_call(
        functools.partial(_attn_a_kernel, nq_lat=nq_lat, n_lat_chunks=s_len // tk, tk=tk,
                          lam_init=lam_init),
        grid=(nbatch, A_HEADS, nq),
        in_specs=[pl.BlockSpec((tq, V7X_LANES), lambda b, h, qi: (qidx(b, qi), h)),
                  pl.BlockSpec((1, s_len, V7X_LANES), lambda b, h, qi: (h, b, 0)),
                  pl.BlockSpec((1, s_len, V7X_LANES), lambda b, h, qi: (h, b, 0)),
                  pl.BlockSpec((1, ctx_len, V7X_LANES), lambda b, h, qi: (h, ctx_blk0 + b, 0)),
                  pl.BlockSpec((1, ctx_len, V7X_LANES), lambda b, h, qi: (h, ctx_blk0 + b, 0)),
                  pl.BlockSpec((4, HEAD_DIM), lambda b, h, qi: (0, 0)),
                  pl.BlockSpec((1, V7X_LANES), lambda b, h, qi: (0, 0))],
        out_specs=pl.BlockSpec((tq, V7X_LANES), lambda b, h, qi: (qidx(b, qi), h)),
        out_shape=jax.ShapeDtypeStruct((t_rows, A_HEADS * V7X_LANES), BF16),
        compiler_params=_cparams(sem),
        name="attn_a",
    )(qa, ka, va, ka, va, lamv, subln_w)

    gw = 2 * V7X_LANES
    ob = pl.pallas_call(
        functools.partial(_attn_b_kernel, nq_lat=nq_lat, n_lat_chunks=s_len // tk, tk=tk),
        grid=(nbatch, B_KV, nq),
        in_specs=[pl.BlockSpec((tq, gw), lambda b, g, qi: (qidx(b, qi), g)),
                  pl.BlockSpec((1, s_len, V7X_LANES), lambda b, g, qi: (g, b, 0)),
                  pl.BlockSpec((1, s_len, V7X_LANES), lambda b, g, qi: (g, b, 0)),
                  pl.BlockSpec((1, ctx_len, V7X_LANES), lambda b, g, qi: (g, ctx_blk0 + b, 0)),
                  pl.BlockSpec((1, ctx_len, V7X_LANES), lambda b, g, qi: (g, ctx_blk0 + b, 0))],
        out_specs=pl.BlockSpec((tq, gw), lambda b, g, qi: (qidx(b, qi), g)),
        out_shape=jax.ShapeDtypeStruct((t_rows, B_KV * gw), BF16),
        compiler_params=_cparams(sem),
        name="attn_b",
    )(qb, kbd, vbd, kbd, vbd)
    return oa, ob


def _merge_kernel(x_ref, mod_ref, oa_ref, ob_ref, u_ref, up_ref, un_ref, cw_ref, cb_ref,
                  lnw_ref, lnb_ref, wg_ref, wa_ref, wb_ref, wc_ref, wo_ref, o_ref, uext_ref, *,
                  nps, ncps, n_lat_blocks):
    bt = x_ref.shape[0]
    r = pl.program_id(0)
    is_lat = r < n_lat_blocks
    pos = jnp.where(is_lat, r % nps, (r - n_lat_blocks) % ncps)
    last_pos = jnp.where(is_lat, nps - 1, ncps - 1)
    keep_prev = (pos != 0).astype(F32)
    keep_next = (pos != last_pos).astype(F32)

    x = x_ref[...]
    mod = mod_ref[0]
    h = _modulated(x, mod, 0, 1).astype(BF16)

    uext_ref[0:HALO, :] = up_ref[...] * keep_prev
    uext_ref[HALO:HALO + bt, :] = u_ref[...]
    uext_ref[HALO + bt:2 * HALO + bt, :] = un_ref[...] * keep_next
    cw = cw_ref[...]
    y = jnp.zeros((bt, C_WIDTH), F32) + cb_ref[...]
    base = HALO - C_KW // 2
    for k in range(C_KW):
        y = y + cw[k:k + 1, :] * uext_ref[base + k:base + k + bt, :]
    mu = jnp.mean(y, axis=-1, keepdims=True)
    yc = y - mu
    var = jnp.mean(yc * yc, axis=-1, keepdims=True)
    yn = yc * lax.rsqrt(var + EPS) * lnw_ref[...] + lnb_ref[...]
    oc = (yn * jax.nn.sigmoid(yn)).astype(BF16)

    def gate(i):
        logits = jnp.dot(h, wg_ref[:, i * D_MODEL:(i + 1) * D_MODEL], preferred_element_type=F32)
        return jax.nn.sigmoid(logits)

    merged = gate(0) * jnp.dot(oa_ref[...], wa_ref[...], preferred_element_type=F32)
    merged = merged + gate(1) * jnp.dot(ob_ref[...], wb_ref[...], preferred_element_type=F32)
    merged = merged + gate(2) * jnp.dot(oc, wc_ref[...], preferred_element_type=F32)
    out = jnp.dot(merged.astype(BF16), wo_ref[...], preferred_element_type=F32)
    o_ref[...] = x + mod[2:3] * out


def _merge(xy, mod_l, oa, ob, u, cw, cb, lnw, lnb, wg, wa, wb, wc, wo, n_blocks, dims):
    bt = TOK_BLOCK
    t_rows, nps, nbatch = dims["T"], dims["S"] // bt, dims["B"]
    hpb = bt // HALO
    n_halo = t_rows // HALO
    row = lambda r: (r, 0)
    full = lambda r: (0, 0)
    return pl.pallas_call(
        functools.partial(_merge_kernel, nps=nps, ncps=dims["CTX"] // bt,
                          n_lat_blocks=nbatch * nps),
        grid=(n_blocks,),
        in_specs=[pl.BlockSpec((bt, D_MODEL), row),
                  pl.BlockSpec((1, 6, D_MODEL), lambda r: (jnp.minimum(r // nps, nbatch), 0, 0)),
                  pl.BlockSpec((bt, 4 * V7X_LANES), row),
                  pl.BlockSpec((bt, 4 * V7X_LANES), row),
                  pl.BlockSpec((bt, C_WIDTH), row),
                  pl.BlockSpec((HALO, C_WIDTH), lambda r: (jnp.maximum(r * hpb - 1, 0), 0)),
                  pl.BlockSpec((HALO, C_WIDTH), lambda r: (jnp.minimum((r + 1) * hpb, n_halo - 1), 0)),
                  pl.BlockSpec((C_KW + 1, C_WIDTH), full),
                  pl.BlockSpec((1, C_WIDTH), full),
                  pl.BlockSpec((1, C_WIDTH), full),
                  pl.BlockSpec((1, C_WIDTH), full),
                  pl.BlockSpec((D_MODEL, 3 * D_MODEL), full),
                  pl.BlockSpec((C_WIDTH, D_MODEL), full),
                  pl.BlockSpec((C_WIDTH, D_MODEL), full),
                  pl.BlockSpec((C_WIDTH, D_MODEL), full),
                  pl.BlockSpec((D_MODEL, D_MODEL), full)],
        out_specs=pl.BlockSpec((bt, D_MODEL), row),
        out_shape=jax.ShapeDtypeStruct((n_blocks * bt, D_MODEL), F32),
        scratch_shapes=[pltpu.VMEM((bt + 2 * HALO, C_WIDTH), F32)],
        compiler_params=_cparams(("arbitrary",)),
        name="merge",
    )(xy, mod_l, oa, ob, u, u, u, cw, cb, lnw, lnb, wg, wa, wb, wc, wo)


def _topk_desc(s, k):
    row = lax.broadcasted_iota(jnp.int32, (k, s.shape[1]), 0)

    def body(i, carry):
        work, vals = carry
        m = jnp.max(work, axis=0, keepdims=True)
        vals = jnp.where(row == i, m, vals)
        work = jnp.where(work == m, -jnp.inf, work)
        return work, vals

    _, vals = lax.fori_loop(0, k, body, (s, jnp.zeros((k, s.shape[1]), F32)))
    return vals


def _candidate_sums(sv1, sv2):
    half = P_TOPK // 2
    lead, rest = sv1[:half], sv1[half:]
    row = lax.broadcasted_iota(jnp.int32, lead.shape, 0)
    tiles = [lead + sv2[0:1], rest + sv2[0:1], lead + sv2[1:2]]
    for k2 in range(2, half):
        tiles.append(jnp.where(row < P_TOPK // (k2 + 1), lead + sv2[k2:k2 + 1], -jnp.inf))
    tiles.append(sv1[0:1] + sv2[half:])
    return jnp.concatenate(tiles, axis=0)


def _bf16_pair_words(x):
    bits = pltpu.bitcast(x.astype(BF16).astype(F32), jnp.int32)
    hi = lax.shift_right_logical(bits, 16)
    return hi | lax.shift_left(hi, 16)


def _peer_select_kernel(x_ref, mod_ref, wq_ref, k1_ref, k2_ref,
                        t_ref, need_ref, e1_ref, r2_ref, e2_ref, s1_scr):
    bt = x_ref.shape[0]
    n_lt = bt // V7X_LANES
    hb = _modulated(x_ref[...], mod_ref[0], 3, 4).astype(BF16)
    t_ref[...] = hb
    q = jnp.dot(hb, wq_ref[...], preferred_element_type=F32).astype(BF16)
    s1p = lax.dot_general(k1_ref[...], q, NT_DIMS, preferred_element_type=F32)
    for lt in range(n_lt):
        s1_scr[lt] = s1p[:, lt * V7X_LANES:(lt + 1) * V7X_LANES]
    head_row = lax.broadcasted_iota(jnp.int32, (P_HEADS, bt), 0)
    zero_t = jnp.zeros((P_HEADS, bt), F32)
    thr1, max1, tau_t = zero_t, zero_t, zero_t
    sv2_t = [zero_t] * P_TOPK
    for hh in range(P_HEADS):
        mine = head_row == hh
        qh = q[:, hh * V7X_LANES:(hh + 1) * V7X_LANES]
        s1 = jnp.concatenate([s1_scr[lt, pl.ds(hh, P_NKEYS, stride=P_HEADS), :]
                              for lt in range(n_lt)], axis=1)
        s2 = lax.dot_general(k2_ref[hh], qh, NT_DIMS, preferred_element_type=F32)
        sv1 = _topk_desc(s1, P_TOPK)
        sv2 = _topk_desc(s2, P_TOPK)
        top = _topk_desc(_candidate_sums(sv1, sv2), P_TOPK)
        z = jnp.sum(jnp.exp(top - top[0:1]), axis=0, keepdims=True)
        rank2 = jnp.zeros_like(s2)
        for k in range(P_TOPK):
            rank2 = rank2 + jnp.where(sv2[k:k + 1] > s2, 1.0, 0.0)
            sv2_t[k] = jnp.where(mine, sv2[k:k + 1], sv2_t[k])
        r2_ref[0, hh] = pltpu.bitcast(rank2.astype(BF16), jnp.int32)
        e2_ref[0, hh] = pltpu.bitcast((jnp.exp(s2 - sv2[0:1]) / z).astype(BF16), jnp.int32)
        tau_t = jnp.where(mine, top[P_TOPK - 1:P_TOPK], tau_t)
        thr1 = jnp.where(mine, sv1[P_TOPK - 1:P_TOPK], thr1)
        max1 = jnp.where(mine, sv1[0:1], max1)
    for lt in range(n_lt):
        ls = slice(lt * V7X_LANES, (lt + 1) * V7X_LANES)
        s1t = s1_scr[lt].reshape(P_NKEYS, P_HEADS, V7X_LANES)
        count = jnp.zeros_like(s1t)
        for k in range(P_TOPK):
            count = count + jnp.where(s1t + sv2_t[k][None, :, ls] >= tau_t[None, :, ls], 1.0, 0.0)
        need = jnp.where(s1t >= thr1[None, :, ls], count, 0.0)
        need_ref[0, lt] = _bf16_pair_words(need.reshape(P_NKEYS * P_HEADS, V7X_LANES))
        e1 = jnp.exp(s1t - max1[None, :, ls])
        e1_ref[0, lt] = _bf16_pair_words(e1.reshape(P_NKEYS * P_HEADS, V7X_LANES))


def _peer_dense_kernel(x_ref, mod_ref, t_ref, need_ref, e1_ref, r2_ref, e2_ref, u_ref, vt_ref,
                       fnw_ref, o_ref, a_scr, m_scr, acc_scr, *, n_chunks, final_norm):
    bt = x_ref.shape[0]
    c = pl.program_id(1)
    ipc = u_ref.shape[0] // P_NKEYS
    n_lt = bt // V7X_LANES

    @pl.when(c == 0)
    def _init():
        acc_scr[...] = jnp.zeros_like(acc_scr)
        a_scr[...] = jnp.zeros_like(a_scr)

    cprev = jnp.maximum(c - 1, 0)

    def one_key(ii, _):
        roff = pl.multiple_of(ii * P_NKEYS, P_NKEYS)
        hrow = pl.ds(pl.multiple_of((cprev * ipc + ii) * P_HEADS, P_HEADS), P_HEADS)
        for lt in range(n_lt):
            ls = slice(lt * V7X_LANES, (lt + 1) * V7X_LANES)
            w = [jnp.zeros((ROW_GROUP, V7X_LANES), BF16) for _ in range(P_NKEYS // ROW_GROUP)]
            need_t = need_ref[0, lt, hrow, :]
            e1_t = e1_ref[0, lt, hrow, :]
            zero = jnp.zeros((ROW_GROUP, V7X_LANES), BF16)
            for hh in range(P_HEADS):
                words = (ROW_GROUP // 2, V7X_LANES)
                need_b = pltpu.bitcast(jnp.broadcast_to(need_t[hh:hh + 1], words), BF16)
                e1_b = pltpu.bitcast(jnp.broadcast_to(e1_t[hh:hh + 1], words), BF16)
                for jg in range(P_NKEYS // ROW_GROUP):
                    js = slice(jg * (ROW_GROUP // 2), (jg + 1) * (ROW_GROUP // 2))
                    rank2 = pltpu.bitcast(r2_ref[0, hh, js, ls], BF16)
                    e2 = pltpu.bitcast(e2_ref[0, hh, js, ls], BF16)
                    w[jg] = w[jg] + jnp.where(rank2 < need_b, e1_b * e2, zero)
            for jg in range(P_NKEYS // ROW_GROUP):
                rows = pl.ds(roff + jg * ROW_GROUP, ROW_GROUP)
                a = a_scr[rows, ls]
                g = 0.5 * a * (1.0 + lax.erf(a * (2.0 ** -0.5)))
                m_scr[rows, ls] = g.astype(BF16) * w[jg]
        return 0

    lax.fori_loop(0, ipc, one_key, 0)

    acc_scr[...] += jnp.dot(vt_ref[0], m_scr[...], preferred_element_type=F32)
    a_scr[...] = lax.dot_general(u_ref[...], t_ref[...], NT_DIMS, preferred_element_type=F32)

    @pl.when(c == n_chunks)
    def _finish():
        x = x_ref[...]
        y = x + mod_ref[0][5:6] * acc_scr[...].T
        if final_norm:
            ms = jnp.mean(y * y, axis=-1, keepdims=True)
            y = y * lax.rsqrt(ms + EPS) * fnw_ref[...]
        o_ref[...] = y


def _peer(x1, mod_l, wq, k1p, k2p, u_tab, vt_tab, fnw, n_blocks, final_norm, dims):
    bt, ec = PEER_BLOCK, vt_tab.shape[2]
    nps, nbatch = dims["S"] // bt, dims["B"]
    n_chunks = vt_tab.shape[0]
    n_lt = bt // V7X_LANES
    rows_ih = P_NKEYS * P_HEADS
    mod_spec = pl.BlockSpec((1, 6, D_MODEL), lambda r, *_: (jnp.minimum(r // nps, nbatch), 0, 0))
    ih_shape = jax.ShapeDtypeStruct((n_blocks, n_lt, rows_ih, V7X_LANES), jnp.int32)
    hj_shape = jax.ShapeDtypeStruct((n_blocks, P_HEADS, P_NKEYS // 2, bt), jnp.int32)
    ih_spec = pl.BlockSpec((1, n_lt, rows_ih, V7X_LANES), lambda r, *_: (r, 0, 0, 0))
    hj_spec = pl.BlockSpec((1, P_HEADS, P_NKEYS // 2, bt), lambda r, *_: (r, 0, 0, 0))
    tok_spec = pl.BlockSpec((bt, D_MODEL), lambda r, *_: (r, 0))

    t_mod, need, e1, rank2, e2 = pl.pallas_call(
        _peer_select_kernel,
        grid=(n_blocks,),
        in_specs=[tok_spec, mod_spec,
                  pl.BlockSpec((D_MODEL, P_HEADS * V7X_LANES), lambda r: (0, 0)),
                  pl.BlockSpec((rows_ih, P_HEADS * V7X_LANES), lambda r: (0, 0)),
                  pl.BlockSpec((P_HEADS, P_NKEYS, V7X_LANES), lambda r: (0, 0, 0))],
        out_specs=(tok_spec, ih_spec, ih_spec, hj_spec, hj_spec),
        out_shape=(jax.ShapeDtypeStruct((n_blocks * bt, D_MODEL), BF16),
                   ih_shape,
                   ih_shape,
                   hj_shape,
                   hj_shape),
        scratch_shapes=[pltpu.VMEM((n_lt, rows_ih, V7X_LANES), F32)],
        compiler_params=_cparams(("arbitrary",)),
        name="peer_select",
    )(x1, mod_l, wq, k1p, k2p)

    return pl.pallas_call(
        functools.partial(_peer_dense_kernel, n_chunks=n_chunks, final_norm=final_norm),
        grid=(n_blocks, n_chunks + 1),
        in_specs=[tok_spec, mod_spec, tok_spec, ih_spec, ih_spec, hj_spec, hj_spec,
                  pl.BlockSpec((ec, D_MODEL), lambda r, c: (jnp.minimum(c, n_chunks - 1), 0)),
                  pl.BlockSpec((1, D_MODEL, ec), lambda r, c: (jnp.maximum(c - 1, 0), 0, 0)),
                  pl.BlockSpec((1, D_MODEL), lambda r, c: (0, 0))],
        out_specs=tok_spec,
        out_shape=jax.ShapeDtypeStruct((n_blocks * bt, D_MODEL), F32),
        scratch_shapes=[pltpu.VMEM((ec, bt), F32),
                        pltpu.VMEM((ec, bt), BF16),
                        pltpu.VMEM((D_MODEL, bt), F32)],
        compiler_params=_cparams(("arbitrary", "arbitrary")),
        name="peer_dense",
    )(x1, mod_l, t_mod, need, e1, rank2, e2, u_tab, vt_tab, fnw)


def _rope_tables(s_len, nbatch, n_ctx_rows):
    t = jnp.arange(s_len, dtype=jnp.int32)
    row = (t // GRID_W).astype(F32)
    col = (t % GRID_W).astype(F32)
    axis_dim = HEAD_DIM // 2
    inv = ROPE_THETA ** (-jnp.arange(0, axis_dim, 2, dtype=F32) / axis_dim)
    ar = row[:, None] * inv[None, :]
    ac = col[:, None] * inv[None, :]
    ang = jnp.concatenate([ar, ar, ac, ac], axis=-1)
    cos, sin = jnp.cos(ang), jnp.sin(ang)
    quarter = jnp.arange(HEAD_DIM) // (HEAD_DIM // 4)
    first = (quarter % 2 == 0)[None, :]
    sin_a = jnp.where(first, -sin, 0.0)
    sin_b = jnp.where(first, 0.0, sin)

    def expand(tab, ctx_fill):
        lat = jnp.tile(jnp.concatenate([tab, tab], axis=-1), (nbatch, 1))
        return jnp.concatenate([lat, jnp.full((n_ctx_rows, V7X_LANES), ctx_fill, F32)], axis=0)

    return expand(cos, 1.0), expand(sin_a, 0.0), expand(sin_b, 0.0)


def _padded_keys(keys_l):
    z = jnp.zeros_like(keys_l[:, 0])
    k0 = jnp.concatenate([keys_l[:, 0], z], axis=-1).transpose(1, 0, 2)
    eye = jnp.eye(P_HEADS, dtype=keys_l.dtype)
    k1p = (k0[:, :, None, :] * eye[None, :, :, None]).reshape(
        P_NKEYS * P_HEADS, P_HEADS * V7X_LANES)
    k2p = jnp.concatenate([z, keys_l[:, 1]], axis=-1)
    return k1p.astype(BF16), k2p.astype(BF16)


def _chunked_transpose(v_tab, ec):
    n_exp, d = v_tab.shape
    return v_tab.astype(BF16).reshape(n_exp // ec, ec, d).transpose(0, 2, 1)


def kernel(x, c, ctx, c_ctx, w_ada, b_ada, w_in, lam_q1, lam_k1, lam_q2, lam_k2, subln_w, q_norm_w, k_norm_w, conv_w, conv_b, conv_ln_w, conv_ln_b, w_branch_a, w_branch_b, w_branch_c, w_out, peer_wq, peer_keys, peer_u, peer_v, final_norm_w):
    nbatch, s_len, d = x.shape
    ctx_len = ctx.shape[1]
    depth = w_ada.shape[0]
    n_lat, n_ctx = nbatch * s_len, nbatch * ctx_len
    dims = {"B": nbatch, "S": s_len, "CTX": ctx_len, "T": n_lat + n_ctx}
    assert d == D_MODEL and nbatch + 1 <= MOD_ROWS
    assert s_len % PEER_BLOCK == 0 and ctx_len % ATT_TQ == 0 and ctx_len % TOK_BLOCK == 0
    assert n_ctx % PEER_BLOCK == 0 and n_lat % ctx_len == 0

    cvec = jnp.zeros((MOD_ROWS, d), F32).at[:nbatch].set(c).at[nbatch].set(c_ctx)
    mod = _ada_rows(cvec, w_ada, b_ada).reshape(depth, MOD_ROWS, 6, d)
    tabs = _rope_tables(s_len, nbatch, n_ctx)
    tile2 = lambda v: jnp.concatenate([v, v], axis=-1)[None, :]

    xy = jnp.concatenate([x.reshape(n_lat, d), ctx.reshape(n_ctx, d)], axis=0)
    for l in range(depth):
        last = l == depth - 1
        lam_init = 0.8 - 0.6 * math.exp(-0.3 * l)
        w_bf = w_in[l].astype(BF16)
        lamv = jnp.stack([lam_q1[l], lam_k1[l], lam_q2[l], lam_k2[l]], axis=0)

        qa, ka, va, qb, kbd, vbd, u = _inproj(
            xy, mod[l], w_bf[:, :OFF_GATE], tabs, tile2(q_norm_w[l]), tile2(k_norm_w[l]), dims)
        oa, ob = _attention(qa, ka, va, qb, kbd, vbd, lamv, subln_w[l][None, :], lam_init,
                            not last, dims)
        n_rows = n_lat if last else n_lat + n_ctx
        cw = jnp.concatenate([conv_w[l], jnp.zeros((1, C_WIDTH), F32)], axis=0)
        x1 = _merge(xy, mod[l], oa, ob, u, cw, conv_b[l][None, :], conv_ln_w[l][None, :],
                    conv_ln_b[l][None, :], w_bf[:, OFF_GATE:], w_branch_a[l].astype(BF16),
                    w_branch_b[l].astype(BF16), w_branch_c[l].astype(BF16),
                    w_out[l].astype(BF16), n_rows // TOK_BLOCK, dims)
        k1p, k2p = _padded_keys(peer_keys[l])
        xy = _peer(x1, mod[l], peer_wq[l].astype(BF16), k1p, k2p,
                   peer_u[l].astype(BF16), _chunked_transpose(peer_v[l], PEER_ECHUNK),
                   final_norm_w[None, :],
                   n_rows // PEER_BLOCK, last, dims)
    return xy.reshape(nbatch, s_len, d)
```

```python
import functools
import math

import jax
import jax.numpy as jnp
from jax import lax
from jax.experimental import pallas as pl
from jax.experimental.pallas import tpu as pltpu

F32 = jnp.float32
BF16 = jnp.bfloat16

D_MODEL = 1024
DEPTH = 2
GRID_W = 64
HEAD_DIM = 64
ROPE_THETA = 10000.0
EPS = 1e-6
A_HEADS = 4
B_KV = 2
C_WIDTH = 512
C_KW = 31
P_HEADS = 8
P_NKEYS = 128
P_TOPK = 16

COL_KA, COL_VA, COL_KB, COL_VB, COL_QA, COL_QB, COL_GLU = 512, 512, 128, 128, 512, 512, 1024
OFF_KA = 0
OFF_VA = OFF_KA + COL_KA
OFF_KB = OFF_VA + COL_VA
OFF_QA = OFF_KB + COL_KB + COL_VB
OFF_QB = OFF_QA + COL_QA
OFF_GLU = OFF_QB + COL_QB
OFF_GATE = OFF_GLU + COL_GLU

V7X_LANES = 128
V7X_VMEM_LIMIT = 56 * 1024 * 1024
HALO = 16

TOK_BLOCK = 256
ATT_TQ = 512
ATT_TK = 512
ATT_UNROLL = 8
PEER_BLOCK = 512
PEER_ECHUNK = 1024
MOD_ROWS = 8
ROW_GROUP = 16

Q_SCALE = (HEAD_DIM ** -0.5) * math.log2(math.e)
NEG_BIG = -1e30
NT_DIMS = (((1,), (1,)), ((), ()))


def _cparams(sem):
    return pltpu.CompilerParams(dimension_semantics=sem, vmem_limit_bytes=V7X_VMEM_LIMIT)


def _modulated(x, mod, shift_row, scale_row):
    ms = jnp.mean(x * x, axis=-1, keepdims=True)
    xn = x * lax.rsqrt(ms + EPS)
    return xn * (1.0 + mod[scale_row:scale_row + 1]) + mod[shift_row:shift_row + 1]


def _ada_kernel(c_ref, w_ref, b_ref, o_ref):
    c = c_ref[...]
    sc = c * jax.nn.sigmoid(c)
    o_ref[0] = jnp.dot(sc, w_ref[0], preferred_element_type=F32,
                       precision=lax.Precision.HIGHEST) + b_ref[0]


def _ada_rows(cvec, w_ada, b_ada):
    depth, d, n = w_ada.shape
    tn = 1536
    return pl.pallas_call(
        _ada_kernel,
        grid=(depth, n // tn),
        in_specs=[pl.BlockSpec((MOD_ROWS, d), lambda l, j: (0, 0)),
                  pl.BlockSpec((1, d, tn), lambda l, j: (l, 0, j)),
                  pl.BlockSpec((1, 1, tn), lambda l, j: (l, 0, j))],
        out_specs=pl.BlockSpec((1, MOD_ROWS, tn), lambda l, j: (l, 0, j)),
        out_shape=jax.ShapeDtypeStruct((depth, MOD_ROWS, n), F32),
        compiler_params=_cparams(("arbitrary", "arbitrary")),
        name="ada_rows",
    )(cvec, w_ada, b_ada.reshape(depth, 1, n))


def _inproj_kernel(x_ref, mod_ref, w_ref, cos_ref, sa_ref, sb_ref, qnw_ref, knw_ref,
                   qa_ref, ka_ref, va_ref, qb_ref, kbd_ref, vbd_ref, u_ref):
    bt = x_ref.shape[0]
    h = _modulated(x_ref[...], mod_ref[0], 0, 1).astype(BF16)
    cos, sa, sb = cos_ref[...], sa_ref[...], sb_ref[...]
    lo = lax.broadcasted_iota(jnp.int32, (bt, V7X_LANES), 1) < HEAD_DIM

    def proj(c0, width):
        return jnp.dot(h, w_ref[:, c0:c0 + width], preferred_element_type=F32)

    def rope(x):
        return x * cos + pltpu.roll(x, V7X_LANES - 16, 1) * sa + pltpu.roll(x, 16, 1) * sb

    def headnorm(x, w):
        x2 = x * x
        s_lo = jnp.sum(jnp.where(lo, x2, 0.0), axis=-1, keepdims=True)
        s_hi = jnp.sum(jnp.where(lo, 0.0, x2), axis=-1, keepdims=True)
        r = jnp.where(lo, lax.rsqrt(s_lo / HEAD_DIM + EPS), lax.rsqrt(s_hi / HEAD_DIM + EPS))
        return x * r * w

    def tile(p, c):
        return p[:, c * V7X_LANES:(c + 1) * V7X_LANES]

    p = proj(OFF_KA, COL_KA)
    for c in range(A_HEADS):
        ka_ref[c] = rope(tile(p, c)).astype(BF16)

    p = proj(OFF_VA, COL_VA)
    for c in range(A_HEADS):
        va_ref[c] = tile(p, c).astype(BF16)

    p = proj(OFF_KB, COL_KB + COL_VB)
    kb = rope(headnorm(tile(p, 0), knw_ref[...]))
    kb_sw = pltpu.roll(kb, HEAD_DIM, 1)
    kbd_ref[0] = jnp.where(lo, kb, kb_sw).astype(BF16)
    kbd_ref[1] = jnp.where(lo, kb_sw, kb).astype(BF16)
    vb = tile(p, 1)
    vb_sw = pltpu.roll(vb, HEAD_DIM, 1)
    vbd_ref[0] = jnp.where(lo, vb, vb_sw).astype(BF16)
    vbd_ref[1] = jnp.where(lo, vb_sw, vb).astype(BF16)

    p = proj(OFF_QA, COL_QA)
    for c in range(COL_QA // V7X_LANES):
        qa_ref[:, c * V7X_LANES:(c + 1) * V7X_LANES] = (rope(tile(p, c)) * Q_SCALE).astype(BF16)

    p = proj(OFF_QB, COL_QB)
    qnw = qnw_ref[...]
    for c in range(COL_QB // V7X_LANES):
        qb_ref[:, c * V7X_LANES:(c + 1) * V7X_LANES] = (
            rope(headnorm(tile(p, c), qnw)) * Q_SCALE).astype(BF16)

    p = proj(OFF_GLU, COL_GLU)
    u_ref[...] = p[:, :C_WIDTH] * jax.nn.sigmoid(p[:, C_WIDTH:])


def _inproj(xy, mod_l, w_main, tabs, qnw, knw, dims):
    t_rows, nps, nbatch = dims["T"], dims["S"] // TOK_BLOCK, dims["B"]
    bt = TOK_BLOCK
    row = lambda r: (r, 0)
    full = lambda r: (0, 0)
    out_shapes = (
        jax.ShapeDtypeStruct((t_rows, COL_QA), BF16),
        jax.ShapeDtypeStruct((A_HEADS, t_rows, V7X_LANES), BF16),
        jax.ShapeDtypeStruct((A_HEADS, t_rows, V7X_LANES), BF16),
        jax.ShapeDtypeStruct((t_rows, COL_QB), BF16),
        jax.ShapeDtypeStruct((B_KV, t_rows, V7X_LANES), BF16),
        jax.ShapeDtypeStruct((B_KV, t_rows, V7X_LANES), BF16),
        jax.ShapeDtypeStruct((t_rows, C_WIDTH), F32),
    )
    grp = pl.BlockSpec((B_KV, bt, V7X_LANES), lambda r: (0, r, 0))
    heads = pl.BlockSpec((A_HEADS, bt, V7X_LANES), lambda r: (0, r, 0))
    return pl.pallas_call(
        _inproj_kernel,
        grid=(t_rows // bt,),
        in_specs=[pl.BlockSpec((bt, D_MODEL), row),
                  pl.BlockSpec((1, 6, D_MODEL), lambda r: (jnp.minimum(r // nps, nbatch), 0, 0)),
                  pl.BlockSpec((D_MODEL, OFF_GATE), full),
                  pl.BlockSpec((bt, V7X_LANES), row),
                  pl.BlockSpec((bt, V7X_LANES), row),
                  pl.BlockSpec((bt, V7X_LANES), row),
                  pl.BlockSpec((1, V7X_LANES), full),
                  pl.BlockSpec((1, V7X_LANES), full)],
        out_specs=(pl.BlockSpec((bt, COL_QA), row), heads, heads, pl.BlockSpec((bt, COL_QB), row),
                   grp, grp, pl.BlockSpec((bt, C_WIDTH), row)),
        out_shape=out_shapes,
        compiler_params=_cparams(("arbitrary",)),
        name="inproj",
    )(xy, mod_l, w_main, tabs[0], tabs[1], tabs[2], qnw, knw)


def _kv_unroll(n_chunks):
    return ATT_UNROLL if n_chunks % ATT_UNROLL == 0 else 1


def _softmax_step(s, v, m, l, acc):
    mn = jnp.maximum(m, jnp.max(s, axis=-1, keepdims=True))
    p = jnp.exp2(s - mn)
    alpha = jnp.exp2(m - mn)
    l = alpha * l + jnp.sum(p, axis=-1, keepdims=True)
    acc = alpha * acc + jnp.dot(p.astype(BF16), v, preferred_element_type=F32)
    return mn, l, acc


def _flash_rows(q, load_lat, kc, vc, *, nq_lat, n_lat_chunks, tk):
    rows = q.shape[0]

    def step(k, v, carry):
        m, l, a = carry
        s = lax.dot_general(q, k, NT_DIMS, preferred_element_type=F32)
        return _softmax_step(s, v, m, l, a)

    unroll = _kv_unroll(n_lat_chunks)

    def lat_body(j, carry):
        for r in range(unroll):
            off = pl.multiple_of((j * unroll + r) * tk, tk)
            carry = step(*load_lat(off), carry)
        return carry

    carry = (jnp.full((rows, 1), NEG_BIG, F32), jnp.zeros((rows, 1), F32),
             jnp.zeros((rows, V7X_LANES), F32))
    n_lat = jnp.where(pl.program_id(2) < nq_lat, n_lat_chunks // unroll, 0)
    carry = lax.fori_loop(0, n_lat, lat_body, carry)
    _, l, a = step(kc, vc, carry)
    return a / l


def _attn_a_kernel(q_ref, kl_ref, vl_ref, kc_ref, vc_ref, lamv_ref, sw_ref, o_ref, *,
                   nq_lat, n_lat_chunks, tk, lam_init):
    tq = q_ref.shape[0]
    q = q_ref[...]
    lo = lax.broadcasted_iota(jnp.int32, (tq, V7X_LANES), 1) < HEAD_DIM
    zero = jnp.zeros_like(q)
    q12 = jnp.concatenate([jnp.where(lo, q, zero), jnp.where(lo, zero, q)], axis=0)
    o12 = _flash_rows(q12, lambda off: (kl_ref[0, pl.ds(off, tk), :], vl_ref[0, pl.ds(off, tk), :]),
                      kc_ref[0], vc_ref[0], nq_lat=nq_lat, n_lat_chunks=n_lat_chunks, tk=tk)

    lv = lamv_ref[...]
    lam = (jnp.exp(jnp.sum(lv[0:1] * lv[1:2], axis=-1, keepdims=True))
           - jnp.exp(jnp.sum(lv[2:3] * lv[3:4], axis=-1, keepdims=True)) + lam_init)
    o = o12[:tq] - lam * o12[tq:]
    ms = jnp.mean(o * o, axis=-1, keepdims=True)
    o_ref[...] = (o * lax.rsqrt(ms + EPS) * sw_ref[...] * (1.0 - lam_init)).astype(BF16)


def _attn_b_kernel(q_ref, kl_ref, vl_ref, kc_ref, vc_ref, o_ref, *, nq_lat, n_lat_chunks, tk):
    tq = q_ref.shape[0]
    lo = lax.broadcasted_iota(jnp.int32, (tq, V7X_LANES), 1) < HEAD_DIM
    parts = []
    for c in range(2):
        qc = q_ref[:, c * V7X_LANES:(c + 1) * V7X_LANES]
        zero = jnp.zeros_like(qc)
        parts += [jnp.where(lo, qc, zero), jnp.where(lo, zero, qc)]
    q4 = jnp.concatenate(parts, axis=0)
    o = _flash_rows(q4, lambda off: (kl_ref[0, pl.ds(off, tk), :], vl_ref[0, pl.ds(off, tk), :]),
                    kc_ref[0], vc_ref[0], nq_lat=nq_lat, n_lat_chunks=n_lat_chunks, tk=tk)
    for c in range(2):
        o_ref[:, c * V7X_LANES:(c + 1) * V7X_LANES] = jnp.where(
            lo, o[(2 * c) * tq:(2 * c + 1) * tq], o[(2 * c + 1) * tq:(2 * c + 2) * tq]).astype(BF16)


def _attention_call(kern, name, q, k, v, extras, q_lanes, n_groups, tq, latent_queries, dims):
    s_len, ctx_len, nbatch = dims["S"], dims["CTX"], dims["B"]
    tk = min(ATT_TK, s_len)
    q_len = s_len if latent_queries else ctx_len
    nq = q_len // tq
    q_blk0 = 0 if latent_queries else (nbatch * s_len) // tq
    ctx_blk0 = (nbatch * s_len) // ctx_len
    extra_specs = [pl.BlockSpec(e.shape, lambda b, g, qi: (0, 0)) for e in extras]
    return pl.pallas_call(
        functools.partial(kern, nq_lat=nq if latent_queries else 0, n_lat_chunks=s_len // tk, tk=tk),
        grid=(nbatch, n_groups, nq),
        in_specs=[pl.BlockSpec((tq, q_lanes), lambda b, g, qi: (q_blk0 + b * nq + qi, g)),
                  pl.BlockSpec((1, s_len, V7X_LANES), lambda b, g, qi: (g, b, 0)),
                  pl.BlockSpec((1, s_len, V7X_LANES), lambda b, g, qi: (g, b, 0)),
                  pl.BlockSpec((1, ctx_len, V7X_LANES), lambda b, g, qi: (g, ctx_blk0 + b, 0)),
                  pl.BlockSpec((1, ctx_len, V7X_LANES), lambda b, g, qi: (g, ctx_blk0 + b, 0)),
                  *extra_specs],
        out_specs=pl.BlockSpec((tq, q_lanes), lambda b, g, qi: (b * nq + qi, g)),
        out_shape=jax.ShapeDtypeStruct((nbatch * q_len, n_groups * q_lanes), BF16),
        compiler_params=_cparams(("arbitrary", "arbitrary", "arbitrary")),
        name=name,
    )(q, k, v, k, v, *extras)


def _attention(qa, ka, va, qb, kbd, vbd, lamv, subln_w, lam_init, with_ctx_queries, dims):
    kern_a = functools.partial(_attn_a_kernel, lam_init=lam_init)
    gw = 2 * V7X_LANES

    def both(latent, tag):
        q_len = dims["S"] if latent else dims["CTX"]
        oa = _attention_call(kern_a, "attn_a" + tag, qa, ka, va, (lamv, subln_w), V7X_LANES,
                             A_HEADS, min(ATT_TQ, q_len), latent, dims)
        ob = _attention_call(_attn_b_kernel, "attn_b" + tag, qb, kbd, vbd, (), gw, B_KV,
                             min(ATT_TQ // 2, q_len), latent, dims)
        return oa, ob

    oa, ob = both(True, "")
    if with_ctx_queries:
        oa_c, ob_c = both(False, "_ctx")
        oa, ob = jnp.concatenate([oa, oa_c], axis=0), jnp.concatenate([ob, ob_c], axis=0)
    return oa, ob


def _merge_kernel(x_ref, mod_ref, oa_ref, ob_ref, u_ref, up_ref, un_ref, cw_ref, cb_ref,
                  lnw_ref, lnb_ref, wg_ref, wa_ref, wb_ref, wc_ref, wo_ref, o_ref, uext_ref, *,
                  nps, ncps, n_lat_blocks):
    bt = x_ref.shape[0]
    r = pl.program_id(0)
    is_lat = r < n_lat_blocks
    pos = jnp.where(is_lat, r % nps, (r - n_lat_blocks) % ncps)
    last_pos = jnp.where(is_lat, nps - 1, ncps - 1)
    keep_prev = (pos != 0).astype(F32)
    keep_next = (pos != last_pos).astype(F32)

    x = x_ref[...]
    mod = mod_ref[0]
    h = _modulated(x, mod, 0, 1).astype(BF16)

    uext_ref[0:HALO, :] = up_ref[...] * keep_prev
    uext_ref[HALO:HALO + bt, :] = u_ref[...]
    uext_ref[HALO + bt:2 * HALO + bt, :] = un_ref[...] * keep_next
    cw = cw_ref[...]
    y = jnp.zeros((bt, C_WIDTH), F32) + cb_ref[...]
    base = HALO - C_KW // 2
    for k in range(C_KW):
        y = y + cw[k:k + 1, :] * uext_ref[base + k:base + k + bt, :]
    mu = jnp.mean(y, axis=-1, keepdims=True)
    yc = y - mu
    var = jnp.mean(yc * yc, axis=-1, keepdims=True)
    yn = yc * lax.rsqrt(var + EPS) * lnw_ref[...] + lnb_ref[...]
    oc = (yn * jax.nn.sigmoid(yn)).astype(BF16)

    def gate(i):
        logits = jnp.dot(h, wg_ref[:, i * D_MODEL:(i + 1) * D_MODEL], preferred_element_type=F32)
        return jax.nn.sigmoid(logits)

    merged = gate(0) * jnp.dot(oa_ref[...], wa_ref[...], preferred_element_type=F32)
    merged = merged + gate(1) * jnp.dot(ob_ref[...], wb_ref[...], preferred_element_type=F32)
    merged = merged + gate(2) * jnp.dot(oc, wc_ref[...], preferred_element_type=F32)
    out = jnp.dot(merged.astype(BF16), wo_ref[...], preferred_element_type=F32)
    o_ref[...] = x + mod[2:3] * out


def _merge(xy, mod_l, oa, ob, u, cw, cb, lnw, lnb, wg, wa, wb, wc, wo, n_blocks, dims):
    bt = TOK_BLOCK
    t_rows, nps, nbatch = dims["T"], dims["S"] // bt, dims["B"]
    hpb = bt // HALO
    n_halo = t_rows // HALO
    row = lambda r: (r, 0)
    full = lambda r: (0, 0)
    return pl.pallas_call(
        functools.partial(_merge_kernel, nps=nps, ncps=dims["CTX"] // bt,
                          n_lat_blocks=nbatch * nps),
        grid=(n_blocks,),
        in_specs=[pl.BlockSpec((bt, D_MODEL), row),
                  pl.BlockSpec((1, 6, D_MODEL), lambda r: (jnp.minimum(r // nps, nbatch), 0, 0)),
                  pl.BlockSpec((bt, 4 * V7X_LANES), row),
                  pl.BlockSpec((bt, 4 * V7X_LANES), row),
                  pl.BlockSpec((bt, C_WIDTH), row),
                  pl.BlockSpec((HALO, C_WIDTH), lambda r: (jnp.maximum(r * hpb - 1, 0), 0)),
                  pl.BlockSpec((HALO, C_WIDTH), lambda r: (jnp.minimum((r + 1) * hpb, n_halo - 1), 0)),
                  pl.BlockSpec((C_KW + 1, C_WIDTH), full),
                  pl.BlockSpec((1, C_WIDTH), full),
                  pl.BlockSpec((1, C_WIDTH), full),
                  pl.BlockSpec((1, C_WIDTH), full),
                  pl.BlockSpec((D_MODEL, 3 * D_MODEL), full),
                  pl.BlockSpec((C_WIDTH, D_MODEL), full),
                  pl.BlockSpec((C_WIDTH, D_MODEL), full),
                  pl.BlockSpec((C_WIDTH, D_MODEL), full),
                  pl.BlockSpec((D_MODEL, D_MODEL), full)],
        out_specs=pl.BlockSpec((bt, D_MODEL), row),
        out_shape=jax.ShapeDtypeStruct((n_blocks * bt, D_MODEL), F32),
        scratch_shapes=[pltpu.VMEM((bt + 2 * HALO, C_WIDTH), F32)],
        compiler_params=_cparams(("arbitrary",)),
        name="merge",
    )(xy, mod_l, oa, ob, u, u, u, cw, cb, lnw, lnb, wg, wa, wb, wc, wo)


def _topk_desc(s, k):
    row = lax.broadcasted_iota(jnp.int32, (k, s.shape[1]), 0)

    def body(i, carry):
        work, vals = carry
        m = jnp.max(work, axis=0, keepdims=True)
        vals = jnp.where(row == i, m, vals)
        work = jnp.where(work == m, -jnp.inf, work)
        return work, vals

    _, vals = lax.fori_loop(0, k, body, (s, jnp.zeros((k, s.shape[1]), F32)))
    return vals


def _candidate_sums(sv1, sv2):
    half = P_TOPK // 2
    lead, rest = sv1[:half], sv1[half:]
    row = lax.broadcasted_iota(jnp.int32, lead.shape, 0)
    tiles = [lead + sv2[0:1], rest + sv2[0:1], lead + sv2[1:2]]
    for k2 in range(2, half):
        tiles.append(jnp.where(row < P_TOPK // (k2 + 1), lead + sv2[k2:k2 + 1], -jnp.inf))
    tiles.append(sv1[0:1] + sv2[half:])
    return jnp.concatenate(tiles, axis=0)


def _bf16_pair_words(x):
    bits = pltpu.bitcast(x.astype(BF16).astype(F32), jnp.int32)
    hi = lax.shift_right_logical(bits, 16)
    return hi | lax.shift_left(hi, 16)


def _peer_select_kernel(x_ref, mod_ref, wq_ref, k1_ref, k2_ref,
                        t_ref, need_ref, e1_ref, r2_ref, e2_ref, s1_scr):
    bt = x_ref.shape[0]
    n_lt = bt // V7X_LANES
    hb = _modulated(x_ref[...], mod_ref[0], 3, 4).astype(BF16)
    t_ref[...] = hb
    q = jnp.dot(hb, wq_ref[...], preferred_element_type=F32).astype(BF16)
    s1p = lax.dot_general(k1_ref[...], q, NT_DIMS, preferred_element_type=F32)
    for lt in range(n_lt):
        s1_scr[lt] = s1p[:, lt * V7X_LANES:(lt + 1) * V7X_LANES]
    head_row = lax.broadcasted_iota(jnp.int32, (P_HEADS, bt), 0)
    zero_t = jnp.zeros((P_HEADS, bt), F32)
    thr1, max1, tau_t = zero_t, zero_t, zero_t
    sv2_t = [zero_t] * P_TOPK
    for hh in range(P_HEADS):
        mine = head_row == hh
        qh = q[:, hh * V7X_LANES:(hh + 1) * V7X_LANES]
        s1 = jnp.concatenate([s1_scr[lt, pl.ds(hh, P_NKEYS, stride=P_HEADS), :]
                              for lt in range(n_lt)], axis=1)
        s2 = lax.dot_general(k2_ref[hh], qh, NT_DIMS, preferred_element_type=F32)
        sv1 = _topk_desc(s1, P_TOPK)
        sv2 = _topk_desc(s2, P_TOPK)
        top = _topk_desc(_candidate_sums(sv1, sv2), P_TOPK)
        z = jnp.sum(jnp.exp(top - top[0:1]), axis=0, keepdims=True)
        rank2 = jnp.zeros_like(s2)
        for k in range(P_TOPK):
            rank2 = rank2 + jnp.where(sv2[k:k + 1] > s2, 1.0, 0.0)
            sv2_t[k] = jnp.where(mine, sv2[k:k + 1], sv2_t[k])
        r2_ref[0, hh] = pltpu.bitcast(rank2.astype(BF16), jnp.int32)
        e2_ref[0, hh] = pltpu.bitcast((jnp.exp(s2 - sv2[0:1]) / z).astype(BF16), jnp.int32)
        tau_t = jnp.where(mine, top[P_TOPK - 1:P_TOPK], tau_t)
        thr1 = jnp.where(mine, sv1[P_TOPK - 1:P_TOPK], thr1)
        max1 = jnp.where(mine, sv1[0:1], max1)
    for lt in range(n_lt):
        ls = slice(lt * V7X_LANES, (lt + 1) * V7X_LANES)
        s1t = s1_scr[lt].reshape(P_NKEYS, P_HEADS, V7X_LANES)
        count = jnp.zeros_like(s1t)
        for k in range(P_TOPK):
            count = count + jnp.where(s1t + sv2_t[k][None, :, ls] >= tau_t[None, :, ls], 1.0, 0.0)
        need = jnp.where(s1t >= thr1[None, :, ls], count, 0.0)
        need_ref[0, lt] = _bf16_pair_words(need.reshape(P_NKEYS * P_HEADS, V7X_LANES))
        e1 = jnp.exp(s1t - max1[None, :, ls])
        e1_ref[0, lt] = _bf16_pair_words(e1.reshape(P_NKEYS * P_HEADS, V7X_LANES))


def _peer_dense_kernel(x_ref, mod_ref, t_ref, need_ref, e1_ref, r2_ref, e2_ref, u_ref, vt_ref,
                       fnw_ref, o_ref, a_scr, m_scr, acc_scr, *, n_chunks, final_norm):
    bt = x_ref.shape[0]
    c = pl.program_id(1)
    ipc = u_ref.shape[0] // P_NKEYS
    n_lt = bt // V7X_LANES

    @pl.when(c == 0)
    def _init():
        acc_scr[...] = jnp.zeros_like(acc_scr)
        a_scr[...] = jnp.zeros_like(a_scr)

    cprev = jnp.maximum(c - 1, 0)

    def one_key(ii, _):
        roff = pl.multiple_of(ii * P_NKEYS, P_NKEYS)
        hrow = pl.ds(pl.multiple_of((cprev * ipc + ii) * P_HEADS, P_HEADS), P_HEADS)
        for lt in range(n_lt):
            ls = slice(lt * V7X_LANES, (lt + 1) * V7X_LANES)
            w = [jnp.zeros((ROW_GROUP, V7X_LANES), BF16) for _ in range(P_NKEYS // ROW_GROUP)]
            need_t = need_ref[0, lt, hrow, :]
            e1_t = e1_ref[0, lt, hrow, :]
            zero = jnp.zeros((ROW_GROUP, V7X_LANES), BF16)
            for hh in range(P_HEADS):
                words = (ROW_GROUP // 2, V7X_LANES)
                need_b = pltpu.bitcast(jnp.broadcast_to(need_t[hh:hh + 1], words), BF16)
                e1_b = pltpu.bitcast(jnp.broadcast_to(e1_t[hh:hh + 1], words), BF16)
                for jg in range(P_NKEYS // ROW_GROUP):
                    js = slice(jg * (ROW_GROUP // 2), (jg + 1) * (ROW_GROUP // 2))
                    rank2 = pltpu.bitcast(r2_ref[0, hh, js, ls], BF16)
                    e2 = pltpu.bitcast(e2_ref[0, hh, js, ls], BF16)
                    w[jg] = w[jg] + jnp.where(rank2 < need_b, e1_b * e2, zero)
            for jg in range(P_NKEYS // ROW_GROUP):
                rows = pl.ds(roff + jg * ROW_GROUP, ROW_GROUP)
                a = a_scr[rows, ls]
                g = 0.5 * a * (1.0 + lax.erf(a * (2.0 ** -0.5)))
                m_scr[rows, ls] = g.astype(BF16) * w[jg]
        return 0

    lax.fori_loop(0, ipc, one_key, 0)

    acc_scr[...] += jnp.dot(vt_ref[0], m_scr[...], preferred_element_type=F32)
    a_scr[...] = lax.dot_general(u_ref[...], t_ref[...], NT_DIMS, preferred_element_type=F32)

    @pl.when(c == n_chunks)
    def _finish():
        x = x_ref[...]
        y = x + mod_ref[0][5:6] * acc_scr[...].T
        if final_norm:
            ms = jnp.mean(y * y, axis=-1, keepdims=True)
            y = y * lax.rsqrt(ms + EPS) * fnw_ref[...]
        o_ref[...] = y


def _peer(x1, mod_l, wq, k1p, k2p, u_tab, vt_tab, fnw, n_blocks, final_norm, dims):
    bt, ec = PEER_BLOCK, vt_tab.shape[2]
    nps, nbatch = dims["S"] // bt, dims["B"]
    n_chunks = vt_tab.shape[0]
    n_lt = bt // V7X_LANES
    rows_ih = P_NKEYS * P_HEADS
    mod_spec = pl.BlockSpec((1, 6, D_MODEL), lambda r, *_: (jnp.minimum(r // nps, nbatch), 0, 0))
    ih_shape = jax.ShapeDtypeStruct((n_blocks, n_lt, rows_ih, V7X_LANES), jnp.int32)
    hj_shape = jax.ShapeDtypeStruct((n_blocks, P_HEADS, P_NKEYS // 2, bt), jnp.int32)
    ih_spec = pl.BlockSpec((1, n_lt, rows_ih, V7X_LANES), lambda r, *_: (r, 0, 0, 0))
    hj_spec = pl.BlockSpec((1, P_HEADS, P_NKEYS // 2, bt), lambda r, *_: (r, 0, 0, 0))
    tok_spec = pl.BlockSpec((bt, D_MODEL), lambda r, *_: (r, 0))

    t_mod, need, e1, rank2, e2 = pl.pallas_call(
        _peer_select_kernel,
        grid=(n_blocks,),
        in_specs=[tok_spec, mod_spec,
                  pl.BlockSpec((D_MODEL, P_HEADS * V7X_LANES), lambda r: (0, 0)),
                  pl.BlockSpec((rows_ih, P_HEADS * V7X_LANES), lambda r: (0, 0)),
                  pl.BlockSpec((P_HEADS, P_NKEYS, V7X_LANES), lambda r: (0, 0, 0))],
        out_specs=(tok_spec, ih_spec, ih_spec, hj_spec, hj_spec),
        out_shape=(jax.ShapeDtypeStruct((n_blocks * bt, D_MODEL), BF16),
                   ih_shape,
                   ih_shape,
                   hj_shape,
                   hj_shape),
        scratch_shapes=[pltpu.VMEM((n_lt, rows_ih, V7X_LANES), F32)],
        compiler_params=_cparams(("arbitrary",)),
        name="peer_select",
    )(x1, mod_l, wq, k1p, k2p)

    return pl.pallas_call(
        functools.partial(_peer_dense_kernel, n_chunks=n_chunks, final_norm=final_norm),
        grid=(n_blocks, n_chunks + 1),
        in_specs=[tok_spec, mod_spec, tok_spec, ih_spec, ih_spec, hj_spec, hj_spec,
                  pl.BlockSpec((ec, D_MODEL), lambda r, c: (jnp.minimum(c, n_chunks - 1), 0)),
                  pl.BlockSpec((1, D_MODEL, ec), lambda r, c: (jnp.maximum(c - 1, 0), 0, 0)),
                  pl.BlockSpec((1, D_MODEL), lambda r, c: (0, 0))],
        out_specs=tok_spec,
        out_shape=jax.ShapeDtypeStruct((n_blocks * bt, D_MODEL), F32),
        scratch_shapes=[pltpu.VMEM((ec, bt), F32),
                        pltpu.VMEM((ec, bt), BF16),
                        pltpu.VMEM((D_MODEL, bt), F32)],
        compiler_params=_cparams(("arbitrary", "arbitrary")),
        name="peer_dense",
    )(x1, mod_l, t_mod, need, e1, rank2, e2, u_tab, vt_tab, fnw)


def _rope_tables(s_len, nbatch, n_ctx_rows):
    t = jnp.arange(s_len, dtype=jnp.int32)
    row = (t // GRID_W).astype(F32)
    col = (t % GRID_W).astype(F32)
    axis_dim = HEAD_DIM // 2
    inv = ROPE_THETA ** (-jnp.arange(0, axis_dim, 2, dtype=F32) / axis_dim)
    ar = row[:, None] * inv[None, :]
    ac = col[:, None] * inv[None, :]
    ang = jnp.concatenate([ar, ar, ac, ac], axis=-1)
    cos, sin = jnp.cos(ang), jnp.sin(ang)
    quarter = jnp.arange(HEAD_DIM) // (HEAD_DIM // 4)
    first = (quarter % 2 == 0)[None, :]
    sin_a = jnp.where(first, -sin, 0.0)
    sin_b = jnp.where(first, 0.0, sin)

    def expand(tab, ctx_fill):
        lat = jnp.tile(jnp.concatenate([tab, tab], axis=-1), (nbatch, 1))
        return jnp.concatenate([lat, jnp.full((n_ctx_rows, V7X_LANES), ctx_fill, F32)], axis=0)

    return expand(cos, 1.0), expand(sin_a, 0.0), expand(sin_b, 0.0)


def _padded_keys(keys_l):
    z = jnp.zeros_like(keys_l[:, 0])
    k0 = jnp.concatenate([keys_l[:, 0], z], axis=-1).transpose(1, 0, 2)
    eye = jnp.eye(P_HEADS, dtype=keys_l.dtype)
    k1p = (k0[:, :, None, :] * eye[None, :, :, None]).reshape(
        P_NKEYS * P_HEADS, P_HEADS * V7X_LANES)
    k2p = jnp.concatenate([z, keys_l[:, 1]], axis=-1)
    return k1p.astype(BF16), k2p.astype(BF16)


def _chunked_transpose(v_tab, ec):
    n_exp, d = v_tab.shape
    return v_tab.astype(BF16).reshape(n_exp // ec, ec, d).transpose(0, 2, 1)


def kernel(x, c, ctx, c_ctx, w_ada, b_ada, w_in, lam_q1, lam_k1, lam_q2, lam_k2, subln_w, q_norm_w, k_norm_w, conv_w, conv_b, conv_ln_w, conv_ln_b, w_branch_a, w_branch_b, w_branch_c, w_out, peer_wq, peer_keys, peer_u, peer_v, final_norm_w):
    nbatch, s_len, d = x.shape
    ctx_len = ctx.shape[1]
    depth = w_ada.shape[0]
    n_lat, n_ctx = nbatch * s_len, nbatch * ctx_len
    dims = {"B": nbatch, "S": s_len, "CTX": ctx_len, "T": n_lat + n_ctx}
    assert d == D_MODEL and nbatch + 1 <= MOD_ROWS
    assert s_len % PEER_BLOCK == 0 and s_len % ATT_TQ == 0 and ctx_len % TOK_BLOCK == 0
    assert ctx_len % min(ATT_TQ, ctx_len) == 0 and n_lat % min(ATT_TQ, ctx_len) == 0
    assert n_ctx % PEER_BLOCK == 0 and n_lat % ctx_len == 0

    cvec = jnp.zeros((MOD_ROWS, d), F32).at[:nbatch].set(c).at[nbatch].set(c_ctx)
    mod = _ada_rows(cvec, w_ada, b_ada).reshape(depth, MOD_ROWS, 6, d)
    tabs = _rope_tables(s_len, nbatch, n_ctx)
    tile2 = lambda v: jnp.concatenate([v, v], axis=-1)[None, :]

    xy = jnp.concatenate([x.reshape(n_lat, d), ctx.reshape(n_ctx, d)], axis=0)
    for l in range(depth):
        last = l == depth - 1
        lam_init = 0.8 - 0.6 * math.exp(-0.3 * l)
        w_bf = w_in[l].astype(BF16)
        lamv = jnp.stack([lam_q1[l], lam_k1[l], lam_q2[l], lam_k2[l]], axis=0)

        qa, ka, va, qb, kbd, vbd, u = _inproj(
            xy, mod[l], w_bf[:, :OFF_GATE], tabs, tile2(q_norm_w[l]), tile2(k_norm_w[l]), dims)
        oa, ob = _attention(qa, ka, va, qb, kbd, vbd, lamv, subln_w[l][None, :], lam_init,
                            not last, dims)
        n_rows = n_lat if last else n_lat + n_ctx
        cw = jnp.concatenate([conv_w[l], jnp.zeros((1, C_WIDTH), F32)], axis=0)
        x1 = _merge(xy, mod[l], oa, ob, u, cw, conv_b[l][None, :], conv_ln_w[l][None, :],
                    conv_ln_b[l][None, :], w_bf[:, OFF_GATE:], w_branch_a[l].astype(BF16),
                    w_branch_b[l].astype(BF16), w_branch_c[l].astype(BF16),
                    w_out[l].astype(BF16), n_rows // TOK_BLOCK, dims)
        k1p, k2p = _padded_keys(peer_keys[l])
        xy = _peer(x1, mod[l], peer_wq[l].astype(BF16), k1p, k2p,
                   peer_u[l].astype(BF16), _chunked_transpose(peer_v[l], PEER_ECHUNK),
                   final_norm_w[None, :],
                   n_rows // PEER_BLOCK, last, dims)
    return xy.reshape(nbatch, s_len, d)
```

```python
import functools
import math

import jax
import jax.numpy as jnp
from jax import lax
from jax.experimental import pallas as pl
from jax.experimental.pallas import tpu as pltpu

F32 = jnp.float32
BF16 = jnp.bfloat16

D_MODEL = 1024
DEPTH = 2
GRID_W = 64
HEAD_DIM = 64
ROPE_THETA = 10000.0
EPS = 1e-6
A_HEADS = 4
B_KV = 2
C_WIDTH = 512
C_KW = 31
P_HEADS = 8
P_NKEYS = 128
P_TOPK = 16

COL_KA, COL_VA, COL_KB, COL_VB, COL_QA, COL_QB, COL_GLU = 512, 512, 128, 128, 512, 512, 1024
OFF_KA = 0
OFF_VA = OFF_KA + COL_KA
OFF_KB = OFF_VA + COL_VA
OFF_QA = OFF_KB + COL_KB + COL_VB
OFF_QB = OFF_QA + COL_QA
OFF_GLU = OFF_QB + COL_QB
OFF_GATE = OFF_GLU + COL_GLU

V7X_LANES = 128
V7X_VMEM_LIMIT = 56 * 1024 * 1024
HALO = 16

TOK_BLOCK = 256
ATT_TQ = 512
ATT_TK = 512
ATT_UNROLL = 8
PEER_BLOCK = 512
PEER_ECHUNK = 1024
MOD_ROWS = 8
ROW_GROUP = 16

Q_SCALE = (HEAD_DIM ** -0.5) * math.log2(math.e)
NEG_BIG = -1e30
NT_DIMS = (((1,), (1,)), ((), ()))


def _cparams(sem):
    return pltpu.CompilerParams(dimension_semantics=sem, vmem_limit_bytes=V7X_VMEM_LIMIT)


def _modulated(x, mod, shift_row, scale_row):
    ms = jnp.mean(x * x, axis=-1, keepdims=True)
    xn = x * lax.rsqrt(ms + EPS)
    return xn * (1.0 + mod[scale_row:scale_row + 1]) + mod[shift_row:shift_row + 1]


def _ada_kernel(c_ref, w_ref, b_ref, o_ref):
    c = c_ref[...]
    sc = c * jax.nn.sigmoid(c)
    o_ref[0] = jnp.dot(sc, w_ref[0], preferred_element_type=F32,
                       precision=lax.Precision.HIGHEST) + b_ref[0]


def _ada_rows(cvec, w_ada, b_ada):
    depth, d, n = w_ada.shape
    tn = 1536
    return pl.pallas_call(
        _ada_kernel,
        grid=(depth, n // tn),
        in_specs=[pl.BlockSpec((MOD_ROWS, d), lambda l, j: (0, 0)),
                  pl.BlockSpec((1, d, tn), lambda l, j: (l, 0, j)),
                  pl.BlockSpec((1, 1, tn), lambda l, j: (l, 0, j))],
        out_specs=pl.BlockSpec((1, MOD_ROWS, tn), lambda l, j: (l, 0, j)),
        out_shape=jax.ShapeDtypeStruct((depth, MOD_ROWS, n), F32),
        compiler_params=_cparams(("arbitrary", "arbitrary")),
        name="ada_rows",
    )(cvec, w_ada, b_ada.reshape(depth, 1, n))


def _inproj_kernel(x_ref, mod_ref, w_ref, cos_ref, sa_ref, sb_ref, qnw_ref, knw_ref,
                   qa_ref, ka_ref, va_ref, qb_ref, kbd_ref, vbd_ref, u_ref):
    bt = x_ref.shape[0]
    h = _modulated(x_ref[...], mod_ref[0], 0, 1).astype(BF16)
    cos, sa, sb = cos_ref[...], sa_ref[...], sb_ref[...]
    lo = lax.broadcasted_iota(jnp.int32, (bt, V7X_LANES), 1) < HEAD_DIM

    def proj(c0, width):
        return jnp.dot(h, w_ref[:, c0:c0 + width], preferred_element_type=F32)

    def rope(x):
        return x * cos + pltpu.roll(x, V7X_LANES - 16, 1) * sa + pltpu.roll(x, 16, 1) * sb

    def headnorm(x, w):
        x2 = x * x
        s_lo = jnp.sum(jnp.where(lo, x2, 0.0), axis=-1, keepdims=True)
        s_hi = jnp.sum(jnp.where(lo, 0.0, x2), axis=-1, keepdims=True)
        r = jnp.where(lo, lax.rsqrt(s_lo / HEAD_DIM + EPS), lax.rsqrt(s_hi / HEAD_DIM + EPS))
        return x * r * w

    def tile(p, c):
        return p[:, c * V7X_LANES:(c + 1) * V7X_LANES]

    p = proj(OFF_KA, COL_KA)
    for c in range(A_HEADS):
        ka_ref[c] = rope(tile(p, c)).astype(BF16)

    p = proj(OFF_VA, COL_VA)
    for c in range(A_HEADS):
        va_ref[c] = tile(p, c).astype(BF16)

    p = proj(OFF_KB, COL_KB + COL_VB)
    kb = rope(headnorm(tile(p, 0), knw_ref[...]))
    kb_sw = pltpu.roll(kb, HEAD_DIM, 1)
    kbd_ref[0] = jnp.where(lo, kb, kb_sw).astype(BF16)
    kbd_ref[1] = jnp.where(lo, kb_sw, kb).astype(BF16)
    vb = tile(p, 1)
    vb_sw = pltpu.roll(vb, HEAD_DIM, 1)
    vbd_ref[0] = jnp.where(lo, vb, vb_sw).astype(BF16)
    vbd_ref[1] = jnp.where(lo, vb_sw, vb).astype(BF16)

    p = proj(OFF_QA, COL_QA)
    for c in range(COL_QA // V7X_LANES):
        qa_ref[:, c * V7X_LANES:(c + 1) * V7X_LANES] = (rope(tile(p, c)) * Q_SCALE).astype(BF16)

    p = proj(OFF_QB, COL_QB)
    qnw = qnw_ref[...]
    for c in range(COL_QB // V7X_LANES):
        qb_ref[:, c * V7X_LANES:(c + 1) * V7X_LANES] = (
            rope(headnorm(tile(p, c), qnw)) * Q_SCALE).astype(BF16)

    p = proj(OFF_GLU, COL_GLU)
    u_ref[...] = p[:, :C_WIDTH] * jax.nn.sigmoid(p[:, C_WIDTH:])


def _inproj(xy, mod_l, w_main, tabs, qnw, knw, dims):
    t_rows, nps, nbatch = dims["T"], dims["S"] // TOK_BLOCK, dims["B"]
    bt = TOK_BLOCK
    row = lambda r: (r, 0)
    full = lambda r: (0, 0)
    out_shapes = (
        jax.ShapeDtypeStruct((t_rows, COL_QA), BF16),
        jax.ShapeDtypeStruct((A_HEADS, t_rows, V7X_LANES), BF16),
        jax.ShapeDtypeStruct((A_HEADS, t_rows, V7X_LANES), BF16),
        jax.ShapeDtypeStruct((t_rows, COL_QB), BF16),
        jax.ShapeDtypeStruct((B_KV, t_rows, V7X_LANES), BF16),
        jax.ShapeDtypeStruct((B_KV, t_rows, V7X_LANES), BF16),
        jax.ShapeDtypeStruct((t_rows, C_WIDTH), F32),
    )
    grp = pl.BlockSpec((B_KV, bt, V7X_LANES), lambda r: (0, r, 0))
    heads = pl.BlockSpec((A_HEADS, bt, V7X_LANES), lambda r: (0, r, 0))
    return pl.pallas_call(
        _inproj_kernel,
        grid=(t_rows // bt,),
        in_specs=[pl.BlockSpec((bt, D_MODEL), row),
                  pl.BlockSpec((1, 6, D_MODEL), lambda r: (jnp.minimum(r // nps, nbatch), 0, 0)),
                  pl.BlockSpec((D_MODEL, OFF_GATE), full),
                  pl.BlockSpec((bt, V7X_LANES), row),
                  pl.BlockSpec((bt, V7X_LANES), row),
                  pl.BlockSpec((bt, V7X_LANES), row),
                  pl.BlockSpec((1, V7X_LANES), full),
                  pl.BlockSpec((1, V7X_LANES), full)],
        out_specs=(pl.BlockSpec((bt, COL_QA), row), heads, heads, pl.BlockSpec((bt, COL_QB), row),
                   grp, grp, pl.BlockSpec((bt, C_WIDTH), row)),
        out_shape=out_shapes,
        compiler_params=_cparams(("arbitrary",)),
        name="inproj",
    )(xy, mod_l, w_main, tabs[0], tabs[1], tabs[2], qnw, knw)


def _kv_unroll(n_chunks):
    return ATT_UNROLL if n_chunks % ATT_UNROLL == 0 else 1


def _softmax_step(s, v, m, l, acc):
    mn = jnp.maximum(m, jnp.max(s, axis=-1, keepdims=True))
    p = jnp.exp2(s - mn)
    alpha = jnp.exp2(m - mn)
    l = alpha * l + jnp.sum(p, axis=-1, keepdims=True)
    acc = alpha * acc + jnp.dot(p.astype(BF16), v, preferred_element_type=F32)
    return mn, l, acc


def _flash_rows(q, load_lat, kc, vc, *, nq_lat, n_lat_chunks, tk):
    rows = q.shape[0]

    def step(k, v, carry):
        m, l, a = carry
        s = lax.dot_general(q, k, NT_DIMS, preferred_element_type=F32)
        return _softmax_step(s, v, m, l, a)

    unroll = _kv_unroll(n_lat_chunks)

    def lat_body(j, carry):
        for r in range(unroll):
            off = pl.multiple_of((j * unroll + r) * tk, tk)
            carry = step(*load_lat(off), carry)
        return carry

    carry = (jnp.full((rows, 1), NEG_BIG, F32), jnp.zeros((rows, 1), F32),
             jnp.zeros((rows, V7X_LANES), F32))
    n_lat = jnp.where(pl.program_id(2) < nq_lat, n_lat_chunks // unroll, 0)
    carry = lax.fori_loop(0, n_lat, lat_body, carry)
    _, l, a = step(kc, vc, carry)
    return a / l


def _attn_a_kernel(q_ref, kl_ref, vl_ref, kc_ref, vc_ref, lamv_ref, sw_ref, o_ref, *,
                   nq_lat, n_lat_chunks, tk, lam_init):
    tq = q_ref.shape[0]
    q = q_ref[...]
    lo = lax.broadcasted_iota(jnp.int32, (tq, V7X_LANES), 1) < HEAD_DIM
    zero = jnp.zeros_like(q)
    q12 = jnp.concatenate([jnp.where(lo, q, zero), jnp.where(lo, zero, q)], axis=0)
    o12 = _flash_rows(q12, lambda off: (kl_ref[0, pl.ds(off, tk), :], vl_ref[0, pl.ds(off, tk), :]),
                      kc_ref[0], vc_ref[0], nq_lat=nq_lat, n_lat_chunks=n_lat_chunks, tk=tk)

    lv = lamv_ref[...]
    lam = (jnp.exp(jnp.sum(lv[0:1] * lv[1:2], axis=-1, keepdims=True))
           - jnp.exp(jnp.sum(lv[2:3] * lv[3:4], axis=-1, keepdims=True)) + lam_init)
    o = o12[:tq] - lam * o12[tq:]
    ms = jnp.mean(o * o, axis=-1, keepdims=True)
    o_ref[...] = (o * lax.rsqrt(ms + EPS) * sw_ref[...] * (1.0 - lam_init)).astype(BF16)


def _attn_b_kernel(q_ref, kl_ref, vl_ref, kc_ref, vc_ref, o_ref, *, nq_lat, n_lat_chunks, tk):
    tq = q_ref.shape[0]
    lo = lax.broadcasted_iota(jnp.int32, (tq, V7X_LANES), 1) < HEAD_DIM
    parts = []
    for c in range(2):
        qc = q_ref[:, c * V7X_LANES:(c + 1) * V7X_LANES]
        zero = jnp.zeros_like(qc)
        parts += [jnp.where(lo, qc, zero), jnp.where(lo, zero, qc)]
    q4 = jnp.concatenate(parts, axis=0)
    o = _flash_rows(q4, lambda off: (kl_ref[0, pl.ds(off, tk), :], vl_ref[0, pl.ds(off, tk), :]),
                    kc_ref[0], vc_ref[0], nq_lat=nq_lat, n_lat_chunks=n_lat_chunks, tk=tk)
    for c in range(2):
        o_ref[:, c * V7X_LANES:(c + 1) * V7X_LANES] = jnp.where(
            lo, o[(2 * c) * tq:(2 * c + 1) * tq], o[(2 * c + 1) * tq:(2 * c + 2) * tq]).astype(BF16)


def _attention_call(kern, name, q, k, v, extras, q_lanes, n_groups, tq, latent_queries, dims):
    s_len, ctx_len, nbatch = dims["S"], dims["CTX"], dims["B"]
    tk = min(ATT_TK, s_len)
    q_len = s_len if latent_queries else ctx_len
    nq = q_len // tq
    q_blk0 = 0 if latent_queries else (nbatch * s_len) // tq
    ctx_blk0 = (nbatch * s_len) // ctx_len
    extra_specs = [pl.BlockSpec(e.shape, lambda b, g, qi: (0, 0)) for e in extras]
    return pl.pallas_call(
        functools.partial(kern, nq_lat=nq if latent_queries else 0, n_lat_chunks=s_len // tk, tk=tk),
        grid=(nbatch, n_groups, nq),
        in_specs=[pl.BlockSpec((tq, q_lanes), lambda b, g, qi: (q_blk0 + b * nq + qi, g)),
                  pl.BlockSpec((1, s_len, V7X_LANES), lambda b, g, qi: (g, b, 0)),
                  pl.BlockSpec((1, s_len, V7X_LANES), lambda b, g, qi: (g, b, 0)),
                  pl.BlockSpec((1, ctx_len, V7X_LANES), lambda b, g, qi: (g, ctx_blk0 + b, 0)),
                  pl.BlockSpec((1, ctx_len, V7X_LANES), lambda b, g, qi: (g, ctx_blk0 + b, 0)),
                  *extra_specs],
        out_specs=pl.BlockSpec((tq, q_lanes), lambda b, g, qi: (b * nq + qi, g)),
        out_shape=jax.ShapeDtypeStruct((nbatch * q_len, n_groups * q_lanes), BF16),
        compiler_params=_cparams(("arbitrary", "arbitrary", "arbitrary")),
        name=name,
    )(q, k, v, k, v, *extras)


def _attention(qa, ka, va, qb, kbd, vbd, lamv, subln_w, lam_init, with_ctx_queries, dims):
    kern_a = functools.partial(_attn_a_kernel, lam_init=lam_init)
    gw = 2 * V7X_LANES

    def both(latent, tag):
        q_len = dims["S"] if latent else dims["CTX"]
        oa = _attention_call(kern_a, "attn_a" + tag, qa, ka, va, (lamv, subln_w), V7X_LANES,
                             A_HEADS, min(ATT_TQ, q_len), latent, dims)
        ob = _attention_call(_attn_b_kernel, "attn_b" + tag, qb, kbd, vbd, (), gw, B_KV,
                             min(ATT_TQ // 2, q_len), latent, dims)
        return oa, ob

    oa, ob = both(True, "")
    if with_ctx_queries:
        oa_c, ob_c = both(False, "_ctx")
        oa, ob = jnp.concatenate([oa, oa_c], axis=0), jnp.concatenate([ob, ob_c], axis=0)
    return oa, ob


def _merge_kernel(x_ref, mod_ref, oa_ref, ob_ref, u_ref, up_ref, un_ref, cw_ref, cb_ref,
                  lnw_ref, lnb_ref, wg_ref, wa_ref, wb_ref, wc_ref, wo_ref, o_ref, uext_ref, *,
                  nps, ncps, n_lat_blocks):
    bt = x_ref.shape[0]
    r = pl.program_id(0)
    is_lat = r < n_lat_blocks
    pos = jnp.where(is_lat, r % nps, (r - n_lat_blocks) % ncps)
    last_pos = jnp.where(is_lat, nps - 1, ncps - 1)
    keep_prev = (pos != 0).astype(F32)
    keep_next = (pos != last_pos).astype(F32)

    x = x_ref[...]
    mod = mod_ref[0]
    h = _modulated(x, mod, 0, 1).astype(BF16)

    uext_ref[0:HALO, :] = up_ref[...] * keep_prev
    uext_ref[HALO:HALO + bt, :] = u_ref[...]
    uext_ref[HALO + bt:2 * HALO + bt, :] = un_ref[...] * keep_next
    cw = cw_ref[...]
    y = jnp.zeros((bt, C_WIDTH), F32) + cb_ref[...]
    base = HALO - C_KW // 2
    for k in range(C_KW):
        y = y + cw[k:k + 1, :] * uext_ref[base + k:base + k + bt, :]
    mu = jnp.mean(y, axis=-1, keepdims=True)
    yc = y - mu
    var = jnp.mean(yc * yc, axis=-1, keepdims=True)
    yn = yc * lax.rsqrt(var + EPS) * lnw_ref[...] + lnb_ref[...]
    oc = (yn * jax.nn.sigmoid(yn)).astype(BF16)

    def gate(i):
        logits = jnp.dot(h, wg_ref[:, i * D_MODEL:(i + 1) * D_MODEL], preferred_element_type=F32)
        return jax.nn.sigmoid(logits)

    merged = gate(0) * jnp.dot(oa_ref[...], wa_ref[...], preferred_element_type=F32)
    merged = merged + gate(1) * jnp.dot(ob_ref[...], wb_ref[...], preferred_element_type=F32)
    merged = merged + gate(2) * jnp.dot(oc, wc_ref[...], preferred_element_type=F32)
    out = jnp.dot(merged.astype(BF16), wo_ref[...], preferred_element_type=F32)
    o_ref[...] = x + mod[2:3] * out


def _merge(xy, mod_l, oa, ob, u, cw, cb, lnw, lnb, wg, wa, wb, wc, wo, n_blocks, dims):
    bt = TOK_BLOCK
    t_rows, nps, nbatch = dims["T"], dims["S"] // bt, dims["B"]
    hpb = bt // HALO
    n_halo = t_rows // HALO
    row = lambda r: (r, 0)
    full = lambda r: (0, 0)
    return pl.pallas_call(
        functools.partial(_merge_kernel, nps=nps, ncps=dims["CTX"] // bt,
                          n_lat_blocks=nbatch * nps),
        grid=(n_blocks,),
        in_specs=[pl.BlockSpec((bt, D_MODEL), row),
                  pl.BlockSpec((1, 6, D_MODEL), lambda r: (jnp.minimum(r // nps, nbatch), 0, 0)),
                  pl.BlockSpec((bt, 4 * V7X_LANES), row),
                  pl.BlockSpec((bt, 4 * V7X_LANES), row),
                  pl.BlockSpec((bt, C_WIDTH), row),
                  pl.BlockSpec((HALO, C_WIDTH), lambda r: (jnp.maximum(r * hpb - 1, 0), 0)),
                  pl.BlockSpec((HALO, C_WIDTH), lambda r: (jnp.minimum((r + 1) * hpb, n_halo - 1), 0)),
                  pl.BlockSpec((C_KW + 1, C_WIDTH), full),
                  pl.BlockSpec((1, C_WIDTH), full),
                  pl.BlockSpec((1, C_WIDTH), full),
                  pl.BlockSpec((1, C_WIDTH), full),
                  pl.BlockSpec((D_MODEL, 3 * D_MODEL), full),
                  pl.BlockSpec((C_WIDTH, D_MODEL), full),
                  pl.BlockSpec((C_WIDTH, D_MODEL), full),
                  pl.BlockSpec((C_WIDTH, D_MODEL), full),
                  pl.BlockSpec((D_MODEL, D_MODEL), full)],
        out_specs=pl.BlockSpec((bt, D_MODEL), row),
        out_shape=jax.ShapeDtypeStruct((n_blocks * bt, D_MODEL), F32),
        scratch_shapes=[pltpu.VMEM((bt + 2 * HALO, C_WIDTH), F32)],
        compiler_params=_cparams(("arbitrary",)),
        name="merge",
    )(xy, mod_l, oa, ob, u, u, u, cw, cb, lnw, lnb, wg, wa, wb, wc, wo)


def _topk_desc(s, k):
    row = lax.broadcasted_iota(jnp.int32, (k, s.shape[1]), 0)

    def body(i, carry):
        work, vals = carry
        m = jnp.max(work, axis=0, keepdims=True)
        vals = jnp.where(row == i, m, vals)
        work = jnp.where(work == m, -jnp.inf, work)
        return work, vals

    _, vals = lax.fori_loop(0, k, body, (s, jnp.zeros((k, s.shape[1]), F32)))
    return vals


def _candidate_sums(sv1, sv2):
    half = P_TOPK // 2
    lead, rest = sv1[:half], sv1[half:]
    row = lax.broadcasted_iota(jnp.int32, lead.shape, 0)
    tiles = [lead + sv2[0:1], rest + sv2[0:1], lead + sv2[1:2]]
    for k2 in range(2, half):
        tiles.append(jnp.where(row < P_TOPK // (k2 + 1), lead + sv2[k2:k2 + 1], -jnp.inf))
    tiles.append(sv1[0:1] + sv2[half:])
    return jnp.concatenate(tiles, axis=0)


def _bf16_pair_words(x):
    bits = pltpu.bitcast(x.astype(BF16).astype(F32), jnp.int32)
    hi = lax.shift_right_logical(bits, 16)
    return hi | lax.shift_left(hi, 16)


def _peer_select_kernel(x_ref, mod_ref, wq_ref, k1_ref, k2_ref,
                        t_ref, need_ref, e1_ref, r2_ref, e2_ref, s1_scr):
    bt = x_ref.shape[0]
    n_lt = bt // V7X_LANES
    hb = _modulated(x_ref[...], mod_ref[0], 3, 4).astype(BF16)
    t_ref[...] = hb
    q = jnp.dot(hb, wq_ref[...], preferred_element_type=F32).astype(BF16)
    s1p = lax.dot_general(k1_ref[...], q, NT_DIMS, preferred_element_type=F32)
    for lt in range(n_lt):
        s1_scr[lt] = s1p[:, lt * V7X_LANES:(lt + 1) * V7X_LANES]
    head_row = lax.broadcasted_iota(jnp.int32, (P_HEADS, bt), 0)
    zero_t = jnp.zeros((P_HEADS, bt), F32)
    thr1, max1, tau_t = zero_t, zero_t, zero_t
    sv2_t = [zero_t] * P_TOPK
    for hh in range(P_HEADS):
        mine = head_row == hh
        qh = q[:, hh * V7X_LANES:(hh + 1) * V7X_LANES]
        s1 = jnp.concatenate([s1_scr[lt, pl.ds(hh, P_NKEYS, stride=P_HEADS), :]
                              for lt in range(n_lt)], axis=1)
        s2 = lax.dot_general(k2_ref[hh], qh, NT_DIMS, preferred_element_type=F32)
        sv1 = _topk_desc(s1, P_TOPK)
        sv2 = _topk_desc(s2, P_TOPK)
        top = _topk_desc(_candidate_sums(sv1, sv2), P_TOPK)
        z = jnp.sum(jnp.exp(top - top[0:1]), axis=0, keepdims=True)
        rank2 = jnp.zeros_like(s2)
        for k in range(P_TOPK):
            rank2 = rank2 + jnp.where(sv2[k:k + 1] > s2, 1.0, 0.0)
            sv2_t[k] = jnp.where(mine, sv2[k:k + 1], sv2_t[k])
        r2_ref[0, hh] = pltpu.bitcast(rank2.astype(BF16), jnp.int32)
        e2_ref[0, hh] = pltpu.bitcast((jnp.exp(s2 - sv2[0:1]) / z).astype(BF16), jnp.int32)
        tau_t = jnp.where(mine, top[P_TOPK - 1:P_TOPK], tau_t)
        thr1 = jnp.where(mine, sv1[P_TOPK - 1:P_TOPK], thr1)
        max1 = jnp.where(mine, sv1[0:1], max1)
    for lt in range(n_lt):
        ls = slice(lt * V7X_LANES, (lt + 1) * V7X_LANES)
        s1t = s1_scr[lt].reshape(P_NKEYS, P_HEADS, V7X_LANES)
        count = jnp.zeros_like(s1t)
        for k in range(P_TOPK):
            count = count + jnp.where(s1t + sv2_t[k][None, :, ls] >= tau_t[None, :, ls], 1.0, 0.0)
        need = jnp.where(s1t >= thr1[None, :, ls], count, 0.0)
        need_ref[0, lt] = _bf16_pair_words(need.reshape(P_NKEYS * P_HEADS, V7X_LANES))
        e1 = jnp.exp(s1t - max1[None, :, ls])
        e1_ref[0, lt] = _bf16_pair_words(e1.reshape(P_NKEYS * P_HEADS, V7X_LANES))


def _peer_dense_kernel(x_ref, mod_ref, t_ref, need_ref, e1_ref, r2_ref, e2_ref, u_ref, vt_ref,
                       fnw_ref, o_ref, a_scr, m_scr, acc_scr, *, n_chunks, final_norm):
    bt = x_ref.shape[0]
    c = pl.program_id(1)
    ipc = u_ref.shape[0] // P_NKEYS
    n_lt = bt // V7X_LANES

    @pl.when(c == 0)
    def _init():
        acc_scr[...] = jnp.zeros_like(acc_scr)
        a_scr[...] = jnp.zeros_like(a_scr)

    cprev = jnp.maximum(c - 1, 0)

    def one_key(ii, _):
        roff = ii * P_NKEYS
        hrow = pl.ds(pl.multiple_of((cprev * ipc + ii) * P_HEADS, P_HEADS), P_HEADS)
        for lt in range(n_lt):
            ls = slice(lt * V7X_LANES, (lt + 1) * V7X_LANES)
            w = [jnp.zeros((ROW_GROUP, V7X_LANES), BF16) for _ in range(P_NKEYS // ROW_GROUP)]
            need_t = need_ref[0, lt, hrow, :]
            e1_t = e1_ref[0, lt, hrow, :]
            zero = jnp.zeros((ROW_GROUP, V7X_LANES), BF16)
            for hh in range(P_HEADS):
                words = (ROW_GROUP // 2, V7X_LANES)
                need_b = pltpu.bitcast(jnp.broadcast_to(need_t[hh:hh + 1], words), BF16)
                e1_b = pltpu.bitcast(jnp.broadcast_to(e1_t[hh:hh + 1], words), BF16)
                for jg in range(P_NKEYS // ROW_GROUP):
                    js = slice(jg * (ROW_GROUP // 2), (jg + 1) * (ROW_GROUP // 2))
                    rank2 = pltpu.bitcast(r2_ref[0, hh, js, ls], BF16)
                    e2 = pltpu.bitcast(e2_ref[0, hh, js, ls], BF16)
                    w[jg] = w[jg] + jnp.where(rank2 < need_b, e1_b * e2, zero)
            for jg in range(P_NKEYS // ROW_GROUP):
                rows = pl.ds(roff + jg * ROW_GROUP, ROW_GROUP)
                a = a_scr[rows, ls]
                g = 0.5 * a * (1.0 + lax.erf(a * (2.0 ** -0.5)))
                m_scr[rows, ls] = g.astype(BF16) * w[jg]
        return 0

    for ii in range(ipc):
        one_key(ii, 0)

    acc_scr[...] += jnp.dot(vt_ref[0], m_scr[...], preferred_element_type=F32)
    a_scr[...] = lax.dot_general(u_ref[...], t_ref[...], NT_DIMS, preferred_element_type=F32)

    @pl.when(c == n_chunks)
    def _finish():
        x = x_ref[...]
        y = x + mod_ref[0][5:6] * acc_scr[...].T
        if final_norm:
            ms = jnp.mean(y * y, axis=-1, keepdims=True)
            y = y * lax.rsqrt(ms + EPS) * fnw_ref[...]
        o_ref[...] = y


def _peer(x1, mod_l, wq, k1p, k2p, u_tab, vt_tab, fnw, n_blocks, final_norm, dims):
    bt, ec = PEER_BLOCK, vt_tab.shape[2]
    nps, nbatch = dims["S"] // bt, dims["B"]
    n_chunks = vt_tab.shape[0]
    n_lt = bt // V7X_LANES
    rows_ih = P_NKEYS * P_HEADS
    mod_spec = pl.BlockSpec((1, 6, D_MODEL), lambda r, *_: (jnp.minimum(r // nps, nbatch), 0, 0))
    ih_shape = jax.ShapeDtypeStruct((n_blocks, n_lt, rows_ih, V7X_LANES), jnp.int32)
    hj_shape = jax.ShapeDtypeStruct((n_blocks, P_HEADS, P_NKEYS // 2, bt), jnp.int32)
    ih_spec = pl.BlockSpec((1, n_lt, rows_ih, V7X_LANES), lambda r, *_: (r, 0, 0, 0))
    hj_spec = pl.BlockSpec((1, P_HEADS, P_NKEYS // 2, bt), lambda r, *_: (r, 0, 0, 0))
    tok_spec = pl.BlockSpec((bt, D_MODEL), lambda r, *_: (r, 0))

    t_mod, need, e1, rank2, e2 = pl.pallas_call(
        _peer_select_kernel,
        grid=(n_blocks,),
        in_specs=[tok_spec, mod_spec,
                  pl.BlockSpec((D_MODEL, P_HEADS * V7X_LANES), lambda r: (0, 0)),
                  pl.BlockSpec((rows_ih, P_HEADS * V7X_LANES), lambda r: (0, 0)),
                  pl.BlockSpec((P_HEADS, P_NKEYS, V7X_LANES), lambda r: (0, 0, 0))],
        out_specs=(tok_spec, ih_spec, ih_spec, hj_spec, hj_spec),
        out_shape=(jax.ShapeDtypeStruct((n_blocks * bt, D_MODEL), BF16),
                   ih_shape,
                   ih_shape,
                   hj_shape,
                   hj_shape),
        scratch_shapes=[pltpu.VMEM((n_lt, rows_ih, V7X_LANES), F32)],
        compiler_params=_cparams(("arbitrary",)),
        name="peer_select",
    )(x1, mod_l, wq, k1p, k2p)

    return pl.pallas_call(
        functools.partial(_peer_dense_kernel, n_chunks=n_chunks, final_norm=final_norm),
        grid=(n_blocks, n_chunks + 1),
        in_specs=[tok_spec, mod_spec, tok_spec, ih_spec, ih_spec, hj_spec, hj_spec,
                  pl.BlockSpec((ec, D_MODEL), lambda r, c: (jnp.minimum(c, n_chunks - 1), 0)),
                  pl.BlockSpec((1, D_MODEL, ec), lambda r, c: (jnp.maximum(c - 1, 0), 0, 0)),
                  pl.BlockSpec((1, D_MODEL), lambda r, c: (0, 0))],
        out_specs=tok_spec,
        out_shape=jax.ShapeDtypeStruct((n_blocks * bt, D_MODEL), F32),
        scratch_shapes=[pltpu.VMEM((ec, bt), F32),
                        pltpu.VMEM((ec, bt), BF16),
                        pltpu.VMEM((D_MODEL, bt), F32)],
        compiler_params=_cparams(("arbitrary", "arbitrary")),
        name="peer_dense",
    )(x1, mod_l, t_mod, need, e1, rank2, e2, u_tab, vt_tab, fnw)


def _rope_tables(s_len, nbatch, n_ctx_rows):
    t = jnp.arange(s_len, dtype=jnp.int32)
    row = (t // GRID_W).astype(F32)
    col = (t % GRID_W).astype(F32)
    axis_dim = HEAD_DIM // 2
    inv = ROPE_THETA ** (-jnp.arange(0, axis_dim, 2, dtype=F32) / axis_dim)
    ar = row[:, None] * inv[None, :]
    ac = col[:, None] * inv[None, :]
    ang = jnp.concatenate([ar, ar, ac, ac], axis=-1)
    cos, sin = jnp.cos(ang), jnp.sin(ang)
    quarter = jnp.arange(HEAD_DIM) // (HEAD_DIM // 4)
    first = (quarter % 2 == 0)[None, :]
    sin_a = jnp.where(first, -sin, 0.0)
    sin_b = jnp.where(first, 0.0, sin)

    def expand(tab, ctx_fill):
        lat = jnp.tile(jnp.concatenate([tab, tab], axis=-1), (nbatch, 1))
        return jnp.concatenate([lat, jnp.full((n_ctx_rows, V7X_LANES), ctx_fill, F32)], axis=0)

    return expand(cos, 1.0), expand(sin_a, 0.0), expand(sin_b, 0.0)


def _padded_keys(keys_l):
    z = jnp.zeros_like(keys_l[:, 0])
    k0 = jnp.concatenate([keys_l[:, 0], z], axis=-1).transpose(1, 0, 2)
    eye = jnp.eye(P_HEADS, dtype=keys_l.dtype)
    k1p = (k0[:, :, None, :] * eye[None, :, :, None]).reshape(
        P_NKEYS * P_HEADS, P_HEADS * V7X_LANES)
    k2p = jnp.concatenate([z, keys_l[:, 1]], axis=-1)
    return k1p.astype(BF16), k2p.astype(BF16)


def _chunked_transpose(v_tab, ec):
    n_exp, d = v_tab.shape
    return v_tab.astype(BF16).reshape(n_exp // ec, ec, d).transpose(0, 2, 1)


def kernel(x, c, ctx, c_ctx, w_ada, b_ada, w_in, lam_q1, lam_k1, lam_q2, lam_k2, subln_w, q_norm_w, k_norm_w, conv_w, conv_b, conv_ln_w, conv_ln_b, w_branch_a, w_branch_b, w_branch_c, w_out, peer_wq, peer_keys, peer_u, peer_v, final_norm_w):
    nbatch, s_len, d = x.shape
    ctx_len = ctx.shape[1]
    depth = w_ada.shape[0]
    n_lat, n_ctx = nbatch * s_len, nbatch * ctx_len
    dims = {"B": nbatch, "S": s_len, "CTX": ctx_len, "T": n_lat + n_ctx}
    assert d == D_MODEL and nbatch + 1 <= MOD_ROWS
    assert s_len % PEER_BLOCK == 0 and s_len % ATT_TQ == 0 and ctx_len % TOK_BLOCK == 0
    assert ctx_len % min(ATT_TQ, ctx_len) == 0 and n_lat % min(ATT_TQ, ctx_len) == 0
    assert n_ctx % PEER_BLOCK == 0 and n_lat % ctx_len == 0

    cvec = jnp.zeros((MOD_ROWS, d), F32).at[:nbatch].set(c).at[nbatch].set(c_ctx)
    mod = _ada_rows(cvec, w_ada, b_ada).reshape(depth, MOD_ROWS, 6, d)
    tabs = _rope_tables(s_len, nbatch, n_ctx)
    tile2 = lambda v: jnp.concatenate([v, v], axis=-1)[None, :]

    xy = jnp.concatenate([x.reshape(n_lat, d), ctx.reshape(n_ctx, d)], axis=0)
    for l in range(depth):
        last = l == depth - 1
        lam_init = 0.8 - 0.6 * math.exp(-0.3 * l)
        w_bf = w_in[l].astype(BF16)
        lamv = jnp.stack([lam_q1[l], lam_k1[l], lam_q2[l], lam_k2[l]], axis=0)

        qa, ka, va, qb, kbd, vbd, u = _inproj(
            xy, mod[l], w_bf[:, :OFF_GATE], tabs, tile2(q_norm_w[l]), tile2(k_norm_w[l]), dims)
        oa, ob = _attention(qa, ka, va, qb, kbd, vbd, lamv, subln_w[l][None, :], lam_init,
                            not last, dims)
        n_rows = n_lat if last else n_lat + n_ctx
        cw = jnp.concatenate([conv_w[l], jnp.zeros((1, C_WIDTH), F32)], axis=0)
        x1 = _merge(xy, mod[l], oa, ob, u, cw, conv_b[l][None, :], conv_ln_w[l][None, :],
                    conv_ln_b[l][None, :], w_bf[:, OFF_GATE:], w_branch_a[l].astype(BF16),
                    w_branch_b[l].astype(BF16), w_branch_c[l].astype(BF16),
                    w_out[l].astype(BF16), n_rows // TOK_BLOCK, dims)
        k1p, k2p = _padded_keys(peer_keys[l])
        xy = _peer(x1, mod[l], peer_wq[l].astype(BF16), k1p, k2p,
                   peer_u[l].astype(BF16), _chunked_transpose(peer_v[l], PEER_ECHUNK),
                   final_norm_w[None, :],
                   n_rows // PEER_BLOCK, last, dims)
    return xy.reshape(nbatch, s_len, d)
```

```python
import functools
import math

import jax
import jax.numpy as jnp
from jax import lax
from jax.experimental import pallas as pl
from jax.experimental.pallas import tpu as pltpu

F32 = jnp.float32
BF16 = jnp.bfloat16

D_MODEL = 1024
DEPTH = 2
GRID_W = 64
HEAD_DIM = 64
ROPE_THETA = 10000.0
EPS = 1e-6
A_HEADS = 4
B_KV = 2
C_WIDTH = 512
C_KW = 31
P_HEADS = 8
P_NKEYS = 128
P_TOPK = 16

COL_KA, COL_VA, COL_KB, COL_VB, COL_QA, COL_QB, COL_GLU = 512, 512, 128, 128, 512, 512, 1024
OFF_KA = 0
OFF_VA = OFF_KA + COL_KA
OFF_KB = OFF_VA + COL_VA
OFF_QA = OFF_KB + COL_KB + COL_VB
OFF_QB = OFF_QA + COL_QA
OFF_GLU = OFF_QB + COL_QB
OFF_GATE = OFF_GLU + COL_GLU

V7X_LANES = 128
V7X_VMEM_LIMIT = 56 * 1024 * 1024
HALO = 16

TOK_BLOCK = 256
ATT_TQ = 512
ATT_TK = 512
PEER_BLOCK = 512
PEER_ECHUNK = 1024
MOD_ROWS = 8
ROW_GROUP = 16

Q_SCALE = (HEAD_DIM ** -0.5) * math.log2(math.e)
NEG_BIG = -1e30
NT_DIMS = (((1,), (1,)), ((), ()))


def _cparams(sem):
    return pltpu.CompilerParams(dimension_semantics=sem, vmem_limit_bytes=V7X_VMEM_LIMIT)


def _modulated(x, mod, shift_row, scale_row):
    ms = jnp.mean(x * x, axis=-1, keepdims=True)
    xn = x * lax.rsqrt(ms + EPS)
    return xn * (1.0 + mod[scale_row:scale_row + 1]) + mod[shift_row:shift_row + 1]


def _ada_kernel(c_ref, w_ref, b_ref, o_ref):
    c = c_ref[...]
    sc = c * jax.nn.sigmoid(c)
    o_ref[0] = jnp.dot(sc, w_ref[0], preferred_element_type=F32,
                       precision=lax.Precision.HIGHEST) + b_ref[0]


def _ada_rows(cvec, w_ada, b_ada):
    depth, d, n = w_ada.shape
    tn = 1536
    return pl.pallas_call(
        _ada_kernel,
        grid=(depth, n // tn),
        in_specs=[pl.BlockSpec((MOD_ROWS, d), lambda l, j: (0, 0)),
                  pl.BlockSpec((1, d, tn), lambda l, j: (l, 0, j)),
                  pl.BlockSpec((1, 1, tn), lambda l, j: (l, 0, j))],
        out_specs=pl.BlockSpec((1, MOD_ROWS, tn), lambda l, j: (l, 0, j)),
        out_shape=jax.ShapeDtypeStruct((depth, MOD_ROWS, n), F32),
        compiler_params=_cparams(("arbitrary", "arbitrary")),
        name="ada_rows",
    )(cvec, w_ada, b_ada.reshape(depth, 1, n))


def _inproj_kernel(x_ref, mod_ref, w_ref, cos_ref, sa_ref, sb_ref, qnw_ref, knw_ref,
                   qa_ref, ka_ref, va_ref, qb_ref, kbd_ref, vbd_ref, u_ref):
    bt = x_ref.shape[0]
    h = _modulated(x_ref[...], mod_ref[0], 0, 1).astype(BF16)
    cos, sa, sb = cos_ref[...], sa_ref[...], sb_ref[...]
    lo = lax.broadcasted_iota(jnp.int32, (bt, V7X_LANES), 1) < HEAD_DIM

    def proj(c0, width):
        return jnp.dot(h, w_ref[:, c0:c0 + width], preferred_element_type=F32)

    def rope(x):
        return x * cos + pltpu.roll(x, V7X_LANES - 16, 1) * sa + pltpu.roll(x, 16, 1) * sb

    def headnorm(x, w):
        x2 = x * x
        s_lo = jnp.sum(jnp.where(lo, x2, 0.0), axis=-1, keepdims=True)
        s_hi = jnp.sum(jnp.where(lo, 0.0, x2), axis=-1, keepdims=True)
        r = jnp.where(lo, lax.rsqrt(s_lo / HEAD_DIM + EPS), lax.rsqrt(s_hi / HEAD_DIM + EPS))
        return x * r * w

    def tile(p, c):
        return p[:, c * V7X_LANES:(c + 1) * V7X_LANES]

    p = proj(OFF_KA, COL_KA)
    for c in range(A_HEADS):
        ka_ref[c] = rope(tile(p, c)).astype(BF16)

    p = proj(OFF_VA, COL_VA)
    for c in range(A_HEADS):
        va_ref[c] = tile(p, c).astype(BF16)

    p = proj(OFF_KB, COL_KB + COL_VB)
    kb = rope(headnorm(tile(p, 0), knw_ref[...]))
    kb_sw = pltpu.roll(kb, HEAD_DIM, 1)
    kbd_ref[0] = jnp.where(lo, kb, kb_sw).astype(BF16)
    kbd_ref[1] = jnp.where(lo, kb_sw, kb).astype(BF16)
    vb = tile(p, 1)
    vb_sw = pltpu.roll(vb, HEAD_DIM, 1)
    vbd_ref[0] = jnp.where(lo, vb, vb_sw).astype(BF16)
    vbd_ref[1] = jnp.where(lo, vb_sw, vb).astype(BF16)

    p = proj(OFF_QA, COL_QA)
    for c in range(COL_QA // V7X_LANES):
        qa_ref[:, c * V7X_LANES:(c + 1) * V7X_LANES] = (rope(tile(p, c)) * Q_SCALE).astype(BF16)

    p = proj(OFF_QB, COL_QB)
    qnw = qnw_ref[...]
    for c in range(COL_QB // V7X_LANES):
        qb_ref[:, c * V7X_LANES:(c + 1) * V7X_LANES] = (
            rope(headnorm(tile(p, c), qnw)) * Q_SCALE).astype(BF16)

    p = proj(OFF_GLU, COL_GLU)
    u_ref[...] = p[:, :C_WIDTH] * jax.nn.sigmoid(p[:, C_WIDTH:])


def _inproj(xy, mod_l, w_main, tabs, qnw, knw, dims):
    t_rows, nps, nbatch = dims["T"], dims["S"] // TOK_BLOCK, dims["B"]
    bt = TOK_BLOCK
    row = lambda r: (r, 0)
    full = lambda r: (0, 0)
    out_shapes = (
        jax.ShapeDtypeStruct((t_rows, COL_QA), BF16),
        jax.ShapeDtypeStruct((A_HEADS, t_rows, V7X_LANES), BF16),
        jax.ShapeDtypeStruct((A_HEADS, t_rows, V7X_LANES), BF16),
        jax.ShapeDtypeStruct((t_rows, COL_QB), BF16),
        jax.ShapeDtypeStruct((B_KV, t_rows, V7X_LANES), BF16),
        jax.ShapeDtypeStruct((B_KV, t_rows, V7X_LANES), BF16),
        jax.ShapeDtypeStruct((t_rows, C_WIDTH), F32),
    )
    grp = pl.BlockSpec((B_KV, bt, V7X_LANES), lambda r: (0, r, 0))
    heads = pl.BlockSpec((A_HEADS, bt, V7X_LANES), lambda r: (0, r, 0))
    return pl.pallas_call(
        _inproj_kernel,
        grid=(t_rows // bt,),
        in_specs=[pl.BlockSpec((bt, D_MODEL), row),
                  pl.BlockSpec((1, 6, D_MODEL), lambda r: (jnp.minimum(r // nps, nbatch), 0, 0)),
                  pl.BlockSpec((D_MODEL, OFF_GATE), full),
                  pl.BlockSpec((bt, V7X_LANES), row),
                  pl.BlockSpec((bt, V7X_LANES), row),
                  pl.BlockSpec((bt, V7X_LANES), row),
                  pl.BlockSpec((1, V7X_LANES), full),
                  pl.BlockSpec((1, V7X_LANES), full)],
        out_specs=(pl.BlockSpec((bt, COL_QA), row), heads, heads, pl.BlockSpec((bt, COL_QB), row),
                   grp, grp, pl.BlockSpec((bt, C_WIDTH), row)),
        out_shape=out_shapes,
        compiler_params=_cparams(("arbitrary",)),
        name="inproj",
    )(xy, mod_l, w_main, tabs[0], tabs[1], tabs[2], qnw, knw)


def _softmax_step(s, v, m, l, acc):
    mn = jnp.maximum(m, jnp.max(s, axis=-1, keepdims=True))
    p = jnp.exp2(s - mn)
    alpha = jnp.exp2(m - mn)
    l = alpha * l + jnp.sum(p, axis=-1, keepdims=True)
    acc = alpha * acc + jnp.dot(p.astype(BF16), v, preferred_element_type=F32)
    return mn, l, acc


def _flash_rows(q, load_lat, kc, vc, *, n_lat_chunks, tk):
    rows = q.shape[0]

    def step(k, v, carry):
        m, l, a = carry
        s = lax.dot_general(q, k, NT_DIMS, preferred_element_type=F32)
        return _softmax_step(s, v, m, l, a)

    carry = (jnp.full((rows, 1), NEG_BIG, F32), jnp.zeros((rows, 1), F32),
             jnp.zeros((rows, V7X_LANES), F32))
    for j in range(n_lat_chunks):
        carry = step(*load_lat(j * tk), carry)
    _, l, a = step(kc, vc, carry)
    return a / l


def _attn_a_kernel(q_ref, kl_ref, vl_ref, kc_ref, vc_ref, lamv_ref, sw_ref, o_ref, *,
                   n_lat_chunks, tk, lam_init):
    tq = q_ref.shape[0]
    q = q_ref[...]
    lo = lax.broadcasted_iota(jnp.int32, (tq, V7X_LANES), 1) < HEAD_DIM
    zero = jnp.zeros_like(q)
    q12 = jnp.concatenate([jnp.where(lo, q, zero), jnp.where(lo, zero, q)], axis=0)
    o12 = _flash_rows(q12, lambda off: (kl_ref[0, pl.ds(off, tk), :], vl_ref[0, pl.ds(off, tk), :]),
                      kc_ref[0], vc_ref[0], n_lat_chunks=n_lat_chunks, tk=tk)

    lv = lamv_ref[...]
    lam = (jnp.exp(jnp.sum(lv[0:1] * lv[1:2], axis=-1, keepdims=True))
           - jnp.exp(jnp.sum(lv[2:3] * lv[3:4], axis=-1, keepdims=True)) + lam_init)
    o = o12[:tq] - lam * o12[tq:]
    ms = jnp.mean(o * o, axis=-1, keepdims=True)
    o_ref[...] = (o * lax.rsqrt(ms + EPS) * sw_ref[...] * (1.0 - lam_init)).astype(BF16)


def _attn_b_kernel(q_ref, kl_ref, vl_ref, kc_ref, vc_ref, o_ref, *, n_lat_chunks, tk):
    tq = q_ref.shape[0]
    lo = lax.broadcasted_iota(jnp.int32, (tq, V7X_LANES), 1) < HEAD_DIM
    parts = []
    for c in range(2):
        qc = q_ref[:, c * V7X_LANES:(c + 1) * V7X_LANES]
        zero = jnp.zeros_like(qc)
        parts += [jnp.where(lo, qc, zero), jnp.where(lo, zero, qc)]
    q4 = jnp.concatenate(parts, axis=0)
    o = _flash_rows(q4, lambda off: (kl_ref[0, pl.ds(off, tk), :], vl_ref[0, pl.ds(off, tk), :]),
                    kc_ref[0], vc_ref[0], n_lat_chunks=n_lat_chunks, tk=tk)
    for c in range(2):
        o_ref[:, c * V7X_LANES:(c + 1) * V7X_LANES] = jnp.where(
            lo, o[(2 * c) * tq:(2 * c + 1) * tq], o[(2 * c + 1) * tq:(2 * c + 2) * tq]).astype(BF16)


def _attention_call(kern, name, q, k, v, extras, q_lanes, n_groups, tq, latent_queries, dims):
    s_len, ctx_len, nbatch = dims["S"], dims["CTX"], dims["B"]
    tk = min(ATT_TK, s_len)
    q_len = s_len if latent_queries else ctx_len
    nq = q_len // tq
    q_blk0 = 0 if latent_queries else (nbatch * s_len) // tq
    ctx_blk0 = (nbatch * s_len) // ctx_len
    extra_specs = [pl.BlockSpec(e.shape, lambda b, g, qi: (0, 0)) for e in extras]
    return pl.pallas_call(
        functools.partial(kern, n_lat_chunks=s_len // tk if latent_queries else 0, tk=tk),
        grid=(nbatch, n_groups, nq),
        in_specs=[pl.BlockSpec((tq, q_lanes), lambda b, g, qi: (q_blk0 + b * nq + qi, g)),
                  pl.BlockSpec((1, s_len, V7X_LANES), lambda b, g, qi: (g, b, 0)),
                  pl.BlockSpec((1, s_len, V7X_LANES), lambda b, g, qi: (g, b, 0)),
                  pl.BlockSpec((1, ctx_len, V7X_LANES), lambda b, g, qi: (g, ctx_blk0 + b, 0)),
                  pl.BlockSpec((1, ctx_len, V7X_LANES), lambda b, g, qi: (g, ctx_blk0 + b, 0)),
                  *extra_specs],
        out_specs=pl.BlockSpec((tq, q_lanes), lambda b, g, qi: (b * nq + qi, g)),
        out_shape=jax.ShapeDtypeStruct((nbatch * q_len, n_groups * q_lanes), BF16),
        compiler_params=_cparams(("arbitrary", "arbitrary", "arbitrary")),
        name=name,
    )(q, k, v, k, v, *extras)


def _attention(qa, ka, va, qb, kbd, vbd, lamv, subln_w, lam_init, with_ctx_queries, dims):
    kern_a = functools.partial(_attn_a_kernel, lam_init=lam_init)
    gw = 2 * V7X_LANES

    def both(latent, tag):
        q_len = dims["S"] if latent else dims["CTX"]
        oa = _attention_call(kern_a, "attn_a" + tag, qa, ka, va, (lamv, subln_w), V7X_LANES,
                             A_HEADS, min(ATT_TQ, q_len), latent, dims)
        ob = _attention_call(_attn_b_kernel, "attn_b" + tag, qb, kbd, vbd, (), gw, B_KV,
                             min(ATT_TQ // 2, q_len), latent, dims)
        return oa, ob

    oa, ob = both(True, "")
    if with_ctx_queries:
        oa_c, ob_c = both(False, "_ctx")
        oa, ob = jnp.concatenate([oa, oa_c], axis=0), jnp.concatenate([ob, ob_c], axis=0)
    return oa, ob


def _merge_kernel(x_ref, mod_ref, oa_ref, ob_ref, u_ref, up_ref, un_ref, cw_ref, cb_ref,
                  lnw_ref, lnb_ref, wg_ref, wa_ref, wb_ref, wc_ref, wo_ref, o_ref, uext_ref, *,
                  nps, ncps, n_lat_blocks):
    bt = x_ref.shape[0]
    r = pl.program_id(0)
    is_lat = r < n_lat_blocks
    pos = jnp.where(is_lat, r % nps, (r - n_lat_blocks) % ncps)
    last_pos = jnp.where(is_lat, nps - 1, ncps - 1)
    keep_prev = (pos != 0).astype(F32)
    keep_next = (pos != last_pos).astype(F32)

    x = x_ref[...]
    mod = mod_ref[0]
    h = _modulated(x, mod, 0, 1).astype(BF16)

    uext_ref[0:HALO, :] = up_ref[...] * keep_prev
    uext_ref[HALO:HALO + bt, :] = u_ref[...]
    uext_ref[HALO + bt:2 * HALO + bt, :] = un_ref[...] * keep_next
    cw = cw_ref[...]
    y = jnp.zeros((bt, C_WIDTH), F32) + cb_ref[...]
    base = HALO - C_KW // 2
    for k in range(C_KW):
        y = y + cw[k:k + 1, :] * uext_ref[base + k:base + k + bt, :]
    mu = jnp.mean(y, axis=-1, keepdims=True)
    yc = y - mu
    var = jnp.mean(yc * yc, axis=-1, keepdims=True)
    yn = yc * lax.rsqrt(var + EPS) * lnw_ref[...] + lnb_ref[...]
    oc = (yn * jax.nn.sigmoid(yn)).astype(BF16)

    def gate(i):
        logits = jnp.dot(h, wg_ref[:, i * D_MODEL:(i + 1) * D_MODEL], preferred_element_type=F32)
        return jax.nn.sigmoid(logits)

    merged = gate(0) * jnp.dot(oa_ref[...], wa_ref[...], preferred_element_type=F32)
    merged = merged + gate(1) * jnp.dot(ob_ref[...], wb_ref[...], preferred_element_type=F32)
    merged = merged + gate(2) * jnp.dot(oc, wc_ref[...], preferred_element_type=F32)
    out = jnp.dot(merged.astype(BF16), wo_ref[...], preferred_element_type=F32)
    o_ref[...] = x + mod[2:3] * out


def _merge(xy, mod_l, oa, ob, u, cw, cb, lnw, lnb, wg, wa, wb, wc, wo, n_blocks, dims):
    bt = TOK_BLOCK
    t_rows, nps, nbatch = dims["T"], dims["S"] // bt, dims["B"]
    hpb = bt // HALO
    n_halo = t_rows // HALO
    row = lambda r: (r, 0)
    full = lambda r: (0, 0)
    return pl.pallas_call(
        functools.partial(_merge_kernel, nps=nps, ncps=dims["CTX"] // bt,
                          n_lat_blocks=nbatch * nps),
        grid=(n_blocks,),
        in_specs=[pl.BlockSpec((bt, D_MODEL), row),
                  pl.BlockSpec((1, 6, D_MODEL), lambda r: (jnp.minimum(r // nps, nbatch), 0, 0)),
                  pl.BlockSpec((bt, 4 * V7X_LANES), row),
                  pl.BlockSpec((bt, 4 * V7X_LANES), row),
                  pl.BlockSpec((bt, C_WIDTH), row),
                  pl.BlockSpec((HALO, C_WIDTH), lambda r: (jnp.maximum(r * hpb - 1, 0), 0)),
                  pl.BlockSpec((HALO, C_WIDTH), lambda r: (jnp.minimum((r + 1) * hpb, n_halo - 1), 0)),
                  pl.BlockSpec((C_KW + 1, C_WIDTH), full),
                  pl.BlockSpec((1, C_WIDTH), full),
                  pl.BlockSpec((1, C_WIDTH), full),
                  pl.BlockSpec((1, C_WIDTH), full),
                  pl.BlockSpec((D_MODEL, 3 * D_MODEL), full),
                  pl.BlockSpec((C_WIDTH, D_MODEL), full),
                  pl.BlockSpec((C_WIDTH, D_MODEL), full),
                  pl.BlockSpec((C_WIDTH, D_MODEL), full),
                  pl.BlockSpec((D_MODEL, D_MODEL), full)],
        out_specs=pl.BlockSpec((bt, D_MODEL), row),
        out_shape=jax.ShapeDtypeStruct((n_blocks * bt, D_MODEL), F32),
        scratch_shapes=[pltpu.VMEM((bt + 2 * HALO, C_WIDTH), F32)],
        compiler_params=_cparams(("arbitrary",)),
        name="merge",
    )(xy, mod_l, oa, ob, u, u, u, cw, cb, lnw, lnb, wg, wa, wb, wc, wo)


def _topk_desc(s, k, with_rank=False):
    row = lax.broadcasted_iota(jnp.int32, (k, s.shape[1]), 0)

    def body(i, carry):
        work, vals, rank = carry
        m = jnp.max(work, axis=0, keepdims=True)
        hit = work == m
        vals = jnp.where(row == i, m, vals)
        if with_rank:
            rank = jnp.where(hit, i.astype(F32), rank)
        work = jnp.where(hit, -jnp.inf, work)
        return work, vals, rank

    rank0 = jnp.full(s.shape, float(k), F32) if with_rank else jnp.zeros((1, 1), F32)
    _, vals, rank = lax.fori_loop(0, k, body, (s, jnp.zeros((k, s.shape[1]), F32), rank0))
    return (vals, rank) if with_rank else vals


def _candidate_sums(sv1, sv2):
    half = P_TOPK // 2
    lead, rest = sv1[:half], sv1[half:]
    row = lax.broadcasted_iota(jnp.int32, lead.shape, 0)
    tiles = [lead + sv2[0:1], rest + sv2[0:1], lead + sv2[1:2]]
    for k2 in range(2, half):
        tiles.append(jnp.where(row < P_TOPK // (k2 + 1), lead + sv2[k2:k2 + 1], -jnp.inf))
    tiles.append(sv1[0:1] + sv2[half:])
    return jnp.concatenate(tiles, axis=0)


def _bf16_pair_words(x):
    bits = pltpu.bitcast(x.astype(BF16).astype(F32), jnp.int32)
    hi = lax.shift_right_logical(bits, 16)
    return hi | lax.shift_left(hi, 16)


def _peer_select_kernel(x_ref, mod_ref, wq_ref, k1_ref, k2_ref,
                        t_ref, need_ref, e1_ref, r2_ref, e2_ref, s1_scr):
    bt = x_ref.shape[0]
    n_lt = bt // V7X_LANES
    hb = _modulated(x_ref[...], mod_ref[0], 3, 4).astype(BF16)
    t_ref[...] = hb
    q = jnp.dot(hb, wq_ref[...], preferred_element_type=F32).astype(BF16)
    s1p = lax.dot_general(k1_ref[...], q, NT_DIMS, preferred_element_type=F32)
    for lt in range(n_lt):
        s1_scr[lt] = s1p[:, lt * V7X_LANES:(lt + 1) * V7X_LANES]
    head_row = lax.broadcasted_iota(jnp.int32, (P_HEADS, bt), 0)
    zero_t = jnp.zeros((P_HEADS, bt), F32)
    thr1, max1, tau_t = zero_t, zero_t, zero_t
    sv2_t = [zero_t] * P_TOPK
    for hh in range(P_HEADS):
        mine = head_row == hh
        qh = q[:, hh * V7X_LANES:(hh + 1) * V7X_LANES]
        s1 = jnp.concatenate([s1_scr[lt, pl.ds(hh, P_NKEYS, stride=P_HEADS), :]
                              for lt in range(n_lt)], axis=1)
        s2 = lax.dot_general(k2_ref[hh], qh, NT_DIMS, preferred_element_type=F32)
        sv1 = _topk_desc(s1, P_TOPK)
        sv2, rank2 = _topk_desc(s2, P_TOPK, with_rank=True)
        top = _topk_desc(_candidate_sums(sv1, sv2), P_TOPK)
        z = jnp.sum(jnp.exp(top - top[0:1]), axis=0, keepdims=True)
        for k in range(P_TOPK):
            sv2_t[k] = jnp.where(mine, sv2[k:k + 1], sv2_t[k])
        r2_ref[0, hh] = pltpu.bitcast(rank2.astype(BF16), jnp.int32)
        e2_ref[0, hh] = pltpu.bitcast((jnp.exp(s2 - sv2[0:1]) / z).astype(BF16), jnp.int32)
        tau_t = jnp.where(mine, top[P_TOPK - 1:P_TOPK], tau_t)
        thr1 = jnp.where(mine, sv1[P_TOPK - 1:P_TOPK], thr1)
        max1 = jnp.where(mine, sv1[0:1], max1)
    for lt in range(n_lt):
        ls = slice(lt * V7X_LANES, (lt + 1) * V7X_LANES)
        s1t = s1_scr[lt].reshape(P_NKEYS, P_HEADS, V7X_LANES)
        count = jnp.zeros_like(s1t)
        for k in range(P_TOPK):
            count = count + jnp.where(s1t + sv2_t[k][None, :, ls] >= tau_t[None, :, ls], 1.0, 0.0)
        need = jnp.where(s1t >= thr1[None, :, ls], count, 0.0)
        need_ref[0, lt] = _bf16_pair_words(need.reshape(P_NKEYS * P_HEADS, V7X_LANES))
        e1 = jnp.exp(s1t - max1[None, :, ls])
        e1_ref[0, lt] = _bf16_pair_words(e1.reshape(P_NKEYS * P_HEADS, V7X_LANES))


def _peer_dense_kernel(x_ref, mod_ref, t_ref, need_ref, e1_ref, r2_ref, e2_ref, u_ref, vt_ref,
                       fnw_ref, o_ref, a_scr, m_scr, acc_scr, *, n_chunks, final_norm):
    bt = x_ref.shape[0]
    c = pl.program_id(1)
    ipc = u_ref.shape[0] // P_NKEYS
    n_lt = bt // V7X_LANES

    @pl.when(c == 0)
    def _init():
        acc_scr[...] = jnp.zeros_like(acc_scr)
        a_scr[...] = jnp.zeros_like(a_scr)

    cprev = jnp.maximum(c - 1, 0)

    def one_key(ii, _):
        roff = ii * P_NKEYS
        hrow = pl.ds(pl.multiple_of((cprev * ipc + ii) * P_HEADS, P_HEADS), P_HEADS)
        for lt in range(n_lt):
            ls = slice(lt * V7X_LANES, (lt + 1) * V7X_LANES)
            w = [jnp.zeros((ROW_GROUP, V7X_LANES), BF16) for _ in range(P_NKEYS // ROW_GROUP)]
            need_t = need_ref[0, lt, hrow, :]
            e1_t = e1_ref[0, lt, hrow, :]
            zero = jnp.zeros((ROW_GROUP, V7X_LANES), BF16)
            for hh in range(P_HEADS):
                words = (ROW_GROUP // 2, V7X_LANES)
                need_b = pltpu.bitcast(jnp.broadcast_to(need_t[hh:hh + 1], words), BF16)
                e1_b = pltpu.bitcast(jnp.broadcast_to(e1_t[hh:hh + 1], words), BF16)
                for jg in range(P_NKEYS // ROW_GROUP):
                    js = slice(jg * (ROW_GROUP // 2), (jg + 1) * (ROW_GROUP // 2))
                    rank2 = pltpu.bitcast(r2_ref[0, hh, js, ls], BF16)
                    e2 = pltpu.bitcast(e2_ref[0, hh, js, ls], BF16)
                    w[jg] = w[jg] + jnp.where(rank2 < need_b, e1_b * e2, zero)
            for jg in range(P_NKEYS // ROW_GROUP):
                rows = pl.ds(roff + jg * ROW_GROUP, ROW_GROUP)
                a = a_scr[rows, ls]
                g = 0.5 * a * (1.0 + lax.erf(a * (2.0 ** -0.5)))
                m_scr[rows, ls] = g.astype(BF16) * w[jg]
        return 0

    for ii in range(ipc):
        one_key(ii, 0)

    acc_scr[...] += jnp.dot(vt_ref[0], m_scr[...], preferred_element_type=F32)
    a_scr[...] = lax.dot_general(u_ref[...], t_ref[...], NT_DIMS, preferred_element_type=F32)

    @pl.when(c == n_chunks)
    def _finish():
        x = x_ref[...]
        y = x + mod_ref[0][5:6] * acc_scr[...].T
        if final_norm:
            ms = jnp.mean(y * y, axis=-1, keepdims=True)
            y = y * lax.rsqrt(ms + EPS) * fnw_ref[...]
        o_ref[...] = y


def _peer(x1, mod_l, wq, k1p, k2p, u_tab, vt_tab, fnw, n_blocks, final_norm, dims):
    bt, ec = PEER_BLOCK, vt_tab.shape[2]
    nps, nbatch = dims["S"] // bt, dims["B"]
    n_chunks = vt_tab.shape[0]
    n_lt = bt // V7X_LANES
    rows_ih = P_NKEYS * P_HEADS
    mod_spec = pl.BlockSpec((1, 6, D_MODEL), lambda r, *_: (jnp.minimum(r // nps, nbatch), 0, 0))
    ih_shape = jax.ShapeDtypeStruct((n_blocks, n_lt, rows_ih, V7X_LANES), jnp.int32)
    hj_shape = jax.ShapeDtypeStruct((n_blocks, P_HEADS, P_NKEYS // 2, bt), jnp.int32)
    ih_spec = pl.BlockSpec((1, n_lt, rows_ih, V7X_LANES), lambda r, *_: (r, 0, 0, 0))
    hj_spec = pl.BlockSpec((1, P_HEADS, P_NKEYS // 2, bt), lambda r, *_: (r, 0, 0, 0))
    tok_spec = pl.BlockSpec((bt, D_MODEL), lambda r, *_: (r, 0))

    t_mod, need, e1, rank2, e2 = pl.pallas_call(
        _peer_select_kernel,
        grid=(n_blocks,),
        in_specs=[tok_spec, mod_spec,
                  pl.BlockSpec((D_MODEL, P_HEADS * V7X_LANES), lambda r: (0, 0)),
                  pl.BlockSpec((rows_ih, P_HEADS * V7X_LANES), lambda r: (0, 0)),
                  pl.BlockSpec((P_HEADS, P_NKEYS, V7X_LANES), lambda r: (0, 0, 0))],
        out_specs=(tok_spec, ih_spec, ih_spec, hj_spec, hj_spec),
        out_shape=(jax.ShapeDtypeStruct((n_blocks * bt, D_MODEL), BF16),
                   ih_shape,
                   ih_shape,
                   hj_shape,
                   hj_shape),
        scratch_shapes=[pltpu.VMEM((n_lt, rows_ih, V7X_LANES), F32)],
        compiler_params=_cparams(("arbitrary",)),
        name="peer_select",
    )(x1, mod_l, wq, k1p, k2p)

    return pl.pallas_call(
        functools.partial(_peer_dense_kernel, n_chunks=n_chunks, final_norm=final_norm),
        grid=(n_blocks, n_chunks + 1),
        in_specs=[tok_spec, mod_spec, tok_spec, ih_spec, ih_spec, hj_spec, hj_spec,
                  pl.BlockSpec((ec, D_MODEL), lambda r, c: (jnp.minimum(c, n_chunks - 1), 0)),
                  pl.BlockSpec((1, D_MODEL, ec), lambda r, c: (jnp.maximum(c - 1, 0), 0, 0)),
                  pl.BlockSpec((1, D_MODEL), lambda r, c: (0, 0))],
        out_specs=tok_spec,
        out_shape=jax.ShapeDtypeStruct((n_blocks * bt, D_MODEL), F32),
        scratch_shapes=[pltpu.VMEM((ec, bt), F32),
                        pltpu.VMEM((ec, bt), BF16),
                        pltpu.VMEM((D_MODEL, bt), F32)],
        compiler_params=_cparams(("arbitrary", "arbitrary")),
        name="peer_dense",
    )(x1, mod_l, t_mod, need, e1, rank2, e2, u_tab, vt_tab, fnw)


def _rope_tables(s_len, nbatch, n_ctx_rows):
    t = jnp.arange(s_len, dtype=jnp.int32)
    row = (t // GRID_W).astype(F32)
    col = (t % GRID_W).astype(F32)
    axis_dim = HEAD_DIM // 2
    inv = ROPE_THETA ** (-jnp.arange(0, axis_dim, 2, dtype=F32) / axis_dim)
    ar = row[:, None] * inv[None, :]
    ac = col[:, None] * inv[None, :]
    ang = jnp.concatenate([ar, ar, ac, ac], axis=-1)
    cos, sin = jnp.cos(ang), jnp.sin(ang)
    quarter = jnp.arange(HEAD_DIM) // (HEAD_DIM // 4)
    first = (quarter % 2 == 0)[None, :]
    sin_a = jnp.where(first, -sin, 0.0)
    sin_b = jnp.where(first, 0.0, sin)

    def expand(tab, ctx_fill):
        lat = jnp.tile(jnp.concatenate([tab, tab], axis=-1), (nbatch, 1))
        return jnp.concatenate([lat, jnp.full((n_ctx_rows, V7X_LANES), ctx_fill, F32)], axis=0)

    return expand(cos, 1.0), expand(sin_a, 0.0), expand(sin_b, 0.0)


def _padded_keys(keys_l):
    z = jnp.zeros_like(keys_l[:, 0])
    k0 = jnp.concatenate([keys_l[:, 0], z], axis=-1).transpose(1, 0, 2)
    eye = jnp.eye(P_HEADS, dtype=keys_l.dtype)
    k1p = (k0[:, :, None, :] * eye[None, :, :, None]).reshape(
        P_NKEYS * P_HEADS, P_HEADS * V7X_LANES)
    k2p = jnp.concatenate([z, keys_l[:, 1]], axis=-1)
    return k1p.astype(BF16), k2p.astype(BF16)


def _chunked_transpose(v_tab, ec):
    n_exp, d = v_tab.shape
    return v_tab.astype(BF16).reshape(n_exp // ec, ec, d).transpose(0, 2, 1)


def kernel(x, c, ctx, c_ctx, w_ada, b_ada, w_in, lam_q1, lam_k1, lam_q2, lam_k2, subln_w, q_norm_w, k_norm_w, conv_w, conv_b, conv_ln_w, conv_ln_b, w_branch_a, w_branch_b, w_branch_c, w_out, peer_wq, peer_keys, peer_u, peer_v, final_norm_w):
    nbatch, s_len, d = x.shape
    ctx_len = ctx.shape[1]
    depth = w_ada.shape[0]
    n_lat, n_ctx = nbatch * s_len, nbatch * ctx_len
    dims = {"B": nbatch, "S": s_len, "CTX": ctx_len, "T": n_lat + n_ctx}
    assert d == D_MODEL and nbatch + 1 <= MOD_ROWS
    assert s_len % PEER_BLOCK == 0 and s_len % ATT_TQ == 0 and ctx_len % TOK_BLOCK == 0
    assert ctx_len % min(ATT_TQ, ctx_len) == 0 and n_lat % min(ATT_TQ, ctx_len) == 0
    assert n_ctx % PEER_BLOCK == 0 and n_lat % ctx_len == 0

    cvec = jnp.zeros((MOD_ROWS, d), F32).at[:nbatch].set(c).at[nbatch].set(c_ctx)
    mod = _ada_rows(cvec, w_ada, b_ada).reshape(depth, MOD_ROWS, 6, d)
    tabs = _rope_tables(s_len, nbatch, n_ctx)
    tile2 = lambda v: jnp.concatenate([v, v], axis=-1)[None, :]

    xy = jnp.concatenate([x.reshape(n_lat, d), ctx.reshape(n_ctx, d)], axis=0)
    for l in range(depth):
        last = l == depth - 1
        lam_init = 0.8 - 0.6 * math.exp(-0.3 * l)
        w_bf = w_in[l].astype(BF16)
        lamv = jnp.stack([lam_q1[l], lam_k1[l], lam_q2[l], lam_k2[l]], axis=0)

        qa, ka, va, qb, kbd, vbd, u = _inproj(
            xy, mod[l], w_bf[:, :OFF_GATE], tabs, tile2(q_norm_w[l]), tile2(k_norm_w[l]), dims)
        oa, ob = _attention(qa, ka, va, qb, kbd, vbd, lamv, subln_w[l][None, :], lam_init,
                            not last, dims)
        n_rows = n_lat if last else n_lat + n_ctx
        cw = jnp.concatenate([conv_w[l], jnp.zeros((1, C_WIDTH), F32)], axis=0)
        x1 = _merge(xy, mod[l], oa, ob, u, cw, conv_b[l][None, :], conv_ln_w[l][None, :],
                    conv_ln_b[l][None, :], w_bf[:, OFF_GATE:], w_branch_a[l].astype(BF16),
                    w_branch_b[l].astype(BF16), w_branch_c[l].astype(BF16),
                    w_out[l].astype(BF16), n_rows // TOK_BLOCK, dims)
        k1p, k2p = _padded_keys(peer_keys[l])
        xy = _peer(x1, mod[l], peer_wq[l].astype(BF16), k1p, k2p,
                   peer_u[l].astype(BF16), _chunked_transpose(peer_v[l], PEER_ECHUNK),
                   final_norm_w[None, :],
                   n_rows // PEER_BLOCK, last, dims)
    return xy.reshape(nbatch, s_len, d)
```

```python
import functools
import math

import jax
import jax.numpy as jnp
from jax import lax
from jax.experimental import pallas as pl
from jax.experimental.pallas import tpu as pltpu

F32 = jnp.float32
BF16 = jnp.bfloat16

D_MODEL = 1024
DEPTH = 2
GRID_W = 64
HEAD_DIM = 64
ROPE_THETA = 10000.0
EPS = 1e-6
A_HEADS = 4
B_KV = 2
C_WIDTH = 512
C_KW = 31
P_HEADS = 8
P_NKEYS = 128
P_TOPK = 16

COL_KA, COL_VA, COL_KB, COL_VB, COL_QA, COL_QB, COL_GLU = 512, 512, 128, 128, 512, 512, 1024
OFF_KA = 0
OFF_VA = OFF_KA + COL_KA
OFF_KB = OFF_VA + COL_VA
OFF_QA = OFF_KB + COL_KB + COL_VB
OFF_QB = OFF_QA + COL_QA
OFF_GLU = OFF_QB + COL_QB
OFF_GATE = OFF_GLU + COL_GLU

V7X_LANES = 128
V7X_VMEM_LIMIT = 56 * 1024 * 1024
HALO = 16

TOK_BLOCK = 256
ATT_TQ = 512
ATT_TK = 512
PEER_BLOCK = 512
PEER_ECHUNK = 1024
MOD_ROWS = 8
ROW_GROUP = 16

Q_SCALE = (HEAD_DIM ** -0.5) * math.log2(math.e)
NEG_BIG = -1e30
NT_DIMS = (((1,), (1,)), ((), ()))


def _cparams(sem):
    return pltpu.CompilerParams(dimension_semantics=sem, vmem_limit_bytes=V7X_VMEM_LIMIT)


def _modulated(x, mod, shift_row, scale_row):
    ms = jnp.mean(x * x, axis=-1, keepdims=True)
    xn = x * lax.rsqrt(ms + EPS)
    return xn * (1.0 + mod[scale_row:scale_row + 1]) + mod[shift_row:shift_row + 1]


def _ada_kernel(c_ref, w_ref, b_ref, o_ref):
    c = c_ref[...]
    sc = c * jax.nn.sigmoid(c)
    o_ref[0] = jnp.dot(sc, w_ref[0], preferred_element_type=F32,
                       precision=lax.Precision.HIGHEST) + b_ref[0]


def _ada_rows(cvec, w_ada, b_ada):
    depth, d, n = w_ada.shape
    tn = 1536
    return pl.pallas_call(
        _ada_kernel,
        grid=(depth, n // tn),
        in_specs=[pl.BlockSpec((MOD_ROWS, d), lambda l, j: (0, 0)),
                  pl.BlockSpec((1, d, tn), lambda l, j: (l, 0, j)),
                  pl.BlockSpec((1, 1, tn), lambda l, j: (l, 0, j))],
        out_specs=pl.BlockSpec((1, MOD_ROWS, tn), lambda l, j: (l, 0, j)),
        out_shape=jax.ShapeDtypeStruct((depth, MOD_ROWS, n), F32),
        compiler_params=_cparams(("arbitrary", "arbitrary")),
        name="ada_rows",
    )(cvec, w_ada, b_ada.reshape(depth, 1, n))


def _inproj_kernel(x_ref, mod_ref, w_ref, cos_ref, sa_ref, sb_ref, qnw_ref, knw_ref,
                   qa_ref, ka_ref, va_ref, qb_ref, kbd_ref, vbd_ref, u_ref):
    bt = x_ref.shape[0]
    h = _modulated(x_ref[...], mod_ref[0], 0, 1).astype(BF16)
    cos, sa, sb = cos_ref[...], sa_ref[...], sb_ref[...]
    lo = lax.broadcasted_iota(jnp.int32, (bt, V7X_LANES), 1) < HEAD_DIM

    def proj(c0, width):
        return jnp.dot(h, w_ref[:, c0:c0 + width], preferred_element_type=F32)

    def rope(x):
        return x * cos + pltpu.roll(x, V7X_LANES - 16, 1) * sa + pltpu.roll(x, 16, 1) * sb

    def headnorm(x, w):
        x2 = x * x
        s_lo = jnp.sum(jnp.where(lo, x2, 0.0), axis=-1, keepdims=True)
        s_hi = jnp.sum(jnp.where(lo, 0.0, x2), axis=-1, keepdims=True)
        r = jnp.where(lo, lax.rsqrt(s_lo / HEAD_DIM + EPS), lax.rsqrt(s_hi / HEAD_DIM + EPS))
        return x * r * w

    def tile(p, c):
        return p[:, c * V7X_LANES:(c + 1) * V7X_LANES]

    p = proj(OFF_KA, COL_KA)
    for c in range(A_HEADS):
        ka_ref[c] = rope(tile(p, c)).astype(BF16)

    p = proj(OFF_VA, COL_VA)
    for c in range(A_HEADS):
        va_ref[c] = tile(p, c).astype(BF16)

    p = proj(OFF_KB, COL_KB + COL_VB)
    kb = rope(headnorm(tile(p, 0), knw_ref[...]))
    kb_sw = pltpu.roll(kb, HEAD_DIM, 1)
    kbd_ref[0] = jnp.where(lo, kb, kb_sw).astype(BF16)
    kbd_ref[1] = jnp.where(lo, kb_sw, kb).astype(BF16)
    vb = tile(p, 1)
    vb_sw = pltpu.roll(vb, HEAD_DIM, 1)
    vbd_ref[0] = jnp.where(lo, vb, vb_sw).astype(BF16)
    vbd_ref[1] = jnp.where(lo, vb_sw, vb).astype(BF16)

    p = proj(OFF_QA, COL_QA)
    for c in range(COL_QA // V7X_LANES):
        qa_ref[:, c * V7X_LANES:(c + 1) * V7X_LANES] = (rope(tile(p, c)) * Q_SCALE).astype(BF16)

    p = proj(OFF_QB, COL_QB)
    qnw = qnw_ref[...]
    for c in range(COL_QB // V7X_LANES):
        qb_ref[:, c * V7X_LANES:(c + 1) * V7X_LANES] = (
            rope(headnorm(tile(p, c), qnw)) * Q_SCALE).astype(BF16)

    p = proj(OFF_GLU, COL_GLU)
    u_ref[...] = p[:, :C_WIDTH] * jax.nn.sigmoid(p[:, C_WIDTH:])


def _inproj(xy, mod_l, w_main, tabs, qnw, knw, dims):
    t_rows, nps, nbatch = dims["T"], dims["S"] // TOK_BLOCK, dims["B"]
    bt = TOK_BLOCK
    row = lambda r: (r, 0)
    full = lambda r: (0, 0)
    out_shapes = (
        jax.ShapeDtypeStruct((t_rows, COL_QA), BF16),
        jax.ShapeDtypeStruct((A_HEADS, t_rows, V7X_LANES), BF16),
        jax.ShapeDtypeStruct((A_HEADS, t_rows, V7X_LANES), BF16),
        jax.ShapeDtypeStruct((t_rows, COL_QB), BF16),
        jax.ShapeDtypeStruct((B_KV, t_rows, V7X_LANES), BF16),
        jax.ShapeDtypeStruct((B_KV, t_rows, V7X_LANES), BF16),
        jax.ShapeDtypeStruct((t_rows, C_WIDTH), F32),
    )
    grp = pl.BlockSpec((B_KV, bt, V7X_LANES), lambda r: (0, r, 0))
    heads = pl.BlockSpec((A_HEADS, bt, V7X_LANES), lambda r: (0, r, 0))
    return pl.pallas_call(
        _inproj_kernel,
        grid=(t_rows // bt,),
        in_specs=[pl.BlockSpec((bt, D_MODEL), row),
                  pl.BlockSpec((1, 6, D_MODEL), lambda r: (jnp.minimum(r // nps, nbatch), 0, 0)),
                  pl.BlockSpec((D_MODEL, OFF_GATE), full),
                  pl.BlockSpec((bt, V7X_LANES), row),
                  pl.BlockSpec((bt, V7X_LANES), row),
                  pl.BlockSpec((bt, V7X_LANES), row),
                  pl.BlockSpec((1, V7X_LANES), full),
                  pl.BlockSpec((1, V7X_LANES), full)],
        out_specs=(pl.BlockSpec((bt, COL_QA), row), heads, heads, pl.BlockSpec((bt, COL_QB), row),
                   grp, grp, pl.BlockSpec((bt, C_WIDTH), row)),
        out_shape=out_shapes,
        compiler_params=_cparams(("arbitrary",)),
        name="inproj",
    )(xy, mod_l, w_main, tabs[0], tabs[1], tabs[2], qnw, knw)


def _softmax_step(s, v, m, l, acc):
    mn = jnp.maximum(m, jnp.max(s, axis=-1, keepdims=True))
    p = jnp.exp2(s - mn)
    alpha = jnp.exp2(m - mn)
    l = alpha * l + jnp.sum(p, axis=-1, keepdims=True)
    acc = alpha * acc + jnp.dot(p.astype(BF16), v, preferred_element_type=F32)
    return mn, l, acc


def _flash_rows(q, load_lat, kc, vc, *, n_lat_chunks, tk):
    rows = q.shape[0]

    def step(k, v, carry):
        m, l, a = carry
        s = lax.dot_general(q, k, NT_DIMS, preferred_element_type=F32)
        return _softmax_step(s, v, m, l, a)

    carry = (jnp.full((rows, 1), NEG_BIG, F32), jnp.zeros((rows, 1), F32),
             jnp.zeros((rows, V7X_LANES), F32))
    for j in range(n_lat_chunks):
        carry = step(*load_lat(j * tk), carry)
    _, l, a = step(kc, vc, carry)
    return a / l


def _attn_a_kernel(q_ref, kl_ref, vl_ref, kc_ref, vc_ref, lamv_ref, sw_ref, o_ref, *,
                   n_lat_chunks, tk, lam_init):
    tq = q_ref.shape[0]
    q = q_ref[...]
    lo = lax.broadcasted_iota(jnp.int32, (tq, V7X_LANES), 1) < HEAD_DIM
    zero = jnp.zeros_like(q)
    q12 = jnp.concatenate([jnp.where(lo, q, zero), jnp.where(lo, zero, q)], axis=0)
    o12 = _flash_rows(q12, lambda off: (kl_ref[0, pl.ds(off, tk), :], vl_ref[0, pl.ds(off, tk), :]),
                      kc_ref[0], vc_ref[0], n_lat_chunks=n_lat_chunks, tk=tk)

    lv = lamv_ref[...]
    lam = (jnp.exp(jnp.sum(lv[0:1] * lv[1:2], axis=-1, keepdims=True))
           - jnp.exp(jnp.sum(lv[2:3] * lv[3:4], axis=-1, keepdims=True)) + lam_init)
    o = o12[:tq] - lam * o12[tq:]
    ms = jnp.mean(o * o, axis=-1, keepdims=True)
    o_ref[...] = (o * lax.rsqrt(ms + EPS) * sw_ref[...] * (1.0 - lam_init)).astype(BF16)


def _attn_b_kernel(q_ref, kl_ref, vl_ref, kc_ref, vc_ref, o_ref, *, n_lat_chunks, tk):
    tq = q_ref.shape[0]
    lo = lax.broadcasted_iota(jnp.int32, (tq, V7X_LANES), 1) < HEAD_DIM
    parts = []
    for c in range(2):
        qc = q_ref[:, c * V7X_LANES:(c + 1) * V7X_LANES]
        zero = jnp.zeros_like(qc)
        parts += [jnp.where(lo, qc, zero), jnp.where(lo, zero, qc)]
    q4 = jnp.concatenate(parts, axis=0)
    o = _flash_rows(q4, lambda off: (kl_ref[0, pl.ds(off, tk), :], vl_ref[0, pl.ds(off, tk), :]),
                    kc_ref[0], vc_ref[0], n_lat_chunks=n_lat_chunks, tk=tk)
    for c in range(2):
        o_ref[:, c * V7X_LANES:(c + 1) * V7X_LANES] = jnp.where(
            lo, o[(2 * c) * tq:(2 * c + 1) * tq], o[(2 * c + 1) * tq:(2 * c + 2) * tq]).astype(BF16)


def _attention_call(kern, name, q, k, v, extras, q_lanes, n_groups, tq, latent_queries, dims):
    s_len, ctx_len, nbatch = dims["S"], dims["CTX"], dims["B"]
    tk = min(ATT_TK, s_len)
    q_len = s_len if latent_queries else ctx_len
    nq = q_len // tq
    q_blk0 = 0 if latent_queries else (nbatch * s_len) // tq
    ctx_blk0 = (nbatch * s_len) // ctx_len
    extra_specs = [pl.BlockSpec(e.shape, lambda b, g, qi: (0, 0)) for e in extras]
    return pl.pallas_call(
        functools.partial(kern, n_lat_chunks=s_len // tk if latent_queries else 0, tk=tk),
        grid=(nbatch, n_groups, nq),
        in_specs=[pl.BlockSpec((tq, q_lanes), lambda b, g, qi: (q_blk0 + b * nq + qi, g)),
                  pl.BlockSpec((1, s_len, V7X_LANES), lambda b, g, qi: (g, b, 0)),
                  pl.BlockSpec((1, s_len, V7X_LANES), lambda b, g, qi: (g, b, 0)),
                  pl.BlockSpec((1, ctx_len, V7X_LANES), lambda b, g, qi: (g, ctx_blk0 + b, 0)),
                  pl.BlockSpec((1, ctx_len, V7X_LANES), lambda b, g, qi: (g, ctx_blk0 + b, 0)),
                  *extra_specs],
        out_specs=pl.BlockSpec((tq, q_lanes), lambda b, g, qi: (b * nq + qi, g)),
        out_shape=jax.ShapeDtypeStruct((nbatch * q_len, n_groups * q_lanes), BF16),
        compiler_params=_cparams(("arbitrary", "arbitrary", "arbitrary")),
        name=name,
    )(q, k, v, k, v, *extras)


def _attention(qa, ka, va, qb, kbd, vbd, lamv, subln_w, lam_init, with_ctx_queries, dims):
    kern_a = functools.partial(_attn_a_kernel, lam_init=lam_init)
    gw = 2 * V7X_LANES

    def both(latent, tag):
        q_len = dims["S"] if latent else dims["CTX"]
        oa = _attention_call(kern_a, "attn_a" + tag, qa, ka, va, (lamv, subln_w), V7X_LANES,
                             A_HEADS, min(ATT_TQ, q_len), latent, dims)
        ob = _attention_call(_attn_b_kernel, "attn_b" + tag, qb, kbd, vbd, (), gw, B_KV,
                             min(ATT_TQ // 2, q_len), latent, dims)
        return oa, ob

    oa, ob = both(True, "")
    if with_ctx_queries:
        oa_c, ob_c = both(False, "_ctx")
        oa, ob = jnp.concatenate([oa, oa_c], axis=0), jnp.concatenate([ob, ob_c], axis=0)
    return oa, ob


def _merge_kernel(x_ref, mod_ref, oa_ref, ob_ref, u_ref, up_ref, un_ref, cw_ref, cb_ref,
                  lnw_ref, lnb_ref, wg_ref, wa_ref, wb_ref, wc_ref, wo_ref, o_ref, uext_ref, *,
                  nps, ncps, n_lat_blocks):
    bt = x_ref.shape[0]
    r = pl.program_id(0)
    is_lat = r < n_lat_blocks
    pos = jnp.where(is_lat, r % nps, (r - n_lat_blocks) % ncps)
    last_pos = jnp.where(is_lat, nps - 1, ncps - 1)
    keep_prev = (pos != 0).astype(F32)
    keep_next = (pos != last_pos).astype(F32)

    x = x_ref[...]
    mod = mod_ref[0]
    h = _modulated(x, mod, 0, 1).astype(BF16)

    uext_ref[0:HALO, :] = up_ref[...] * keep_prev
    uext_ref[HALO:HALO + bt, :] = u_ref[...]
    uext_ref[HALO + bt:2 * HALO + bt, :] = un_ref[...] * keep_next
    cw = cw_ref[...]
    y = jnp.zeros((bt, C_WIDTH), F32) + cb_ref[...]
    base = HALO - C_KW // 2
    for k in range(C_KW):
        y = y + cw[k:k + 1, :] * uext_ref[base + k:base + k + bt, :]
    mu = jnp.mean(y, axis=-1, keepdims=True)
    yc = y - mu
    var = jnp.mean(yc * yc, axis=-1, keepdims=True)
    yn = yc * lax.rsqrt(var + EPS) * lnw_ref[...] + lnb_ref[...]
    oc = (yn * jax.nn.sigmoid(yn)).astype(BF16)

    def gate(i):
        logits = jnp.dot(h, wg_ref[:, i * D_MODEL:(i + 1) * D_MODEL], preferred_element_type=F32)
        return jax.nn.sigmoid(logits)

    merged = gate(0) * jnp.dot(oa_ref[...], wa_ref[...], preferred_element_type=F32)
    merged = merged + gate(1) * jnp.dot(ob_ref[...], wb_ref[...], preferred_element_type=F32)
    merged = merged + gate(2) * jnp.dot(oc, wc_ref[...], preferred_element_type=F32)
    out = jnp.dot(merged.astype(BF16), wo_ref[...], preferred_element_type=F32)
    o_ref[...] = x + mod[2:3] * out


def _merge(xy, mod_l, oa, ob, u, cw, cb, lnw, lnb, wg, wa, wb, wc, wo, n_blocks, dims):
    bt = TOK_BLOCK
    t_rows, nps, nbatch = dims["T"], dims["S"] // bt, dims["B"]
    hpb = bt // HALO
    n_halo = t_rows // HALO
    row = lambda r: (r, 0)
    full = lambda r: (0, 0)
    return pl.pallas_call(
        functools.partial(_merge_kernel, nps=nps, ncps=dims["CTX"] // bt,
                          n_lat_blocks=nbatch * nps),
        grid=(n_blocks,),
        in_specs=[pl.BlockSpec((bt, D_MODEL), row),
                  pl.BlockSpec((1, 6, D_MODEL), lambda r: (jnp.minimum(r // nps, nbatch), 0, 0)),
                  pl.BlockSpec((bt, 4 * V7X_LANES), row),
                  pl.BlockSpec((bt, 4 * V7X_LANES), row),
                  pl.BlockSpec((bt, C_WIDTH), row),
                  pl.BlockSpec((HALO, C_WIDTH), lambda r: (jnp.maximum(r * hpb - 1, 0), 0)),
                  pl.BlockSpec((HALO, C_WIDTH), lambda r: (jnp.minimum((r + 1) * hpb, n_halo - 1), 0)),
                  pl.BlockSpec((C_KW + 1, C_WIDTH), full),
                  pl.BlockSpec((1, C_WIDTH), full),
                  pl.BlockSpec((1, C_WIDTH), full),
                  pl.BlockSpec((1, C_WIDTH), full),
                  pl.BlockSpec((D_MODEL, 3 * D_MODEL), full),
                  pl.BlockSpec((C_WIDTH, D_MODEL), full),
                  pl.BlockSpec((C_WIDTH, D_MODEL), full),
                  pl.BlockSpec((C_WIDTH, D_MODEL), full),
                  pl.BlockSpec((D_MODEL, D_MODEL), full)],
        out_specs=pl.BlockSpec((bt, D_MODEL), row),
        out_shape=jax.ShapeDtypeStruct((n_blocks * bt, D_MODEL), F32),
        scratch_shapes=[pltpu.VMEM((bt + 2 * HALO, C_WIDTH), F32)],
        compiler_params=_cparams(("arbitrary",)),
        name="merge",
    )(xy, mod_l, oa, ob, u, u, u, cw, cb, lnw, lnb, wg, wa, wb, wc, wo)


def _topk_desc(s, k):
    row = lax.broadcasted_iota(jnp.int32, (k, s.shape[1]), 0)

    def body(i, carry):
        prev, vals = carry
        m = jnp.max(jnp.where(s < prev, s, -jnp.inf), axis=0, keepdims=True)
        return m, jnp.where(row == i, m, vals)

    init = (jnp.full((1, s.shape[1]), jnp.inf, F32), jnp.zeros((k, s.shape[1]), F32))
    _, vals = lax.fori_loop(0, k, body, init)
    return vals


def _candidate_sums(sv1, sv2):
    half = P_TOPK // 2
    lead, rest = sv1[:half], sv1[half:]
    row = lax.broadcasted_iota(jnp.int32, lead.shape, 0)
    tiles = [lead + sv2[0:1], rest + sv2[0:1], lead + sv2[1:2]]
    for k2 in range(2, half):
        tiles.append(jnp.where(row < P_TOPK // (k2 + 1), lead + sv2[k2:k2 + 1], -jnp.inf))
    tiles.append(sv1[0:1] + sv2[half:])
    return jnp.concatenate(tiles, axis=0)


def _bf16_pair_words(x):
    bits = pltpu.bitcast(x.astype(BF16).astype(F32), jnp.int32)
    hi = lax.shift_right_logical(bits, 16)
    return hi | lax.shift_left(hi, 16)


def _peer_select_kernel(x_ref, mod_ref, wq_ref, k1_ref, k2_ref,
                        t_ref, need_ref, e1_ref, r2_ref, e2_ref, s1_scr):
    bt = x_ref.shape[0]
    n_lt = bt // V7X_LANES
    hb = _modulated(x_ref[...], mod_ref[0], 3, 4).astype(BF16)
    t_ref[...] = hb
    q = jnp.dot(hb, wq_ref[...], preferred_element_type=F32).astype(BF16)
    s1p = lax.dot_general(k1_ref[...], q, NT_DIMS, preferred_element_type=F32)
    for lt in range(n_lt):
        s1_scr[lt] = s1p[:, lt * V7X_LANES:(lt + 1) * V7X_LANES]
    head_row = lax.broadcasted_iota(jnp.int32, (P_HEADS, bt), 0)
    zero_t = jnp.zeros((P_HEADS, bt), F32)
    thr1, max1, tau_t = zero_t, zero_t, zero_t
    sv2_t = [zero_t] * P_TOPK
    for hh in range(P_HEADS):
        mine = head_row == hh
        qh = q[:, hh * V7X_LANES:(hh + 1) * V7X_LANES]
        s1 = jnp.concatenate([s1_scr[lt, pl.ds(hh, P_NKEYS, stride=P_HEADS), :]
                              for lt in range(n_lt)], axis=1)
        s2 = lax.dot_general(k2_ref[hh], qh, NT_DIMS, preferred_element_type=F32)
        sv1 = _topk_desc(s1, P_TOPK)
        sv2 = _topk_desc(s2, P_TOPK)
        top = _topk_desc(_candidate_sums(sv1, sv2), P_TOPK)
        z = jnp.sum(jnp.exp(top - top[0:1]), axis=0, keepdims=True)
        rank2 = jnp.zeros_like(s2)
        for k in range(P_TOPK):
            rank2 = rank2 + jnp.where(sv2[k:k + 1] > s2, 1.0, 0.0)
            sv2_t[k] = jnp.where(mine, sv2[k:k + 1], sv2_t[k])
        r2_ref[0, hh] = pltpu.bitcast(rank2.astype(BF16), jnp.int32)
        e2_ref[0, hh] = pltpu.bitcast((jnp.exp(s2 - sv2[0:1]) / z).astype(BF16), jnp.int32)
        tau_t = jnp.where(mine, top[P_TOPK - 1:P_TOPK], tau_t)
        thr1 = jnp.where(mine, sv1[P_TOPK - 1:P_TOPK], thr1)
        max1 = jnp.where(mine, sv1[0:1], max1)
    for lt in range(n_lt):
        ls = slice(lt * V7X_LANES, (lt + 1) * V7X_LANES)
        s1t = s1_scr[lt].reshape(P_NKEYS, P_HEADS, V7X_LANES)
        count = jnp.zeros_like(s1t)
        for k in range(P_TOPK):
            count = count + jnp.where(s1t + sv2_t[k][None, :, ls] >= tau_t[None, :, ls], 1.0, 0.0)
        need = jnp.where(s1t >= thr1[None, :, ls], count, 0.0)
        need_ref[0, lt] = _bf16_pair_words(need.reshape(P_NKEYS * P_HEADS, V7X_LANES))
        e1 = jnp.exp(s1t - max1[None, :, ls])
        e1_ref[0, lt] = _bf16_pair_words(e1.reshape(P_NKEYS * P_HEADS, V7X_LANES))


def _peer_dense_kernel(x_ref, mod_ref, t_ref, need_ref, e1_ref, r2_ref, e2_ref, u_ref, vt_ref,
                       fnw_ref, o_ref, a_scr, m_scr, acc_scr, *, n_chunks, final_norm):
    bt = x_ref.shape[0]
    c = pl.program_id(1)
    ipc = u_ref.shape[0] // P_NKEYS
    n_lt = bt // V7X_LANES

    @pl.when(c == 0)
    def _init():
        acc_scr[...] = jnp.zeros_like(acc_scr)
        a_scr[...] = jnp.zeros_like(a_scr)

    cprev = jnp.maximum(c - 1, 0)

    def one_key(ii, _):
        roff = ii * P_NKEYS
        hrow = pl.ds(pl.multiple_of((cprev * ipc + ii) * P_HEADS, P_HEADS), P_HEADS)
        for lt in range(n_lt):
            ls = slice(lt * V7X_LANES, (lt + 1) * V7X_LANES)
            w = [jnp.zeros((ROW_GROUP, V7X_LANES), BF16) for _ in range(P_NKEYS // ROW_GROUP)]
            need_t = need_ref[0, lt, hrow, :]
            e1_t = e1_ref[0, lt, hrow, :]
            zero = jnp.zeros((ROW_GROUP, V7X_LANES), BF16)
            for hh in range(P_HEADS):
                words = (ROW_GROUP // 2, V7X_LANES)
                need_b = pltpu.bitcast(jnp.broadcast_to(need_t[hh:hh + 1], words), BF16)
                e1_b = pltpu.bitcast(jnp.broadcast_to(e1_t[hh:hh + 1], words), BF16)
                for jg in range(P_NKEYS // ROW_GROUP):
                    js = slice(jg * (ROW_GROUP // 2), (jg + 1) * (ROW_GROUP // 2))
                    rank2 = pltpu.bitcast(r2_ref[0, hh, js, ls], BF16)
                    e2 = pltpu.bitcast(e2_ref[0, hh, js, ls], BF16)
                    w[jg] = w[jg] + jnp.where(rank2 < need_b, e1_b * e2, zero)
            for jg in range(P_NKEYS // ROW_GROUP):
                rows = pl.ds(roff + jg * ROW_GROUP, ROW_GROUP)
                a = a_scr[rows, ls]
                g = 0.5 * a * (1.0 + lax.erf(a * (2.0 ** -0.5)))
                m_scr[rows, ls] = g.astype(BF16) * w[jg]
        return 0

    for ii in range(ipc):
        one_key(ii, 0)

    acc_scr[...] += jnp.dot(vt_ref[0], m_scr[...], preferred_element_type=F32)
    a_scr[...] = lax.dot_general(u_ref[...], t_ref[...], NT_DIMS, preferred_element_type=F32)

    @pl.when(c == n_chunks)
    def _finish():
        x = x_ref[...]
        y = x + mod_ref[0][5:6] * acc_scr[...].T
        if final_norm:
            ms = jnp.mean(y * y, axis=-1, keepdims=True)
            y = y * lax.rsqrt(ms + EPS) * fnw_ref[...]
        o_ref[...] = y


def _peer(x1, mod_l, wq, k1p, k2p, u_tab, vt_tab, fnw, n_blocks, final_norm, dims):
    bt, ec = PEER_BLOCK, vt_tab.shape[2]
    nps, nbatch = dims["S"] // bt, dims["B"]
    n_chunks = vt_tab.shape[0]
    n_lt = bt // V7X_LANES
    rows_ih = P_NKEYS * P_HEADS
    mod_spec = pl.BlockSpec((1, 6, D_MODEL), lambda r, *_: (jnp.minimum(r // nps, nbatch), 0, 0))
    ih_shape = jax.ShapeDtypeStruct((n_blocks, n_lt, rows_ih, V7X_LANES), jnp.int32)
    hj_shape = jax.ShapeDtypeStruct((n_blocks, P_HEADS, P_NKEYS // 2, bt), jnp.int32)
    ih_spec = pl.BlockSpec((1, n_lt, rows_ih, V7X_LANES), lambda r, *_: (r, 0, 0, 0))
    hj_spec = pl.BlockSpec((1, P_HEADS, P_NKEYS // 2, bt), lambda r, *_: (r, 0, 0, 0))
    tok_spec = pl.BlockSpec((bt, D_MODEL), lambda r, *_: (r, 0))

    t_mod, need, e1, rank2, e2 = pl.pallas_call(
        _peer_select_kernel,
        grid=(n_blocks,),
        in_specs=[tok_spec, mod_spec,
                  pl.BlockSpec((D_MODEL, P_HEADS * V7X_LANES), lambda r: (0, 0)),
                  pl.BlockSpec((rows_ih, P_HEADS * V7X_LANES), lambda r: (0, 0)),
                  pl.BlockSpec((P_HEADS, P_NKEYS, V7X_LANES), lambda r: (0, 0, 0))],
        out_specs=(tok_spec, ih_spec, ih_spec, hj_spec, hj_spec),
        out_shape=(jax.ShapeDtypeStruct((n_blocks * bt, D_MODEL), BF16),
                   ih_shape,
                   ih_shape,
                   hj_shape,
                   hj_shape),
        scratch_shapes=[pltpu.VMEM((n_lt, rows_ih, V7X_LANES), F32)],
        compiler_params=_cparams(("arbitrary",)),
        name="peer_select",
    )(x1, mod_l, wq, k1p, k2p)

    return pl.pallas_call(
        functools.partial(_peer_dense_kernel, n_chunks=n_chunks, final_norm=final_norm),
        grid=(n_blocks, n_chunks + 1),
        in_specs=[tok_spec, mod_spec, tok_spec, ih_spec, ih_spec, hj_spec, hj_spec,
                  pl.BlockSpec((ec, D_MODEL), lambda r, c: (jnp.minimum(c, n_chunks - 1), 0)),
                  pl.BlockSpec((1, D_MODEL, ec), lambda r, c: (jnp.maximum(c - 1, 0), 0, 0)),
                  pl.BlockSpec((1, D_MODEL), lambda r, c: (0, 0))],
        out_specs=tok_spec,
        out_shape=jax.ShapeDtypeStruct((n_blocks * bt, D_MODEL), F32),
        scratch_shapes=[pltpu.VMEM((ec, bt), F32),
                        pltpu.VMEM((ec, bt), BF16),
                        pltpu.VMEM((D_MODEL, bt), F32)],
        compiler_params=_cparams(("arbitrary", "arbitrary")),
        name="peer_dense",
    )(x1, mod_l, t_mod, need, e1, rank2, e2, u_tab, vt_tab, fnw)


def _rope_tables(s_len, nbatch, n_ctx_rows):
    t = jnp.arange(s_len, dtype=jnp.int32)
    row = (t // GRID_W).astype(F32)
    col = (t % GRID_W).astype(F32)
    axis_dim = HEAD_DIM // 2
    inv = ROPE_THETA ** (-jnp.arange(0, axis_dim, 2, dtype=F32) / axis_dim)
    ar = row[:, None] * inv[None, :]
    ac = col[:, None] * inv[None, :]
    ang = jnp.concatenate([ar, ar, ac, ac], axis=-1)
    cos, sin = jnp.cos(ang), jnp.sin(ang)
    quarter = jnp.arange(HEAD_DIM) // (HEAD_DIM // 4)
    first = (quarter % 2 == 0)[None, :]
    sin_a = jnp.where(first, -sin, 0.0)
    sin_b = jnp.where(first, 0.0, sin)

    def expand(tab, ctx_fill):
        lat = jnp.tile(jnp.concatenate([tab, tab], axis=-1), (nbatch, 1))
        return jnp.concatenate([lat, jnp.full((n_ctx_rows, V7X_LANES), ctx_fill, F32)], axis=0)

    return expand(cos, 1.0), expand(sin_a, 0.0), expand(sin_b, 0.0)


def _padded_keys(keys_l):
    z = jnp.zeros_like(keys_l[:, 0])
    k0 = jnp.concatenate([keys_l[:, 0], z], axis=-1).transpose(1, 0, 2)
    eye = jnp.eye(P_HEADS, dtype=keys_l.dtype)
    k1p = (k0[:, :, None, :] * eye[None, :, :, None]).reshape(
        P_NKEYS * P_HEADS, P_HEADS * V7X_LANES)
    k2p = jnp.concatenate([z, keys_l[:, 1]], axis=-1)
    return k1p.astype(BF16), k2p.astype(BF16)


def _chunked_transpose(v_tab, ec):
    n_exp, d = v_tab.shape
    return v_tab.astype(BF16).reshape(n_exp // ec, ec, d).transpose(0, 2, 1)


def kernel(x, c, ctx, c_ctx, w_ada, b_ada, w_in, lam_q1, lam_k1, lam_q2, lam_k2, subln_w, q_norm_w, k_norm_w, conv_w, conv_b, conv_ln_w, conv_ln_b, w_branch_a, w_branch_b, w_branch_c, w_out, peer_wq, peer_keys, peer_u, peer_v, final_norm_w):
    nbatch, s_len, d = x.shape
    ctx_len = ctx.shape[1]
    depth = w_ada.shape[0]
    n_lat, n_ctx = nbatch * s_len, nbatch * ctx_len
    dims = {"B": nbatch, "S": s_len, "CTX": ctx_len, "T": n_lat + n_ctx}
    assert d == D_MODEL and nbatch + 1 <= MOD_ROWS
    assert s_len % PEER_BLOCK == 0 and s_len % ATT_TQ == 0 and ctx_len % TOK_BLOCK == 0
    assert ctx_len % min(ATT_TQ, ctx_len) == 0 and n_lat % min(ATT_TQ, ctx_len) == 0
    assert n_ctx % PEER_BLOCK == 0 and n_lat % ctx_len == 0

    cvec = jnp.zeros((MOD_ROWS, d), F32).at[:nbatch].set(c).at[nbatch].set(c_ctx)
    mod = _ada_rows(cvec, w_ada, b_ada).reshape(depth, MOD_ROWS, 6, d)
    tabs = _rope_tables(s_len, nbatch, n_ctx)
    tile2 = lambda v: jnp.concatenate([v, v], axis=-1)[None, :]

    xy = jnp.concatenate([x.reshape(n_lat, d), ctx.reshape(n_ctx, d)], axis=0)
    for l in range(depth):
        last = l == depth - 1
        lam_init = 0.8 - 0.6 * math.exp(-0.3 * l)
        w_bf = w_in[l].astype(BF16)
        lamv = jnp.stack([lam_q1[l], lam_k1[l], lam_q2[l], lam_k2[l]], axis=0)

        qa, ka, va, qb, kbd, vbd, u = _inproj(
            xy, mod[l], w_bf[:, :OFF_GATE], tabs, tile2(q_norm_w[l]), tile2(k_norm_w[l]), dims)
        oa, ob = _attention(qa, ka, va, qb, kbd, vbd, lamv, subln_w[l][None, :], lam_init,
                            not last, dims)
        n_rows = n_lat if last else n_lat + n_ctx
        cw = jnp.concatenate([conv_w[l], jnp.zeros((1, C_WIDTH), F32)], axis=0)
        x1 = _merge(xy, mod[l], oa, ob, u, cw, conv_b[l][None, :], conv_ln_w[l][None, :],
                    conv_ln_b[l][None, :], w_bf[:, OFF_GATE:], w_branch_a[l].astype(BF16),
                    w_branch_b[l].astype(BF16), w_branch_c[l].astype(BF16),
                    w_out[l].astype(BF16), n_rows // TOK_BLOCK, dims)
        k1p, k2p = _padded_keys(peer_keys[l])
        xy = _peer(x1, mod[l], peer_wq[l].astype(BF16), k1p, k2p,
                   peer_u[l].astype(BF16), _chunked_transpose(peer_v[l], PEER_ECHUNK),
                   final_norm_w[None, :],
                   n_rows // PEER_BLOCK, last, dims)
    return xy.reshape(nbatch, s_len, d)
```

```python
import functools
import math

import jax
import jax.numpy as jnp
from jax import lax
from jax.experimental import pallas as pl
from jax.experimental.pallas import tpu as pltpu

F32 = jnp.float32
BF16 = jnp.bfloat16

D_MODEL = 1024
DEPTH = 2
GRID_W = 64
HEAD_DIM = 64
ROPE_THETA = 10000.0
EPS = 1e-6
A_HEADS = 4
B_KV = 2
C_WIDTH = 512
C_KW = 31
P_HEADS = 8
P_NKEYS = 128
P_TOPK = 16

COL_KA, COL_VA, COL_KB, COL_VB, COL_QA, COL_QB, COL_GLU = 512, 512, 128, 128, 512, 512, 1024
OFF_KA = 0
OFF_VA = OFF_KA + COL_KA
OFF_KB = OFF_VA + COL_VA
OFF_QA = OFF_KB + COL_KB + COL_VB
OFF_QB = OFF_QA + COL_QA
OFF_GLU = OFF_QB + COL_QB
OFF_GATE = OFF_GLU + COL_GLU

V7X_LANES = 128
V7X_VMEM_LIMIT = 56 * 1024 * 1024
HALO = 16

TOK_BLOCK = 256
ATT_TQ = 512
ATT_TK = 2048
PEER_BLOCK = 512
PEER_ECHUNK = 1024
MOD_ROWS = 8
ROW_GROUP = 16

Q_SCALE = (HEAD_DIM ** -0.5) * math.log2(math.e)
NEG_BIG = -1e30
NT_DIMS = (((1,), (1,)), ((), ()))


def _cparams(sem):
    return pltpu.CompilerParams(dimension_semantics=sem, vmem_limit_bytes=V7X_VMEM_LIMIT)


def _modulated(x, mod, shift_row, scale_row):
    ms = jnp.mean(x * x, axis=-1, keepdims=True)
    xn = x * lax.rsqrt(ms + EPS)
    return xn * (1.0 + mod[scale_row:scale_row + 1]) + mod[shift_row:shift_row + 1]


def _ada_kernel(c_ref, w_ref, b_ref, o_ref):
    c = c_ref[...]
    sc = c * jax.nn.sigmoid(c)
    o_ref[0] = jnp.dot(sc, w_ref[0], preferred_element_type=F32,
                       precision=lax.Precision.HIGHEST) + b_ref[0]


def _ada_rows(cvec, w_ada, b_ada):
    depth, d, n = w_ada.shape
    tn = 1536
    return pl.pallas_call(
        _ada_kernel,
        grid=(depth, n // tn),
        in_specs=[pl.BlockSpec((MOD_ROWS, d), lambda l, j: (0, 0)),
                  pl.BlockSpec((1, d, tn), lambda l, j: (l, 0, j)),
                  pl.BlockSpec((1, 1, tn), lambda l, j: (l, 0, j))],
        out_specs=pl.BlockSpec((1, MOD_ROWS, tn), lambda l, j: (l, 0, j)),
        out_shape=jax.ShapeDtypeStruct((depth, MOD_ROWS, n), F32),
        compiler_params=_cparams(("arbitrary", "arbitrary")),
        name="ada_rows",
    )(cvec, w_ada, b_ada.reshape(depth, 1, n))


def _inproj_kernel(x_ref, mod_ref, w_ref, cos_ref, sa_ref, sb_ref, qnw_ref, knw_ref,
                   qa_ref, ka_ref, va_ref, qb_ref, kbd_ref, vbd_ref, u_ref):
    bt = x_ref.shape[0]
    h = _modulated(x_ref[...], mod_ref[0], 0, 1).astype(BF16)
    cos, sa, sb = cos_ref[...], sa_ref[...], sb_ref[...]
    lo = lax.broadcasted_iota(jnp.int32, (bt, V7X_LANES), 1) < HEAD_DIM

    def proj(c0, width):
        return jnp.dot(h, w_ref[:, c0:c0 + width], preferred_element_type=F32)

    def rope(x):
        return x * cos + pltpu.roll(x, V7X_LANES - 16, 1) * sa + pltpu.roll(x, 16, 1) * sb

    def headnorm(x, w):
        x2 = x * x
        s_lo = jnp.sum(jnp.where(lo, x2, 0.0), axis=-1, keepdims=True)
        s_hi = jnp.sum(jnp.where(lo, 0.0, x2), axis=-1, keepdims=True)
        r = jnp.where(lo, lax.rsqrt(s_lo / HEAD_DIM + EPS), lax.rsqrt(s_hi / HEAD_DIM + EPS))
        return x * r * w

    def tile(p, c):
        return p[:, c * V7X_LANES:(c + 1) * V7X_LANES]

    p = proj(OFF_KA, COL_KA)
    for c in range(A_HEADS):
        ka_ref[c] = rope(tile(p, c)).astype(BF16)

    p = proj(OFF_VA, COL_VA)
    for c in range(A_HEADS):
        va_ref[c] = tile(p, c).astype(BF16)

    p = proj(OFF_KB, COL_KB + COL_VB)
    kb = rope(headnorm(tile(p, 0), knw_ref[...]))
    kb_sw = pltpu.roll(kb, HEAD_DIM, 1)
    kbd_ref[0] = jnp.where(lo, kb, kb_sw).astype(BF16)
    kbd_ref[1] = jnp.where(lo, kb_sw, kb).astype(BF16)
    vb = tile(p, 1)
    vb_sw = pltpu.roll(vb, HEAD_DIM, 1)
    vbd_ref[0] = jnp.where(lo, vb, vb_sw).astype(BF16)
    vbd_ref[1] = jnp.where(lo, vb_sw, vb).astype(BF16)

    p = proj(OFF_QA, COL_QA)
    for c in range(COL_QA // V7X_LANES):
        qa_ref[:, c * V7X_LANES:(c + 1) * V7X_LANES] = (rope(tile(p, c)) * Q_SCALE).astype(BF16)

    p = proj(OFF_QB, COL_QB)
    qnw = qnw_ref[...]
    for c in range(COL_QB // V7X_LANES):
        qb_ref[:, c * V7X_LANES:(c + 1) * V7X_LANES] = (
            rope(headnorm(tile(p, c), qnw)) * Q_SCALE).astype(BF16)

    p = proj(OFF_GLU, COL_GLU)
    u_ref[...] = p[:, :C_WIDTH] * jax.nn.sigmoid(p[:, C_WIDTH:])


def _inproj(xy, mod_l, w_main, tabs, qnw, knw, dims):
    t_rows, nps, nbatch = dims["T"], dims["S"] // TOK_BLOCK, dims["B"]
    bt = TOK_BLOCK
    row = lambda r: (r, 0)
    full = lambda r: (0, 0)
    out_shapes = (
        jax.ShapeDtypeStruct((t_rows, COL_QA), BF16),
        jax.ShapeDtypeStruct((A_HEADS, t_rows, V7X_LANES), BF16),
        jax.ShapeDtypeStruct((A_HEADS, t_rows, V7X_LANES), BF16),
        jax.ShapeDtypeStruct((t_rows, COL_QB), BF16),
        jax.ShapeDtypeStruct((B_KV, t_rows, V7X_LANES), BF16),
        jax.ShapeDtypeStruct((B_KV, t_rows, V7X_LANES), BF16),
        jax.ShapeDtypeStruct((t_rows, C_WIDTH), F32),
    )
    grp = pl.BlockSpec((B_KV, bt, V7X_LANES), lambda r: (0, r, 0))
    heads = pl.BlockSpec((A_HEADS, bt, V7X_LANES), lambda r: (0, r, 0))
    return pl.pallas_call(
        _inproj_kernel,
        grid=(t_rows // bt,),
        in_specs=[pl.BlockSpec((bt, D_MODEL), row),
                  pl.BlockSpec((1, 6, D_MODEL), lambda r: (jnp.minimum(r // nps, nbatch), 0, 0)),
                  pl.BlockSpec((D_MODEL, OFF_GATE), full),
                  pl.BlockSpec((bt, V7X_LANES), row),
                  pl.BlockSpec((bt, V7X_LANES), row),
                  pl.BlockSpec((bt, V7X_LANES), row),
                  pl.BlockSpec((1, V7X_LANES), full),
                  pl.BlockSpec((1, V7X_LANES), full)],
        out_specs=(pl.BlockSpec((bt, COL_QA), row), heads, heads, pl.BlockSpec((bt, COL_QB), row),
                   grp, grp, pl.BlockSpec((bt, C_WIDTH), row)),
        out_shape=out_shapes,
        compiler_params=_cparams(("arbitrary",)),
        name="inproj",
    )(xy, mod_l, w_main, tabs[0], tabs[1], tabs[2], qnw, knw)


def _softmax_step(s, v, m, l, acc):
    mn = jnp.maximum(m, jnp.max(s, axis=-1, keepdims=True))
    p = jnp.exp2(s - mn)
    alpha = jnp.exp2(m - mn)
    l = alpha * l + jnp.sum(p, axis=-1, keepdims=True)
    acc = alpha * acc + jnp.dot(p.astype(BF16), v, preferred_element_type=F32)
    return mn, l, acc


def _flash_rows(q, load_lat, kc, vc, *, n_lat_chunks, tk):
    rows = q.shape[0]

    def step(k, v, carry):
        m, l, a = carry
        s = lax.dot_general(q, k, NT_DIMS, preferred_element_type=F32)
        return _softmax_step(s, v, m, l, a)

    carry = (jnp.full((rows, 1), NEG_BIG, F32), jnp.zeros((rows, 1), F32),
             jnp.zeros((rows, V7X_LANES), F32))
    for j in range(n_lat_chunks):
        carry = step(*load_lat(j * tk), carry)
    _, l, a = step(kc, vc, carry)
    return a / l


def _attn_a_kernel(q_ref, kl_ref, vl_ref, kc_ref, vc_ref, lamv_ref, sw_ref, o_ref, *,
                   n_lat_chunks, tk, lam_init):
    tq = q_ref.shape[0]
    q = q_ref[...]
    lo = lax.broadcasted_iota(jnp.int32, (tq, V7X_LANES), 1) < HEAD_DIM
    zero = jnp.zeros_like(q)
    q12 = jnp.concatenate([jnp.where(lo, q, zero), jnp.where(lo, zero, q)], axis=0)
    o12 = _flash_rows(q12, lambda off: (kl_ref[0, pl.ds(off, tk), :], vl_ref[0, pl.ds(off, tk), :]),
                      kc_ref[0], vc_ref[0], n_lat_chunks=n_lat_chunks, tk=tk)

    lv = lamv_ref[...]
    lam = (jnp.exp(jnp.sum(lv[0:1] * lv[1:2], axis=-1, keepdims=True))
           - jnp.exp(jnp.sum(lv[2:3] * lv[3:4], axis=-1, keepdims=True)) + lam_init)
    o = o12[:tq] - lam * o12[tq:]
    ms = jnp.mean(o * o, axis=-1, keepdims=True)
    o_ref[...] = (o * lax.rsqrt(ms + EPS) * sw_ref[...] * (1.0 - lam_init)).astype(BF16)


def _attn_b_kernel(q_ref, kl_ref, vl_ref, kc_ref, vc_ref, o_ref, *, n_lat_chunks, tk):
    tq = q_ref.shape[0]
    lo = lax.broadcasted_iota(jnp.int32, (tq, V7X_LANES), 1) < HEAD_DIM
    parts = []
    for c in range(2):
        qc = q_ref[:, c * V7X_LANES:(c + 1) * V7X_LANES]
        zero = jnp.zeros_like(qc)
        parts += [jnp.where(lo, qc, zero), jnp.where(lo, zero, qc)]
    q4 = jnp.concatenate(parts, axis=0)
    o = _flash_rows(q4, lambda off: (kl_ref[0, pl.ds(off, tk), :], vl_ref[0, pl.ds(off, tk), :]),
                    kc_ref[0], vc_ref[0], n_lat_chunks=n_lat_chunks, tk=tk)
    for c in range(2):
        o_ref[:, c * V7X_LANES:(c + 1) * V7X_LANES] = jnp.where(
            lo, o[(2 * c) * tq:(2 * c + 1) * tq], o[(2 * c + 1) * tq:(2 * c + 2) * tq]).astype(BF16)


def _attention_call(kern, name, q, k, v, extras, q_lanes, n_groups, tq, latent_queries, dims):
    s_len, ctx_len, nbatch = dims["S"], dims["CTX"], dims["B"]
    tk = min(ATT_TK, s_len)
    q_len = s_len if latent_queries else ctx_len
    nq = q_len // tq
    q_blk0 = 0 if latent_queries else (nbatch * s_len) // tq
    ctx_blk0 = (nbatch * s_len) // ctx_len
    extra_specs = [pl.BlockSpec(e.shape, lambda b, g, qi: (0, 0)) for e in extras]
    return pl.pallas_call(
        functools.partial(kern, n_lat_chunks=s_len // tk if latent_queries else 0, tk=tk),
        grid=(nbatch, n_groups, nq),
        in_specs=[pl.BlockSpec((tq, q_lanes), lambda b, g, qi: (q_blk0 + b * nq + qi, g)),
                  pl.BlockSpec((1, s_len, V7X_LANES), lambda b, g, qi: (g, b, 0)),
                  pl.BlockSpec((1, s_len, V7X_LANES), lambda b, g, qi: (g, b, 0)),
                  pl.BlockSpec((1, ctx_len, V7X_LANES), lambda b, g, qi: (g, ctx_blk0 + b, 0)),
                  pl.BlockSpec((1, ctx_len, V7X_LANES), lambda b, g, qi: (g, ctx_blk0 + b, 0)),
                  *extra_specs],
        out_specs=pl.BlockSpec((tq, q_lanes), lambda b, g, qi: (b * nq + qi, g)),
        out_shape=jax.ShapeDtypeStruct((nbatch * q_len, n_groups * q_lanes), BF16),
        compiler_params=_cparams(("arbitrary", "arbitrary", "arbitrary")),
        name=name,
    )(q, k, v, k, v, *extras)


def _attention(qa, ka, va, qb, kbd, vbd, lamv, subln_w, lam_init, with_ctx_queries, dims):
    kern_a = functools.partial(_attn_a_kernel, lam_init=lam_init)
    gw = 2 * V7X_LANES

    def both(latent, tag):
        q_len = dims["S"] if latent else dims["CTX"]
        oa = _attention_call(kern_a, "attn_a" + tag, qa, ka, va, (lamv, subln_w), V7X_LANES,
                             A_HEADS, min(ATT_TQ, q_len), latent, dims)
        ob = _attention_call(_attn_b_kernel, "attn_b" + tag, qb, kbd, vbd, (), gw, B_KV,
                             min(ATT_TQ // 2, q_len), latent, dims)
        return oa, ob

    oa, ob = both(True, "")
    if with_ctx_queries:
        oa_c, ob_c = both(False, "_ctx")
        oa, ob = jnp.concatenate([oa, oa_c], axis=0), jnp.concatenate([ob, ob_c], axis=0)
    return oa, ob


def _merge_kernel(x_ref, mod_ref, oa_ref, ob_ref, u_ref, up_ref, un_ref, cw_ref, cb_ref,
                  lnw_ref, lnb_ref, wg_ref, wa_ref, wb_ref, wc_ref, wo_ref, o_ref, uext_ref, *,
                  nps, ncps, n_lat_blocks):
    bt = x_ref.shape[0]
    r = pl.program_id(0)
    is_lat = r < n_lat_blocks
    pos = jnp.where(is_lat, r % nps, (r - n_lat_blocks) % ncps)
    last_pos = jnp.where(is_lat, nps - 1, ncps - 1)
    keep_prev = (pos != 0).astype(F32)
    keep_next = (pos != last_pos).astype(F32)

    x = x_ref[...]
    mod = mod_ref[0]
    h = _modulated(x, mod, 0, 1).astype(BF16)

    uext_ref[0:HALO, :] = up_ref[...] * keep_prev
    uext_ref[HALO:HALO + bt, :] = u_ref[...]
    uext_ref[HALO + bt:2 * HALO + bt, :] = un_ref[...] * keep_next
    cw = cw_ref[...]
    y = jnp.zeros((bt, C_WIDTH), F32) + cb_ref[...]
    base = HALO - C_KW // 2
    for k in range(C_KW):
        y = y + cw[k:k + 1, :] * uext_ref[base + k:base + k + bt, :]
    mu = jnp.mean(y, axis=-1, keepdims=True)
    yc = y - mu
    var = jnp.mean(yc * yc, axis=-1, keepdims=True)
    yn = yc * lax.rsqrt(var + EPS) * lnw_ref[...] + lnb_ref[...]
    oc = (yn * jax.nn.sigmoid(yn)).astype(BF16)

    def gate(i):
        logits = jnp.dot(h, wg_ref[:, i * D_MODEL:(i + 1) * D_MODEL], preferred_element_type=F32)
        return jax.nn.sigmoid(logits)

    merged = gate(0) * jnp.dot(oa_ref[...], wa_ref[...], preferred_element_type=F32)
    merged = merged + gate(1) * jnp.dot(ob_ref[...], wb_ref[...], preferred_element_type=F32)
    merged = merged + gate(2) * jnp.dot(oc, wc_ref[...], preferred_element_type=F32)
    out = jnp.dot(merged.astype(BF16), wo_ref[...], preferred_element_type=F32)
    o_ref[...] = x + mod[2:3] * out


def _merge(xy, mod_l, oa, ob, u, cw, cb, lnw, lnb, wg, wa, wb, wc, wo, n_blocks, dims):
    bt = TOK_BLOCK
    t_rows, nps, nbatch = dims["T"], dims["S"] // bt, dims["B"]
    hpb = bt // HALO
    n_halo = t_rows // HALO
    row = lambda r: (r, 0)
    full = lambda r: (0, 0)
    return pl.pallas_call(
        functools.partial(_merge_kernel, nps=nps, ncps=dims["CTX"] // bt,
                          n_lat_blocks=nbatch * nps),
        grid=(n_blocks,),
        in_specs=[pl.BlockSpec((bt, D_MODEL), row),
                  pl.BlockSpec((1, 6, D_MODEL), lambda r: (jnp.minimum(r // nps, nbatch), 0, 0)),
                  pl.BlockSpec((bt, 4 * V7X_LANES), row),
                  pl.BlockSpec((bt, 4 * V7X_LANES), row),
                  pl.BlockSpec((bt, C_WIDTH), row),
                  pl.BlockSpec((HALO, C_WIDTH), lambda r: (jnp.maximum(r * hpb - 1, 0), 0)),
                  pl.BlockSpec((HALO, C_WIDTH), lambda r: (jnp.minimum((r + 1) * hpb, n_halo - 1), 0)),
                  pl.BlockSpec((C_KW + 1, C_WIDTH), full),
                  pl.BlockSpec((1, C_WIDTH), full),
                  pl.BlockSpec((1, C_WIDTH), full),
                  pl.BlockSpec((1, C_WIDTH), full),
                  pl.BlockSpec((D_MODEL, 3 * D_MODEL), full),
                  pl.BlockSpec((C_WIDTH, D_MODEL), full),
                  pl.BlockSpec((C_WIDTH, D_MODEL), full),
                  pl.BlockSpec((C_WIDTH, D_MODEL), full),
                  pl.BlockSpec((D_MODEL, D_MODEL), full)],
        out_specs=pl.BlockSpec((bt, D_MODEL), row),
        out_shape=jax.ShapeDtypeStruct((n_blocks * bt, D_MODEL), F32),
        scratch_shapes=[pltpu.VMEM((bt + 2 * HALO, C_WIDTH), F32)],
        compiler_params=_cparams(("arbitrary",)),
        name="merge",
    )(xy, mod_l, oa, ob, u, u, u, cw, cb, lnw, lnb, wg, wa, wb, wc, wo)


def _topk_desc(s, k):
    row = lax.broadcasted_iota(jnp.int32, (k, s.shape[1]), 0)

    def body(i, carry):
        prev, vals = carry
        m = jnp.max(jnp.where(s < prev, s, -jnp.inf), axis=0, keepdims=True)
        return m, jnp.where(row == i, m, vals)

    init = (jnp.full((1, s.shape[1]), jnp.inf, F32), jnp.zeros((k, s.shape[1]), F32))
    _, vals = lax.fori_loop(0, k, body, init)
    return vals


def _candidate_sums(sv1, sv2):
    half = P_TOPK // 2
    lead, rest = sv1[:half], sv1[half:]
    row = lax.broadcasted_iota(jnp.int32, lead.shape, 0)
    tiles = [lead + sv2[0:1], rest + sv2[0:1], lead + sv2[1:2]]
    for k2 in range(2, half):
        tiles.append(jnp.where(row < P_TOPK // (k2 + 1), lead + sv2[k2:k2 + 1], -jnp.inf))
    tiles.append(sv1[0:1] + sv2[half:])
    return jnp.concatenate(tiles, axis=0)


def _bf16_pair_words(x):
    bits = pltpu.bitcast(x.astype(BF16).astype(F32), jnp.int32)
    hi = lax.shift_right_logical(bits, 16)
    return hi | lax.shift_left(hi, 16)


def _peer_select_kernel(x_ref, mod_ref, wq_ref, k1_ref, k2_ref,
                        t_ref, need_ref, e1_ref, r2_ref, e2_ref, s1_scr):
    bt = x_ref.shape[0]
    n_lt = bt // V7X_LANES
    hb = _modulated(x_ref[...], mod_ref[0], 3, 4).astype(BF16)
    t_ref[...] = hb
    q = jnp.dot(hb, wq_ref[...], preferred_element_type=F32).astype(BF16)
    s1p = lax.dot_general(k1_ref[...], q, NT_DIMS, preferred_element_type=F32)
    for lt in range(n_lt):
        s1_scr[lt] = s1p[:, lt * V7X_LANES:(lt + 1) * V7X_LANES]
    head_row = lax.broadcasted_iota(jnp.int32, (P_HEADS, bt), 0)
    zero_t = jnp.zeros((P_HEADS, bt), F32)
    thr1, max1, tau_t = zero_t, zero_t, zero_t
    sv2_t = [zero_t] * P_TOPK
    for hh in range(P_HEADS):
        mine = head_row == hh
        qh = q[:, hh * V7X_LANES:(hh + 1) * V7X_LANES]
        s1 = jnp.concatenate([s1_scr[lt, pl.ds(hh, P_NKEYS, stride=P_HEADS), :]
                              for lt in range(n_lt)], axis=1)
        s2 = lax.dot_general(k2_ref[hh], qh, NT_DIMS, preferred_element_type=F32)
        sv1 = _topk_desc(s1, P_TOPK)
        sv2 = _topk_desc(s2, P_TOPK)
        top = _topk_desc(_candidate_sums(sv1, sv2), P_TOPK)
        z = jnp.sum(jnp.exp(top - top[0:1]), axis=0, keepdims=True)
        rank2 = jnp.zeros_like(s2)
        for k in range(P_TOPK):
            rank2 = rank2 + jnp.where(sv2[k:k + 1] > s2, 1.0, 0.0)
            sv2_t[k] = jnp.where(mine, sv2[k:k + 1], sv2_t[k])
        r2_ref[0, hh] = pltpu.bitcast(rank2.astype(BF16), jnp.int32)
        e2_ref[0, hh] = pltpu.bitcast((jnp.exp(s2 - sv2[0:1]) / z).astype(BF16), jnp.int32)
        tau_t = jnp.where(mine, top[P_TOPK - 1:P_TOPK], tau_t)
        thr1 = jnp.where(mine, sv1[P_TOPK - 1:P_TOPK], thr1)
        max1 = jnp.where(mine, sv1[0:1], max1)
    for lt in range(n_lt):
        ls = slice(lt * V7X_LANES, (lt + 1) * V7X_LANES)
        s1t = s1_scr[lt].reshape(P_NKEYS, P_HEADS, V7X_LANES)
        count = jnp.zeros_like(s1t)
        for k in range(P_TOPK):
            count = count + jnp.where(s1t + sv2_t[k][None, :, ls] >= tau_t[None, :, ls], 1.0, 0.0)
        need = jnp.where(s1t >= thr1[None, :, ls], count, 0.0)
        need_ref[0, lt] = _bf16_pair_words(need.reshape(P_NKEYS * P_HEADS, V7X_LANES))
        e1 = jnp.exp(s1t - max1[None, :, ls])
        e1_ref[0, lt] = _bf16_pair_words(e1.reshape(P_NKEYS * P_HEADS, V7X_LANES))


def _peer_dense_kernel(x_ref, mod_ref, t_ref, need_ref, e1_ref, r2_ref, e2_ref, u_ref, vt_ref,
                       fnw_ref, o_ref, a_scr, m_scr, acc_scr, *, n_chunks, final_norm):
    bt = x_ref.shape[0]
    c = pl.program_id(1)
    ipc = u_ref.shape[0] // P_NKEYS
    n_lt = bt // V7X_LANES

    @pl.when(c == 0)
    def _init():
        acc_scr[...] = jnp.zeros_like(acc_scr)
        a_scr[...] = jnp.zeros_like(a_scr)

    cprev = jnp.maximum(c - 1, 0)

    def one_key(ii, _):
        roff = ii * P_NKEYS
        hrow = pl.ds(pl.multiple_of((cprev * ipc + ii) * P_HEADS, P_HEADS), P_HEADS)
        for lt in range(n_lt):
            ls = slice(lt * V7X_LANES, (lt + 1) * V7X_LANES)
            w = [jnp.zeros((ROW_GROUP, V7X_LANES), BF16) for _ in range(P_NKEYS // ROW_GROUP)]
            need_t = need_ref[0, lt, hrow, :]
            e1_t = e1_ref[0, lt, hrow, :]
            zero = jnp.zeros((ROW_GROUP, V7X_LANES), BF16)
            for hh in range(P_HEADS):
                words = (ROW_GROUP // 2, V7X_LANES)
                need_b = pltpu.bitcast(jnp.broadcast_to(need_t[hh:hh + 1], words), BF16)
                e1_b = pltpu.bitcast(jnp.broadcast_to(e1_t[hh:hh + 1], words), BF16)
                for jg in range(P_NKEYS // ROW_GROUP):
                    js = slice(jg * (ROW_GROUP // 2), (jg + 1) * (ROW_GROUP // 2))
                    rank2 = pltpu.bitcast(r2_ref[0, hh, js, ls], BF16)
                    e2 = pltpu.bitcast(e2_ref[0, hh, js, ls], BF16)
                    w[jg] = w[jg] + jnp.where(rank2 < need_b, e1_b * e2, zero)
            for jg in range(P_NKEYS // ROW_GROUP):
                rows = pl.ds(roff + jg * ROW_GROUP, ROW_GROUP)
                a = a_scr[rows, ls]
                g = 0.5 * a * (1.0 + lax.erf(a * (2.0 ** -0.5)))
                m_scr[rows, ls] = g.astype(BF16) * w[jg]
        return 0

    for ii in range(ipc):
        one_key(ii, 0)

    acc_scr[...] += jnp.dot(vt_ref[0], m_scr[...], preferred_element_type=F32)
    a_scr[...] = lax.dot_general(u_ref[...], t_ref[...], NT_DIMS, preferred_element_type=F32)

    @pl.when(c == n_chunks)
    def _finish():
        x = x_ref[...]
        y = x + mod_ref[0][5:6] * acc_scr[...].T
        if final_norm:
            ms = jnp.mean(y * y, axis=-1, keepdims=True)
            y = y * lax.rsqrt(ms + EPS) * fnw_ref[...]
        o_ref[...] = y


def _peer(x1, mod_l, wq, k1p, k2p, u_tab, vt_tab, fnw, n_blocks, final_norm, dims):
    bt, ec = PEER_BLOCK, vt_tab.shape[2]
    nps, nbatch = dims["S"] // bt, dims["B"]
    n_chunks = vt_tab.shape[0]
    n_lt = bt // V7X_LANES
    rows_ih = P_NKEYS * P_HEADS
    mod_spec = pl.BlockSpec((1, 6, D_MODEL), lambda r, *_: (jnp.minimum(r // nps, nbatch), 0, 0))
    ih_shape = jax.ShapeDtypeStruct((n_blocks, n_lt, rows_ih, V7X_LANES), jnp.int32)
    hj_shape = jax.ShapeDtypeStruct((n_blocks, P_HEADS, P_NKEYS // 2, bt), jnp.int32)
    ih_spec = pl.BlockSpec((1, n_lt, rows_ih, V7X_LANES), lambda r, *_: (r, 0, 0, 0))
    hj_spec = pl.BlockSpec((1, P_HEADS, P_NKEYS // 2, bt), lambda r, *_: (r, 0, 0, 0))
    tok_spec = pl.BlockSpec((bt, D_MODEL), lambda r, *_: (r, 0))

    t_mod, need, e1, rank2, e2 = pl.pallas_call(
        _peer_select_kernel,
        grid=(n_blocks,),
        in_specs=[tok_spec, mod_spec,
                  pl.BlockSpec((D_MODEL, P_HEADS * V7X_LANES), lambda r: (0, 0)),
                  pl.BlockSpec((rows_ih, P_HEADS * V7X_LANES), lambda r: (0, 0)),
                  pl.BlockSpec((P_HEADS, P_NKEYS, V7X_LANES), lambda r: (0, 0, 0))],
        out_specs=(tok_spec, ih_spec, ih_spec, hj_spec, hj_spec),
        out_shape=(jax.ShapeDtypeStruct((n_blocks * bt, D_MODEL), BF16),
                   ih_shape,
                   ih_shape,
                   hj_shape,
                   hj_shape),
        scratch_shapes=[pltpu.VMEM((n_lt, rows_ih, V7X_LANES), F32)],
        compiler_params=_cparams(("arbitrary",)),
        name="peer_select",
    )(x1, mod_l, wq, k1p, k2p)

    return pl.pallas_call(
        functools.partial(_peer_dense_kernel, n_chunks=n_chunks, final_norm=final_norm),
        grid=(n_blocks, n_chunks + 1),
        in_specs=[tok_spec, mod_spec, tok_spec, ih_spec, ih_spec, hj_spec, hj_spec,
                  pl.BlockSpec((ec, D_MODEL), lambda r, c: (jnp.minimum(c, n_chunks - 1), 0)),
                  pl.BlockSpec((1, D_MODEL, ec), lambda r, c: (jnp.maximum(c - 1, 0), 0, 0)),
                  pl.BlockSpec((1, D_MODEL), lambda r, c: (0, 0))],
        out_specs=tok_spec,
        out_shape=jax.ShapeDtypeStruct((n_blocks * bt, D_MODEL), F32),
        scratch_shapes=[pltpu.VMEM((ec, bt), F32),
                        pltpu.VMEM((ec, bt), BF16),
                        pltpu.VMEM((D_MODEL, bt), F32)],
        compiler_params=_cparams(("arbitrary", "arbitrary")),
        name="peer_dense",
    )(x1, mod_l, t_mod, need, e1, rank2, e2, u_tab, vt_tab, fnw)


def _rope_tables(s_len, nbatch, n_ctx_rows):
    t = jnp.arange(s_len, dtype=jnp.int32)
    row = (t // GRID_W).astype(F32)
    col = (t % GRID_W).astype(F32)
    axis_dim = HEAD_DIM // 2
    inv = ROPE_THETA ** (-jnp.arange(0, axis_dim, 2, dtype=F32) / axis_dim)
    ar = row[:, None] * inv[None, :]
    ac = col[:, None] * inv[None, :]
    ang = jnp.concatenate([ar, ar, ac, ac], axis=-1)
    cos, sin = jnp.cos(ang), jnp.sin(ang)
    quarter = jnp.arange(HEAD_DIM) // (HEAD_DIM // 4)
    first = (quarter % 2 == 0)[None, :]
    sin_a = jnp.where(first, -sin, 0.0)
    sin_b = jnp.where(first, 0.0, sin)

    def expand(tab, ctx_fill):
        lat = jnp.tile(jnp.concatenate([tab, tab], axis=-1), (nbatch, 1))
        return jnp.concatenate([lat, jnp.full((n_ctx_rows, V7X_LANES), ctx_fill, F32)], axis=0)

    return expand(cos, 1.0), expand(sin_a, 0.0), expand(sin_b, 0.0)


def _padded_keys(keys_l):
    z = jnp.zeros_like(keys_l[:, 0])
    k0 = jnp.concatenate([keys_l[:, 0], z], axis=-1).transpose(1, 0, 2)
    eye = jnp.eye(P_HEADS, dtype=keys_l.dtype)
    k1p = (k0[:, :, None, :] * eye[None, :, :, None]).reshape(
        P_NKEYS * P_HEADS, P_HEADS * V7X_LANES)
    k2p = jnp.concatenate([z, keys_l[:, 1]], axis=-1)
    return k1p.astype(BF16), k2p.astype(BF16)


def _chunked_transpose(v_tab, ec):
    n_exp, d = v_tab.shape
    return v_tab.astype(BF16).reshape(n_exp // ec, ec, d).transpose(0, 2, 1)


def kernel(x, c, ctx, c_ctx, w_ada, b_ada, w_in, lam_q1, lam_k1, lam_q2, lam_k2, subln_w, q_norm_w, k_norm_w, conv_w, conv_b, conv_ln_w, conv_ln_b, w_branch_a, w_branch_b, w_branch_c, w_out, peer_wq, peer_keys, peer_u, peer_v, final_norm_w):
    nbatch, s_len, d = x.shape
    ctx_len = ctx.shape[1]
    depth = w_ada.shape[0]
    n_lat, n_ctx = nbatch * s_len, nbatch * ctx_len
    dims = {"B": nbatch, "S": s_len, "CTX": ctx_len, "T": n_lat + n_ctx}
    assert d == D_MODEL and nbatch + 1 <= MOD_ROWS
    assert s_len % PEER_BLOCK == 0 and s_len % ATT_TQ == 0 and ctx_len % TOK_BLOCK == 0
    assert ctx_len % min(ATT_TQ, ctx_len) == 0 and n_lat % min(ATT_TQ, ctx_len) == 0
    assert n_ctx % PEER_BLOCK == 0 and n_lat % ctx_len == 0

    cvec = jnp.zeros((MOD_ROWS, d), F32).at[:nbatch].set(c).at[nbatch].set(c_ctx)
    mod = _ada_rows(cvec, w_ada, b_ada).reshape(depth, MOD_ROWS, 6, d)
    tabs = _rope_tables(s_len, nbatch, n_ctx)
    tile2 = lambda v: jnp.concatenate([v, v], axis=-1)[None, :]

    xy = jnp.concatenate([x.reshape(n_lat, d), ctx.reshape(n_ctx, d)], axis=0)
    for l in range(depth):
        last = l == depth - 1
        lam_init = 0.8 - 0.6 * math.exp(-0.3 * l)
        w_bf = w_in[l].astype(BF16)
        lamv = jnp.stack([lam_q1[l], lam_k1[l], lam_q2[l], lam_k2[l]], axis=0)

        qa, ka, va, qb, kbd, vbd, u = _inproj(
            xy, mod[l], w_bf[:, :OFF_GATE], tabs, tile2(q_norm_w[l]), tile2(k_norm_w[l]), dims)
        oa, ob = _attention(qa, ka, va, qb, kbd, vbd, lamv, subln_w[l][None, :], lam_init,
                            not last, dims)
        n_rows = n_lat if last else n_lat + n_ctx
        cw = jnp.concatenate([conv_w[l], jnp.zeros((1, C_WIDTH), F32)], axis=0)
        x1 = _merge(xy, mod[l], oa, ob, u, cw, conv_b[l][None, :], conv_ln_w[l][None, :],
                    conv_ln_b[l][None, :], w_bf[:, OFF_GATE:], w_branch_a[l].astype(BF16),
                    w_branch_b[l].astype(BF16), w_branch_c[l].astype(BF16),
                    w_out[l].astype(BF16), n_rows // TOK_BLOCK, dims)
        k1p, k2p = _padded_keys(peer_keys[l])
        xy = _peer(x1, mod[l], peer_wq[l].astype(BF16), k1p, k2p,
                   peer_u[l].astype(BF16), _chunked_transpose(peer_v[l], PEER_ECHUNK),
                   final_norm_w[None, :],
                   n_rows // PEER_BLOCK, last, dims)
    return xy.reshape(nbatch, s_len, d)
```

```python
import functools
import math

import jax
import jax.numpy as jnp
from jax import lax
from jax.experimental import pallas as pl
from jax.experimental.pallas import tpu as pltpu

F32 = jnp.float32
BF16 = jnp.bfloat16

D_MODEL = 1024
DEPTH = 2
GRID_W = 64
HEAD_DIM = 64
ROPE_THETA = 10000.0
EPS = 1e-6
A_HEADS = 4
B_KV = 2
C_WIDTH = 512
C_KW = 31
P_HEADS = 8
P_NKEYS = 128
P_TOPK = 16

COL_KA, COL_VA, COL_KB, COL_VB, COL_QA, COL_QB, COL_GLU = 512, 512, 128, 128, 512, 512, 1024
OFF_KA = 0
OFF_VA = OFF_KA + COL_KA
OFF_KB = OFF_VA + COL_VA
OFF_QA = OFF_KB + COL_KB + COL_VB
OFF_QB = OFF_QA + COL_QA
OFF_GLU = OFF_QB + COL_QB
OFF_GATE = OFF_GLU + COL_GLU

V7X_LANES = 128
V7X_VMEM_LIMIT = 56 * 1024 * 1024
HALO = 16

TOK_BLOCK = 256
ATT_TQ = 512
ATT_TK = 2048
PEER_BLOCK = 512
PEER_ECHUNK = 2048
MOD_ROWS = 8
ROW_GROUP = 16

Q_SCALE = (HEAD_DIM ** -0.5) * math.log2(math.e)
NEG_BIG = -1e30
NT_DIMS = (((1,), (1,)), ((), ()))


def _cparams(sem):
    return pltpu.CompilerParams(dimension_semantics=sem, vmem_limit_bytes=V7X_VMEM_LIMIT)


def _modulated(x, mod, shift_row, scale_row):
    ms = jnp.mean(x * x, axis=-1, keepdims=True)
    xn = x * lax.rsqrt(ms + EPS)
    return xn * (1.0 + mod[scale_row:scale_row + 1]) + mod[shift_row:shift_row + 1]


def _ada_kernel(c_ref, w_ref, b_ref, o_ref):
    c = c_ref[...]
    sc = c * jax.nn.sigmoid(c)
    o_ref[0] = jnp.dot(sc, w_ref[0], preferred_element_type=F32,
                       precision=lax.Precision.HIGHEST) + b_ref[0]


def _ada_rows(cvec, w_ada, b_ada):
    depth, d, n = w_ada.shape
    tn = 1536
    return pl.pallas_call(
        _ada_kernel,
        grid=(depth, n // tn),
        in_specs=[pl.BlockSpec((MOD_ROWS, d), lambda l, j: (0, 0)),
                  pl.BlockSpec((1, d, tn), lambda l, j: (l, 0, j)),
                  pl.BlockSpec((1, 1, tn), lambda l, j: (l, 0, j))],
        out_specs=pl.BlockSpec((1, MOD_ROWS, tn), lambda l, j: (l, 0, j)),
        out_shape=jax.ShapeDtypeStruct((depth, MOD_ROWS, n), F32),
        compiler_params=_cparams(("arbitrary", "arbitrary")),
        name="ada_rows",
    )(cvec, w_ada, b_ada.reshape(depth, 1, n))


def _inproj_kernel(x_ref, mod_ref, w_ref, cos_ref, sa_ref, sb_ref, qnw_ref, knw_ref,
                   qa_ref, ka_ref, va_ref, qb_ref, kbd_ref, vbd_ref, u_ref):
    bt = x_ref.shape[0]
    h = _modulated(x_ref[...], mod_ref[0], 0, 1).astype(BF16)
    cos, sa, sb = cos_ref[...], sa_ref[...], sb_ref[...]
    lo = lax.broadcasted_iota(jnp.int32, (bt, V7X_LANES), 1) < HEAD_DIM

    def proj(c0, width):
        return jnp.dot(h, w_ref[:, c0:c0 + width], preferred_element_type=F32)

    def rope(x):
        return x * cos + pltpu.roll(x, V7X_LANES - 16, 1) * sa + pltpu.roll(x, 16, 1) * sb

    def headnorm(x, w):
        x2 = x * x
        s_lo = jnp.sum(jnp.where(lo, x2, 0.0), axis=-1, keepdims=True)
        s_hi = jnp.sum(jnp.where(lo, 0.0, x2), axis=-1, keepdims=True)
        r = jnp.where(lo, lax.rsqrt(s_lo / HEAD_DIM + EPS), lax.rsqrt(s_hi / HEAD_DIM + EPS))
        return x * r * w

    def tile(p, c):
        return p[:, c * V7X_LANES:(c + 1) * V7X_LANES]

    p = proj(OFF_KA, COL_KA)
    for c in range(A_HEADS):
        ka_ref[c] = rope(tile(p, c)).astype(BF16)

    p = proj(OFF_VA, COL_VA)
    for c in range(A_HEADS):
        va_ref[c] = tile(p, c).astype(BF16)

    p = proj(OFF_KB, COL_KB + COL_VB)
    kb = rope(headnorm(tile(p, 0), knw_ref[...]))
    kb_sw = pltpu.roll(kb, HEAD_DIM, 1)
    kbd_ref[0] = jnp.where(lo, kb, kb_sw).astype(BF16)
    kbd_ref[1] = jnp.where(lo, kb_sw, kb).astype(BF16)
    vb = tile(p, 1)
    vb_sw = pltpu.roll(vb, HEAD_DIM, 1)
    vbd_ref[0] = jnp.where(lo, vb, vb_sw).astype(BF16)
    vbd_ref[1] = jnp.where(lo, vb_sw, vb).astype(BF16)

    p = proj(OFF_QA, COL_QA)
    for c in range(COL_QA // V7X_LANES):
        qa_ref[:, c * V7X_LANES:(c + 1) * V7X_LANES] = (rope(tile(p, c)) * Q_SCALE).astype(BF16)

    p = proj(OFF_QB, COL_QB)
    qnw = qnw_ref[...]
    for c in range(COL_QB // V7X_LANES):
        qb_ref[:, c * V7X_LANES:(c + 1) * V7X_LANES] = (
            rope(headnorm(tile(p, c), qnw)) * Q_SCALE).astype(BF16)

    p = proj(OFF_GLU, COL_GLU)
    u_ref[...] = p[:, :C_WIDTH] * jax.nn.sigmoid(p[:, C_WIDTH:])


def _inproj(xy, mod_l, w_main, tabs, qnw, knw, dims):
    t_rows, nps, nbatch = dims["T"], dims["S"] // TOK_BLOCK, dims["B"]
    bt = TOK_BLOCK
    row = lambda r: (r, 0)
    full = lambda r: (0, 0)
    out_shapes = (
        jax.ShapeDtypeStruct((t_rows, COL_QA), BF16),
        jax.ShapeDtypeStruct((A_HEADS, t_rows, V7X_LANES), BF16),
        jax.ShapeDtypeStruct((A_HEADS, t_rows, V7X_LANES), BF16),
        jax.ShapeDtypeStruct((t_rows, COL_QB), BF16),
        jax.ShapeDtypeStruct((B_KV, t_rows, V7X_LANES), BF16),
        jax.ShapeDtypeStruct((B_KV, t_rows, V7X_LANES), BF16),
        jax.ShapeDtypeStruct((t_rows, C_WIDTH), F32),
    )
    grp = pl.BlockSpec((B_KV, bt, V7X_LANES), lambda r: (0, r, 0))
    heads = pl.BlockSpec((A_HEADS, bt, V7X_LANES), lambda r: (0, r, 0))
    return pl.pallas_call(
        _inproj_kernel,
        grid=(t_rows // bt,),
        in_specs=[pl.BlockSpec((bt, D_MODEL), row),
                  pl.BlockSpec((1, 6, D_MODEL), lambda r: (jnp.minimum(r // nps, nbatch), 0, 0)),
                  pl.BlockSpec((D_MODEL, OFF_GATE), full),
                  pl.BlockSpec((bt, V7X_LANES), row),
                  pl.BlockSpec((bt, V7X_LANES), row),
                  pl.BlockSpec((bt, V7X_LANES), row),
                  pl.BlockSpec((1, V7X_LANES), full),
                  pl.BlockSpec((1, V7X_LANES), full)],
        out_specs=(pl.BlockSpec((bt, COL_QA), row), heads, heads, pl.BlockSpec((bt, COL_QB), row),
                   grp, grp, pl.BlockSpec((bt, C_WIDTH), row)),
        out_shape=out_shapes,
        compiler_params=_cparams(("arbitrary",)),
        name="inproj",
    )(xy, mod_l, w_main, tabs[0], tabs[1], tabs[2], qnw, knw)


def _softmax_step(s, v, m, l, acc):
    mn = jnp.maximum(m, jnp.max(s, axis=-1, keepdims=True))
    p = jnp.exp2(s - mn)
    alpha = jnp.exp2(m - mn)
    l = alpha * l + jnp.sum(p, axis=-1, keepdims=True)
    acc = alpha * acc + jnp.dot(p.astype(BF16), v, preferred_element_type=F32)
    return mn, l, acc


def _flash_rows(q, load_lat, kc, vc, *, n_lat_chunks, tk):
    rows = q.shape[0]

    def step(k, v, carry):
        m, l, a = carry
        s = lax.dot_general(q, k, NT_DIMS, preferred_element_type=F32)
        return _softmax_step(s, v, m, l, a)

    carry = (jnp.full((rows, 1), NEG_BIG, F32), jnp.zeros((rows, 1), F32),
             jnp.zeros((rows, V7X_LANES), F32))
    for j in range(n_lat_chunks):
        carry = step(*load_lat(j * tk), carry)
    _, l, a = step(kc, vc, carry)
    return a / l


def _attn_a_kernel(q_ref, kl_ref, vl_ref, kc_ref, vc_ref, lamv_ref, sw_ref, o_ref, *,
                   n_lat_chunks, tk, lam_init):
    tq = q_ref.shape[0]
    q = q_ref[...]
    lo = lax.broadcasted_iota(jnp.int32, (tq, V7X_LANES), 1) < HEAD_DIM
    zero = jnp.zeros_like(q)
    q12 = jnp.concatenate([jnp.where(lo, q, zero), jnp.where(lo, zero, q)], axis=0)
    o12 = _flash_rows(q12, lambda off: (kl_ref[0, pl.ds(off, tk), :], vl_ref[0, pl.ds(off, tk), :]),
                      kc_ref[0], vc_ref[0], n_lat_chunks=n_lat_chunks, tk=tk)

    lv = lamv_ref[...]
    lam = (jnp.exp(jnp.sum(lv[0:1] * lv[1:2], axis=-1, keepdims=True))
           - jnp.exp(jnp.sum(lv[2:3] * lv[3:4], axis=-1, keepdims=True)) + lam_init)
    o = o12[:tq] - lam * o12[tq:]
    ms = jnp.mean(o * o, axis=-1, keepdims=True)
    o_ref[...] = (o * lax.rsqrt(ms + EPS) * sw_ref[...] * (1.0 - lam_init)).astype(BF16)


def _attn_b_kernel(q_ref, kl_ref, vl_ref, kc_ref, vc_ref, o_ref, *, n_lat_chunks, tk):
    tq = q_ref.shape[0]
    lo = lax.broadcasted_iota(jnp.int32, (tq, V7X_LANES), 1) < HEAD_DIM
    parts = []
    for c in range(2):
        qc = q_ref[:, c * V7X_LANES:(c + 1) * V7X_LANES]
        zero = jnp.zeros_like(qc)
        parts += [jnp.where(lo, qc, zero), jnp.where(lo, zero, qc)]
    q4 = jnp.concatenate(parts, axis=0)
    o = _flash_rows(q4, lambda off: (kl_ref[0, pl.ds(off, tk), :], vl_ref[0, pl.ds(off, tk), :]),
                    kc_ref[0], vc_ref[0], n_lat_chunks=n_lat_chunks, tk=tk)
    for c in range(2):
        o_ref[:, c * V7X_LANES:(c + 1) * V7X_LANES] = jnp.where(
            lo, o[(2 * c) * tq:(2 * c + 1) * tq], o[(2 * c + 1) * tq:(2 * c + 2) * tq]).astype(BF16)


def _attention_call(kern, name, q, k, v, extras, q_lanes, n_groups, tq, latent_queries, dims):
    s_len, ctx_len, nbatch = dims["S"], dims["CTX"], dims["B"]
    tk = min(ATT_TK, s_len)
    q_len = s_len if latent_queries else ctx_len
    nq = q_len // tq
    q_blk0 = 0 if latent_queries else (nbatch * s_len) // tq
    ctx_blk0 = (nbatch * s_len) // ctx_len
    extra_specs = [pl.BlockSpec(e.shape, lambda b, g, qi: (0, 0)) for e in extras]
    return pl.pallas_call(
        functools.partial(kern, n_lat_chunks=s_len // tk if latent_queries else 0, tk=tk),
        grid=(nbatch, n_groups, nq),
        in_specs=[pl.BlockSpec((tq, q_lanes), lambda b, g, qi: (q_blk0 + b * nq + qi, g)),
                  pl.BlockSpec((1, s_len, V7X_LANES), lambda b, g, qi: (g, b, 0)),
                  pl.BlockSpec((1, s_len, V7X_LANES), lambda b, g, qi: (g, b, 0)),
                  pl.BlockSpec((1, ctx_len, V7X_LANES), lambda b, g, qi: (g, ctx_blk0 + b, 0)),
                  pl.BlockSpec((1, ctx_len, V7X_LANES), lambda b, g, qi: (g, ctx_blk0 + b, 0)),
                  *extra_specs],
        out_specs=pl.BlockSpec((tq, q_lanes), lambda b, g, qi: (b * nq + qi, g)),
        out_shape=jax.ShapeDtypeStruct((nbatch * q_len, n_groups * q_lanes), BF16),
        compiler_params=_cparams(("arbitrary", "arbitrary", "arbitrary")),
        name=name,
    )(q, k, v, k, v, *extras)


def _attention(qa, ka, va, qb, kbd, vbd, lamv, subln_w, lam_init, with_ctx_queries, dims):
    kern_a = functools.partial(_attn_a_kernel, lam_init=lam_init)
    gw = 2 * V7X_LANES

    def both(latent, tag):
        q_len = dims["S"] if latent else dims["CTX"]
        oa = _attention_call(kern_a, "attn_a" + tag, qa, ka, va, (lamv, subln_w), V7X_LANES,
                             A_HEADS, min(ATT_TQ, q_len), latent, dims)
        ob = _attention_call(_attn_b_kernel, "attn_b" + tag, qb, kbd, vbd, (), gw, B_KV,
                             min(ATT_TQ // 2, q_len), latent, dims)
        return oa, ob

    oa, ob = both(True, "")
    if with_ctx_queries:
        oa_c, ob_c = both(False, "_ctx")
        oa, ob = jnp.concatenate([oa, oa_c], axis=0), jnp.concatenate([ob, ob_c], axis=0)
    return oa, ob


def _merge_kernel(x_ref, mod_ref, oa_ref, ob_ref, u_ref, up_ref, un_ref, cw_ref, cb_ref,
                  lnw_ref, lnb_ref, wg_ref, wa_ref, wb_ref, wc_ref, wo_ref, o_ref, uext_ref, *,
                  nps, ncps, n_lat_blocks):
    bt = x_ref.shape[0]
    r = pl.program_id(0)
    is_lat = r < n_lat_blocks
    pos = jnp.where(is_lat, r % nps, (r - n_lat_blocks) % ncps)
    last_pos = jnp.where(is_lat, nps - 1, ncps - 1)
    keep_prev = (pos != 0).astype(F32)
    keep_next = (pos != last_pos).astype(F32)

    x = x_ref[...]
    mod = mod_ref[0]
    h = _modulated(x, mod, 0, 1).astype(BF16)

    uext_ref[0:HALO, :] = up_ref[...] * keep_prev
    uext_ref[HALO:HALO + bt, :] = u_ref[...]
    uext_ref[HALO + bt:2 * HALO + bt, :] = un_ref[...] * keep_next
    cw = cw_ref[...]
    y = jnp.zeros((bt, C_WIDTH), F32) + cb_ref[...]
    base = HALO - C_KW // 2
    for k in range(C_KW):
        y = y + cw[k:k + 1, :] * uext_ref[base + k:base + k + bt, :]
    mu = jnp.mean(y, axis=-1, keepdims=True)
    yc = y - mu
    var = jnp.mean(yc * yc, axis=-1, keepdims=True)
    yn = yc * lax.rsqrt(var + EPS) * lnw_ref[...] + lnb_ref[...]
    oc = (yn * jax.nn.sigmoid(yn)).astype(BF16)

    def gate(i):
        logits = jnp.dot(h, wg_ref[:, i * D_MODEL:(i + 1) * D_MODEL], preferred_element_type=F32)
        return jax.nn.sigmoid(logits)

    merged = gate(0) * jnp.dot(oa_ref[...], wa_ref[...], preferred_element_type=F32)
    merged = merged + gate(1) * jnp.dot(ob_ref[...], wb_ref[...], preferred_element_type=F32)
    merged = merged + gate(2) * jnp.dot(oc, wc_ref[...], preferred_element_type=F32)
    out = jnp.dot(merged.astype(BF16), wo_ref[...], preferred_element_type=F32)
    o_ref[...] = x + mod[2:3] * out


def _merge(xy, mod_l, oa, ob, u, cw, cb, lnw, lnb, wg, wa, wb, wc, wo, n_blocks, dims):
    bt = TOK_BLOCK
    t_rows, nps, nbatch = dims["T"], dims["S"] // bt, dims["B"]
    hpb = bt // HALO
    n_halo = t_rows // HALO
    row = lambda r: (r, 0)
    full = lambda r: (0, 0)
    return pl.pallas_call(
        functools.partial(_merge_kernel, nps=nps, ncps=dims["CTX"] // bt,
                          n_lat_blocks=nbatch * nps),
        grid=(n_blocks,),
        in_specs=[pl.BlockSpec((bt, D_MODEL), row),
                  pl.BlockSpec((1, 6, D_MODEL), lambda r: (jnp.minimum(r // nps, nbatch), 0, 0)),
                  pl.BlockSpec((bt, 4 * V7X_LANES), row),
                  pl.BlockSpec((bt, 4 * V7X_LANES), row),
                  pl.BlockSpec((bt, C_WIDTH), row),
                  pl.BlockSpec((HALO, C_WIDTH), lambda r: (jnp.maximum(r * hpb - 1, 0), 0)),
                  pl.BlockSpec((HALO, C_WIDTH), lambda r: (jnp.minimum((r + 1) * hpb, n_halo - 1), 0)),
                  pl.BlockSpec((C_KW + 1, C_WIDTH), full),
                  pl.BlockSpec((1, C_WIDTH), full),
                  pl.BlockSpec((1, C_WIDTH), full),
                  pl.BlockSpec((1, C_WIDTH), full),
                  pl.BlockSpec((D_MODEL, 3 * D_MODEL), full),
                  pl.BlockSpec((C_WIDTH, D_MODEL), full),
                  pl.BlockSpec((C_WIDTH, D_MODEL), full),
                  pl.BlockSpec((C_WIDTH, D_MODEL), full),
                  pl.BlockSpec((D_MODEL, D_MODEL), full)],
        out_specs=pl.BlockSpec((bt, D_MODEL), row),
        out_shape=jax.ShapeDtypeStruct((n_blocks * bt, D_MODEL), F32),
        scratch_shapes=[pltpu.VMEM((bt + 2 * HALO, C_WIDTH), F32)],
        compiler_params=_cparams(("arbitrary",)),
        name="merge",
    )(xy, mod_l, oa, ob, u, u, u, cw, cb, lnw, lnb, wg, wa, wb, wc, wo)


def _topk_desc(s, k):
    row = lax.broadcasted_iota(jnp.int32, (k, s.shape[1]), 0)

    def body(i, carry):
        prev, vals = carry
        m = jnp.max(jnp.where(s < prev, s, -jnp.inf), axis=0, keepdims=True)
        return m, jnp.where(row == i, m, vals)

    init = (jnp.full((1, s.shape[1]), jnp.inf, F32), jnp.zeros((k, s.shape[1]), F32))
    _, vals = lax.fori_loop(0, k, body, init)
    return vals


def _candidate_sums(sv1, sv2):
    half = P_TOPK // 2
    lead, rest = sv1[:half], sv1[half:]
    row = lax.broadcasted_iota(jnp.int32, lead.shape, 0)
    tiles = [lead + sv2[0:1], rest + sv2[0:1], lead + sv2[1:2]]
    for k2 in range(2, half):
        tiles.append(jnp.where(row < P_TOPK // (k2 + 1), lead + sv2[k2:k2 + 1], -jnp.inf))
    tiles.append(sv1[0:1] + sv2[half:])
    return jnp.concatenate(tiles, axis=0)


def _bf16_pair_words(x):
    bits = pltpu.bitcast(x.astype(BF16).astype(F32), jnp.int32)
    hi = lax.shift_right_logical(bits, 16)
    return hi | lax.shift_left(hi, 16)


def _peer_select_kernel(x_ref, mod_ref, wq_ref, k1_ref, k2_ref,
                        t_ref, need_ref, e1_ref, r2_ref, e2_ref, s1_scr):
    bt = x_ref.shape[0]
    n_lt = bt // V7X_LANES
    hb = _modulated(x_ref[...], mod_ref[0], 3, 4).astype(BF16)
    t_ref[...] = hb
    q = jnp.dot(hb, wq_ref[...], preferred_element_type=F32).astype(BF16)
    s1p = lax.dot_general(k1_ref[...], q, NT_DIMS, preferred_element_type=F32)
    for lt in range(n_lt):
        s1_scr[lt] = s1p[:, lt * V7X_LANES:(lt + 1) * V7X_LANES]
    head_row = lax.broadcasted_iota(jnp.int32, (P_HEADS, bt), 0)
    zero_t = jnp.zeros((P_HEADS, bt), F32)
    thr1, max1, tau_t = zero_t, zero_t, zero_t
    sv2_t = [zero_t] * P_TOPK
    for hh in range(P_HEADS):
        mine = head_row == hh
        qh = q[:, hh * V7X_LANES:(hh + 1) * V7X_LANES]
        s1 = jnp.concatenate([s1_scr[lt, pl.ds(hh, P_NKEYS, stride=P_HEADS), :]
                              for lt in range(n_lt)], axis=1)
        s2 = lax.dot_general(k2_ref[hh], qh, NT_DIMS, preferred_element_type=F32)
        sv1 = _topk_desc(s1, P_TOPK)
        sv2 = _topk_desc(s2, P_TOPK)
        top = _topk_desc(_candidate_sums(sv1, sv2), P_TOPK)
        z = jnp.sum(jnp.exp(top - top[0:1]), axis=0, keepdims=True)
        rank2 = jnp.zeros_like(s2)
        for k in range(P_TOPK):
            rank2 = rank2 + jnp.where(sv2[k:k + 1] > s2, 1.0, 0.0)
            sv2_t[k] = jnp.where(mine, sv2[k:k + 1], sv2_t[k])
        r2_ref[0, hh] = pltpu.bitcast(rank2.astype(BF16), jnp.int32)
        e2_ref[0, hh] = pltpu.bitcast((jnp.exp(s2 - sv2[0:1]) / z).astype(BF16), jnp.int32)
        tau_t = jnp.where(mine, top[P_TOPK - 1:P_TOPK], tau_t)
        thr1 = jnp.where(mine, sv1[P_TOPK - 1:P_TOPK], thr1)
        max1 = jnp.where(mine, sv1[0:1], max1)
    for lt in range(n_lt):
        ls = slice(lt * V7X_LANES, (lt + 1) * V7X_LANES)
        s1t = s1_scr[lt].reshape(P_NKEYS, P_HEADS, V7X_LANES)
        count = jnp.zeros_like(s1t)
        for k in range(P_TOPK):
            count = count + jnp.where(s1t + sv2_t[k][None, :, ls] >= tau_t[None, :, ls], 1.0, 0.0)
        need = jnp.where(s1t >= thr1[None, :, ls], count, 0.0)
        need_ref[0, lt] = _bf16_pair_words(need.reshape(P_NKEYS * P_HEADS, V7X_LANES))
        e1 = jnp.exp(s1t - max1[None, :, ls])
        e1_ref[0, lt] = _bf16_pair_words(e1.reshape(P_NKEYS * P_HEADS, V7X_LANES))


def _peer_dense_kernel(x_ref, mod_ref, t_ref, need_ref, e1_ref, r2_ref, e2_ref, u_ref, vt_ref,
                       fnw_ref, o_ref, a_scr, m_scr, acc_scr, *, n_chunks, final_norm):
    bt = x_ref.shape[0]
    c = pl.program_id(1)
    ipc = u_ref.shape[0] // P_NKEYS
    n_lt = bt // V7X_LANES

    @pl.when(c == 0)
    def _init():
        acc_scr[...] = jnp.zeros_like(acc_scr)
        a_scr[...] = jnp.zeros_like(a_scr)

    cprev = jnp.maximum(c - 1, 0)

    def one_key(ii, _):
        roff = ii * P_NKEYS
        hrow = pl.ds(pl.multiple_of((cprev * ipc + ii) * P_HEADS, P_HEADS), P_HEADS)
        for lt in range(n_lt):
            ls = slice(lt * V7X_LANES, (lt + 1) * V7X_LANES)
            w = [jnp.zeros((ROW_GROUP, V7X_LANES), BF16) for _ in range(P_NKEYS // ROW_GROUP)]
            need_t = need_ref[0, lt, hrow, :]
            e1_t = e1_ref[0, lt, hrow, :]
            zero = jnp.zeros((ROW_GROUP, V7X_LANES), BF16)
            for hh in range(P_HEADS):
                words = (ROW_GROUP // 2, V7X_LANES)
                need_b = pltpu.bitcast(jnp.broadcast_to(need_t[hh:hh + 1], words), BF16)
                e1_b = pltpu.bitcast(jnp.broadcast_to(e1_t[hh:hh + 1], words), BF16)
                for jg in range(P_NKEYS // ROW_GROUP):
                    js = slice(jg * (ROW_GROUP // 2), (jg + 1) * (ROW_GROUP // 2))
                    rank2 = pltpu.bitcast(r2_ref[0, hh, js, ls], BF16)
                    e2 = pltpu.bitcast(e2_ref[0, hh, js, ls], BF16)
                    w[jg] = w[jg] + jnp.where(rank2 < need_b, e1_b * e2, zero)
            for jg in range(P_NKEYS // ROW_GROUP):
                rows = pl.ds(roff + jg * ROW_GROUP, ROW_GROUP)
                a = a_scr[rows, ls]
                g = 0.5 * a * (1.0 + lax.erf(a * (2.0 ** -0.5)))
                m_scr[rows, ls] = g.astype(BF16) * w[jg]
        return 0

    for ii in range(ipc):
        one_key(ii, 0)

    acc_scr[...] += jnp.dot(vt_ref[0], m_scr[...], preferred_element_type=F32)
    a_scr[...] = lax.dot_general(u_ref[...], t_ref[...], NT_DIMS, preferred_element_type=F32)

    @pl.when(c == n_chunks)
    def _finish():
        x = x_ref[...]
        y = x + mod_ref[0][5:6] * acc_scr[...].T
        if final_norm:
            ms = jnp.mean(y * y, axis=-1, keepdims=True)
            y = y * lax.rsqrt(ms + EPS) * fnw_ref[...]
        o_ref[...] = y


def _peer(x1, mod_l, wq, k1p, k2p, u_tab, vt_tab, fnw, n_blocks, final_norm, dims):
    bt, ec = PEER_BLOCK, vt_tab.shape[2]
    nps, nbatch = dims["S"] // bt, dims["B"]
    n_chunks = vt_tab.shape[0]
    n_lt = bt // V7X_LANES
    rows_ih = P_NKEYS * P_HEADS
    mod_spec = pl.BlockSpec((1, 6, D_MODEL), lambda r, *_: (jnp.minimum(r // nps, nbatch), 0, 0))
    ih_shape = jax.ShapeDtypeStruct((n_blocks, n_lt, rows_ih, V7X_LANES), jnp.int32)
    hj_shape = jax.ShapeDtypeStruct((n_blocks, P_HEADS, P_NKEYS // 2, bt), jnp.int32)
    ih_spec = pl.BlockSpec((1, n_lt, rows_ih, V7X_LANES), lambda r, *_: (r, 0, 0, 0))
    hj_spec = pl.BlockSpec((1, P_HEADS, P_NKEYS // 2, bt), lambda r, *_: (r, 0, 0, 0))
    tok_spec = pl.BlockSpec((bt, D_MODEL), lambda r, *_: (r, 0))

    t_mod, need, e1, rank2, e2 = pl.pallas_call(
        _peer_select_kernel,
        grid=(n_blocks,),
        in_specs=[tok_spec, mod_spec,
                  pl.BlockSpec((D_MODEL, P_HEADS * V7X_LANES), lambda r: (0, 0)),
                  pl.BlockSpec((rows_ih, P_HEADS * V7X_LANES), lambda r: (0, 0)),
                  pl.BlockSpec((P_HEADS, P_NKEYS, V7X_LANES), lambda r: (0, 0, 0))],
        out_specs=(tok_spec, ih_spec, ih_spec, hj_spec, hj_spec),
        out_shape=(jax.ShapeDtypeStruct((n_blocks * bt, D_MODEL), BF16),
                   ih_shape,
                   ih_shape,
                   hj_shape,
                   hj_shape),
        scratch_shapes=[pltpu.VMEM((n_lt, rows_ih, V7X_LANES), F32)],
        compiler_params=_cparams(("arbitrary",)),
        name="peer_select",
    )(x1, mod_l, wq, k1p, k2p)

    return pl.pallas_call(
        functools.partial(_peer_dense_kernel, n_chunks=n_chunks, final_norm=final_norm),
        grid=(n_blocks, n_chunks + 1),
        in_specs=[tok_spec, mod_spec, tok_spec, ih_spec, ih_spec, hj_spec, hj_spec,
                  pl.BlockSpec((ec, D_MODEL), lambda r, c: (jnp.minimum(c, n_chunks - 1), 0)),
                  pl.BlockSpec((1, D_MODEL, ec), lambda r, c: (jnp.maximum(c - 1, 0), 0, 0)),
                  pl.BlockSpec((1, D_MODEL), lambda r, c: (0, 0))],
        out_specs=tok_spec,
        out_shape=jax.ShapeDtypeStruct((n_blocks * bt, D_MODEL), F32),
        scratch_shapes=[pltpu.VMEM((ec, bt), F32),
                        pltpu.VMEM((ec, bt), BF16),
                        pltpu.VMEM((D_MODEL, bt), F32)],
        compiler_params=_cparams(("arbitrary", "arbitrary")),
        name="peer_dense",
    )(x1, mod_l, t_mod, need, e1, rank2, e2, u_tab, vt_tab, fnw)


def _rope_tables(s_len, nbatch, n_ctx_rows):
    t = jnp.arange(s_len, dtype=jnp.int32)
    row = (t // GRID_W).astype(F32)
    col = (t % GRID_W).astype(F32)
    axis_dim = HEAD_DIM // 2
    inv = ROPE_THETA ** (-jnp.arange(0, axis_dim, 2, dtype=F32) / axis_dim)
    ar = row[:, None] * inv[None, :]
    ac = col[:, None] * inv[None, :]
    ang = jnp.concatenate([ar, ar, ac, ac], axis=-1)
    cos, sin = jnp.cos(ang), jnp.sin(ang)
    quarter = jnp.arange(HEAD_DIM) // (HEAD_DIM // 4)
    first = (quarter % 2 == 0)[None, :]
    sin_a = jnp.where(first, -sin, 0.0)
    sin_b = jnp.where(first, 0.0, sin)

    def expand(tab, ctx_fill):
        lat = jnp.tile(jnp.concatenate([tab, tab], axis=-1), (nbatch, 1))
        return jnp.concatenate([lat, jnp.full((n_ctx_rows, V7X_LANES), ctx_fill, F32)], axis=0)

    return expand(cos, 1.0), expand(sin_a, 0.0), expand(sin_b, 0.0)


def _padded_keys(keys_l):
    z = jnp.zeros_like(keys_l[:, 0])
    k0 = jnp.concatenate([keys_l[:, 0], z], axis=-1).transpose(1, 0, 2)
    eye = jnp.eye(P_HEADS, dtype=keys_l.dtype)
    k1p = (k0[:, :, None, :] * eye[None, :, :, None]).reshape(
        P_NKEYS * P_HEADS, P_HEADS * V7X_LANES)
    k2p = jnp.concatenate([z, keys_l[:, 1]], axis=-1)
    return k1p.astype(BF16), k2p.astype(BF16)


def _chunked_transpose(v_tab, ec):
    n_exp, d = v_tab.shape
    return v_tab.astype(BF16).reshape(n_exp // ec, ec, d).transpose(0, 2, 1)


def kernel(x, c, ctx, c_ctx, w_ada, b_ada, w_in, lam_q1, lam_k1, lam_q2, lam_k2, subln_w, q_norm_w, k_norm_w, conv_w, conv_b, conv_ln_w, conv_ln_b, w_branch_a, w_branch_b, w_branch_c, w_out, peer_wq, peer_keys, peer_u, peer_v, final_norm_w):
    nbatch, s_len, d = x.shape
    ctx_len = ctx.shape[1]
    depth = w_ada.shape[0]
    n_lat, n_ctx = nbatch * s_len, nbatch * ctx_len
    dims = {"B": nbatch, "S": s_len, "CTX": ctx_len, "T": n_lat + n_ctx}
    assert d == D_MODEL and nbatch + 1 <= MOD_ROWS
    assert s_len % PEER_BLOCK == 0 and s_len % ATT_TQ == 0 and ctx_len % TOK_BLOCK == 0
    assert ctx_len % min(ATT_TQ, ctx_len) == 0 and n_lat % min(ATT_TQ, ctx_len) == 0
    assert n_ctx % PEER_BLOCK == 0 and n_lat % ctx_len == 0

    cvec = jnp.zeros((MOD_ROWS, d), F32).at[:nbatch].set(c).at[nbatch].set(c_ctx)
    mod = _ada_rows(cvec, w_ada, b_ada).reshape(depth, MOD_ROWS, 6, d)
    tabs = _rope_tables(s_len, nbatch, n_ctx)
    tile2 = lambda v: jnp.concatenate([v, v], axis=-1)[None, :]

    xy = jnp.concatenate([x.reshape(n_lat, d), ctx.reshape(n_ctx, d)], axis=0)
    for l in range(depth):
        last = l == depth - 1
        lam_init = 0.8 - 0.6 * math.exp(-0.3 * l)
        w_bf = w_in[l].astype(BF16)
        lamv = jnp.stack([lam_q1[l], lam_k1[l], lam_q2[l], lam_k2[l]], axis=0)

        qa, ka, va, qb, kbd, vbd, u = _inproj(
            xy, mod[l], w_bf[:, :OFF_GATE], tabs, tile2(q_norm_w[l]), tile2(k_norm_w[l]), dims)
        oa, ob = _attention(qa, ka, va, qb, kbd, vbd, lamv, subln_w[l][None, :], lam_init,
                            not last, dims)
        n_rows = n_lat if last else n_lat + n_ctx
        cw = jnp.concatenate([conv_w[l], jnp.zeros((1, C_WIDTH), F32)], axis=0)
        x1 = _merge(xy, mod[l], oa, ob, u, cw, conv_b[l][None, :], conv_ln_w[l][None, :],
                    conv_ln_b[l][None, :], w_bf[:, OFF_GATE:], w_branch_a[l].astype(BF16),
                    w_branch_b[l].astype(BF16), w_branch_c[l].astype(BF16),
                    w_out[l].astype(BF16), n_rows // TOK_BLOCK, dims)
        k1p, k2p = _padded_keys(peer_keys[l])
        xy = _peer(x1, mod[l], peer_wq[l].astype(BF16), k1p, k2p,
                   peer_u[l].astype(BF16), _chunked_transpose(peer_v[l], PEER_ECHUNK),
                   final_norm_w[None, :],
                   n_rows // PEER_BLOCK, last, dims)
    return xy.reshape(nbatch, s_len, d)
```

```python
import functools
import math

import jax
import jax.numpy as jnp
from jax import lax
from jax.experimental import pallas as pl
from jax.experimental.pallas import tpu as pltpu

F32 = jnp.float32
BF16 = jnp.bfloat16

D_MODEL = 1024
DEPTH = 2
GRID_W = 64
HEAD_DIM = 64
ROPE_THETA = 10000.0
EPS = 1e-6
A_HEADS = 4
B_KV = 2
C_WIDTH = 512
C_KW = 31
P_HEADS = 8
P_NKEYS = 128
P_TOPK = 16

COL_KA, COL_VA, COL_KB, COL_VB, COL_QA, COL_QB, COL_GLU = 512, 512, 128, 128, 512, 512, 1024
OFF_KA = 0
OFF_VA = OFF_KA + COL_KA
OFF_KB = OFF_VA + COL_VA
OFF_QA = OFF_KB + COL_KB + COL_VB
OFF_QB = OFF_QA + COL_QA
OFF_GLU = OFF_QB + COL_QB
OFF_GATE = OFF_GLU + COL_GLU

V7X_LANES = 128
V7X_VMEM_LIMIT = 56 * 1024 * 1024
HALO = 16

TOK_BLOCK = 256
ATT_TQ = 512
ATT_TK = 2048
PEER_BLOCK = 512
PEER_ECHUNK = 1024
MOD_ROWS = 8
ROW_GROUP = 16

Q_SCALE = (HEAD_DIM ** -0.5) * math.log2(math.e)
NEG_BIG = -1e30
NT_DIMS = (((1,), (1,)), ((), ()))


def _cparams(sem):
    return pltpu.CompilerParams(dimension_semantics=sem, vmem_limit_bytes=V7X_VMEM_LIMIT)


def _modulated(x, mod, shift_row, scale_row):
    ms = jnp.mean(x * x, axis=-1, keepdims=True)
    xn = x * lax.rsqrt(ms + EPS)
    return xn * (1.0 + mod[scale_row:scale_row + 1]) + mod[shift_row:shift_row + 1]


def _ada_kernel(c_ref, w_ref, b_ref, o_ref):
    c = c_ref[...]
    sc = c * jax.nn.sigmoid(c)
    o_ref[0] = jnp.dot(sc, w_ref[0], preferred_element_type=F32,
                       precision=lax.Precision.HIGHEST) + b_ref[0]


def _ada_rows(cvec, w_ada, b_ada):
    depth, d, n = w_ada.shape
    tn = 1536
    return pl.pallas_call(
        _ada_kernel,
        grid=(depth, n // tn),
        in_specs=[pl.BlockSpec((MOD_ROWS, d), lambda l, j: (0, 0)),
                  pl.BlockSpec((1, d, tn), lambda l, j: (l, 0, j)),
                  pl.BlockSpec((1, 1, tn), lambda l, j: (l, 0, j))],
        out_specs=pl.BlockSpec((1, MOD_ROWS, tn), lambda l, j: (l, 0, j)),
        out_shape=jax.ShapeDtypeStruct((depth, MOD_ROWS, n), F32),
        compiler_params=_cparams(("arbitrary", "arbitrary")),
        name="ada_rows",
    )(cvec, w_ada, b_ada.reshape(depth, 1, n))


def _inproj_kernel(x_ref, mod_ref, w_ref, cos_ref, sa_ref, sb_ref, qnw_ref, knw_ref,
                   qa_ref, ka_ref, va_ref, qb_ref, kbd_ref, vbd_ref, u_ref):
    bt = x_ref.shape[0]
    h = _modulated(x_ref[...], mod_ref[0], 0, 1).astype(BF16)
    cos, sa, sb = cos_ref[...], sa_ref[...], sb_ref[...]
    lo = lax.broadcasted_iota(jnp.int32, (bt, V7X_LANES), 1) < HEAD_DIM

    def proj(c0, width):
        return jnp.dot(h, w_ref[:, c0:c0 + width], preferred_element_type=F32)

    def rope(x):
        return x * cos + pltpu.roll(x, V7X_LANES - 16, 1) * sa + pltpu.roll(x, 16, 1) * sb

    def headnorm(x, w):
        x2 = x * x
        s_lo = jnp.sum(jnp.where(lo, x2, 0.0), axis=-1, keepdims=True)
        s_hi = jnp.sum(jnp.where(lo, 0.0, x2), axis=-1, keepdims=True)
        r = jnp.where(lo, lax.rsqrt(s_lo / HEAD_DIM + EPS), lax.rsqrt(s_hi / HEAD_DIM + EPS))
        return x * r * w

    def tile(p, c):
        return p[:, c * V7X_LANES:(c + 1) * V7X_LANES]

    p = proj(OFF_KA, COL_KA)
    for c in range(A_HEADS):
        ka_ref[c] = rope(tile(p, c)).astype(BF16)

    p = proj(OFF_VA, COL_VA)
    for c in range(A_HEADS):
        va_ref[c] = tile(p, c).astype(BF16)

    p = proj(OFF_KB, COL_KB + COL_VB)
    kb = rope(headnorm(tile(p, 0), knw_ref[...]))
    kb_sw = pltpu.roll(kb, HEAD_DIM, 1)
    kbd_ref[0] = jnp.where(lo, kb, kb_sw).astype(BF16)
    kbd_ref[1] = jnp.where(lo, kb_sw, kb).astype(BF16)
    vb = tile(p, 1)
    vb_sw = pltpu.roll(vb, HEAD_DIM, 1)
    vbd_ref[0] = jnp.where(lo, vb, vb_sw).astype(BF16)
    vbd_ref[1] = jnp.where(lo, vb_sw, vb).astype(BF16)

    p = proj(OFF_QA, COL_QA)
    for c in range(COL_QA // V7X_LANES):
        qa_ref[:, c * V7X_LANES:(c + 1) * V7X_LANES] = (rope(tile(p, c)) * Q_SCALE).astype(BF16)

    p = proj(OFF_QB, COL_QB)
    qnw = qnw_ref[...]
    for c in range(COL_QB // V7X_LANES):
        qb_ref[:, c * V7X_LANES:(c + 1) * V7X_LANES] = (
            rope(headnorm(tile(p, c), qnw)) * Q_SCALE).astype(BF16)

    p = proj(OFF_GLU, COL_GLU)
    u_ref[...] = p[:, :C_WIDTH] * jax.nn.sigmoid(p[:, C_WIDTH:])


def _inproj(xy, mod_l, w_main, tabs, qnw, knw, dims):
    t_rows, nps, nbatch = dims["T"], dims["S"] // TOK_BLOCK, dims["B"]
    bt = TOK_BLOCK
    row = lambda r: (r, 0)
    full = lambda r: (0, 0)
    out_shapes = (
        jax.ShapeDtypeStruct((t_rows, COL_QA), BF16),
        jax.ShapeDtypeStruct((A_HEADS, t_rows, V7X_LANES), BF16),
        jax.ShapeDtypeStruct((A_HEADS, t_rows, V7X_LANES), BF16),
        jax.ShapeDtypeStruct((t_rows, COL_QB), BF16),
        jax.ShapeDtypeStruct((B_KV, t_rows, V7X_LANES), BF16),
        jax.ShapeDtypeStruct((B_KV, t_rows, V7X_LANES), BF16),
        jax.ShapeDtypeStruct((t_rows, C_WIDTH), F32),
    )
    grp = pl.BlockSpec((B_KV, bt, V7X_LANES), lambda r: (0, r, 0))
    heads = pl.BlockSpec((A_HEADS, bt, V7X_LANES), lambda r: (0, r, 0))
    return pl.pallas_call(
        _inproj_kernel,
        grid=(t_rows // bt,),
        in_specs=[pl.BlockSpec((bt, D_MODEL), row),
                  pl.BlockSpec((1, 6, D_MODEL), lambda r: (jnp.minimum(r // nps, nbatch), 0, 0)),
                  pl.BlockSpec((D_MODEL, OFF_GATE), full),
                  pl.BlockSpec((bt, V7X_LANES), row),
                  pl.BlockSpec((bt, V7X_LANES), row),
                  pl.BlockSpec((bt, V7X_LANES), row),
                  pl.BlockSpec((1, V7X_LANES), full),
                  pl.BlockSpec((1, V7X_LANES), full)],
        out_specs=(pl.BlockSpec((bt, COL_QA), row), heads, heads, pl.BlockSpec((bt, COL_QB), row),
                   grp, grp, pl.BlockSpec((bt, C_WIDTH), row)),
        out_shape=out_shapes,
        compiler_params=_cparams(("arbitrary",)),
        name="inproj",
    )(xy, mod_l, w_main, tabs[0], tabs[1], tabs[2], qnw, knw)


def _softmax_step(s, v, m, l, acc):
    mn = jnp.maximum(m, jnp.max(s, axis=-1, keepdims=True))
    p = jnp.exp2(s - mn)
    alpha = jnp.exp2(m - mn)
    l = alpha * l + jnp.sum(p, axis=-1, keepdims=True)
    acc = alpha * acc + jnp.dot(p.astype(BF16), v, preferred_element_type=F32)
    return mn, l, acc


def _flash_rows(q, load_lat, kc, vc, *, n_lat_chunks, tk):
    rows = q.shape[0]

    def step(k, v, carry):
        m, l, a = carry
        s = lax.dot_general(q, k, NT_DIMS, preferred_element_type=F32)
        return _softmax_step(s, v, m, l, a)

    carry = (jnp.full((rows, 1), NEG_BIG, F32), jnp.zeros((rows, 1), F32),
             jnp.zeros((rows, V7X_LANES), F32))
    for j in range(n_lat_chunks):
        carry = step(*load_lat(j * tk), carry)
    _, l, a = step(kc, vc, carry)
    return a / l


def _attn_a_kernel(q_ref, kl_ref, vl_ref, kc_ref, vc_ref, lamv_ref, sw_ref, o_ref, *,
                   n_lat_chunks, tk, lam_init):
    tq = q_ref.shape[0]
    q = q_ref[...]
    lo = lax.broadcasted_iota(jnp.int32, (tq, V7X_LANES), 1) < HEAD_DIM
    zero = jnp.zeros_like(q)
    q12 = jnp.concatenate([jnp.where(lo, q, zero), jnp.where(lo, zero, q)], axis=0)
    o12 = _flash_rows(q12, lambda off: (kl_ref[0, pl.ds(off, tk), :], vl_ref[0, pl.ds(off, tk), :]),
                      kc_ref[0], vc_ref[0], n_lat_chunks=n_lat_chunks, tk=tk)

    lv = lamv_ref[...]
    lam = (jnp.exp(jnp.sum(lv[0:1] * lv[1:2], axis=-1, keepdims=True))
           - jnp.exp(jnp.sum(lv[2:3] * lv[3:4], axis=-1, keepdims=True)) + lam_init)
    o = o12[:tq] - lam * o12[tq:]
    ms = jnp.mean(o * o, axis=-1, keepdims=True)
    o_ref[...] = (o * lax.rsqrt(ms + EPS) * sw_ref[...] * (1.0 - lam_init)).astype(BF16)


def _attn_b_kernel(q_ref, kl_ref, vl_ref, kc_ref, vc_ref, o_ref, *, n_lat_chunks, tk):
    tq = q_ref.shape[0]
    lo = lax.broadcasted_iota(jnp.int32, (tq, V7X_LANES), 1) < HEAD_DIM
    parts = []
    for c in range(2):
        qc = q_ref[:, c * V7X_LANES:(c + 1) * V7X_LANES]
        zero = jnp.zeros_like(qc)
        parts += [jnp.where(lo, qc, zero), jnp.where(lo, zero, qc)]
    q4 = jnp.concatenate(parts, axis=0)
    o = _flash_rows(q4, lambda off: (kl_ref[0, pl.ds(off, tk), :], vl_ref[0, pl.ds(off, tk), :]),
                    kc_ref[0], vc_ref[0], n_lat_chunks=n_lat_chunks, tk=tk)
    for c in range(2):
        o_ref[:, c * V7X_LANES:(c + 1) * V7X_LANES] = jnp.where(
            lo, o[(2 * c) * tq:(2 * c + 1) * tq], o[(2 * c + 1) * tq:(2 * c + 2) * tq]).astype(BF16)


def _attention_call(kern, name, q, k, v, extras, q_lanes, n_groups, tq, latent_queries, dims):
    s_len, ctx_len, nbatch = dims["S"], dims["CTX"], dims["B"]
    tk = min(ATT_TK, s_len)
    q_len = s_len if latent_queries else ctx_len
    nq = q_len // tq
    q_blk0 = 0 if latent_queries else (nbatch * s_len) // tq
    ctx_blk0 = (nbatch * s_len) // ctx_len
    extra_specs = [pl.BlockSpec(e.shape, lambda b, g, qi: (0, 0)) for e in extras]
    return pl.pallas_call(
        functools.partial(kern, n_lat_chunks=s_len // tk if latent_queries else 0, tk=tk),
        grid=(nbatch, n_groups, nq),
        in_specs=[pl.BlockSpec((tq, q_lanes), lambda b, g, qi: (q_blk0 + b * nq + qi, g)),
                  pl.BlockSpec((1, s_len, V7X_LANES), lambda b, g, qi: (g, b, 0)),
                  pl.BlockSpec((1, s_len, V7X_LANES), lambda b, g, qi: (g, b, 0)),
                  pl.BlockSpec((1, ctx_len, V7X_LANES), lambda b, g, qi: (g, ctx_blk0 + b, 0)),
                  pl.BlockSpec((1, ctx_len, V7X_LANES), lambda b, g, qi: (g, ctx_blk0 + b, 0)),
                  *extra_specs],
        out_specs=pl.BlockSpec((tq, q_lanes), lambda b, g, qi: (b * nq + qi, g)),
        out_shape=jax.ShapeDtypeStruct((nbatch * q_len, n_groups * q_lanes), BF16),
        compiler_params=_cparams(("arbitrary", "arbitrary", "arbitrary")),
        name=name,
    )(q, k, v, k, v, *extras)


def _attention(qa, ka, va, qb, kbd, vbd, lamv, subln_w, lam_init, with_ctx_queries, dims):
    kern_a = functools.partial(_attn_a_kernel, lam_init=lam_init)
    gw = 2 * V7X_LANES

    def both(latent, tag):
        q_len = dims["S"] if latent else dims["CTX"]
        oa = _attention_call(kern_a, "attn_a" + tag, qa, ka, va, (lamv, subln_w), V7X_LANES,
                             A_HEADS, min(ATT_TQ, q_len), latent, dims)
        ob = _attention_call(_attn_b_kernel, "attn_b" + tag, qb, kbd, vbd, (), gw, B_KV,
                             min(ATT_TQ // 2, q_len), latent, dims)
        return oa, ob

    oa, ob = both(True, "")
    if with_ctx_queries:
        oa_c, ob_c = both(False, "_ctx")
        oa, ob = jnp.concatenate([oa, oa_c], axis=0), jnp.concatenate([ob, ob_c], axis=0)
    return oa, ob


def _merge_kernel(x_ref, mod_ref, oa_ref, ob_ref, u_ref, up_ref, un_ref, cw_ref, cb_ref,
                  lnw_ref, lnb_ref, wg_ref, wa_ref, wb_ref, wc_ref, wo_ref, o_ref, uext_ref, *,
                  nps, ncps, n_lat_blocks):
    bt = x_ref.shape[0]
    r = pl.program_id(0)
    is_lat = r < n_lat_blocks
    pos = jnp.where(is_lat, r % nps, (r - n_lat_blocks) % ncps)
    last_pos = jnp.where(is_lat, nps - 1, ncps - 1)
    keep_prev = (pos != 0).astype(F32)
    keep_next = (pos != last_pos).astype(F32)

    x = x_ref[...]
    mod = mod_ref[0]
    h = _modulated(x, mod, 0, 1).astype(BF16)

    uext_ref[0:HALO, :] = up_ref[...] * keep_prev
    uext_ref[HALO:HALO + bt, :] = u_ref[...]
    uext_ref[HALO + bt:2 * HALO + bt, :] = un_ref[...] * keep_next
    cw = cw_ref[...]
    y = jnp.zeros((bt, C_WIDTH), F32) + cb_ref[...]
    base = HALO - C_KW // 2
    for k in range(C_KW):
        y = y + cw[k:k + 1, :] * uext_ref[base + k:base + k + bt, :]
    mu = jnp.mean(y, axis=-1, keepdims=True)
    yc = y - mu
    var = jnp.mean(yc * yc, axis=-1, keepdims=True)
    yn = yc * lax.rsqrt(var + EPS) * lnw_ref[...] + lnb_ref[...]
    oc = (yn * jax.nn.sigmoid(yn)).astype(BF16)

    def gate(i):
        logits = jnp.dot(h, wg_ref[:, i * D_MODEL:(i + 1) * D_MODEL], preferred_element_type=F32)
        return jax.nn.sigmoid(logits)

    merged = gate(0) * jnp.dot(oa_ref[...], wa_ref[...], preferred_element_type=F32)
    merged = merged + gate(1) * jnp.dot(ob_ref[...], wb_ref[...], preferred_element_type=F32)
    merged = merged + gate(2) * jnp.dot(oc, wc_ref[...], preferred_element_type=F32)
    out = jnp.dot(merged.astype(BF16), wo_ref[...], preferred_element_type=F32)
    o_ref[...] = x + mod[2:3] * out


def _merge(xy, mod_l, oa, ob, u, cw, cb, lnw, lnb, wg, wa, wb, wc, wo, n_blocks, dims):
    bt = TOK_BLOCK
    t_rows, nps, nbatch = dims["T"], dims["S"] // bt, dims["B"]
    hpb = bt // HALO
    n_halo = t_rows // HALO
    row = lambda r: (r, 0)
    full = lambda r: (0, 0)
    return pl.pallas_call(
        functools.partial(_merge_kernel, nps=nps, ncps=dims["CTX"] // bt,
                          n_lat_blocks=nbatch * nps),
        grid=(n_blocks,),
        in_specs=[pl.BlockSpec((bt, D_MODEL), row),
                  pl.BlockSpec((1, 6, D_MODEL), lambda r: (jnp.minimum(r // nps, nbatch), 0, 0)),
                  pl.BlockSpec((bt, 4 * V7X_LANES), row),
                  pl.BlockSpec((bt, 4 * V7X_LANES), row),
                  pl.BlockSpec((bt, C_WIDTH), row),
                  pl.BlockSpec((HALO, C_WIDTH), lambda r: (jnp.maximum(r * hpb - 1, 0), 0)),
                  pl.BlockSpec((HALO, C_WIDTH), lambda r: (jnp.minimum((r + 1) * hpb, n_halo - 1), 0)),
                  pl.BlockSpec((C_KW + 1, C_WIDTH), full),
                  pl.BlockSpec((1, C_WIDTH), full),
                  pl.BlockSpec((1, C_WIDTH), full),
                  pl.BlockSpec((1, C_WIDTH), full),
                  pl.BlockSpec((D_MODEL, 3 * D_MODEL), full),
                  pl.BlockSpec((C_WIDTH, D_MODEL), full),
                  pl.BlockSpec((C_WIDTH, D_MODEL), full),
                  pl.BlockSpec((C_WIDTH, D_MODEL), full),
                  pl.BlockSpec((D_MODEL, D_MODEL), full)],
        out_specs=pl.BlockSpec((bt, D_MODEL), row),
        out_shape=jax.ShapeDtypeStruct((n_blocks * bt, D_MODEL), F32),
        scratch_shapes=[pltpu.VMEM((bt + 2 * HALO, C_WIDTH), F32)],
        compiler_params=_cparams(("arbitrary",)),
        name="merge",
    )(xy, mod_l, oa, ob, u, u, u, cw, cb, lnw, lnb, wg, wa, wb, wc, wo)


def _topk_desc(s, k):
    row = lax.broadcasted_iota(jnp.int32, (k, s.shape[1]), 0)

    def body(i, carry):
        prev, vals = carry
        m = jnp.max(jnp.where(s < prev, s, -jnp.inf), axis=0, keepdims=True)
        return m, jnp.where(row == i, m, vals)

    init = (jnp.full((1, s.shape[1]), jnp.inf, F32), jnp.zeros((k, s.shape[1]), F32))
    _, vals = lax.fori_loop(0, k, body, init)
    return vals


def _candidate_sums(sv1, sv2):
    half = P_TOPK // 2
    lead, rest = sv1[:half], sv1[half:]
    row = lax.broadcasted_iota(jnp.int32, lead.shape, 0)
    tiles = [lead + sv2[0:1], rest + sv2[0:1], lead + sv2[1:2]]
    for k2 in range(2, half):
        tiles.append(jnp.where(row < P_TOPK // (k2 + 1), lead + sv2[k2:k2 + 1], -jnp.inf))
    tiles.append(sv1[0:1] + sv2[half:])
    return jnp.concatenate(tiles, axis=0)


def _bf16_pair_words(x):
    bits = pltpu.bitcast(x.astype(BF16).astype(F32), jnp.int32)
    hi = lax.shift_right_logical(bits, 16)
    return hi | lax.shift_left(hi, 16)


def _peer_select_kernel(x_ref, mod_ref, wq_ref, k1_ref, k2_ref,
                        t_ref, need_ref, e1_ref, r2_ref, e2_ref, s1_scr):
    bt = x_ref.shape[0]
    n_lt = bt // V7X_LANES
    hb = _modulated(x_ref[...], mod_ref[0], 3, 4).astype(BF16)
    t_ref[...] = hb
    q = jnp.dot(hb, wq_ref[...], preferred_element_type=F32).astype(BF16)
    s1p = lax.dot_general(k1_ref[...], q, NT_DIMS, preferred_element_type=F32)
    for lt in range(n_lt):
        s1_scr[lt] = s1p[:, lt * V7X_LANES:(lt + 1) * V7X_LANES]
    head_row = lax.broadcasted_iota(jnp.int32, (P_HEADS, bt), 0)
    zero_t = jnp.zeros((P_HEADS, bt), F32)
    thr1, max1, tau_t = zero_t, zero_t, zero_t
    sv2_t = [zero_t] * P_TOPK
    for hh in range(P_HEADS):
        mine = head_row == hh
        qh = q[:, hh * V7X_LANES:(hh + 1) * V7X_LANES]
        s1 = jnp.concatenate([s1_scr[lt, pl.ds(hh, P_NKEYS, stride=P_HEADS), :]
                              for lt in range(n_lt)], axis=1)
        s2 = lax.dot_general(k2_ref[hh], qh, NT_DIMS, preferred_element_type=F32)
        sv1 = _topk_desc(s1, P_TOPK)
        sv2 = _topk_desc(s2, P_TOPK)
        top = _topk_desc(_candidate_sums(sv1, sv2), P_TOPK)
        z = jnp.sum(jnp.exp(top - top[0:1]), axis=0, keepdims=True)
        rank2 = jnp.zeros_like(s2)
        for k in range(P_TOPK):
            rank2 = rank2 + jnp.where(sv2[k:k + 1] > s2, 1.0, 0.0)
            sv2_t[k] = jnp.where(mine, sv2[k:k + 1], sv2_t[k])
        r2_ref[0, hh] = pltpu.bitcast(rank2.astype(BF16), jnp.int32)
        e2_ref[0, hh] = pltpu.bitcast((jnp.exp(s2 - sv2[0:1]) / z).astype(BF16), jnp.int32)
        tau_t = jnp.where(mine, top[P_TOPK - 1:P_TOPK], tau_t)
        thr1 = jnp.where(mine, sv1[P_TOPK - 1:P_TOPK], thr1)
        max1 = jnp.where(mine, sv1[0:1], max1)
    for lt in range(n_lt):
        ls = slice(lt * V7X_LANES, (lt + 1) * V7X_LANES)
        s1t = s1_scr[lt].reshape(P_NKEYS, P_HEADS, V7X_LANES)
        count = jnp.zeros_like(s1t)
        for k in range(P_TOPK):
            count = count + jnp.where(s1t + sv2_t[k][None, :, ls] >= tau_t[None, :, ls], 1.0, 0.0)
        need = jnp.where(s1t >= thr1[None, :, ls], count, 0.0)
        need_ref[0, lt] = _bf16_pair_words(need.reshape(P_NKEYS * P_HEADS, V7X_LANES))
        e1 = jnp.exp(s1t - max1[None, :, ls])
        e1_ref[0, lt] = _bf16_pair_words(e1.reshape(P_NKEYS * P_HEADS, V7X_LANES))


def _peer_dense_kernel(x_ref, mod_ref, t_ref, need_ref, e1_ref, r2_ref, e2_ref, u0_ref, u_ref,
                       vt_ref, fnw_ref, o_ref, a_scr, m_scr, acc_scr, *, n_chunks, final_norm):
    bt = x_ref.shape[0]
    c = pl.program_id(1)
    ipc = u_ref.shape[0] // P_NKEYS
    n_lt = bt // V7X_LANES

    @pl.when(c == 0)
    def _init():
        acc_scr[...] = jnp.zeros_like(acc_scr)
        a_scr[...] = lax.dot_general(u0_ref[...], t_ref[...], NT_DIMS, preferred_element_type=F32)


    def one_key(ii, _):
        roff = ii * P_NKEYS
        hrow = pl.ds(pl.multiple_of((c * ipc + ii) * P_HEADS, P_HEADS), P_HEADS)
        for lt in range(n_lt):
            ls = slice(lt * V7X_LANES, (lt + 1) * V7X_LANES)
            w = [jnp.zeros((ROW_GROUP, V7X_LANES), BF16) for _ in range(P_NKEYS // ROW_GROUP)]
            need_t = need_ref[0, lt, hrow, :]
            e1_t = e1_ref[0, lt, hrow, :]
            zero = jnp.zeros((ROW_GROUP, V7X_LANES), BF16)
            for hh in range(P_HEADS):
                words = (ROW_GROUP // 2, V7X_LANES)
                need_b = pltpu.bitcast(jnp.broadcast_to(need_t[hh:hh + 1], words), BF16)
                e1_b = pltpu.bitcast(jnp.broadcast_to(e1_t[hh:hh + 1], words), BF16)
                for jg in range(P_NKEYS // ROW_GROUP):
                    js = slice(jg * (ROW_GROUP // 2), (jg + 1) * (ROW_GROUP // 2))
                    rank2 = pltpu.bitcast(r2_ref[0, hh, js, ls], BF16)
                    e2 = pltpu.bitcast(e2_ref[0, hh, js, ls], BF16)
                    w[jg] = w[jg] + jnp.where(rank2 < need_b, e1_b * e2, zero)
            for jg in range(P_NKEYS // ROW_GROUP):
                rows = pl.ds(roff + jg * ROW_GROUP, ROW_GROUP)
                a = a_scr[rows, ls]
                g = 0.5 * a * (1.0 + lax.erf(a * (2.0 ** -0.5)))
                m_scr[rows, ls] = g.astype(BF16) * w[jg]
        return 0

    for ii in range(ipc):
        one_key(ii, 0)

    acc_scr[...] += jnp.dot(vt_ref[0], m_scr[...], preferred_element_type=F32)
    a_scr[...] = lax.dot_general(u_ref[...], t_ref[...], NT_DIMS, preferred_element_type=F32)

    @pl.when(c == n_chunks - 1)
    def _finish():
        x = x_ref[...]
        y = x + mod_ref[0][5:6] * acc_scr[...].T
        if final_norm:
            ms = jnp.mean(y * y, axis=-1, keepdims=True)
            y = y * lax.rsqrt(ms + EPS) * fnw_ref[...]
        o_ref[...] = y


def _peer(x1, mod_l, wq, k1p, k2p, u_tab, vt_tab, fnw, n_blocks, final_norm, dims):
    bt, ec = PEER_BLOCK, vt_tab.shape[2]
    nps, nbatch = dims["S"] // bt, dims["B"]
    n_chunks = vt_tab.shape[0]
    n_lt = bt // V7X_LANES
    rows_ih = P_NKEYS * P_HEADS
    mod_spec = pl.BlockSpec((1, 6, D_MODEL), lambda r, *_: (jnp.minimum(r // nps, nbatch), 0, 0))
    ih_shape = jax.ShapeDtypeStruct((n_blocks, n_lt, rows_ih, V7X_LANES), jnp.int32)
    hj_shape = jax.ShapeDtypeStruct((n_blocks, P_HEADS, P_NKEYS // 2, bt), jnp.int32)
    ih_spec = pl.BlockSpec((1, n_lt, rows_ih, V7X_LANES), lambda r, *_: (r, 0, 0, 0))
    hj_spec = pl.BlockSpec((1, P_HEADS, P_NKEYS // 2, bt), lambda r, *_: (r, 0, 0, 0))
    tok_spec = pl.BlockSpec((bt, D_MODEL), lambda r, *_: (r, 0))

    t_mod, need, e1, rank2, e2 = pl.pallas_call(
        _peer_select_kernel,
        grid=(n_blocks,),
        in_specs=[tok_spec, mod_spec,
                  pl.BlockSpec((D_MODEL, P_HEADS * V7X_LANES), lambda r: (0, 0)),
                  pl.BlockSpec((rows_ih, P_HEADS * V7X_LANES), lambda r: (0, 0)),
                  pl.BlockSpec((P_HEADS, P_NKEYS, V7X_LANES), lambda r: (0, 0, 0))],
        out_specs=(tok_spec, ih_spec, ih_spec, hj_spec, hj_spec),
        out_shape=(jax.ShapeDtypeStruct((n_blocks * bt, D_MODEL), BF16),
                   ih_shape,
                   ih_shape,
                   hj_shape,
                   hj_shape),
        scratch_shapes=[pltpu.VMEM((n_lt, rows_ih, V7X_LANES), F32)],
        compiler_params=_cparams(("arbitrary",)),
        name="peer_select",
    )(x1, mod_l, wq, k1p, k2p)

    return pl.pallas_call(
        functools.partial(_peer_dense_kernel, n_chunks=n_chunks, final_norm=final_norm),
        grid=(n_blocks, n_chunks),
        in_specs=[tok_spec, mod_spec, tok_spec, ih_spec, ih_spec, hj_spec, hj_spec,
                  pl.BlockSpec((ec, D_MODEL), lambda r, c: (0, 0)),
                  pl.BlockSpec((ec, D_MODEL), lambda r, c: (jnp.minimum(c + 1, n_chunks - 1), 0)),
                  pl.BlockSpec((1, D_MODEL, ec), lambda r, c: (c, 0, 0)),
                  pl.BlockSpec((1, D_MODEL), lambda r, c: (0, 0))],
        out_specs=tok_spec,
        out_shape=jax.ShapeDtypeStruct((n_blocks * bt, D_MODEL), F32),
        scratch_shapes=[pltpu.VMEM((ec, bt), F32),
                        pltpu.VMEM((ec, bt), BF16),
                        pltpu.VMEM((D_MODEL, bt), F32)],
        compiler_params=_cparams(("arbitrary", "arbitrary")),
        name="peer_dense",
    )(x1, mod_l, t_mod, need, e1, rank2, e2, u_tab, u_tab, vt_tab, fnw)


def _rope_tables(s_len, nbatch, n_ctx_rows):
    t = jnp.arange(s_len, dtype=jnp.int32)
    row = (t // GRID_W).astype(F32)
    col = (t % GRID_W).astype(F32)
    axis_dim = HEAD_DIM // 2
    inv = ROPE_THETA ** (-jnp.arange(0, axis_dim, 2, dtype=F32) / axis_dim)
    ar = row[:, None] * inv[None, :]
    ac = col[:, None] * inv[None, :]
    ang = jnp.concatenate([ar, ar, ac, ac], axis=-1)
    cos, sin = jnp.cos(ang), jnp.sin(ang)
    quarter = jnp.arange(HEAD_DIM) // (HEAD_DIM // 4)
    first = (quarter % 2 == 0)[None, :]
    sin_a = jnp.where(first, -sin, 0.0)
    sin_b = jnp.where(first, 0.0, sin)

    def expand(tab, ctx_fill):
        lat = jnp.tile(jnp.concatenate([tab, tab], axis=-1), (nbatch, 1))
        return jnp.concatenate([lat, jnp.full((n_ctx_rows, V7X_LANES), ctx_fill, F32)], axis=0)

    return expand(cos, 1.0), expand(sin_a, 0.0), expand(sin_b, 0.0)


def _padded_keys(keys_l):
    z = jnp.zeros_like(keys_l[:, 0])
    k0 = jnp.concatenate([keys_l[:, 0], z], axis=-1).transpose(1, 0, 2)
    eye = jnp.eye(P_HEADS, dtype=keys_l.dtype)
    k1p = (k0[:, :, None, :] * eye[None, :, :, None]).reshape(
        P_NKEYS * P_HEADS, P_HEADS * V7X_LANES)
    k2p = jnp.concatenate([z, keys_l[:, 1]], axis=-1)
    return k1p.astype(BF16), k2p.astype(BF16)


def _chunked_transpose(v_tab, ec):
    n_exp, d = v_tab.shape
    return v_tab.astype(BF16).reshape(n_exp // ec, ec, d).transpose(0, 2, 1)


def kernel(x, c, ctx, c_ctx, w_ada, b_ada, w_in, lam_q1, lam_k1, lam_q2, lam_k2, subln_w, q_norm_w, k_norm_w, conv_w, conv_b, conv_ln_w, conv_ln_b, w_branch_a, w_branch_b, w_branch_c, w_out, peer_wq, peer_keys, peer_u, peer_v, final_norm_w):
    nbatch, s_len, d = x.shape
    ctx_len = ctx.shape[1]
    depth = w_ada.shape[0]
    n_lat, n_ctx = nbatch * s_len, nbatch * ctx_len
    dims = {"B": nbatch, "S": s_len, "CTX": ctx_len, "T": n_lat + n_ctx}
    assert d == D_MODEL and nbatch + 1 <= MOD_ROWS
    assert s_len % PEER_BLOCK == 0 and s_len % ATT_TQ == 0 and ctx_len % TOK_BLOCK == 0
    assert ctx_len % min(ATT_TQ, ctx_len) == 0 and n_lat % min(ATT_TQ, ctx_len) == 0
    assert n_ctx % PEER_BLOCK == 0 and n_lat % ctx_len == 0

    cvec = jnp.zeros((MOD_ROWS, d), F32).at[:nbatch].set(c).at[nbatch].set(c_ctx)
    mod = _ada_rows(cvec, w_ada, b_ada).reshape(depth, MOD_ROWS, 6, d)
    tabs = _rope_tables(s_len, nbatch, n_ctx)
    tile2 = lambda v: jnp.concatenate([v, v], axis=-1)[None, :]

    xy = jnp.concatenate([x.reshape(n_lat, d), ctx.reshape(n_ctx, d)], axis=0)
    for l in range(depth):
        last = l == depth - 1
        lam_init = 0.8 - 0.6 * math.exp(-0.3 * l)
        w_bf = w_in[l].astype(BF16)
        lamv = jnp.stack([lam_q1[l], lam_k1[l], lam_q2[l], lam_k2[l]], axis=0)

        qa, ka, va, qb, kbd, vbd, u = _inproj(
            xy, mod[l], w_bf[:, :OFF_GATE], tabs, tile2(q_norm_w[l]), tile2(k_norm_w[l]), dims)
        oa, ob = _attention(qa, ka, va, qb, kbd, vbd, lamv, subln_w[l][None, :], lam_init,
                            not last, dims)
        n_rows = n_lat if last else n_lat + n_ctx
        cw = jnp.concatenate([conv_w[l], jnp.zeros((1, C_WIDTH), F32)], axis=0)
        x1 = _merge(xy, mod[l], oa, ob, u, cw, conv_b[l][None, :], conv_ln_w[l][None, :],
                    conv_ln_b[l][None, :], w_bf[:, OFF_GATE:], w_branch_a[l].astype(BF16),
                    w_branch_b[l].astype(BF16), w_branch_c[l].astype(BF16),
                    w_out[l].astype(BF16), n_rows // TOK_BLOCK, dims)
        k1p, k2p = _padded_keys(peer_keys[l])
        xy = _peer(x1, mod[l], peer_wq[l].astype(BF16), k1p, k2p,
                   peer_u[l].astype(BF16), _chunked_transpose(peer_v[l], PEER_ECHUNK),
                   final_norm_w[None, :],
                   n_rows // PEER_BLOCK, last, dims)
    return xy.reshape(nbatch, s_len, d)
```

```python
import functools
import math

import jax
import jax.numpy as jnp
from jax import lax
from jax.experimental import pallas as pl
from jax.experimental.pallas import tpu as pltpu

F32 = jnp.float32
BF16 = jnp.bfloat16

D_MODEL = 1024
GRID_W = 64
HEAD_DIM = 64
ROPE_THETA = 10000.0
EPS = 1e-6
A_HEADS = 4
B_KV = 2
C_WIDTH = 512
C_KW = 31
P_HEADS = 8
P_NKEYS = 128
P_TOPK = 16

COL_KA, COL_VA, COL_KB, COL_VB, COL_QA, COL_QB, COL_GLU = 512, 512, 128, 128, 512, 512, 1024
OFF_KA = 0
OFF_VA = OFF_KA + COL_KA
OFF_KB = OFF_VA + COL_VA
OFF_QA = OFF_KB + COL_KB + COL_VB
OFF_QB = OFF_QA + COL_QA
OFF_GLU = OFF_QB + COL_QB
OFF_GATE = OFF_GLU + COL_GLU

V7X_LANES = 128
V7X_VMEM_LIMIT = 56 * 1024 * 1024
HALO = 16

TOK_BLOCK = 256
ATT_TQ = 512
ATT_TK = 2048
PEER_BLOCK = 512
PEER_ECHUNK = 1024
MOD_ROWS = 8
ADA_COLS = 1536
ROW_GROUP = 16

Q_SCALE = (HEAD_DIM ** -0.5) * math.log2(math.e)
NEG_BIG = -1e30
NT_DIMS = (((1,), (1,)), ((), ()))


def _cparams(sem):
    return pltpu.CompilerParams(dimension_semantics=sem, vmem_limit_bytes=V7X_VMEM_LIMIT)


def _modulated(x, mod, shift_row, scale_row):
    ms = jnp.mean(x * x, axis=-1, keepdims=True)
    xn = x * lax.rsqrt(ms + EPS)
    return xn * (1.0 + mod[scale_row:scale_row + 1]) + mod[shift_row:shift_row + 1]


def _ada_kernel(c_ref, w_ref, b_ref, o_ref):
    c = c_ref[...]
    sc = c * jax.nn.sigmoid(c)
    o_ref[0] = jnp.dot(sc, w_ref[0], preferred_element_type=F32,
                       precision=lax.Precision.HIGHEST) + b_ref[0]


def _ada_rows(cvec, w_ada, b_ada):
    depth, d, n = w_ada.shape
    tn = ADA_COLS
    assert n % tn == 0
    return pl.pallas_call(
        _ada_kernel,
        grid=(depth, n // tn),
        in_specs=[pl.BlockSpec((MOD_ROWS, d), lambda l, j: (0, 0)),
                  pl.BlockSpec((1, d, tn), lambda l, j: (l, 0, j)),
                  pl.BlockSpec((1, 1, tn), lambda l, j: (l, 0, j))],
        out_specs=pl.BlockSpec((1, MOD_ROWS, tn), lambda l, j: (l, 0, j)),
        out_shape=jax.ShapeDtypeStruct((depth, MOD_ROWS, n), F32),
        compiler_params=_cparams(("arbitrary", "arbitrary")),
        name="ada_rows",
    )(cvec, w_ada, b_ada.reshape(depth, 1, n))


def _inproj_kernel(x_ref, mod_ref, w_ref, cos_ref, sa_ref, sb_ref, qnw_ref, knw_ref,
                   qa_ref, ka_ref, va_ref, qb_ref, kbd_ref, vbd_ref, u_ref):
    bt = x_ref.shape[0]
    h = _modulated(x_ref[...], mod_ref[0], 0, 1).astype(BF16)
    cos, sa, sb = cos_ref[...], sa_ref[...], sb_ref[...]
    lo = lax.broadcasted_iota(jnp.int32, (bt, V7X_LANES), 1) < HEAD_DIM

    def proj(c0, width):
        return jnp.dot(h, w_ref[:, c0:c0 + width], preferred_element_type=F32)

    def rope(x):
        return x * cos + pltpu.roll(x, V7X_LANES - 16, 1) * sa + pltpu.roll(x, 16, 1) * sb

    def headnorm(x, w):
        x2 = x * x
        s_lo = jnp.sum(jnp.where(lo, x2, 0.0), axis=-1, keepdims=True)
        s_hi = jnp.sum(jnp.where(lo, 0.0, x2), axis=-1, keepdims=True)
        r = jnp.where(lo, lax.rsqrt(s_lo / HEAD_DIM + EPS), lax.rsqrt(s_hi / HEAD_DIM + EPS))
        return x * r * w

    def tile(p, c):
        return p[:, c * V7X_LANES:(c + 1) * V7X_LANES]

    p = proj(OFF_KA, COL_KA)
    for c in range(A_HEADS):
        ka_ref[c] = rope(tile(p, c)).astype(BF16)

    p = proj(OFF_VA, COL_VA)
    for c in range(A_HEADS):
        va_ref[c] = tile(p, c).astype(BF16)

    p = proj(OFF_KB, COL_KB + COL_VB)
    kb = rope(headnorm(tile(p, 0), knw_ref[...]))
    kb_sw = pltpu.roll(kb, HEAD_DIM, 1)
    kbd_ref[0] = jnp.where(lo, kb, kb_sw).astype(BF16)
    kbd_ref[1] = jnp.where(lo, kb_sw, kb).astype(BF16)
    vb = tile(p, 1)
    vb_sw = pltpu.roll(vb, HEAD_DIM, 1)
    vbd_ref[0] = jnp.where(lo, vb, vb_sw).astype(BF16)
    vbd_ref[1] = jnp.where(lo, vb_sw, vb).astype(BF16)

    p = proj(OFF_QA, COL_QA)
    for c in range(COL_QA // V7X_LANES):
        qa_ref[:, c * V7X_LANES:(c + 1) * V7X_LANES] = (rope(tile(p, c)) * Q_SCALE).astype(BF16)

    p = proj(OFF_QB, COL_QB)
    qnw = qnw_ref[...]
    for c in range(COL_QB // V7X_LANES):
        qb_ref[:, c * V7X_LANES:(c + 1) * V7X_LANES] = (
            rope(headnorm(tile(p, c), qnw)) * Q_SCALE).astype(BF16)

    p = proj(OFF_GLU, COL_GLU)
    u_ref[...] = p[:, :C_WIDTH] * jax.nn.sigmoid(p[:, C_WIDTH:])


def _inproj(xy, mod_l, w_main, tabs, qnw, knw, dims):
    t_rows, nps, nbatch = dims["T"], dims["S"] // TOK_BLOCK, dims["B"]
    bt = TOK_BLOCK
    row = lambda r: (r, 0)
    full = lambda r: (0, 0)
    out_shapes = (
        jax.ShapeDtypeStruct((t_rows, COL_QA), BF16),
        jax.ShapeDtypeStruct((A_HEADS, t_rows, V7X_LANES), BF16),
        jax.ShapeDtypeStruct((A_HEADS, t_rows, V7X_LANES), BF16),
        jax.ShapeDtypeStruct((t_rows, COL_QB), BF16),
        jax.ShapeDtypeStruct((B_KV, t_rows, V7X_LANES), BF16),
        jax.ShapeDtypeStruct((B_KV, t_rows, V7X_LANES), BF16),
        jax.ShapeDtypeStruct((t_rows, C_WIDTH), F32),
    )
    grp = pl.BlockSpec((B_KV, bt, V7X_LANES), lambda r: (0, r, 0))
    heads = pl.BlockSpec((A_HEADS, bt, V7X_LANES), lambda r: (0, r, 0))
    return pl.pallas_call(
        _inproj_kernel,
        grid=(t_rows // bt,),
        in_specs=[pl.BlockSpec((bt, D_MODEL), row),
                  pl.BlockSpec((1, 6, D_MODEL), lambda r: (jnp.minimum(r // nps, nbatch), 0, 0)),
                  pl.BlockSpec((D_MODEL, OFF_GATE), full),
                  pl.BlockSpec((bt, V7X_LANES), row),
                  pl.BlockSpec((bt, V7X_LANES), row),
                  pl.BlockSpec((bt, V7X_LANES), row),
                  pl.BlockSpec((1, V7X_LANES), full),
                  pl.BlockSpec((1, V7X_LANES), full)],
        out_specs=(pl.BlockSpec((bt, COL_QA), row), heads, heads, pl.BlockSpec((bt, COL_QB), row),
                   grp, grp, pl.BlockSpec((bt, C_WIDTH), row)),
        out_shape=out_shapes,
        compiler_params=_cparams(("arbitrary",)),
        name="inproj",
    )(xy, mod_l, w_main, tabs[0], tabs[1], tabs[2], qnw, knw)


def _softmax_step(s, v, m, l, acc):
    mn = jnp.maximum(m, jnp.max(s, axis=-1, keepdims=True))
    p = jnp.exp2(s - mn)
    alpha = jnp.exp2(m - mn)
    l = alpha * l + jnp.sum(p, axis=-1, keepdims=True)
    acc = alpha * acc + jnp.dot(p.astype(BF16), v, preferred_element_type=F32)
    return mn, l, acc


def _flash_rows(q, load_lat, kc, vc, *, n_lat_chunks, tk):
    rows = q.shape[0]

    def step(k, v, carry):
        m, l, a = carry
        s = lax.dot_general(q, k, NT_DIMS, preferred_element_type=F32)
        return _softmax_step(s, v, m, l, a)

    carry = (jnp.full((rows, 1), NEG_BIG, F32), jnp.zeros((rows, 1), F32),
             jnp.zeros((rows, V7X_LANES), F32))
    for j in range(n_lat_chunks):
        carry = step(*load_lat(j * tk), carry)
    _, l, a = step(kc, vc, carry)
    return a / l


def _attn_a_kernel(q_ref, kl_ref, vl_ref, kc_ref, vc_ref, lamv_ref, sw_ref, o_ref, *,
                   n_lat_chunks, tk, lam_init):
    tq = q_ref.shape[0]
    q = q_ref[...]
    lo = lax.broadcasted_iota(jnp.int32, (tq, V7X_LANES), 1) < HEAD_DIM
    zero = jnp.zeros_like(q)
    q12 = jnp.concatenate([jnp.where(lo, q, zero), jnp.where(lo, zero, q)], axis=0)
    o12 = _flash_rows(q12, lambda off: (kl_ref[0, pl.ds(off, tk), :], vl_ref[0, pl.ds(off, tk), :]),
                      kc_ref[0], vc_ref[0], n_lat_chunks=n_lat_chunks, tk=tk)

    lv = lamv_ref[...]
    lam = (jnp.exp(jnp.sum(lv[0:1] * lv[1:2], axis=-1, keepdims=True))
           - jnp.exp(jnp.sum(lv[2:3] * lv[3:4], axis=-1, keepdims=True)) + lam_init)
    o = o12[:tq] - lam * o12[tq:]
    ms = jnp.mean(o * o, axis=-1, keepdims=True)
    o_ref[...] = (o * lax.rsqrt(ms + EPS) * sw_ref[...] * (1.0 - lam_init)).astype(BF16)


def _attn_b_kernel(q_ref, kl_ref, vl_ref, kc_ref, vc_ref, o_ref, *, n_lat_chunks, tk):
    tq = q_ref.shape[0]
    lo = lax.broadcasted_iota(jnp.int32, (tq, V7X_LANES), 1) < HEAD_DIM
    parts = []
    for c in range(2):
        qc = q_ref[:, c * V7X_LANES:(c + 1) * V7X_LANES]
        zero = jnp.zeros_like(qc)
        parts += [jnp.where(lo, qc, zero), jnp.where(lo, zero, qc)]
    q4 = jnp.concatenate(parts, axis=0)
    o = _flash_rows(q4, lambda off: (kl_ref[0, pl.ds(off, tk), :], vl_ref[0, pl.ds(off, tk), :]),
                    kc_ref[0], vc_ref[0], n_lat_chunks=n_lat_chunks, tk=tk)
    for c in range(2):
        o_ref[:, c * V7X_LANES:(c + 1) * V7X_LANES] = jnp.where(
            lo, o[(2 * c) * tq:(2 * c + 1) * tq], o[(2 * c + 1) * tq:(2 * c + 2) * tq]).astype(BF16)


def _attention_call(kern, name, q, k, v, extras, q_lanes, n_groups, tq, latent_queries, dims):
    s_len, ctx_len, nbatch = dims["S"], dims["CTX"], dims["B"]
    tk = min(ATT_TK, s_len)
    q_len = s_len if latent_queries else ctx_len
    nq = q_len // tq
    q_blk0 = 0 if latent_queries else (nbatch * s_len) // tq
    ctx_blk0 = (nbatch * s_len) // ctx_len
    extra_specs = [pl.BlockSpec(e.shape, lambda b, g, qi: (0, 0)) for e in extras]
    return pl.pallas_call(
        functools.partial(kern, n_lat_chunks=s_len // tk if latent_queries else 0, tk=tk),
        grid=(nbatch, n_groups, nq),
        in_specs=[pl.BlockSpec((tq, q_lanes), lambda b, g, qi: (q_blk0 + b * nq + qi, g)),
                  pl.BlockSpec((1, s_len, V7X_LANES), lambda b, g, qi: (g, b, 0)),
                  pl.BlockSpec((1, s_len, V7X_LANES), lambda b, g, qi: (g, b, 0)),
                  pl.BlockSpec((1, ctx_len, V7X_LANES), lambda b, g, qi: (g, ctx_blk0 + b, 0)),
                  pl.BlockSpec((1, ctx_len, V7X_LANES), lambda b, g, qi: (g, ctx_blk0 + b, 0)),
                  *extra_specs],
        out_specs=pl.BlockSpec((tq, q_lanes), lambda b, g, qi: (b * nq + qi, g)),
        out_shape=jax.ShapeDtypeStruct((nbatch * q_len, n_groups * q_lanes), BF16),
        compiler_params=_cparams(("arbitrary", "arbitrary", "arbitrary")),
        name=name,
    )(q, k, v, k, v, *extras)


def _attention(qa, ka, va, qb, kbd, vbd, lamv, subln_w, lam_init, with_ctx_queries, dims):
    kern_a = functools.partial(_attn_a_kernel, lam_init=lam_init)
    gw = 2 * V7X_LANES

    def both(latent, tag):
        q_len = dims["S"] if latent else dims["CTX"]
        oa = _attention_call(kern_a, "attn_a" + tag, qa, ka, va, (lamv, subln_w), V7X_LANES,
                             A_HEADS, min(ATT_TQ, q_len), latent, dims)
        ob = _attention_call(_attn_b_kernel, "attn_b" + tag, qb, kbd, vbd, (), gw, B_KV,
                             min(ATT_TQ // 2, q_len), latent, dims)
        return oa, ob

    oa, ob = both(True, "")
    if with_ctx_queries:
        oa_c, ob_c = both(False, "_ctx")
        oa, ob = jnp.concatenate([oa, oa_c], axis=0), jnp.concatenate([ob, ob_c], axis=0)
    return oa, ob


def _merge_kernel(x_ref, mod_ref, oa_ref, ob_ref, u_ref, up_ref, un_ref, cw_ref, cb_ref,
                  lnw_ref, lnb_ref, wg_ref, wa_ref, wb_ref, wc_ref, wo_ref, o_ref, uext_ref, *,
                  nps, ncps, n_lat_blocks):
    bt = x_ref.shape[0]
    r = pl.program_id(0)
    is_lat = r < n_lat_blocks
    pos = jnp.where(is_lat, r % nps, (r - n_lat_blocks) % ncps)
    last_pos = jnp.where(is_lat, nps - 1, ncps - 1)
    keep_prev = (pos != 0).astype(F32)
    keep_next = (pos != last_pos).astype(F32)

    x = x_ref[...]
    mod = mod_ref[0]
    h = _modulated(x, mod, 0, 1).astype(BF16)

    uext_ref[0:HALO, :] = up_ref[...] * keep_prev
    uext_ref[HALO:HALO + bt, :] = u_ref[...]
    uext_ref[HALO + bt:2 * HALO + bt, :] = un_ref[...] * keep_next
    cw = cw_ref[...]
    y = jnp.zeros((bt, C_WIDTH), F32) + cb_ref[...]
    base = HALO - C_KW // 2
    for k in range(C_KW):
        y = y + cw[k:k + 1, :] * uext_ref[base + k:base + k + bt, :]
    mu = jnp.mean(y, axis=-1, keepdims=True)
    yc = y - mu
    var = jnp.mean(yc * yc, axis=-1, keepdims=True)
    yn = yc * lax.rsqrt(var + EPS) * lnw_ref[...] + lnb_ref[...]
    oc = (yn * jax.nn.sigmoid(yn)).astype(BF16)

    def gate(i):
        logits = jnp.dot(h, wg_ref[:, i * D_MODEL:(i + 1) * D_MODEL], preferred_element_type=F32)
        return jax.nn.sigmoid(logits)

    merged = gate(0) * jnp.dot(oa_ref[...], wa_ref[...], preferred_element_type=F32)
    merged = merged + gate(1) * jnp.dot(ob_ref[...], wb_ref[...], preferred_element_type=F32)
    merged = merged + gate(2) * jnp.dot(oc, wc_ref[...], preferred_element_type=F32)
    out = jnp.dot(merged.astype(BF16), wo_ref[...], preferred_element_type=F32)
    o_ref[...] = x + mod[2:3] * out


def _merge(xy, mod_l, oa, ob, u, cw, cb, lnw, lnb, wg, wa, wb, wc, wo, n_blocks, dims):
    bt = TOK_BLOCK
    t_rows, nps, nbatch = dims["T"], dims["S"] // bt, dims["B"]
    hpb = bt // HALO
    n_halo = t_rows // HALO
    row = lambda r: (r, 0)
    full = lambda r: (0, 0)
    return pl.pallas_call(
        functools.partial(_merge_kernel, nps=nps, ncps=dims["CTX"] // bt,
                          n_lat_blocks=nbatch * nps),
        grid=(n_blocks,),
        in_specs=[pl.BlockSpec((bt, D_MODEL), row),
                  pl.BlockSpec((1, 6, D_MODEL), lambda r: (jnp.minimum(r // nps, nbatch), 0, 0)),
                  pl.BlockSpec((bt, 4 * V7X_LANES), row),
                  pl.BlockSpec((bt, 4 * V7X_LANES), row),
                  pl.BlockSpec((bt, C_WIDTH), row),
                  pl.BlockSpec((HALO, C_WIDTH), lambda r: (jnp.maximum(r * hpb - 1, 0), 0)),
                  pl.BlockSpec((HALO, C_WIDTH), lambda r: (jnp.minimum((r + 1) * hpb, n_halo - 1), 0)),
                  pl.BlockSpec((C_KW + 1, C_WIDTH), full),
                  pl.BlockSpec((1, C_WIDTH), full),
                  pl.BlockSpec((1, C_WIDTH), full),
                  pl.BlockSpec((1, C_WIDTH), full),
                  pl.BlockSpec((D_MODEL, 3 * D_MODEL), full),
                  pl.BlockSpec((C_WIDTH, D_MODEL), full),
                  pl.BlockSpec((C_WIDTH, D_MODEL), full),
                  pl.BlockSpec((C_WIDTH, D_MODEL), full),
                  pl.BlockSpec((D_MODEL, D_MODEL), full)],
        out_specs=pl.BlockSpec((bt, D_MODEL), row),
        out_shape=jax.ShapeDtypeStruct((n_blocks * bt, D_MODEL), F32),
        scratch_shapes=[pltpu.VMEM((bt + 2 * HALO, C_WIDTH), F32)],
        compiler_params=_cparams(("arbitrary",)),
        name="merge",
    )(xy, mod_l, oa, ob, u, u, u, cw, cb, lnw, lnb, wg, wa, wb, wc, wo)


def _topk_desc(s, k):
    row = lax.broadcasted_iota(jnp.int32, (k, s.shape[1]), 0)

    def body(i, carry):
        prev, vals = carry
        m = jnp.max(jnp.where(s < prev, s, -jnp.inf), axis=0, keepdims=True)
        return m, jnp.where(row == i, m, vals)

    init = (jnp.full((1, s.shape[1]), jnp.inf, F32), jnp.zeros((k, s.shape[1]), F32))
    _, vals = lax.fori_loop(0, k, body, init)
    return vals


def _candidate_sums(sv1, sv2):
    half = P_TOPK // 2
    lead, rest = sv1[:half], sv1[half:]
    row = lax.broadcasted_iota(jnp.int32, lead.shape, 0)
    tiles = [lead + sv2[0:1], rest + sv2[0:1], lead + sv2[1:2]]
    for k2 in range(2, half):
        tiles.append(jnp.where(row < P_TOPK // (k2 + 1), lead + sv2[k2:k2 + 1], -jnp.inf))
    tiles.append(sv1[0:1] + sv2[half:])
    return jnp.concatenate(tiles, axis=0)


def _bf16_pair_words(x):
    bits = pltpu.bitcast(x.astype(BF16).astype(F32), jnp.int32)
    hi = lax.shift_right_logical(bits, 16)
    return hi | lax.shift_left(hi, 16)


def _peer_select_kernel(x_ref, mod_ref, wq_ref, k1_ref, k2_ref,
                        t_ref, need_ref, e1_ref, r2_ref, e2_ref, s1_scr):
    bt = x_ref.shape[0]
    n_lt = bt // V7X_LANES
    hb = _modulated(x_ref[...], mod_ref[0], 3, 4).astype(BF16)
    t_ref[...] = hb
    q = jnp.dot(hb, wq_ref[...], preferred_element_type=F32).astype(BF16)
    s1p = lax.dot_general(k1_ref[...], q, NT_DIMS, preferred_element_type=F32)
    for lt in range(n_lt):
        s1_scr[lt] = s1p[:, lt * V7X_LANES:(lt + 1) * V7X_LANES]
    head_row = lax.broadcasted_iota(jnp.int32, (P_HEADS, bt), 0)
    zero_t = jnp.zeros((P_HEADS, bt), F32)
    thr1, max1, tau_t = zero_t, zero_t, zero_t
    sv2_t = [zero_t] * P_TOPK
    for hh in range(P_HEADS):
        mine = head_row == hh
        qh = q[:, hh * V7X_LANES:(hh + 1) * V7X_LANES]
        s1 = jnp.concatenate([s1_scr[lt, pl.ds(hh, P_NKEYS, stride=P_HEADS), :]
                              for lt in range(n_lt)], axis=1)
        s2 = lax.dot_general(k2_ref[hh], qh, NT_DIMS, preferred_element_type=F32)
        sv1 = _topk_desc(s1, P_TOPK)
        sv2 = _topk_desc(s2, P_TOPK)
        top = _topk_desc(_candidate_sums(sv1, sv2), P_TOPK)
        z = jnp.sum(jnp.exp(top - top[0:1]), axis=0, keepdims=True)
        rank2 = jnp.zeros_like(s2)
        for k in range(P_TOPK):
            rank2 = rank2 + jnp.where(sv2[k:k + 1] > s2, 1.0, 0.0)
            sv2_t[k] = jnp.where(mine, sv2[k:k + 1], sv2_t[k])
        r2_ref[0, hh] = pltpu.bitcast(rank2.astype(BF16), jnp.int32)
        e2_ref[0, hh] = pltpu.bitcast((jnp.exp(s2 - sv2[0:1]) / z).astype(BF16), jnp.int32)
        tau_t = jnp.where(mine, top[P_TOPK - 1:P_TOPK], tau_t)
        thr1 = jnp.where(mine, sv1[P_TOPK - 1:P_TOPK], thr1)
        max1 = jnp.where(mine, sv1[0:1], max1)
    for lt in range(n_lt):
        ls = slice(lt * V7X_LANES, (lt + 1) * V7X_LANES)
        s1t = s1_scr[lt].reshape(P_NKEYS, P_HEADS, V7X_LANES)
        count = jnp.zeros_like(s1t)
        for k in range(P_TOPK):
            count = count + jnp.where(s1t + sv2_t[k][None, :, ls] >= tau_t[None, :, ls], 1.0, 0.0)
        need = jnp.where(s1t >= thr1[None, :, ls], count, 0.0)
        need_ref[0, lt] = _bf16_pair_words(need.reshape(P_NKEYS * P_HEADS, V7X_LANES))
        e1 = jnp.exp(s1t - max1[None, :, ls])
        e1_ref[0, lt] = _bf16_pair_words(e1.reshape(P_NKEYS * P_HEADS, V7X_LANES))


def _peer_dense_kernel(x_ref, mod_ref, t_ref, need_ref, e1_ref, r2_ref, e2_ref, u0_ref, u_ref,
                       vt_ref, fnw_ref, o_ref, a_scr, m_scr, acc_scr, *, n_chunks, final_norm):
    bt = x_ref.shape[0]
    c = pl.program_id(1)
    ipc = u_ref.shape[0] // P_NKEYS
    n_lt = bt // V7X_LANES

    @pl.when(c == 0)
    def _init():
        acc_scr[...] = jnp.zeros_like(acc_scr)
        a_scr[...] = lax.dot_general(u0_ref[...], t_ref[...], NT_DIMS, preferred_element_type=F32)


    def one_key(ii, _):
        roff = ii * P_NKEYS
        hrow = pl.ds(pl.multiple_of((c * ipc + ii) * P_HEADS, P_HEADS), P_HEADS)
        for lt in range(n_lt):
            ls = slice(lt * V7X_LANES, (lt + 1) * V7X_LANES)
            w = [jnp.zeros((ROW_GROUP, V7X_LANES), BF16) for _ in range(P_NKEYS // ROW_GROUP)]
            need_t = need_ref[0, lt, hrow, :]
            e1_t = e1_ref[0, lt, hrow, :]
            zero = jnp.zeros((ROW_GROUP, V7X_LANES), BF16)
            for hh in range(P_HEADS):
                words = (ROW_GROUP // 2, V7X_LANES)
                need_b = pltpu.bitcast(jnp.broadcast_to(need_t[hh:hh + 1], words), BF16)
                e1_b = pltpu.bitcast(jnp.broadcast_to(e1_t[hh:hh + 1], words), BF16)
                for jg in range(P_NKEYS // ROW_GROUP):
                    js = slice(jg * (ROW_GROUP // 2), (jg + 1) * (ROW_GROUP // 2))
                    rank2 = pltpu.bitcast(r2_ref[0, hh, js, ls], BF16)
                    e2 = pltpu.bitcast(e2_ref[0, hh, js, ls], BF16)
                    w[jg] = w[jg] + jnp.where(rank2 < need_b, e1_b * e2, zero)
            for jg in range(P_NKEYS // ROW_GROUP):
                rows = pl.ds(roff + jg * ROW_GROUP, ROW_GROUP)
                a = a_scr[rows, ls]
                g = 0.5 * a * (1.0 + lax.erf(a * (2.0 ** -0.5)))
                m_scr[rows, ls] = g.astype(BF16) * w[jg]
        return 0

    for ii in range(ipc):
        one_key(ii, 0)

    acc_scr[...] += jnp.dot(vt_ref[0], m_scr[...], preferred_element_type=F32)
    a_scr[...] = lax.dot_general(u_ref[...], t_ref[...], NT_DIMS, preferred_element_type=F32)

    @pl.when(c == n_chunks - 1)
    def _finish():
        x = x_ref[...]
        y = x + mod_ref[0][5:6] * acc_scr[...].T
        if final_norm:
            ms = jnp.mean(y * y, axis=-1, keepdims=True)
            y = y * lax.rsqrt(ms + EPS) * fnw_ref[...]
        o_ref[...] = y


def _peer(x1, mod_l, wq, k1p, k2p, u_tab, vt_tab, fnw, n_blocks, final_norm, dims):
    bt, ec = PEER_BLOCK, vt_tab.shape[2]
    nps, nbatch = dims["S"] // bt, dims["B"]
    n_chunks = vt_tab.shape[0]
    n_lt = bt // V7X_LANES
    rows_ih = P_NKEYS * P_HEADS
    mod_spec = pl.BlockSpec((1, 6, D_MODEL), lambda r, *_: (jnp.minimum(r // nps, nbatch), 0, 0))
    ih_shape = jax.ShapeDtypeStruct((n_blocks, n_lt, rows_ih, V7X_LANES), jnp.int32)
    hj_shape = jax.ShapeDtypeStruct((n_blocks, P_HEADS, P_NKEYS // 2, bt), jnp.int32)
    ih_spec = pl.BlockSpec((1, n_lt, rows_ih, V7X_LANES), lambda r, *_: (r, 0, 0, 0))
    hj_spec = pl.BlockSpec((1, P_HEADS, P_NKEYS // 2, bt), lambda r, *_: (r, 0, 0, 0))
    tok_spec = pl.BlockSpec((bt, D_MODEL), lambda r, *_: (r, 0))

    t_mod, need, e1, rank2, e2 = pl.pallas_call(
        _peer_select_kernel,
        grid=(n_blocks,),
        in_specs=[tok_spec, mod_spec,
                  pl.BlockSpec((D_MODEL, P_HEADS * V7X_LANES), lambda r: (0, 0)),
                  pl.BlockSpec((rows_ih, P_HEADS * V7X_LANES), lambda r: (0, 0)),
                  pl.BlockSpec((P_HEADS, P_NKEYS, V7X_LANES), lambda r: (0, 0, 0))],
        out_specs=(tok_spec, ih_spec, ih_spec, hj_spec, hj_spec),
        out_shape=(jax.ShapeDtypeStruct((n_blocks * bt, D_MODEL), BF16),
                   ih_shape,
                   ih_shape,
                   hj_shape,
                   hj_shape),
        scratch_shapes=[pltpu.VMEM((n_lt, rows_ih, V7X_LANES), F32)],
        compiler_params=_cparams(("arbitrary",)),
        name="peer_select",
    )(x1, mod_l, wq, k1p, k2p)

    return pl.pallas_call(
        functools.partial(_peer_dense_kernel, n_chunks=n_chunks, final_norm=final_norm),
        grid=(n_blocks, n_chunks),
        in_specs=[tok_spec, mod_spec, tok_spec, ih_spec, ih_spec, hj_spec, hj_spec,
                  pl.BlockSpec((ec, D_MODEL), lambda r, c: (0, 0)),
                  pl.BlockSpec((ec, D_MODEL), lambda r, c: (jnp.minimum(c + 1, n_chunks - 1), 0)),
                  pl.BlockSpec((1, D_MODEL, ec), lambda r, c: (c, 0, 0)),
                  pl.BlockSpec((1, D_MODEL), lambda r, c: (0, 0))],
        out_specs=tok_spec,
        out_shape=jax.ShapeDtypeStruct((n_blocks * bt, D_MODEL), F32),
        scratch_shapes=[pltpu.VMEM((ec, bt), F32),
                        pltpu.VMEM((ec, bt), BF16),
                        pltpu.VMEM((D_MODEL, bt), F32)],
        compiler_params=_cparams(("arbitrary", "arbitrary")),
        name="peer_dense",
    )(x1, mod_l, t_mod, need, e1, rank2, e2, u_tab, u_tab, vt_tab, fnw)


def _rope_tables(s_len, nbatch, n_ctx_rows):
    t = jnp.arange(s_len, dtype=jnp.int32)
    row = (t // GRID_W).astype(F32)
    col = (t % GRID_W).astype(F32)
    axis_dim = HEAD_DIM // 2
    inv = ROPE_THETA ** (-jnp.arange(0, axis_dim, 2, dtype=F32) / axis_dim)
    ar = row[:, None] * inv[None, :]
    ac = col[:, None] * inv[None, :]
    ang = jnp.concatenate([ar, ar, ac, ac], axis=-1)
    cos, sin = jnp.cos(ang), jnp.sin(ang)
    quarter = jnp.arange(HEAD_DIM) // (HEAD_DIM // 4)
    first = (quarter % 2 == 0)[None, :]
    sin_a = jnp.where(first, -sin, 0.0)
    sin_b = jnp.where(first, 0.0, sin)

    def expand(tab, ctx_fill):
        lat = jnp.tile(jnp.concatenate([tab, tab], axis=-1), (nbatch, 1))
        return jnp.concatenate([lat, jnp.full((n_ctx_rows, V7X_LANES), ctx_fill, F32)], axis=0)

    return expand(cos, 1.0), expand(sin_a, 0.0), expand(sin_b, 0.0)


def _padded_keys(keys_l):
    z = jnp.zeros_like(keys_l[:, 0])
    k0 = jnp.concatenate([keys_l[:, 0], z], axis=-1).transpose(1, 0, 2)
    eye = jnp.eye(P_HEADS, dtype=keys_l.dtype)
    k1p = (k0[:, :, None, :] * eye[None, :, :, None]).reshape(
        P_NKEYS * P_HEADS, P_HEADS * V7X_LANES)
    k2p = jnp.concatenate([z, keys_l[:, 1]], axis=-1)
    return k1p.astype(BF16), k2p.astype(BF16)


def _chunked_transpose(v_tab, ec):
    n_exp, d = v_tab.shape
    return v_tab.astype(BF16).reshape(n_exp // ec, ec, d).transpose(0, 2, 1)


def kernel(x, c, ctx, c_ctx, w_ada, b_ada, w_in, lam_q1, lam_k1, lam_q2, lam_k2, subln_w, q_norm_w, k_norm_w, conv_w, conv_b, conv_ln_w, conv_ln_b, w_branch_a, w_branch_b, w_branch_c, w_out, peer_wq, peer_keys, peer_u, peer_v, final_norm_w):
    nbatch, s_len, d = x.shape
    ctx_len = ctx.shape[1]
    depth = w_ada.shape[0]
    n_lat, n_ctx = nbatch * s_len, nbatch * ctx_len
    dims = {"B": nbatch, "S": s_len, "CTX": ctx_len, "T": n_lat + n_ctx}
    assert d == D_MODEL and nbatch + 1 <= MOD_ROWS
    assert s_len % PEER_BLOCK == 0 and s_len % ATT_TQ == 0 and ctx_len % TOK_BLOCK == 0
    assert ctx_len % min(ATT_TQ, ctx_len) == 0 and n_lat % min(ATT_TQ, ctx_len) == 0
    assert n_ctx % PEER_BLOCK == 0 and n_lat % ctx_len == 0

    cvec = jnp.zeros((MOD_ROWS, d), F32).at[:nbatch].set(c).at[nbatch].set(c_ctx)
    mod = _ada_rows(cvec, w_ada, b_ada).reshape(depth, MOD_ROWS, 6, d)
    tabs = _rope_tables(s_len, nbatch, n_ctx)
    tile2 = lambda v: jnp.concatenate([v, v], axis=-1)[None, :]

    xy = jnp.concatenate([x.reshape(n_lat, d), ctx.reshape(n_ctx, d)], axis=0)
    for l in range(depth):
        last = l == depth - 1
        lam_init = 0.8 - 0.6 * math.exp(-0.3 * l)
        w_bf = w_in[l].astype(BF16)
        lamv = jnp.stack([lam_q1[l], lam_k1[l], lam_q2[l], lam_k2[l]], axis=0)

        qa, ka, va, qb, kbd, vbd, u = _inproj(
            xy, mod[l], w_bf[:, :OFF_GATE], tabs, tile2(q_norm_w[l]), tile2(k_norm_w[l]), dims)
        oa, ob = _attention(qa, ka, va, qb, kbd, vbd, lamv, subln_w[l][None, :], lam_init,
                            not last, dims)
        n_rows = n_lat if last else n_lat + n_ctx
        cw = jnp.concatenate([conv_w[l], jnp.zeros((1, C_WIDTH), F32)], axis=0)
        x1 = _merge(xy, mod[l], oa, ob, u, cw, conv_b[l][None, :], conv_ln_w[l][None, :],
                    conv_ln_b[l][None, :], w_bf[:, OFF_GATE:], w_branch_a[l].astype(BF16),
                    w_branch_b[l].astype(BF16), w_branch_c[l].astype(BF16),
                    w_out[l].astype(BF16), n_rows // TOK_BLOCK, dims)
        k1p, k2p = _padded_keys(peer_keys[l])
        xy = _peer(x1, mod[l], peer_wq[l].astype(BF16), k1p, k2p,
                   peer_u[l].astype(BF16), _chunked_transpose(peer_v[l], PEER_ECHUNK),
                   final_norm_w[None, :],
                   n_rows // PEER_BLOCK, last, dims)
    return xy.reshape(nbatch, s_len, d)
```

```python
import functools
import math

import jax
import jax.numpy as jnp
from jax import lax
from jax.experimental import pallas as pl
from jax.experimental.pallas import tpu as pltpu

F32 = jnp.float32
BF16 = jnp.bfloat16

D_MODEL = 1024
GRID_W = 64
HEAD_DIM = 64
ROPE_THETA = 10000.0
EPS = 1e-6
A_HEADS = 4
B_KV = 2
C_WIDTH = 512
C_KW = 31
P_HEADS = 8
P_NKEYS = 128
P_TOPK = 16

COL_KA, COL_VA, COL_KB, COL_VB, COL_QA, COL_QB, COL_GLU = 512, 512, 128, 128, 512, 512, 1024
OFF_KA = 0
OFF_VA = OFF_KA + COL_KA
OFF_KB = OFF_VA + COL_VA
OFF_QA = OFF_KB + COL_KB + COL_VB
OFF_QB = OFF_QA + COL_QA
OFF_GLU = OFF_QB + COL_QB
OFF_GATE = OFF_GLU + COL_GLU

V7X_LANES = 128
V7X_VMEM_LIMIT = 56 * 1024 * 1024
HALO = 16

TOK_BLOCK = 256
ATT_TQ = 512
ATT_TK = 2048
PEER_BLOCK = 512
PEER_ECHUNK = 1024
MOD_ROWS = 8
ADA_COLS = 1536
ROW_GROUP = 16

Q_SCALE = (HEAD_DIM ** -0.5) * math.log2(math.e)
NEG_BIG = -1e30
NT_DIMS = (((1,), (1,)), ((), ()))


def _cparams(sem):
    return pltpu.CompilerParams(dimension_semantics=sem, vmem_limit_bytes=V7X_VMEM_LIMIT)


def _modulated(x, mod, shift_row, scale_row):
    ms = jnp.mean(x * x, axis=-1, keepdims=True)
    xn = x * lax.rsqrt(ms + EPS)
    return xn * (1.0 + mod[scale_row:scale_row + 1]) + mod[shift_row:shift_row + 1]


def _ada_kernel(c_ref, w_ref, b_ref, o_ref):
    c = c_ref[...]
    sc = c * jax.nn.sigmoid(c)
    o_ref[0] = jnp.dot(sc, w_ref[0], preferred_element_type=F32,
                       precision=lax.Precision.HIGHEST) + b_ref[0]


def _ada_rows(cvec, w_ada, b_ada):
    depth, d, n = w_ada.shape
    tn = ADA_COLS
    assert n % tn == 0
    return pl.pallas_call(
        _ada_kernel,
        grid=(depth, n // tn),
        in_specs=[pl.BlockSpec((MOD_ROWS, d), lambda l, j: (0, 0)),
                  pl.BlockSpec((1, d, tn), lambda l, j: (l, 0, j)),
                  pl.BlockSpec((1, 1, tn), lambda l, j: (l, 0, j))],
        out_specs=pl.BlockSpec((1, MOD_ROWS, tn), lambda l, j: (l, 0, j)),
        out_shape=jax.ShapeDtypeStruct((depth, MOD_ROWS, n), F32),
        compiler_params=_cparams(("arbitrary", "arbitrary")),
        name="ada_rows",
    )(cvec, w_ada, b_ada.reshape(depth, 1, n))


def _inproj_kernel(x_ref, mod_ref, w_ref, cos_ref, sa_ref, sb_ref, qnw_ref, knw_ref,
                   qa_ref, ka_ref, va_ref, qb_ref, kbd_ref, vbd_ref, u_ref):
    bt = x_ref.shape[0]
    h = _modulated(x_ref[...], mod_ref[0], 0, 1).astype(BF16)
    cos, sa, sb = cos_ref[...], sa_ref[...], sb_ref[...]
    lo = lax.broadcasted_iota(jnp.int32, (bt, V7X_LANES), 1) < HEAD_DIM

    def proj(c0, width):
        return jnp.dot(h, w_ref[:, c0:c0 + width], preferred_element_type=F32)

    def rope(x):
        return x * cos + pltpu.roll(x, V7X_LANES - 16, 1) * sa + pltpu.roll(x, 16, 1) * sb

    def headnorm(x, w):
        x2 = x * x
        s_lo = jnp.sum(jnp.where(lo, x2, 0.0), axis=-1, keepdims=True)
        s_hi = jnp.sum(jnp.where(lo, 0.0, x2), axis=-1, keepdims=True)
        r = jnp.where(lo, lax.rsqrt(s_lo / HEAD_DIM + EPS), lax.rsqrt(s_hi / HEAD_DIM + EPS))
        return x * r * w

    def tile(p, c):
        return p[:, c * V7X_LANES:(c + 1) * V7X_LANES]

    p = proj(OFF_KA, COL_KA)
    for c in range(A_HEADS):
        ka_ref[c] = rope(tile(p, c)).astype(BF16)

    p = proj(OFF_VA, COL_VA)
    for c in range(A_HEADS):
        va_ref[c] = tile(p, c).astype(BF16)

    p = proj(OFF_KB, COL_KB + COL_VB)
    kb = rope(headnorm(tile(p, 0), knw_ref[...]))
    kb_sw = pltpu.roll(kb, HEAD_DIM, 1)
    kbd_ref[0] = jnp.where(lo, kb, kb_sw).astype(BF16)
    kbd_ref[1] = jnp.where(lo, kb_sw, kb).astype(BF16)
    vb = tile(p, 1)
    vb_sw = pltpu.roll(vb, HEAD_DIM, 1)
    vbd_ref[0] = jnp.where(lo, vb, vb_sw).astype(BF16)
    vbd_ref[1] = jnp.where(lo, vb_sw, vb).astype(BF16)

    p = proj(OFF_QA, COL_QA)
    for c in range(COL_QA // V7X_LANES):
        qa_ref[:, c * V7X_LANES:(c + 1) * V7X_LANES] = (rope(tile(p, c)) * Q_SCALE).astype(BF16)

    p = proj(OFF_QB, COL_QB)
    qnw = qnw_ref[...]
    for c in range(COL_QB // V7X_LANES):
        qb_ref[:, c * V7X_LANES:(c + 1) * V7X_LANES] = (
            rope(headnorm(tile(p, c), qnw)) * Q_SCALE).astype(BF16)

    p = proj(OFF_GLU, COL_GLU)
    u_ref[...] = p[:, :C_WIDTH] * jax.nn.sigmoid(p[:, C_WIDTH:])


def _inproj(xy, mod_l, w_main, tabs, qnw, knw, dims):
    t_rows, nps, nbatch = dims["T"], dims["S"] // TOK_BLOCK, dims["B"]
    bt = TOK_BLOCK
    row = lambda r: (r, 0)
    full = lambda r: (0, 0)
    out_shapes = (
        jax.ShapeDtypeStruct((t_rows, COL_QA), BF16),
        jax.ShapeDtypeStruct((A_HEADS, t_rows, V7X_LANES), BF16),
        jax.ShapeDtypeStruct((A_HEADS, t_rows, V7X_LANES), BF16),
        jax.ShapeDtypeStruct((t_rows, COL_QB), BF16),
        jax.ShapeDtypeStruct((B_KV, t_rows, V7X_LANES), BF16),
        jax.ShapeDtypeStruct((B_KV, t_rows, V7X_LANES), BF16),
        jax.ShapeDtypeStruct((t_rows, C_WIDTH), F32),
    )
    grp = pl.BlockSpec((B_KV, bt, V7X_LANES), lambda r: (0, r, 0))
    heads = pl.BlockSpec((A_HEADS, bt, V7X_LANES), lambda r: (0, r, 0))
    return pl.pallas_call(
        _inproj_kernel,
        grid=(t_rows // bt,),
        in_specs=[pl.BlockSpec((bt, D_MODEL), row),
                  pl.BlockSpec((1, 6, D_MODEL), lambda r: (jnp.minimum(r // nps, nbatch), 0, 0)),
                  pl.BlockSpec((D_MODEL, OFF_GATE), full),
                  pl.BlockSpec((bt, V7X_LANES), row),
                  pl.BlockSpec((bt, V7X_LANES), row),
                  pl.BlockSpec((bt, V7X_LANES), row),
                  pl.BlockSpec((1, V7X_LANES), full),
                  pl.BlockSpec((1, V7X_LANES), full)],
        out_specs=(pl.BlockSpec((bt, COL_QA), row), heads, heads, pl.BlockSpec((bt, COL_QB), row),
                   grp, grp, pl.BlockSpec((bt, C_WIDTH), row)),
        out_shape=out_shapes,
        compiler_params=_cparams(("arbitrary",)),
        name="inproj",
    )(xy, mod_l, w_main, tabs[0], tabs[1], tabs[2], qnw, knw)


def _softmax_step(s, v, m, l, acc):
    mn = jnp.maximum(m, jnp.max(s, axis=-1, keepdims=True))
    p = jnp.exp2(s - mn)
    alpha = jnp.exp2(m - mn)
    l = alpha * l + jnp.sum(p, axis=-1, keepdims=True)
    acc = alpha * acc + jnp.dot(p.astype(BF16), v, preferred_element_type=F32)
    return mn, l, acc


def _flash_rows(q, load_lat, kc, vc, *, n_lat_chunks, tk):
    rows = q.shape[0]

    def step(k, v, carry):
        m, l, a = carry
        s = lax.dot_general(q, k, NT_DIMS, preferred_element_type=F32)
        return _softmax_step(s, v, m, l, a)

    carry = (jnp.full((rows, 1), NEG_BIG, F32), jnp.zeros((rows, 1), F32),
             jnp.zeros((rows, V7X_LANES), F32))
    for j in range(n_lat_chunks):
        carry = step(*load_lat(j * tk), carry)
    _, l, a = step(kc, vc, carry)
    return a / l


def _attn_a_kernel(q_ref, kl_ref, vl_ref, kc_ref, vc_ref, lamv_ref, sw_ref, o_ref, *,
                   n_lat_chunks, tk, lam_init):
    tq = q_ref.shape[0]
    q = q_ref[...]
    lo = lax.broadcasted_iota(jnp.int32, (tq, V7X_LANES), 1) < HEAD_DIM
    zero = jnp.zeros_like(q)
    q12 = jnp.concatenate([jnp.where(lo, q, zero), jnp.where(lo, zero, q)], axis=0)
    o12 = _flash_rows(q12, lambda off: (kl_ref[0, pl.ds(off, tk), :], vl_ref[0, pl.ds(off, tk), :]),
                      kc_ref[0], vc_ref[0], n_lat_chunks=n_lat_chunks, tk=tk)

    lv = lamv_ref[...]
    lam = (jnp.exp(jnp.sum(lv[0:1] * lv[1:2], axis=-1, keepdims=True))
           - jnp.exp(jnp.sum(lv[2:3] * lv[3:4], axis=-1, keepdims=True)) + lam_init)
    o = o12[:tq] - lam * o12[tq:]
    ms = jnp.mean(o * o, axis=-1, keepdims=True)
    o_ref[...] = (o * lax.rsqrt(ms + EPS) * sw_ref[...] * (1.0 - lam_init)).astype(BF16)


def _attn_b_kernel(q_ref, kl_ref, vl_ref, kc_ref, vc_ref, o_ref, *, n_lat_chunks, tk):
    tq = q_ref.shape[0]
    lo = lax.broadcasted_iota(jnp.int32, (tq, V7X_LANES), 1) < HEAD_DIM
    parts = []
    for c in range(2):
        qc = q_ref[:, c * V7X_LANES:(c + 1) * V7X_LANES]
        zero = jnp.zeros_like(qc)
        parts += [jnp.where(lo, qc, zero), jnp.where(lo, zero, qc)]
    q4 = jnp.concatenate(parts, axis=0)
    o = _flash_rows(q4, lambda off: (kl_ref[0, pl.ds(off, tk), :], vl_ref[0, pl.ds(off, tk), :]),
                    kc_ref[0], vc_ref[0], n_lat_chunks=n_lat_chunks, tk=tk)
    for c in range(2):
        o_ref[:, c * V7X_LANES:(c + 1) * V7X_LANES] = jnp.where(
            lo, o[(2 * c) * tq:(2 * c + 1) * tq], o[(2 * c + 1) * tq:(2 * c + 2) * tq]).astype(BF16)


def _attention_call(kern, name, q, k, v, extras, q_lanes, n_groups, tq, latent_queries, dims):
    s_len, ctx_len, nbatch = dims["S"], dims["CTX"], dims["B"]
    tk = min(ATT_TK, s_len)
    q_len = s_len if latent_queries else ctx_len
    nq = q_len // tq
    q_blk0 = 0 if latent_queries else (nbatch * s_len) // tq
    ctx_blk0 = (nbatch * s_len) // ctx_len
    extra_specs = [pl.BlockSpec(e.shape, lambda b, g, qi: (0, 0)) for e in extras]
    return pl.pallas_call(
        functools.partial(kern, n_lat_chunks=s_len // tk if latent_queries else 0, tk=tk),
        grid=(nbatch, n_groups, nq),
        in_specs=[pl.BlockSpec((tq, q_lanes), lambda b, g, qi: (q_blk0 + b * nq + qi, g)),
                  pl.BlockSpec((1, s_len, V7X_LANES), lambda b, g, qi: (g, b, 0)),
                  pl.BlockSpec((1, s_len, V7X_LANES), lambda b, g, qi: (g, b, 0)),
                  pl.BlockSpec((1, ctx_len, V7X_LANES), lambda b, g, qi: (g, ctx_blk0 + b, 0)),
                  pl.BlockSpec((1, ctx_len, V7X_LANES), lambda b, g, qi: (g, ctx_blk0 + b, 0)),
                  *extra_specs],
        out_specs=pl.BlockSpec((tq, q_lanes), lambda b, g, qi: (b * nq + qi, g)),
        out_shape=jax.ShapeDtypeStruct((nbatch * q_len, n_groups * q_lanes), BF16),
        compiler_params=_cparams(("arbitrary", "arbitrary", "arbitrary")),
        name=name,
    )(q, k, v, k, v, *extras)


def _attention(qa, ka, va, qb, kbd, vbd, lamv, subln_w, lam_init, with_ctx_queries, dims):
    kern_a = functools.partial(_attn_a_kernel, lam_init=lam_init)
    gw = 2 * V7X_LANES

    def both(latent, tag):
        q_len = dims["S"] if latent else dims["CTX"]
        oa = _attention_call(kern_a, "attn_a" + tag, qa, ka, va, (lamv, subln_w), V7X_LANES,
                             A_HEADS, min(ATT_TQ, q_len), latent, dims)
        ob = _attention_call(_attn_b_kernel, "attn_b" + tag, qb, kbd, vbd, (), gw, B_KV,
                             min(ATT_TQ // 2, q_len), latent, dims)
        return oa, ob

    oa, ob = both(True, "")
    if with_ctx_queries:
        oa_c, ob_c = both(False, "_ctx")
    else:
        oa_c, ob_c = oa, ob
    return oa, oa_c, ob, ob_c


def _merge_kernel(x_ref, mod_ref, oa_ref, oac_ref, ob_ref, obc_ref, u_ref, up_ref, un_ref, cw_ref, cb_ref,
                  lnw_ref, lnb_ref, wg_ref, wa_ref, wb_ref, wc_ref, wo_ref, o_ref, uext_ref, *,
                  nps, ncps, n_lat_blocks):
    bt = x_ref.shape[0]
    r = pl.program_id(0)
    is_lat = r < n_lat_blocks
    pos = jnp.where(is_lat, r % nps, (r - n_lat_blocks) % ncps)
    last_pos = jnp.where(is_lat, nps - 1, ncps - 1)
    keep_prev = (pos != 0).astype(F32)
    keep_next = (pos != last_pos).astype(F32)

    x = x_ref[...]
    mod = mod_ref[0]
    h = _modulated(x, mod, 0, 1).astype(BF16)

    uext_ref[0:HALO, :] = up_ref[...] * keep_prev
    uext_ref[HALO:HALO + bt, :] = u_ref[...]
    uext_ref[HALO + bt:2 * HALO + bt, :] = un_ref[...] * keep_next
    cw = cw_ref[...]
    y = jnp.zeros((bt, C_WIDTH), F32) + cb_ref[...]
    base = HALO - C_KW // 2
    for k in range(C_KW):
        y = y + cw[k:k + 1, :] * uext_ref[base + k:base + k + bt, :]
    mu = jnp.mean(y, axis=-1, keepdims=True)
    yc = y - mu
    var = jnp.mean(yc * yc, axis=-1, keepdims=True)
    yn = yc * lax.rsqrt(var + EPS) * lnw_ref[...] + lnb_ref[...]
    oc = (yn * jax.nn.sigmoid(yn)).astype(BF16)

    def gate(i):
        logits = jnp.dot(h, wg_ref[:, i * D_MODEL:(i + 1) * D_MODEL], preferred_element_type=F32)
        return jax.nn.sigmoid(logits)

    oa = jnp.where(is_lat, oa_ref[...], oac_ref[...])
    ob = jnp.where(is_lat, ob_ref[...], obc_ref[...])
    merged = gate(0) * jnp.dot(oa, wa_ref[...], preferred_element_type=F32)
    merged = merged + gate(1) * jnp.dot(ob, wb_ref[...], preferred_element_type=F32)
    merged = merged + gate(2) * jnp.dot(oc, wc_ref[...], preferred_element_type=F32)
    out = jnp.dot(merged.astype(BF16), wo_ref[...], preferred_element_type=F32)
    o_ref[...] = x + mod[2:3] * out


def _merge(xy, mod_l, oa, oa_c, ob, ob_c, u, cw, cb, lnw, lnb, wg, wa, wb, wc, wo, n_blocks, dims):
    bt = TOK_BLOCK
    t_rows, nps, nbatch = dims["T"], dims["S"] // bt, dims["B"]
    hpb = bt // HALO
    n_halo = t_rows // HALO
    row = lambda r: (r, 0)
    full = lambda r: (0, 0)
    n_lat_blocks = nbatch * nps
    att_lat = pl.BlockSpec((bt, 4 * V7X_LANES), lambda r: (jnp.minimum(r, n_lat_blocks - 1), 0))
    att_ctx = pl.BlockSpec((bt, 4 * V7X_LANES), lambda r: (jnp.maximum(r - n_lat_blocks, 0), 0))
    return pl.pallas_call(
        functools.partial(_merge_kernel, nps=nps, ncps=dims["CTX"] // bt,
                          n_lat_blocks=n_lat_blocks),
        grid=(n_blocks,),
        in_specs=[pl.BlockSpec((bt, D_MODEL), row),
                  pl.BlockSpec((1, 6, D_MODEL), lambda r: (jnp.minimum(r // nps, nbatch), 0, 0)),
                  att_lat, att_ctx, att_lat, att_ctx,
                  pl.BlockSpec((bt, C_WIDTH), row),
                  pl.BlockSpec((HALO, C_WIDTH), lambda r: (jnp.maximum(r * hpb - 1, 0), 0)),
                  pl.BlockSpec((HALO, C_WIDTH), lambda r: (jnp.minimum((r + 1) * hpb, n_halo - 1), 0)),
                  pl.BlockSpec((C_KW + 1, C_WIDTH), full),
                  pl.BlockSpec((1, C_WIDTH), full),
                  pl.BlockSpec((1, C_WIDTH), full),
                  pl.BlockSpec((1, C_WIDTH), full),
                  pl.BlockSpec((D_MODEL, 3 * D_MODEL), full),
                  pl.BlockSpec((C_WIDTH, D_MODEL), full),
                  pl.BlockSpec((C_WIDTH, D_MODEL), full),
                  pl.BlockSpec((C_WIDTH, D_MODEL), full),
                  pl.BlockSpec((D_MODEL, D_MODEL), full)],
        out_specs=pl.BlockSpec((bt, D_MODEL), row),
        out_shape=jax.ShapeDtypeStruct((n_blocks * bt, D_MODEL), F32),
        scratch_shapes=[pltpu.VMEM((bt + 2 * HALO, C_WIDTH), F32)],
        compiler_params=_cparams(("arbitrary",)),
        name="merge",
    )(xy, mod_l, oa, oa_c, ob, ob_c, u, u, u, cw, cb, lnw, lnb, wg, wa, wb, wc, wo)


def _topk_desc(s, k):
    row = lax.broadcasted_iota(jnp.int32, (k, s.shape[1]), 0)

    def body(i, carry):
        prev, vals = carry
        m = jnp.max(jnp.where(s < prev, s, -jnp.inf), axis=0, keepdims=True)
        return m, jnp.where(row == i, m, vals)

    init = (jnp.full((1, s.shape[1]), jnp.inf, F32), jnp.zeros((k, s.shape[1]), F32))
    _, vals = lax.fori_loop(0, k, body, init)
    return vals


def _candidate_sums(sv1, sv2):
    half = P_TOPK // 2
    lead, rest = sv1[:half], sv1[half:]
    row = lax.broadcasted_iota(jnp.int32, lead.shape, 0)
    tiles = [lead + sv2[0:1], rest + sv2[0:1], lead + sv2[1:2]]
    for k2 in range(2, half):
        tiles.append(jnp.where(row < P_TOPK // (k2 + 1), lead + sv2[k2:k2 + 1], -jnp.inf))
    tiles.append(sv1[0:1] + sv2[half:])
    return jnp.concatenate(tiles, axis=0)


def _bf16_pair_words(x):
    bits = pltpu.bitcast(x.astype(BF16).astype(F32), jnp.int32)
    hi = lax.shift_right_logical(bits, 16)
    return hi | lax.shift_left(hi, 16)


def _peer_select_kernel(x_ref, mod_ref, wq_ref, k1_ref, k2_ref,
                        t_ref, need_ref, e1_ref, r2_ref, e2_ref, s1_scr):
    bt = x_ref.shape[0]
    n_lt = bt // V7X_LANES
    hb = _modulated(x_ref[...], mod_ref[0], 3, 4).astype(BF16)
    t_ref[...] = hb
    q = jnp.dot(hb, wq_ref[...], preferred_element_type=F32).astype(BF16)
    s1p = lax.dot_general(k1_ref[...], q, NT_DIMS, preferred_element_type=F32)
    for lt in range(n_lt):
        s1_scr[lt] = s1p[:, lt * V7X_LANES:(lt + 1) * V7X_LANES]
    head_row = lax.broadcasted_iota(jnp.int32, (P_HEADS, bt), 0)
    zero_t = jnp.zeros((P_HEADS, bt), F32)
    thr1, max1, tau_t = zero_t, zero_t, zero_t
    sv2_t = [zero_t] * P_TOPK
    for hh in range(P_HEADS):
        mine = head_row == hh
        qh = q[:, hh * V7X_LANES:(hh + 1) * V7X_LANES]
        s1 = jnp.concatenate([s1_scr[lt, pl.ds(hh, P_NKEYS, stride=P_HEADS), :]
                              for lt in range(n_lt)], axis=1)
        s2 = lax.dot_general(k2_ref[hh], qh, NT_DIMS, preferred_element_type=F32)
        sv1 = _topk_desc(s1, P_TOPK)
        sv2 = _topk_desc(s2, P_TOPK)
        top = _topk_desc(_candidate_sums(sv1, sv2), P_TOPK)
        z = jnp.sum(jnp.exp(top - top[0:1]), axis=0, keepdims=True)
        rank2 = jnp.zeros_like(s2)
        for k in range(P_TOPK):
            rank2 = rank2 + jnp.where(sv2[k:k + 1] > s2, 1.0, 0.0)
            sv2_t[k] = jnp.where(mine, sv2[k:k + 1], sv2_t[k])
        r2_ref[0, hh] = pltpu.bitcast(rank2.astype(BF16), jnp.int32)
        e2_ref[0, hh] = pltpu.bitcast((jnp.exp(s2 - sv2[0:1]) / z).astype(BF16), jnp.int32)
        tau_t = jnp.where(mine, top[P_TOPK - 1:P_TOPK], tau_t)
        thr1 = jnp.where(mine, sv1[P_TOPK - 1:P_TOPK], thr1)
        max1 = jnp.where(mine, sv1[0:1], max1)
    for lt in range(n_lt):
        ls = slice(lt * V7X_LANES, (lt + 1) * V7X_LANES)
        s1t = s1_scr[lt].reshape(P_NKEYS, P_HEADS, V7X_LANES)
        count = jnp.zeros_like(s1t)
        for k in range(P_TOPK):
            count = count + jnp.where(s1t + sv2_t[k][None, :, ls] >= tau_t[None, :, ls], 1.0, 0.0)
        need = jnp.where(s1t >= thr1[None, :, ls], count, 0.0)
        need_ref[0, lt] = _bf16_pair_words(need.reshape(P_NKEYS * P_HEADS, V7X_LANES))
        e1 = jnp.exp(s1t - max1[None, :, ls])
        e1_ref[0, lt] = _bf16_pair_words(e1.reshape(P_NKEYS * P_HEADS, V7X_LANES))


def _peer_dense_kernel(x_ref, mod_ref, t_ref, need_ref, e1_ref, r2_ref, e2_ref, u0_ref, u_ref,
                       vt_ref, fnw_ref, o_ref, a_scr, m_scr, acc_scr, *, n_chunks, final_norm):
    bt = x_ref.shape[0]
    c = pl.program_id(1)
    ipc = u_ref.shape[0] // P_NKEYS
    n_lt = bt // V7X_LANES

    @pl.when(c == 0)
    def _init():
        acc_scr[...] = jnp.zeros_like(acc_scr)
        a_scr[...] = lax.dot_general(u0_ref[...], t_ref[...], NT_DIMS, preferred_element_type=F32)


    def one_key(ii, _):
        roff = ii * P_NKEYS
        hrow = pl.ds(pl.multiple_of((c * ipc + ii) * P_HEADS, P_HEADS), P_HEADS)
        for lt in range(n_lt):
            ls = slice(lt * V7X_LANES, (lt + 1) * V7X_LANES)
            w = [jnp.zeros((ROW_GROUP, V7X_LANES), BF16) for _ in range(P_NKEYS // ROW_GROUP)]
            need_t = need_ref[0, lt, hrow, :]
            e1_t = e1_ref[0, lt, hrow, :]
            zero = jnp.zeros((ROW_GROUP, V7X_LANES), BF16)
            for hh in range(P_HEADS):
                words = (ROW_GROUP // 2, V7X_LANES)
                need_b = pltpu.bitcast(jnp.broadcast_to(need_t[hh:hh + 1], words), BF16)
                e1_b = pltpu.bitcast(jnp.broadcast_to(e1_t[hh:hh + 1], words), BF16)
                for jg in range(P_NKEYS // ROW_GROUP):
                    js = slice(jg * (ROW_GROUP // 2), (jg + 1) * (ROW_GROUP // 2))
                    rank2 = pltpu.bitcast(r2_ref[0, hh, js, ls], BF16)
                    e2 = pltpu.bitcast(e2_ref[0, hh, js, ls], BF16)
                    w[jg] = w[jg] + jnp.where(rank2 < need_b, e1_b * e2, zero)
            for jg in range(P_NKEYS // ROW_GROUP):
                rows = pl.ds(roff + jg * ROW_GROUP, ROW_GROUP)
                a = a_scr[rows, ls]
                g = 0.5 * a * (1.0 + lax.erf(a * (2.0 ** -0.5)))
                m_scr[rows, ls] = g.astype(BF16) * w[jg]
        return 0

    for ii in range(ipc):
        one_key(ii, 0)

    acc_scr[...] += jnp.dot(vt_ref[0], m_scr[...], preferred_element_type=F32)
    a_scr[...] = lax.dot_general(u_ref[...], t_ref[...], NT_DIMS, preferred_element_type=F32)

    @pl.when(c == n_chunks - 1)
    def _finish():
        x = x_ref[...]
        y = x + mod_ref[0][5:6] * acc_scr[...].T
        if final_norm:
            ms = jnp.mean(y * y, axis=-1, keepdims=True)
            y = y * lax.rsqrt(ms + EPS) * fnw_ref[...]
        o_ref[...] = y


def _peer(x1, mod_l, wq, k1p, k2p, u_tab, vt_tab, fnw, n_blocks, final_norm, dims):
    bt, ec = PEER_BLOCK, vt_tab.shape[2]
    nps, nbatch = dims["S"] // bt, dims["B"]
    n_chunks = vt_tab.shape[0]
    n_lt = bt // V7X_LANES
    rows_ih = P_NKEYS * P_HEADS
    mod_spec = pl.BlockSpec((1, 6, D_MODEL), lambda r, *_: (jnp.minimum(r // nps, nbatch), 0, 0))
    ih_shape = jax.ShapeDtypeStruct((n_blocks, n_lt, rows_ih, V7X_LANES), jnp.int32)
    hj_shape = jax.ShapeDtypeStruct((n_blocks, P_HEADS, P_NKEYS // 2, bt), jnp.int32)
    ih_spec = pl.BlockSpec((1, n_lt, rows_ih, V7X_LANES), lambda r, *_: (r, 0, 0, 0))
    hj_spec = pl.BlockSpec((1, P_HEADS, P_NKEYS // 2, bt), lambda r, *_: (r, 0, 0, 0))
    tok_spec = pl.BlockSpec((bt, D_MODEL), lambda r, *_: (r, 0))

    t_mod, need, e1, rank2, e2 = pl.pallas_call(
        _peer_select_kernel,
        grid=(n_blocks,),
        in_specs=[tok_spec, mod_spec,
                  pl.BlockSpec((D_MODEL, P_HEADS * V7X_LANES), lambda r: (0, 0)),
                  pl.BlockSpec((rows_ih, P_HEADS * V7X_LANES), lambda r: (0, 0)),
                  pl.BlockSpec((P_HEADS, P_NKEYS, V7X_LANES), lambda r: (0, 0, 0))],
        out_specs=(tok_spec, ih_spec, ih_spec, hj_spec, hj_spec),
        out_shape=(jax.ShapeDtypeStruct((n_blocks * bt, D_MODEL), BF16),
                   ih_shape,
                   ih_shape,
                   hj_shape,
                   hj_shape),
        scratch_shapes=[pltpu.VMEM((n_lt, rows_ih, V7X_LANES), F32)],
        compiler_params=_cparams(("arbitrary",)),
        name="peer_select",
    )(x1, mod_l, wq, k1p, k2p)

    return pl.pallas_call(
        functools.partial(_peer_dense_kernel, n_chunks=n_chunks, final_norm=final_norm),
        grid=(n_blocks, n_chunks),
        in_specs=[tok_spec, mod_spec, tok_spec, ih_spec, ih_spec, hj_spec, hj_spec,
                  pl.BlockSpec((ec, D_MODEL), lambda r, c: (0, 0)),
                  pl.BlockSpec((ec, D_MODEL), lambda r, c: (jnp.minimum(c + 1, n_chunks - 1), 0)),
                  pl.BlockSpec((1, D_MODEL, ec), lambda r, c: (c, 0, 0)),
                  pl.BlockSpec((1, D_MODEL), lambda r, c: (0, 0))],
        out_specs=tok_spec,
        out_shape=jax.ShapeDtypeStruct((n_blocks * bt, D_MODEL), F32),
        scratch_shapes=[pltpu.VMEM((ec, bt), F32),
                        pltpu.VMEM((ec, bt), BF16),
                        pltpu.VMEM((D_MODEL, bt), F32)],
        compiler_params=_cparams(("arbitrary", "arbitrary")),
        name="peer_dense",
    )(x1, mod_l, t_mod, need, e1, rank2, e2, u_tab, u_tab, vt_tab, fnw)


def _rope_tables(s_len, nbatch, n_ctx_rows):
    t = jnp.arange(s_len, dtype=jnp.int32)
    row = (t // GRID_W).astype(F32)
    col = (t % GRID_W).astype(F32)
    axis_dim = HEAD_DIM // 2
    inv = ROPE_THETA ** (-jnp.arange(0, axis_dim, 2, dtype=F32) / axis_dim)
    ar = row[:, None] * inv[None, :]
    ac = col[:, None] * inv[None, :]
    ang = jnp.concatenate([ar, ar, ac, ac], axis=-1)
    cos, sin = jnp.cos(ang), jnp.sin(ang)
    quarter = jnp.arange(HEAD_DIM) // (HEAD_DIM // 4)
    first = (quarter % 2 == 0)[None, :]
    sin_a = jnp.where(first, -sin, 0.0)
    sin_b = jnp.where(first, 0.0, sin)

    def expand(tab, ctx_fill):
        lat = jnp.tile(jnp.concatenate([tab, tab], axis=-1), (nbatch, 1))
        return jnp.concatenate([lat, jnp.full((n_ctx_rows, V7X_LANES), ctx_fill, F32)], axis=0)

    return expand(cos, 1.0), expand(sin_a, 0.0), expand(sin_b, 0.0)


def _padded_keys(keys_l):
    z = jnp.zeros_like(keys_l[:, 0])
    k0 = jnp.concatenate([keys_l[:, 0], z], axis=-1).transpose(1, 0, 2)
    eye = jnp.eye(P_HEADS, dtype=keys_l.dtype)
    k1p = (k0[:, :, None, :] * eye[None, :, :, None]).reshape(
        P_NKEYS * P_HEADS, P_HEADS * V7X_LANES)
    k2p = jnp.concatenate([z, keys_l[:, 1]], axis=-1)
    return k1p.astype(BF16), k2p.astype(BF16)


def _chunked_transpose(v_tab, ec):
    n_exp, d = v_tab.shape
    return v_tab.astype(BF16).reshape(n_exp // ec, ec, d).transpose(0, 2, 1)


def kernel(x, c, ctx, c_ctx, w_ada, b_ada, w_in, lam_q1, lam_k1, lam_q2, lam_k2, subln_w, q_norm_w, k_norm_w, conv_w, conv_b, conv_ln_w, conv_ln_b, w_branch_a, w_branch_b, w_branch_c, w_out, peer_wq, peer_keys, peer_u, peer_v, final_norm_w):
    nbatch, s_len, d = x.shape
    ctx_len = ctx.shape[1]
    depth = w_ada.shape[0]
    n_lat, n_ctx = nbatch * s_len, nbatch * ctx_len
    dims = {"B": nbatch, "S": s_len, "CTX": ctx_len, "T": n_lat + n_ctx}
    assert d == D_MODEL and nbatch + 1 <= MOD_ROWS
    assert s_len % PEER_BLOCK == 0 and s_len % ATT_TQ == 0 and ctx_len % TOK_BLOCK == 0
    assert ctx_len % min(ATT_TQ, ctx_len) == 0 and n_lat % min(ATT_TQ, ctx_len) == 0
    assert n_ctx % PEER_BLOCK == 0 and n_lat % ctx_len == 0

    cvec = jnp.zeros((MOD_ROWS, d), F32).at[:nbatch].set(c).at[nbatch].set(c_ctx)
    mod = _ada_rows(cvec, w_ada, b_ada).reshape(depth, MOD_ROWS, 6, d)
    tabs = _rope_tables(s_len, nbatch, n_ctx)
    tile2 = lambda v: jnp.concatenate([v, v], axis=-1)[None, :]

    xy = jnp.concatenate([x.reshape(n_lat, d), ctx.reshape(n_ctx, d)], axis=0)
    for l in range(depth):
        last = l == depth - 1
        lam_init = 0.8 - 0.6 * math.exp(-0.3 * l)
        w_bf = w_in[l].astype(BF16)
        lamv = jnp.stack([lam_q1[l], lam_k1[l], lam_q2[l], lam_k2[l]], axis=0)

        qa, ka, va, qb, kbd, vbd, u = _inproj(
            xy, mod[l], w_bf[:, :OFF_GATE], tabs, tile2(q_norm_w[l]), tile2(k_norm_w[l]), dims)
        oa, oa_c, ob, ob_c = _attention(qa, ka, va, qb, kbd, vbd, lamv, subln_w[l][None, :],
                                        lam_init, not last, dims)
        n_rows = n_lat if last else n_lat + n_ctx
        cw = jnp.concatenate([conv_w[l], jnp.zeros((1, C_WIDTH), F32)], axis=0)
        x1 = _merge(xy, mod[l], oa, oa_c, ob, ob_c, u, cw, conv_b[l][None, :], conv_ln_w[l][None, :],
                    conv_ln_b[l][None, :], w_bf[:, OFF_GATE:], w_branch_a[l].astype(BF16),
                    w_branch_b[l].astype(BF16), w_branch_c[l].astype(BF16),
                    w_out[l].astype(BF16), n_rows // TOK_BLOCK, dims)
        k1p, k2p = _padded_keys(peer_keys[l])
        xy = _peer(x1, mod[l], peer_wq[l].astype(BF16), k1p, k2p,
                   peer_u[l].astype(BF16), _chunked_transpose(peer_v[l], PEER_ECHUNK),
                   final_norm_w[None, :],
                   n_rows // PEER_BLOCK, last, dims)
    return xy.reshape(nbatch, s_len, d)
```
